```python
import math
import jax
import jax.numpy as jnp
from jax import lax
import numpy as np

D_MODEL = 1024
BATCH = 8
SEQ = 4096
DEPTH = 1

D_HEAD = 64
NSA_HEADS = 8
NSA_KV_HEADS = 2
NSA_GROUP = NSA_HEADS // NSA_KV_HEADS
D_NSA = NSA_HEADS * D_HEAD
D_KV = NSA_KV_HEADS * D_HEAD
N_BRANCH = 3
N_GATES = NSA_HEADS * N_BRANCH
CMP_BLOCK = 32
CMP_STRIDE = 16
CMP_HIDDEN = 256
SEL_BLOCK = 64
SEL_TOPK = 16
WINDOW = 512
Q_BLOCK = 64
GMLP_GROUPS = 8
GMLP_GROUP_DIM = 64
D_GMLP = GMLP_GROUPS * GMLP_GROUP_DIM
CHUNK = 128
D_MIX = D_NSA + D_GMLP
D_PROJ = D_NSA + 6 * D_KV + N_GATES + 2 * D_GMLP
SPLITS = (D_NSA, D_NSA + D_KV, D_NSA + 2 * D_KV, D_NSA + 3 * D_KV, D_NSA + 4 * D_KV, D_NSA + 5 * D_KV, D_NSA + 6 * D_KV, D_NSA + 6 * D_KV + N_GATES)
REL_BUCKETS = 32
REL_MAX_DIST = 128
D_FF = 2816
CONV_WIDTH = 3
EPS = 1e-6
NEG_INF = -1e30
FORCED_SCORE = 1e4

kernel_name = "hybrid_nsa_sgu_convffn_block"


def _rms_norm(x, g):
    xf = x.astype(jnp.float32)
    y = xf * lax.rsqrt(jnp.mean(xf * xf, axis=-1, keepdims=True) + EPS)
    return (y * g.astype(jnp.float32)).astype(x.dtype)


def _rel_bucket(dist):
    max_exact = REL_BUCKETS // 2
    d = jnp.maximum(dist, 1).astype(jnp.float32)
    log_b = max_exact + (jnp.log(d / max_exact) / math.log(REL_MAX_DIST / max_exact) * (REL_BUCKETS - max_exact)).astype(jnp.int32)
    log_b = jnp.clip(log_b, max_exact, REL_BUCKETS - 1)
    return jnp.where(dist < max_exact, jnp.maximum(dist, 0), log_b)


def _bias_shared(dist, rel_bias):
    b = rel_bias[_rel_bucket(dist)]
    return jnp.transpose(b, (2, 0, 1)).reshape(NSA_KV_HEADS, NSA_GROUP, dist.shape[0], dist.shape[1]).astype(jnp.float32)


def _bias_gathered(dist, rel_bias):
    table = rel_bias.T.reshape(NSA_KV_HEADS, NSA_GROUP, REL_BUCKETS)
    gi = jnp.arange(NSA_KV_HEADS)[None, :, None, None, None]
    ri = jnp.arange(NSA_GROUP)[None, None, :, None, None]
    return table[gi, ri, _rel_bucket(dist)[:, :, None]].astype(jnp.float32)


def _heads(t, B, T):
    return jnp.transpose(t.reshape(B, T, NSA_KV_HEADS, D_HEAD), (0, 2, 1, 3))


def _compress(tok, pe, w1, b1, w2):
    B, G, T, D = tok.shape
    nc = (T - CMP_BLOCK) // CMP_STRIDE + 1
    idx = jnp.arange(nc)[:, None] * CMP_STRIDE + jnp.arange(CMP_BLOCK)[None, :]
    blk = (tok[:, :, idx] + pe).reshape(B, G, nc, CMP_BLOCK * D)
    return jax.nn.gelu(blk @ w1 + b1) @ w2


def _nsa(q, k_cmp, v_cmp, k_slc, v_slc, k_win, v_win, gates, rel_bias):
    B, G, R, T, D = q.shape
    nc = k_cmp.shape[2]
    ns = T // SEL_BLOCK
    n_sel = min(SEL_TOPK, ns)
    cmp_end = jnp.arange(nc) * CMP_STRIDE + (CMP_BLOCK - 1)
    cs = jnp.arange(nc)[:, None] * CMP_STRIDE
    ss = jnp.arange(ns)[None, :] * SEL_BLOCK
    overlap = ((cs < ss + SEL_BLOCK) & (cs + CMP_BLOCK > ss)).astype(jnp.float32)
    blk_start = jnp.arange(ns) * SEL_BLOCK
    k_blocks = k_slc.reshape(B, G, ns, SEL_BLOCK, D)
    v_blocks = v_slc.reshape(B, G, ns, SEL_BLOCK, D)
    k_pad = jnp.pad(k_win, ((0, 0), (0, 0), (WINDOW, 0), (0, 0)))
    v_pad = jnp.pad(v_win, ((0, 0), (0, 0), (WINDOW, 0), (0, 0)))
    bi = jnp.arange(B)[:, None, None, None]
    gi = jnp.arange(G)[None, :, None, None]

    def block(qi):
        t0 = qi * Q_BLOCK
        tpos = t0 + jnp.arange(Q_BLOCK)
        qb = lax.dynamic_slice_in_dim(q, t0, Q_BLOCK, axis=3)
        gb = lax.dynamic_slice_in_dim(gates, t0, Q_BLOCK, axis=3)

        dist_c = tpos[:, None] - cmp_end[None, :]
        valid_c = dist_c >= 0
        s_c = jnp.einsum('bgrqd,bgnd->bgrqn', qb, k_cmp).astype(jnp.float32) + _bias_shared(dist_c, rel_bias)
        p_c = jnp.where(valid_c, jax.nn.softmax(jnp.where(valid_c, s_c, NEG_INF), axis=-1), 0.0)
        o_c = jnp.einsum('bgrqn,bgnd->bgrqd', p_c.astype(v_cmp.dtype), v_cmp)

        imp = jnp.einsum('bgrqn,ns->bgqs', p_c, overlap)
        cur = tpos[:, None] // SEL_BLOCK
        j = jnp.arange(ns)[None, :]
        forced = (j == 0) | (j == cur) | (j == cur - 1)
        causal_blk = blk_start[None, :] <= tpos[:, None]
        score = jnp.where(forced, FORCED_SCORE, jnp.where(causal_blk, imp, NEG_INF))
        _, sel = lax.top_k(score, n_sel)

        k_sel = k_blocks[bi, gi, sel].reshape(B, G, Q_BLOCK, n_sel * SEL_BLOCK, D)
        v_sel = v_blocks[bi, gi, sel].reshape(B, G, Q_BLOCK, n_sel * SEL_BLOCK, D)
        kpos = (sel[..., None] * SEL_BLOCK + jnp.arange(SEL_BLOCK)).reshape(B, G, Q_BLOCK, n_sel * SEL_BLOCK)
        dist_s = tpos[None, None, :, None] - kpos
        s_s = jnp.einsum('bgrqd,bgqkd->bgrqk', qb, k_sel).astype(jnp.float32) + _bias_gathered(dist_s, rel_bias)
        p_s = jax.nn.softmax(jnp.where((dist_s >= 0)[:, :, None], s_s, NEG_INF), axis=-1)
        o_s = jnp.einsum('bgrqk,bgqkd->bgrqd', p_s.astype(v_sel.dtype), v_sel)

        kw = lax.dynamic_slice_in_dim(k_pad, t0, Q_BLOCK + WINDOW, axis=2)
        vw = lax.dynamic_slice_in_dim(v_pad, t0, Q_BLOCK + WINDOW, axis=2)
        kpos_w = t0 - WINDOW + jnp.arange(Q_BLOCK + WINDOW)
        dist_w = tpos[:, None] - kpos_w[None, :]
        valid_w = (dist_w >= 0) & (dist_w < WINDOW) & (kpos_w[None, :] >= 0)
        s_w = jnp.einsum('bgrqd,bgkd->bgrqk', qb, kw).astype(jnp.float32) + _bias_shared(dist_w, rel_bias)
        p_w = jax.nn.softmax(jnp.where(valid_w, s_w, NEG_INF), axis=-1)
        o_w = jnp.einsum('bgrqk,bgkd->bgrqd', p_w.astype(vw.dtype), vw)

        return gb[..., 0:1] * o_c + gb[..., 1:2] * o_s + gb[..., 2:3] * o_w

    o = lax.map(block, jnp.arange(T // Q_BLOCK))
    return jnp.transpose(o, (1, 0, 4, 2, 3, 5)).reshape(B, T, G * R * D)


def _sgu(uv, sgu_norm, sgu_w, sgu_b):
    B, T, _ = uv.shape
    u, v = jnp.split(jax.nn.gelu(uv), 2, axis=-1)
    v = _rms_norm(v, sgu_norm).reshape(B, T // CHUNK, CHUNK, GMLP_GROUPS, GMLP_GROUP_DIM)
    w = sgu_w * jnp.tril(jnp.ones((CHUNK, CHUNK), sgu_w.dtype))
    z = jnp.einsum('gij,bcjge->bcige', w, v) + sgu_b.T[:, :, None]
    return u * z.reshape(B, T, D_GMLP)


def _conv_ffn(h, w_up, conv_w, conv_b, w_down):
    up = h @ w_up
    up = lax.conv_general_dilated(up, conv_w[:, None, :], (1,), [(CONV_WIDTH - 1, 0)], dimension_numbers=('NWC', 'WIO', 'NWC'), feature_group_count=up.shape[-1]) + conv_b
    a, g = jnp.split(up, 2, axis=-1)
    return (jax.nn.silu(g) * a) @ w_down


def setup_inputs(seed: int = 0) -> dict:
    key = jax.random.key(seed)
    ks = jax.random.split(key, 32)
    L = DEPTH

    def nrm(k, shape, scale):
        return jax.random.normal(k, shape, jnp.float32) * scale

    def gain(k, shape):
        return 1.0 + 0.01 * jax.random.normal(k, shape, jnp.float32)

    return {
        "x": nrm(ks[0], (BATCH, SEQ, D_MODEL), 1.0),
        "rel_bias": nrm(ks[1], (REL_BUCKETS, NSA_HEADS), 0.5),
        "attn_norm": gain(ks[2], (L, D_MODEL)),
        "w_in": nrm(ks[3], (L, D_MODEL, D_PROJ), D_MODEL ** -0.5),
        "q_norm": gain(ks[4], (L, D_HEAD)),
        "k_norm_cmp": gain(ks[5], (L, D_HEAD)),
        "k_norm_slc": gain(ks[6], (L, D_HEAD)),
        "k_norm_win": gain(ks[7], (L, D_HEAD)),
        "cmp_pe_k": nrm(ks[8], (L, CMP_BLOCK, D_HEAD), 0.1),
        "cmp_w1_k": nrm(ks[9], (L, CMP_BLOCK * D_HEAD, CMP_HIDDEN), (CMP_BLOCK * D_HEAD) ** -0.5),
        "cmp_b1_k": nrm(ks[10], (L, CMP_HIDDEN), 0.01),
        "cmp_w2_k": nrm(ks[11], (L, CMP_HIDDEN, D_HEAD), CMP_HIDDEN ** -0.5),
        "cmp_pe_v": nrm(ks[12], (L, CMP_BLOCK, D_HEAD), 0.1),
        "cmp_w1_v": nrm(ks[13], (L, CMP_BLOCK * D_HEAD, CMP_HIDDEN), (CMP_BLOCK * D_HEAD) ** -0.5),
        "cmp_b1_v": nrm(ks[14], (L, CMP_HIDDEN), 0.01),
        "cmp_w2_v": nrm(ks[15], (L, CMP_HIDDEN, D_HEAD), CMP_HIDDEN ** -0.5),
        "sgu_norm": gain(ks[16], (L, D_GMLP)),
        "sgu_w": nrm(ks[17], (L, GMLP_GROUPS, CHUNK, CHUNK), CHUNK ** -0.5),
        "sgu_b": gain(ks[18], (L, GMLP_GROUPS, CHUNK)),
        "w_out": nrm(ks[19], (L, D_MIX, D_MODEL), D_MIX ** -0.5),
        "ffn_norm": gain(ks[20], (L, D_MODEL)),
        "w_up": nrm(ks[21], (L, D_MODEL, 2 * D_FF), D_MODEL ** -0.5),
        "conv_w": nrm(ks[22], (L, CONV_WIDTH, 2 * D_FF), CONV_WIDTH ** -0.5),
        "conv_b": nrm(ks[23], (L, 2 * D_FF), 0.01),
        "w_down": nrm(ks[24], (L, D_FF, D_MODEL), D_FF ** -0.5),
    }


def reference(x, rel_bias, attn_norm, w_in, q_norm, k_norm_cmp, k_norm_slc, k_norm_win,
              cmp_pe_k, cmp_w1_k, cmp_b1_k, cmp_w2_k, cmp_pe_v, cmp_w1_v, cmp_b1_v, cmp_w2_v,
              sgu_norm, sgu_w, sgu_b, w_out, ffn_norm, w_up, conv_w, conv_b, w_down):
    B, T, _ = x.shape
    G, R = NSA_KV_HEADS, NSA_GROUP
    for l in range(DEPTH):
        h = _rms_norm(x, attn_norm[l])
        proj = h @ w_in[l]
        q, kc, vc, ks_, vs, kw, vw, gl, uv = jnp.split(proj, SPLITS, axis=-1)
        q = _rms_norm(q.reshape(B, T, G, R, D_HEAD), q_norm[l]) * (D_HEAD ** -0.5)
        q = jnp.transpose(q, (0, 2, 3, 1, 4))
        k_cmp = _rms_norm(_compress(_heads(kc, B, T), cmp_pe_k[l], cmp_w1_k[l], cmp_b1_k[l], cmp_w2_k[l]), k_norm_cmp[l])
        v_cmp = _compress(_heads(vc, B, T), cmp_pe_v[l], cmp_w1_v[l], cmp_b1_v[l], cmp_w2_v[l])
        k_slc = _rms_norm(_heads(ks_, B, T), k_norm_slc[l])
        v_slc = _heads(vs, B, T)
        k_win = _rms_norm(_heads(kw, B, T), k_norm_win[l])
        v_win = _heads(vw, B, T)
        gates = jnp.transpose(jax.nn.sigmoid(gl.reshape(B, T, G, R, N_BRANCH)), (0, 2, 3, 1, 4))
        o_nsa = _nsa(q, k_cmp, v_cmp, k_slc, v_slc, k_win, v_win, gates, rel_bias)
        o_sgu = _sgu(uv, sgu_norm[l], sgu_w[l], sgu_b[l])
        x = x + jnp.concatenate([o_nsa, o_sgu], axis=-1) @ w_out[l]
        x = x + _conv_ffn(_rms_norm(x, ffn_norm[l]), w_up[l], conv_w[l], conv_b[l], w_down[l])
    return x
```

```python
import functools
import math

import numpy as np
import jax
import jax.numpy as jnp
from jax import lax
from jax.experimental import pallas as pl
from jax.experimental.pallas import tpu as pltpu

F32 = jnp.float32
BF16 = jnp.bfloat16

D_MODEL = 1024
D_HEAD = 64
NSA_HEADS = 8
NSA_KV_HEADS = 2
NSA_GROUP = NSA_HEADS // NSA_KV_HEADS
D_NSA = NSA_HEADS * D_HEAD
D_KV = NSA_KV_HEADS * D_HEAD
N_BRANCH = 3
N_GATES = NSA_HEADS * N_BRANCH
CMP_BLOCK = 32
CMP_STRIDE = 16
CMP_HIDDEN = 256
SEL_BLOCK = 64
SEL_TOPK = 16
WINDOW = 512
GMLP_GROUPS = 8
GMLP_GROUP_DIM = 64
D_GMLP = GMLP_GROUPS * GMLP_GROUP_DIM
CHUNK = 128
D_MIX = D_NSA + D_GMLP
REL_BUCKETS = 32
REL_MAX_DIST = 128
D_FF = 2816
EPS = 1e-6
NEG = -1e30
FORCED_SCORE = 1e4
LOG2E = 1.4426950408889634

LANES = 128
QB = 64
ROWS = NSA_GROUP * QB
KT = 256
BLOCKS_PER_TILE = KT // SEL_BLOCK
WIN_BLOCKS = WINDOW // SEL_BLOCK
KPAD = 768
MAX_BLOCKS = 64
KAUG = 256
VAUG = 128
BAND = 16
FFN_CHUNK = 256
VMEM_LIMIT = 56 * 1024 * 1024

_NT = (((1,), (1,)), ((), ()))


def _dot(a, b):
    return jnp.dot(a, b, preferred_element_type=F32)


def _dot_nt(a, b):
    return lax.dot_general(a, b, _NT, preferred_element_type=F32)


def _split_bf16(x):
    hi = x.astype(BF16)
    lo = (x - hi.astype(F32)).astype(BF16)
    return hi, lo


def _gelu_tanh(x):
    return 0.5 * x * (1.0 + jnp.tanh(0.7978845608028654 * (x + 0.044715 * (x * x * x))))


def _sigmoid(x):
    return 0.5 * (1.0 + jnp.tanh(0.5 * x))


def _group_mean_sq(t, ones_blockdiag, width):
    t2 = t * t
    hi, lo = _split_bf16(t2)
    return (_dot(hi, ones_blockdiag) + _dot(lo, ones_blockdiag)) * (1.0 / width)


def _proj_kernel(x_ref, an_ref, wq_ref, wkv_ref, wg_ref, wuv_ref, qn_ref, ksn_ref, kwn_ref,
                 sgun_ref, sguw_ref, sgub_ref, bd_ref,
                 q_out, kc_out, vc_out, ks_out, vs_out, kw_out, vw_out, g_out, sgu_out):
    tm = x_ref.shape[1]
    x = x_ref[0]
    ms = jnp.mean(x * x, axis=-1, keepdims=True)
    h = (x * lax.rsqrt(ms + EPS) * an_ref[...]).astype(BF16)

    bd = bd_ref[...]
    q = _dot(h, wq_ref[...])
    qn = q * lax.rsqrt(_group_mean_sq(q, bd, D_HEAD) + EPS) * qn_ref[...]
    for hh in range(NSA_HEADS):
        q_out[0, hh] = qn[:, hh * D_HEAD:(hh + 1) * D_HEAD].astype(BF16)

    kv = _dot(h, wkv_ref[...])
    bd_kv = bd[:D_KV, :D_KV]
    kc = kv[:, 0 * D_KV:1 * D_KV]
    vc = kv[:, 1 * D_KV:2 * D_KV]
    ks = kv[:, 2 * D_KV:3 * D_KV]
    vs = kv[:, 3 * D_KV:4 * D_KV]
    kw = kv[:, 4 * D_KV:5 * D_KV]
    vw = kv[:, 5 * D_KV:6 * D_KV]
    ks = ks * lax.rsqrt(_group_mean_sq(ks, bd_kv, D_HEAD) + EPS) * ksn_ref[...]
    kw = kw * lax.rsqrt(_group_mean_sq(kw, bd_kv, D_HEAD) + EPS) * kwn_ref[...]
    for t, o_ref in ((kc, kc_out), (vc, vc_out), (ks, ks_out), (vs, vs_out), (kw, kw_out), (vw, vw_out)):
        for g in range(NSA_KV_HEADS):
            o_ref[0, g] = t[:, g * D_HEAD:(g + 1) * D_HEAD].astype(BF16)

    gl = _dot(h, wg_ref[...])
    g_out[0] = _sigmoid(gl)

    uv = _gelu_tanh(_dot(h, wuv_ref[...]))
    u = uv[:, :D_GMLP]
    v = uv[:, D_GMLP:]
    vms = jnp.mean(v * v, axis=-1, keepdims=True)
    vb = (v * lax.rsqrt(vms + EPS) * sgun_ref[...]).astype(BF16)

    row = lax.broadcasted_iota(jnp.int32, (CHUNK, CHUNK), 0)
    col = lax.broadcasted_iota(jnp.int32, (CHUNK, CHUNK), 1)
    tril = col <= row
    w_tril = [jnp.where(tril, sguw_ref[g], 0.0).astype(BF16) for g in range(GMLP_GROUPS)]
    first_half = lax.broadcasted_iota(jnp.int32, (CHUNK, LANES), 1) < GMLP_GROUP_DIM
    for c in range(tm // CHUNK):
        rows = slice(c * CHUNK, (c + 1) * CHUNK)
        zs = []
        for p in range(D_GMLP // LANES):
            blk = vb[rows, p * LANES:(p + 1) * LANES]
            z0 = _dot(w_tril[2 * p], blk)
            z1 = _dot(w_tril[2 * p + 1], blk)
            zs.append(jnp.where(first_half, z0, z1))
        z = jnp.concatenate(zs, axis=1) + sgub_ref[...]
        sgu_out[0, rows, :] = (u[rows, :] * z).astype(BF16)


def _proj_call(x, attn_norm, w_q, w_kv, w_g, w_uv, qn, ksn, kwn, sgu_norm, sgu_w, sgu_bt, bd, tm):
    B, T, _ = x.shape
    const2 = lambda b, i: (0, 0)
    const3 = lambda b, i: (0, 0, 0)
    head_spec = lambda nh: pl.BlockSpec((1, nh, tm, D_HEAD), lambda b, i: (b, 0, i, 0))
    kv_shape = jax.ShapeDtypeStruct((B, NSA_KV_HEADS, T, D_HEAD), BF16)
    return pl.pallas_call(
        _proj_kernel,
        grid=(B, T // tm),
        in_specs=[
            pl.BlockSpec((1, tm, D_MODEL), lambda b, i: (b, i, 0)),
            pl.BlockSpec((1, D_MODEL), const2),
            pl.BlockSpec(w_q.shape, const2),
            pl.BlockSpec(w_kv.shape, const2),
            pl.BlockSpec(w_g.shape, const2),
            pl.BlockSpec(w_uv.shape, const2),
            pl.BlockSpec((1, D_NSA), const2),
            pl.BlockSpec((1, D_KV), const2),
            pl.BlockSpec((1, D_KV), const2),
            pl.BlockSpec((1, D_GMLP), const2),
            pl.BlockSpec(sgu_w.shape, const3),
            pl.BlockSpec(sgu_bt.shape, const2),
            pl.BlockSpec(bd.shape, const2),
        ],
        out_specs=[
            head_spec(NSA_HEADS),
            head_spec(NSA_KV_HEADS), head_spec(NSA_KV_HEADS), head_spec(NSA_KV_HEADS),
            head_spec(NSA_KV_HEADS), head_spec(NSA_KV_HEADS), head_spec(NSA_KV_HEADS),
            pl.BlockSpec((1, tm, LANES), lambda b, i: (b, i, 0)),
            pl.BlockSpec((1, tm, D_GMLP), lambda b, i: (b, i, 0)),
        ],
        out_shape=[
            jax.ShapeDtypeStruct((B, NSA_HEADS, T, D_HEAD), BF16),
            kv_shape, kv_shape, kv_shape, kv_shape, kv_shape, kv_shape,
            jax.ShapeDtypeStruct((B, T, LANES), F32),
            jax.ShapeDtypeStruct((B, T, D_GMLP), BF16),
        ],
        compiler_params=pltpu.CompilerParams(
            dimension_semantics=("arbitrary", "arbitrary"), vmem_limit_bytes=VMEM_LIMIT),
        name="proj",
    )(x, attn_norm, w_q, w_kv, w_g, w_uv, qn, ksn, kwn, sgu_norm, sgu_w, sgu_bt, bd)


def _compress_one(tok_ref, pe_ref, w1_ref, b1_ref, w2_ref):
    half = CMP_STRIDE * D_HEAD
    tok = tok_ref[0, 0].astype(F32)
    top = (tok + pe_ref[0:1, :]).astype(BF16)
    bot = (tok + pe_ref[1:2, :]).astype(BF16)
    a = _dot(top, w1_ref[:half, :])
    b = _dot(bot, w1_ref[half:, :])
    ncp = a.shape[0]
    pre = a + pltpu.roll(b, ncp - 1, 0) + b1_ref[...]
    hid = _gelu_tanh(pre).astype(BF16)
    return _dot(hid, w2_ref[...])


def _compress_kernel(kc_ref, vc_ref, pek_ref, w1k_ref, b1k_ref, w2k_ref, kn_ref,
                     pev_ref, w1v_ref, b1v_ref, w2v_ref, k_out, v_out):
    k = _compress_one(kc_ref, pek_ref, w1k_ref, b1k_ref, w2k_ref)
    kms = jnp.mean(k * k, axis=-1, keepdims=True)
    k_out[0, 0] = (k * lax.rsqrt(kms + EPS) * kn_ref[...]).astype(BF16)
    v_out[0, 0] = _compress_one(vc_ref, pev_ref, w1v_ref, b1v_ref, w2v_ref).astype(BF16)


def _compress_call(kc, vc, pek, w1k, b1k, w2k, kn, pev, w1v, b1v, w2v):
    B, G, ncp, width = kc.shape
    const2 = lambda b, g: (0, 0)
    tok_spec = pl.BlockSpec((1, 1, ncp, width), lambda b, g: (b, g, 0, 0))
    out_spec = pl.BlockSpec((1, 1, ncp, D_HEAD), lambda b, g: (b, g, 0, 0))
    full = lambda a: pl.BlockSpec(a.shape, const2)
    out_shape = jax.ShapeDtypeStruct((B, G, ncp, D_HEAD), BF16)
    return pl.pallas_call(
        _compress_kernel,
        grid=(B, G),
        in_specs=[tok_spec, tok_spec, full(pek), full(w1k), full(b1k), full(w2k), full(kn),
                  full(pev), full(w1v), full(b1v), full(w2v)],
        out_specs=[out_spec, out_spec],
        out_shape=[out_shape, out_shape],
        compiler_params=pltpu.CompilerParams(
            dimension_semantics=("arbitrary", "arbitrary"), vmem_limit_bytes=VMEM_LIMIT),
        name="compress",
    )(kc, vc, pek, w1k, b1k, w2k, kn, pev, w1v, b1v, w2v)


def _rel_bucket_np(dist):
    max_exact = REL_BUCKETS // 2
    d = np.maximum(dist, 1).astype(np.float32)
    log_b = max_exact + (np.log(d / np.float32(max_exact)) / np.float32(math.log(REL_MAX_DIST / max_exact))
                         * np.float32(REL_BUCKETS - max_exact)).astype(np.int32)
    log_b = np.clip(log_b, max_exact, REL_BUCKETS - 1)
    return np.where(dist < max_exact, np.maximum(dist, 0), log_b)


def _bias_tables(rel_bias):
    rb = rel_bias.astype(F32) * LOG2E
    qi = np.arange(QB)[:, None]
    kk = np.arange(KT)[None, :]
    m = (BLOCKS_PER_TILE - 1) - kk // SEL_BLOCK
    dist = m * SEL_BLOCK + qi - (kk % SEL_BLOCK)
    near = rb[_rel_bucket_np(dist)]
    near = jnp.where((dist >= 0)[:, :, None], near, NEG)
    e = np.arange(BAND)[None, :]
    dist_c = qi + CMP_STRIDE * e - (3 * CMP_STRIDE + CMP_BLOCK - 1)
    band = rb[_rel_bucket_np(dist_c)]
    band = jnp.where((dist_c >= 0)[:, :, None], band, NEG)
    far = rb[REL_BUCKETS - 1]

    def rows(t):
        c = t.shape[1]
        return jnp.transpose(t, (2, 0, 1)).reshape(NSA_KV_HEADS, ROWS, c)

    near_t = rows(near)
    band_t = rows(band)
    far_rows = jnp.repeat(far, QB).reshape(NSA_KV_HEADS, ROWS, 1)
    far_t = jnp.broadcast_to(far_rows, (NSA_KV_HEADS, ROWS, MAX_BLOCKS))
    band_hi = band_t.astype(BF16)
    band_lo = (band_t - band_hi.astype(F32)).astype(BF16)
    far_hi = far_rows.astype(BF16)
    far_lo = (far_rows - far_hi.astype(F32)).astype(BF16)
    neg_col = jnp.full((NSA_KV_HEADS, ROWS, 1), NEG, BF16)
    zeros = jnp.zeros((NSA_KV_HEADS, ROWS, D_HEAD - 2 * BAND - 3), BF16)
    cmp_lhs = jnp.concatenate([band_hi, band_lo, far_hi, far_lo, neg_col, zeros], axis=2)
    return near_t, far_t, cmp_lhs


def _nsa_kernel(q_ref, kc_ref, vc_ref, ks_ref, vs_ref, kw_ref, vw_ref, g_ref,
                near_ref, far_ref, cmpl_ref, ovt_ref, eye_ref,
                o_ref, ksa, vsa, kwa, vwa, kca, vca, sc_scr):
    qb = pl.program_id(2)
    T = ks_ref.shape[2]
    ncp = kc_ref.shape[2]

    @pl.when(qb == 0)
    def _init():
        n_chunks = (KPAD + T) // KT
        r_io = lax.broadcasted_iota(jnp.int32, (KT, KAUG), 0)
        l_io = lax.broadcasted_iota(jnp.int32, (KT, KAUG), 1)
        lv = lax.broadcasted_iota(jnp.int32, (KT, VAUG), 1)
        vpat = jnp.where(lv == D_HEAD, 1.0, 0.0).astype(BF16)
        for c in range(n_chunks):
            rows = slice(c * KT, (c + 1) * KT)
            key = r_io + (c * KT - KPAD)
            blk = key // SEL_BLOCK
            real = key >= 0
            hot = real & (((l_io >= D_HEAD) & (l_io < 2 * D_HEAD) & (l_io - D_HEAD == blk))
                          | ((l_io >= 2 * D_HEAD) & (l_io < 3 * D_HEAD) & (l_io - 2 * D_HEAD == blk)))
            hot = hot | ((~real) & (l_io == 3 * D_HEAD))
            pat = jnp.where(hot, 1.0, 0.0).astype(BF16)
            ksa[rows, :] = pat
            kwa[rows, :] = pat
            vsa[rows, :] = vpat
            vwa[rows, :] = vpat
        ksa[KPAD:, 0:D_HEAD] = ks_ref[0, 0]
        kwa[KPAD:, 0:D_HEAD] = kw_ref[0, 0]
        vsa[KPAD:, 0:D_HEAD] = vs_ref[0, 0]
        vwa[KPAD:, 0:D_HEAD] = vw_ref[0, 0]
        lc = lax.broadcasted_iota(jnp.int32, (ncp, VAUG), 1)
        kca[...] = jnp.zeros((ncp, VAUG), BF16)
        kca[:, 0:D_HEAD] = kc_ref[0, 0]
        vca[...] = jnp.where(lc == D_HEAD, 1.0, 0.0).astype(BF16)
        vca[:, 0:D_HEAD] = vc_ref[0, 0]

    q4 = q_ref[0].reshape(ROWS, D_HEAD)

    lane_c = lax.broadcasted_iota(jnp.int32, (ncp, LANES), 1)
    n_c = lax.broadcasted_iota(jnp.int32, (ncp, LANES), 0)
    e = lane_c - D_HEAD
    base = 4 * qb + 3
    ind = (((e >= 0) & (e < BAND) & (n_c == base - e))
           | ((e >= BAND) & (e < 2 * BAND) & (n_c == base - (e - BAND)))
           | (((e == 2 * BAND) | (e == 2 * BAND + 1)) & (n_c < base - (BAND - 1)))
           | ((e == 2 * BAND + 2) & (n_c > base)))
    rhs_c = jnp.where(lane_c < D_HEAD, kca[...], jnp.where(ind, 1.0, 0.0).astype(BF16))
    lhs_c = jnp.concatenate([q4, cmpl_ref[0]], axis=1)
    s_c = _dot_nt(lhs_c, rhs_c)
    m_c = jnp.max(s_c, axis=1, keepdims=True)
    p_c = jnp.where(s_c > 0.5 * NEG, jnp.exp2(s_c - m_c), 0.0)
    l_c = jnp.sum(p_c, axis=1, keepdims=True)
    inv_c = jnp.where(l_c > 0.0, 1.0 / l_c, 0.0)
    o_c = _dot(p_c.astype(BF16), vca[...])[:, :D_HEAD] * inv_c
    pn = p_c * inv_c
    psum = pn[0:QB] + pn[QB:2 * QB] + pn[2 * QB:3 * QB] + pn[3 * QB:4 * QB]
    ps_hi, ps_lo = _split_bf16(psum)
    ovt = ovt_ref[...]
    imp_t = _dot_nt(ovt, ps_hi) + _dot_nt(ovt, ps_lo)

    j_t = lax.broadcasted_iota(jnp.int32, (MAX_BLOCKS, QB), 0)
    forced = (j_t == 0) | (j_t == qb) | (j_t == qb - 1)
    score_t = jnp.where(forced, FORCED_SCORE, jnp.where(j_t <= qb, imp_t, NEG))
    sc_scr[...] = score_t
    cnt = jnp.zeros((MAX_BLOCKS, QB), jnp.int32)
    for i in range(MAX_BLOCKS):
        row_i = sc_scr[i:i + 1, :]
        beats = (row_i > score_t) | ((row_i == score_t) & (j_t > i))
        cnt = cnt + beats.astype(jnp.int32)
    sel_t = jnp.where(cnt < SEL_TOPK, 1.0, 0.0).astype(BF16)
    sel = _dot_nt(eye_ref[...], sel_t)
    sel4 = jnp.concatenate([sel] * NSA_GROUP, axis=0) > 0.5

    j_q = lax.broadcasted_iota(jnp.int32, (ROWS, MAX_BLOCKS), 1)
    far_c = far_ref[0]
    pad_col = jnp.where(j_q == 0, NEG, 0.0).astype(BF16)

    def make_lhs(val):
        hi, lo = _split_bf16(val)
        return jnp.concatenate([q4, hi, lo, pad_col], axis=1)

    lhs_far = make_lhs(jnp.where(sel4 & (j_q <= qb - BLOCKS_PER_TILE), far_c, NEG))
    lhs_near = make_lhs(jnp.where(sel4, 0.0, NEG))
    in_far_win = (j_q >= qb - WIN_BLOCKS) & (j_q <= qb - BLOCKS_PER_TILE)
    in_near_win = (j_q > qb - BLOCKS_PER_TILE) & (j_q <= qb)
    lhs_win = make_lhs(jnp.where(in_far_win, far_c, jnp.where(in_near_win, 0.0, NEG)))

    near_bias = near_ref[0]

    def first_tile(lhs, k_tile, v_tile, bias):
        s = _dot_nt(lhs, k_tile) + bias
        m = jnp.max(s, axis=1, keepdims=True)
        p = jnp.exp2(s - m)
        return m, _dot(p.astype(BF16), v_tile)

    def next_tile(lhs, k_tile, v_tile, bias, m_prev, acc):
        s = _dot_nt(lhs, k_tile)
        if bias is not None:
            s = s + bias
        m = jnp.maximum(m_prev, jnp.max(s, axis=1, keepdims=True))
        alpha = jnp.exp2(m_prev - m)
        p = jnp.exp2(s - m)
        return m, acc * alpha + _dot(p.astype(BF16), v_tile)

    def finish(acc):
        return acc[:, :D_HEAD] / acc[:, D_HEAD:D_HEAD + 1]

    near_start = pl.multiple_of(KPAD + (qb - (BLOCKS_PER_TILE - 1)) * SEL_BLOCK, SEL_BLOCK)

    m_s, acc_s = first_tile(lhs_near, ksa[pl.ds(near_start, KT), :], vsa[pl.ds(near_start, KT), :], near_bias)

    def far_body(t, carry):
        m_prev, acc = carry
        start = pl.multiple_of(KPAD + t * KT, KT)
        return next_tile(lhs_far, ksa[pl.ds(start, KT), :], vsa[pl.ds(start, KT), :], None, m_prev, acc)

    m_s, acc_s = lax.fori_loop(0, qb // BLOCKS_PER_TILE, far_body, (m_s, acc_s))
    o_s = finish(acc_s)

    m_w, acc_w = first_tile(lhs_win, kwa[pl.ds(near_start, KT), :], vwa[pl.ds(near_start, KT), :], near_bias)
    mid_start = pl.multiple_of(near_start - KT, SEL_BLOCK)
    m_w, acc_w = next_tile(lhs_win, kwa[pl.ds(mid_start, KT), :], vwa[pl.ds(mid_start, KT), :], None, m_w, acc_w)
    old_start = pl.multiple_of(near_start - 2 * KT, SEL_BLOCK)
    r_w = lax.broadcasted_iota(jnp.int32, (ROWS, KT), 0) % QB
    c_w = lax.broadcasted_iota(jnp.int32, (ROWS, KT), 1) - (KT - SEL_BLOCK)
    edge_bias = jnp.where((c_w >= 0) & (c_w <= r_w), NEG, 0.0)
    m_w, acc_w = next_tile(lhs_win, kwa[pl.ds(old_start, KT), :], vwa[pl.ds(old_start, KT), :], edge_bias, m_w, acc_w)
    o_w = finish(acc_w)

    gt = g_ref[0, 0, 0]
    g_hi = gt.astype(BF16)
    g_r1 = gt - g_hi.astype(F32)
    g_mid = g_r1.astype(BF16)
    g_lo = (g_r1 - g_mid.astype(F32)).astype(BF16)
    row_i = lax.broadcasted_iota(jnp.int32, (ROWS, ROWS), 0)
    col_i = lax.broadcasted_iota(jnp.int32, (ROWS, ROWS), 1)
    eye_rows = jnp.where(row_i == col_i, 1.0, 0.0).astype(BF16)
    g_cols = _dot_nt(eye_rows, g_hi) + _dot_nt(eye_rows, g_mid) + _dot_nt(eye_rows, g_lo)

    o = g_cols[:, 0:1] * o_c + g_cols[:, 1:2] * o_s + g_cols[:, 2:3] * o_w
    for hh in range(NSA_GROUP):
        o_ref[0, :, hh * D_HEAD:(hh + 1) * D_HEAD] = o[hh * QB:(hh + 1) * QB, :].astype(BF16)


def _nsa_call(q, kcmp, vcmp, ks, vs, kw, vw, gates_t, near_t, far_t, cmp_lhs, ovt, eye):
    B, _, T, _ = q.shape
    G = NSA_KV_HEADS
    nq = T // QB
    ncp = kcmp.shape[2]
    per_bg = lambda a: pl.BlockSpec((1, 1) + a.shape[2:], lambda b, g, i: (b, g, 0, 0))
    per_g = lambda a: pl.BlockSpec((1,) + a.shape[1:], lambda b, g, i: (g, 0, 0))
    const2 = lambda a: pl.BlockSpec(a.shape, lambda b, g, i: (0, 0))
    return pl.pallas_call(
        _nsa_kernel,
        grid=(B, G, nq),
        in_specs=[
            pl.BlockSpec((1, NSA_GROUP, QB, D_HEAD), lambda b, g, i: (b, g, i, 0)),
            per_bg(kcmp), per_bg(vcmp), per_bg(ks), per_bg(vs), per_bg(kw), per_bg(vw),
            pl.BlockSpec((1, 1, 1, 8, ROWS), lambda b, g, i: (b, g, i, 0, 0)),
            per_g(near_t), per_g(far_t), per_g(cmp_lhs), const2(ovt), const2(eye),
        ],
        out_specs=pl.BlockSpec((1, QB, NSA_GROUP * D_HEAD), lambda b, g, i: (b, i, g)),
        out_shape=jax.ShapeDtypeStruct((B, T, D_NSA), BF16),
        scratch_shapes=[
            pltpu.VMEM((KPAD + T, KAUG), BF16), pltpu.VMEM((KPAD + T, VAUG), BF16),
            pltpu.VMEM((KPAD + T, KAUG), BF16), pltpu.VMEM((KPAD + T, VAUG), BF16),
            pltpu.VMEM((ncp, VAUG), BF16), pltpu.VMEM((ncp, VAUG), BF16),
            pltpu.VMEM((MAX_BLOCKS, QB), F32),
        ],
        compiler_params=pltpu.CompilerParams(
            dimension_semantics=("arbitrary", "arbitrary", "arbitrary"), vmem_limit_bytes=VMEM_LIMIT),
        name="nsa",
    )(q, kcmp, vcmp, ks, vs, kw, vw, gates_t, near_t, far_t, cmp_lhs, ovt, eye)


def _causal_conv3(t, prev_ref, w, b, row):
    p1 = prev_ref[7:8, :]
    p2 = prev_ref[6:7, :]
    s1 = jnp.where(row == 0, p1, pltpu.roll(t, 1, 0))
    s2 = jnp.where(row == 0, p2, jnp.where(row == 1, p1, pltpu.roll(t, 2, 0)))
    y = t * w[2:3] + s1 * w[1:2] + s2 * w[0:1] + b
    prev_ref[...] = t[t.shape[0] - 8:, :]
    return y


def _ffn_kernel(x_ref, on_ref, os_ref, wo_n_ref, wo_s_ref, fn_ref, wup_ref, cw_ref, cb_ref, wdn_ref,
                out_ref, prev_ref, acc_ref, h_ref):
    tm = x_ref.shape[1]

    @pl.when(pl.program_id(1) == 0)
    def _start_of_sequence():
        prev_ref[...] = jnp.zeros(prev_ref.shape, F32)

    x1 = x_ref[0] + _dot(on_ref[0], wo_n_ref[...]) + _dot(os_ref[0], wo_s_ref[...])
    ms = jnp.mean(x1 * x1, axis=-1, keepdims=True)
    h_ref[...] = (x1 * lax.rsqrt(ms + EPS) * fn_ref[...]).astype(BF16)
    acc_ref[...] = x1
    row = lax.broadcasted_iota(jnp.int32, (tm, FFN_CHUNK), 0)

    def chunk(c, carry):
        h = h_ref[...]
        a = _causal_conv3(_dot(h, wup_ref[0, c]), prev_ref.at[0, c], cw_ref[0, c], cb_ref[0, c], row)
        g = _causal_conv3(_dot(h, wup_ref[1, c]), prev_ref.at[1, c], cw_ref[1, c], cb_ref[1, c], row)
        act = (g * _sigmoid(g) * a).astype(BF16)
        acc_ref[...] += _dot(act, wdn_ref[c])
        return carry

    lax.fori_loop(0, wdn_ref.shape[0], chunk, 0)
    out_ref[0] = acc_ref[...]


def _ffn_call(x, o_nsa, o_sgu, wo_n, wo_s, ffn_norm, w_up, conv_w, conv_b, w_down, tm):
    B, T, _ = x.shape
    n_chunks = w_down.shape[0]
    resident = lambda a: pl.BlockSpec(a.shape, lambda b, i: (0,) * a.ndim, pipeline_mode=pl.Buffered(1))
    tile = lambda w: pl.BlockSpec((1, tm, w), lambda b, i: (b, i, 0))
    return pl.pallas_call(
        _ffn_kernel,
        grid=(B, T // tm),
        in_specs=[tile(D_MODEL), tile(D_NSA), tile(D_GMLP), resident(wo_n), resident(wo_s),
                  resident(ffn_norm), resident(w_up), resident(conv_w), resident(conv_b), resident(w_down)],
        out_specs=tile(D_MODEL),
        out_shape=jax.ShapeDtypeStruct((B, T, D_MODEL), F32),
        scratch_shapes=[pltpu.VMEM((2, n_chunks, 8, FFN_CHUNK), F32), pltpu.VMEM((tm, D_MODEL), F32),
                        pltpu.VMEM((tm, D_MODEL), BF16)],
        compiler_params=pltpu.CompilerParams(
            dimension_semantics=("arbitrary", "arbitrary"), vmem_limit_bytes=VMEM_LIMIT),
        name="ffn",
    )(x, o_nsa, o_sgu, wo_n, wo_s, ffn_norm, w_up, conv_w, conv_b, w_down)


def _mixers(x, rel_bias, attn_norm, w_in, q_norm, k_norm_cmp, k_norm_slc, k_norm_win,
            cmp_pe_k, cmp_w1_k, cmp_b1_k, cmp_w2_k, cmp_pe_v, cmp_w1_v, cmp_b1_v, cmp_w2_v,
            sgu_norm, sgu_w, sgu_b):
    B, T, _ = x.shape
    assert T % 512 == 0 and T // SEL_BLOCK <= MAX_BLOCKS and (T // CMP_STRIDE) % LANES == 0
    nq = T // QB
    ncp = T // CMP_STRIDE
    G, R = NSA_KV_HEADS, NSA_GROUP

    o_kv = D_NSA
    o_g = D_NSA + 6 * D_KV
    o_uv = o_g + N_GATES
    w_q = w_in[:, :o_kv].astype(BF16)
    w_kv = w_in[:, o_kv:o_g].astype(BF16)
    w_g = jnp.pad(w_in[:, o_g:o_uv], ((0, 0), (0, LANES - N_GATES))).astype(BF16)
    w_uv = w_in[:, o_uv:].astype(BF16)
    qn = (jnp.tile(q_norm, NSA_HEADS) * (D_HEAD ** -0.5 * LOG2E)).reshape(1, D_NSA)
    ksn = jnp.tile(k_norm_slc, G).reshape(1, D_KV)
    kwn = jnp.tile(k_norm_win, G).reshape(1, D_KV)
    sgu_bt = jnp.repeat(sgu_b.T, GMLP_GROUP_DIM, axis=1)
    grp = np.arange(D_NSA) // D_HEAD
    bd = jnp.asarray(grp[:, None] == grp[None, :], BF16)

    q, kc, vc, ks, vs, kw, vw, gates, o_sgu = _proj_call(
        x, attn_norm.reshape(1, D_MODEL), w_q, w_kv, w_g, w_uv, qn, ksn, kwn,
        sgu_norm.reshape(1, D_GMLP), sgu_w, sgu_bt, bd, tm=512)

    half = CMP_STRIDE * D_HEAD
    kcmp, vcmp = _compress_call(
        kc.reshape(B, G, ncp, half), vc.reshape(B, G, ncp, half),
        cmp_pe_k.reshape(2, half), cmp_w1_k.astype(BF16), cmp_b1_k.reshape(1, CMP_HIDDEN),
        cmp_w2_k.astype(BF16), k_norm_cmp.reshape(1, D_HEAD),
        cmp_pe_v.reshape(2, half), cmp_w1_v.astype(BF16), cmp_b1_v.reshape(1, CMP_HIDDEN),
        cmp_w2_v.astype(BF16))

    gt = gates[:, :, :N_GATES].reshape(B, nq, QB, G, R, N_BRANCH)
    gt = jnp.transpose(gt, (0, 3, 1, 5, 4, 2)).reshape(B, G, nq, N_BRANCH, ROWS)
    gt = jnp.pad(gt, ((0, 0), (0, 0), (0, 0), (0, 8 - N_BRANCH), (0, 0)))

    near_t, far_t, cmp_lhs = _bias_tables(rel_bias)
    jj = np.arange(MAX_BLOCKS)[:, None]
    nn = np.arange(ncp)[None, :]
    ovt = jnp.asarray((nn >= 4 * jj - 1) & (nn <= 4 * jj + 3), BF16)
    eye = jnp.asarray(np.eye(MAX_BLOCKS), BF16)

    o_nsa = _nsa_call(q, kcmp, vcmp, ks, vs, kw, vw, gt, near_t, far_t, cmp_lhs, ovt, eye)
    return o_nsa, o_sgu


def _layer(x, rel_bias, attn_norm, w_in, q_norm, k_norm_cmp, k_norm_slc, k_norm_win,
           cmp_pe_k, cmp_w1_k, cmp_b1_k, cmp_w2_k, cmp_pe_v, cmp_w1_v, cmp_b1_v, cmp_w2_v,
           sgu_norm, sgu_w, sgu_b, w_out, ffn_norm, w_up, conv_w, conv_b, w_down):
    o_nsa, o_sgu = _mixers(x, rel_bias, attn_norm, w_in, q_norm, k_norm_cmp, k_norm_slc, k_norm_win,
                           cmp_pe_k, cmp_w1_k, cmp_b1_k, cmp_w2_k, cmp_pe_v, cmp_w1_v, cmp_b1_v, cmp_w2_v,
                           sgu_norm, sgu_w, sgu_b)
    wo = w_out.astype(BF16)
    n_chunks = D_FF // FFN_CHUNK
    wu = jnp.transpose(w_up.astype(BF16).reshape(D_MODEL, 2, n_chunks, FFN_CHUNK), (1, 2, 0, 3))
    cw = jnp.transpose(conv_w.reshape(conv_w.shape[0], 2, n_chunks, FFN_CHUNK), (1, 2, 0, 3))
    cb = conv_b.reshape(2, n_chunks, 1, FFN_CHUNK)
    wd = w_down.astype(BF16).reshape(n_chunks, FFN_CHUNK, D_MODEL)
    return _ffn_call(x, o_nsa, o_sgu, wo[:D_NSA], wo[D_NSA:], ffn_norm.reshape(1, D_MODEL), wu, cw, cb, wd, tm=512)


def kernel(x, rel_bias, attn_norm, w_in, q_norm, k_norm_cmp, k_norm_slc, k_norm_win,
           cmp_pe_k, cmp_w1_k, cmp_b1_k, cmp_w2_k, cmp_pe_v, cmp_w1_v, cmp_b1_v, cmp_w2_v,
           sgu_norm, sgu_w, sgu_b, w_out, ffn_norm, w_up, conv_w, conv_b, w_down):
    depth = attn_norm.shape[0]
    for l in range(depth):
        x = _layer(x, rel_bias, attn_norm[l], w_in[l], q_norm[l], k_norm_cmp[l], k_norm_slc[l], k_norm_win[l],
                   cmp_pe_k[l], cmp_w1_k[l], cmp_b1_k[l], cmp_w2_k[l], cmp_pe_v[l], cmp_w1_v[l], cmp_b1_v[l],
                   cmp_w2_v[l], sgu_norm[l], sgu_w[l], sgu_b[l], w_out[l], ffn_norm[l], w_up[l], conv_w[l],
                   conv_b[l], w_down[l])
    return x
```

```python
import functools
import math

import numpy as np
import jax
import jax.numpy as jnp
from jax import lax
from jax.experimental import pallas as pl
from jax.experimental.pallas import tpu as pltpu

F32 = jnp.float32
BF16 = jnp.bfloat16

D_MODEL = 1024
D_HEAD = 64
NSA_HEADS = 8
NSA_KV_HEADS = 2
NSA_GROUP = NSA_HEADS // NSA_KV_HEADS
D_NSA = NSA_HEADS * D_HEAD
D_KV = NSA_KV_HEADS * D_HEAD
N_BRANCH = 3
N_GATES = NSA_HEADS * N_BRANCH
CMP_BLOCK = 32
CMP_STRIDE = 16
CMP_HIDDEN = 256
SEL_BLOCK = 64
SEL_TOPK = 16
WINDOW = 512
GMLP_GROUPS = 8
GMLP_GROUP_DIM = 64
D_GMLP = GMLP_GROUPS * GMLP_GROUP_DIM
CHUNK = 128
D_MIX = D_NSA + D_GMLP
REL_BUCKETS = 32
REL_MAX_DIST = 128
D_FF = 2816
EPS = 1e-6
NEG = -1e30
FORCED_SCORE = 1e4
LOG2E = 1.4426950408889634

LANES = 128
TQ = 256
SUB = TQ // SEL_BLOCK
ROWS = NSA_GROUP * TQ
KT = 256
BLOCKS_PER_TILE = KT // SEL_BLOCK
WIN_BLOCKS = WINDOW // SEL_BLOCK
KPAD = 512
N_TOEPLITZ = 3
MAX_BLOCKS = 64
KAUG = 256
VAUG = 128
CBAND = 28
FFN_CHUNK = 256
VMEM_LIMIT = 56 * 1024 * 1024

_NT = (((1,), (1,)), ((), ()))


def _dot(a, b):
    return jnp.dot(a, b, preferred_element_type=F32)


def _dot_nt(a, b):
    return lax.dot_general(a, b, _NT, preferred_element_type=F32)


def _split_bf16(x):
    hi = x.astype(BF16)
    lo = (x - hi.astype(F32)).astype(BF16)
    return hi, lo


def _gelu_tanh(x):
    return 0.5 * x * (1.0 + jnp.tanh(0.7978845608028654 * (x + 0.044715 * (x * x * x))))


def _sigmoid(x):
    return 0.5 * (1.0 + jnp.tanh(0.5 * x))


def _group_mean_sq(t, ones_blockdiag, width):
    t2 = t * t
    hi, lo = _split_bf16(t2)
    return (_dot(hi, ones_blockdiag) + _dot(lo, ones_blockdiag)) * (1.0 / width)


def _proj_kernel(x_ref, an_ref, wq_ref, wkv_ref, wg_ref, wuv_ref, qn_ref, ksn_ref, kwn_ref,
                 sgun_ref, sguw_ref, sgub_ref, bd_ref,
                 q_out, kc_out, vc_out, ks_out, vs_out, kw_out, vw_out, g_out, sgu_out):
    tm = x_ref.shape[1]
    x = x_ref[0]
    ms = jnp.mean(x * x, axis=-1, keepdims=True)
    h = (x * lax.rsqrt(ms + EPS) * an_ref[...]).astype(BF16)

    bd = bd_ref[...]
    q = _dot(h, wq_ref[...])
    qn = q * lax.rsqrt(_group_mean_sq(q, bd, D_HEAD) + EPS) * qn_ref[...]
    for hh in range(NSA_HEADS):
        q_out[0, hh] = qn[:, hh * D_HEAD:(hh + 1) * D_HEAD].astype(BF16)

    kv = _dot(h, wkv_ref[...])
    bd_kv = bd[:D_KV, :D_KV]
    kc = kv[:, 0 * D_KV:1 * D_KV]
    vc = kv[:, 1 * D_KV:2 * D_KV]
    ks = kv[:, 2 * D_KV:3 * D_KV]
    vs = kv[:, 3 * D_KV:4 * D_KV]
    kw = kv[:, 4 * D_KV:5 * D_KV]
    vw = kv[:, 5 * D_KV:6 * D_KV]
    ks = ks * lax.rsqrt(_group_mean_sq(ks, bd_kv, D_HEAD) + EPS) * ksn_ref[...]
    kw = kw * lax.rsqrt(_group_mean_sq(kw, bd_kv, D_HEAD) + EPS) * kwn_ref[...]
    for t, o_ref in ((kc, kc_out), (vc, vc_out), (ks, ks_out), (vs, vs_out), (kw, kw_out), (vw, vw_out)):
        for g in range(NSA_KV_HEADS):
            o_ref[0, g] = t[:, g * D_HEAD:(g + 1) * D_HEAD].astype(BF16)

    gates = _sigmoid(_dot(h, wg_ref[...]))
    for g in range(NSA_KV_HEADS):
        g_out[0, g] = gates[:, g * LANES:(g + 1) * LANES]

    uv = _gelu_tanh(_dot(h, wuv_ref[...]))
    u = uv[:, :D_GMLP]
    v = uv[:, D_GMLP:]
    vms = jnp.mean(v * v, axis=-1, keepdims=True)
    vb = (v * lax.rsqrt(vms + EPS) * sgun_ref[...]).astype(BF16)

    row = lax.broadcasted_iota(jnp.int32, (CHUNK, CHUNK), 0)
    col = lax.broadcasted_iota(jnp.int32, (CHUNK, CHUNK), 1)
    tril = col <= row
    w_tril = [jnp.where(tril, sguw_ref[g], 0.0).astype(BF16) for g in range(GMLP_GROUPS)]
    first_half = lax.broadcasted_iota(jnp.int32, (CHUNK, LANES), 1) < GMLP_GROUP_DIM
    for c in range(tm // CHUNK):
        rows = slice(c * CHUNK, (c + 1) * CHUNK)
        zs = []
        for p in range(D_GMLP // LANES):
            blk = vb[rows, p * LANES:(p + 1) * LANES]
            z0 = _dot(w_tril[2 * p], blk)
            z1 = _dot(w_tril[2 * p + 1], blk)
            zs.append(jnp.where(first_half, z0, z1))
        z = jnp.concatenate(zs, axis=1) + sgub_ref[...]
        sgu_out[0, rows, :] = (u[rows, :] * z).astype(BF16)


def _proj_call(x, attn_norm, w_q, w_kv, w_g, w_uv, qn, ksn, kwn, sgu_norm, sgu_w, sgu_bt, bd, tm):
    B, T, _ = x.shape
    const2 = lambda b, i: (0, 0)
    const3 = lambda b, i: (0, 0, 0)
    head_spec = lambda nh: pl.BlockSpec((1, nh, tm, D_HEAD), lambda b, i: (b, 0, i, 0))
    kv_shape = jax.ShapeDtypeStruct((B, NSA_KV_HEADS, T, D_HEAD), BF16)
    return pl.pallas_call(
        _proj_kernel,
        grid=(B, T // tm),
        in_specs=[
            pl.BlockSpec((1, tm, D_MODEL), lambda b, i: (b, i, 0)),
            pl.BlockSpec((1, D_MODEL), const2),
            pl.BlockSpec(w_q.shape, const2),
            pl.BlockSpec(w_kv.shape, const2),
            pl.BlockSpec(w_g.shape, const2),
            pl.BlockSpec(w_uv.shape, const2),
            pl.BlockSpec((1, D_NSA), const2),
            pl.BlockSpec((1, D_KV), const2),
            pl.BlockSpec((1, D_KV), const2),
            pl.BlockSpec((1, D_GMLP), const2),
            pl.BlockSpec(sgu_w.shape, const3),
            pl.BlockSpec(sgu_bt.shape, const2),
            pl.BlockSpec(bd.shape, const2),
        ],
        out_specs=[
            head_spec(NSA_HEADS),
            head_spec(NSA_KV_HEADS), head_spec(NSA_KV_HEADS), head_spec(NSA_KV_HEADS),
            head_spec(NSA_KV_HEADS), head_spec(NSA_KV_HEADS), head_spec(NSA_KV_HEADS),
            pl.BlockSpec((1, NSA_KV_HEADS, tm, LANES), lambda b, i: (b, 0, i, 0)),
            pl.BlockSpec((1, tm, D_GMLP), lambda b, i: (b, i, 0)),
        ],
        out_shape=[
            jax.ShapeDtypeStruct((B, NSA_HEADS, T, D_HEAD), BF16),
            kv_shape, kv_shape, kv_shape, kv_shape, kv_shape, kv_shape,
            jax.ShapeDtypeStruct((B, NSA_KV_HEADS, T, LANES), F32),
            jax.ShapeDtypeStruct((B, T, D_GMLP), BF16),
        ],
        compiler_params=pltpu.CompilerParams(
            dimension_semantics=("arbitrary", "arbitrary"), vmem_limit_bytes=VMEM_LIMIT),
        name="proj",
    )(x, attn_norm, w_q, w_kv, w_g, w_uv, qn, ksn, kwn, sgu_norm, sgu_w, sgu_bt, bd)


def _compress_one(tok_ref, pe_ref, w1_ref, b1_ref, w2_ref):
    half = CMP_STRIDE * D_HEAD
    tok = tok_ref[0, 0].astype(F32)
    top = (tok + pe_ref[0:1, :]).astype(BF16)
    bot = (tok + pe_ref[1:2, :]).astype(BF16)
    a = _dot(top, w1_ref[:half, :])
    b = _dot(bot, w1_ref[half:, :])
    ncp = a.shape[0]
    pre = a + pltpu.roll(b, ncp - 1, 0) + b1_ref[...]
    hid = _gelu_tanh(pre).astype(BF16)
    return _dot(hid, w2_ref[...])


def _compress_kernel(kc_ref, vc_ref, pek_ref, w1k_ref, b1k_ref, w2k_ref, kn_ref,
                     pev_ref, w1v_ref, b1v_ref, w2v_ref, k_out, v_out):
    k = _compress_one(kc_ref, pek_ref, w1k_ref, b1k_ref, w2k_ref)
    kms = jnp.mean(k * k, axis=-1, keepdims=True)
    k_out[0, 0] = (k * lax.rsqrt(kms + EPS) * kn_ref[...]).astype(BF16)
    v_out[0, 0] = _compress_one(vc_ref, pev_ref, w1v_ref, b1v_ref, w2v_ref).astype(BF16)


def _compress_call(kc, vc, pek, w1k, b1k, w2k, kn, pev, w1v, b1v, w2v):
    B, G, ncp, width = kc.shape
    const2 = lambda b, g: (0, 0)
    tok_spec = pl.BlockSpec((1, 1, ncp, width), lambda b, g: (b, g, 0, 0))
    out_spec = pl.BlockSpec((1, 1, ncp, D_HEAD), lambda b, g: (b, g, 0, 0))
    full = lambda a: pl.BlockSpec(a.shape, const2)
    out_shape = jax.ShapeDtypeStruct((B, G, ncp, D_HEAD), BF16)
    return pl.pallas_call(
        _compress_kernel,
        grid=(B, G),
        in_specs=[tok_spec, tok_spec, full(pek), full(w1k), full(b1k), full(w2k), full(kn),
                  full(pev), full(w1v), full(b1v), full(w2v)],
        out_specs=[out_spec, out_spec],
        out_shape=[out_shape, out_shape],
        compiler_params=pltpu.CompilerParams(
            dimension_semantics=("arbitrary", "arbitrary"), vmem_limit_bytes=VMEM_LIMIT),
        name="compress",
    )(kc, vc, pek, w1k, b1k, w2k, kn, pev, w1v, b1v, w2v)


def _rel_bucket_np(dist):
    max_exact = REL_BUCKETS // 2
    d = np.maximum(dist, 1).astype(np.float32)
    log_b = max_exact + (np.log(d / np.float32(max_exact)) / np.float32(math.log(REL_MAX_DIST / max_exact))
                         * np.float32(REL_BUCKETS - max_exact)).astype(np.int32)
    log_b = np.clip(log_b, max_exact, REL_BUCKETS - 1)
    return np.where(dist < max_exact, np.maximum(dist, 0), log_b)


def _bias_tables(rel_bias):
    rb = rel_bias.astype(F32) * LOG2E
    qq = np.arange(SEL_BLOCK)[:, None]
    kp = np.arange(SEL_BLOCK)[None, :]
    tile_d = [m * SEL_BLOCK + qq - kp for m in range(N_TOEPLITZ)]
    s_ = np.arange(SUB)[:, None, None]
    e_ = np.arange(CBAND)[None, None, :]
    dist_c = qq[None, :, :] + CMP_STRIDE * (e_ - 4 * (SUB - 1) + 4 * s_) - (3 * CMP_STRIDE + CMP_BLOCK - 1)
    all_d = np.concatenate([d.reshape(-1) for d in tile_d] + [dist_c.reshape(-1)])
    onehot = np.eye(REL_BUCKETS, dtype=np.float32)[_rel_bucket_np(all_d)]
    vals = jnp.dot(jnp.asarray(onehot), rb, precision=lax.Precision.HIGHEST)
    vals = jnp.where(jnp.asarray(all_d >= 0)[:, None], vals, NEG)
    n_t = N_TOEPLITZ * SEL_BLOCK * SEL_BLOCK
    toep = jnp.transpose(vals[:n_t].reshape(N_TOEPLITZ, SEL_BLOCK, SEL_BLOCK, NSA_HEADS), (0, 3, 1, 2))
    band = jnp.transpose(vals[n_t:].reshape(SUB, SEL_BLOCK, CBAND, NSA_HEADS), (3, 0, 1, 2))
    far = rb[REL_BUCKETS - 1]
    far_tile = jnp.broadcast_to(far[:, None, None], (NSA_HEADS, SEL_BLOCK, SEL_BLOCK))
    neg_tile = jnp.full((NSA_HEADS, SEL_BLOCK, SEL_BLOCK), NEG, F32)
    edge_tile = jnp.where(jnp.asarray(kp > qq)[None], far_tile, NEG)

    def tile(m, windowed):
        if m < 0 or (windowed and m > WIN_BLOCKS):
            return neg_tile
        if windowed and m == WIN_BLOCKS:
            return edge_tile
        return toep[m] if m < N_TOEPLITZ else far_tile

    def table(first_m, n_blocks, windowed):
        rows = [jnp.concatenate([tile(first_m + s - c, windowed) for c in range(n_blocks)], axis=2)
                for s in range(SUB)]
        t = jnp.concatenate(rows, axis=1)
        return t.reshape(NSA_KV_HEADS, ROWS, n_blocks * SEL_BLOCK)

    far_rows = jnp.repeat(far, TQ).reshape(NSA_KV_HEADS, ROWS, 1)
    near_t = table(BLOCKS_PER_TILE, 2 * BLOCKS_PER_TILE, False) - far_rows
    win_t = table(WIN_BLOCKS, WIN_BLOCKS + BLOCKS_PER_TILE, True)
    far_t = jnp.broadcast_to(far_rows, (NSA_KV_HEADS, ROWS, MAX_BLOCKS))
    band_t = band.reshape(NSA_KV_HEADS, ROWS, CBAND)
    band_hi = band_t.astype(BF16)
    band_lo = (band_t - band_hi.astype(F32)).astype(BF16)
    far_hi = far_rows.astype(BF16)
    far_lo = (far_rows - far_hi.astype(F32)).astype(BF16)
    neg_col = jnp.full((NSA_KV_HEADS, ROWS, 1), NEG, BF16)
    zeros = jnp.zeros((NSA_KV_HEADS, ROWS, D_HEAD - 2 * CBAND - 3), BF16)
    cmp_lhs = jnp.concatenate([band_hi, band_lo, far_hi, far_lo, neg_col, zeros], axis=2)
    return near_t, win_t, far_t, cmp_lhs


def _softmax_pv(s, v):
    m = jnp.max(s, axis=1, keepdims=True)
    p = jnp.exp2(s - m)
    acc = _dot(p.astype(BF16), v)
    return m, acc


def _nsa_kernel(q_ref, kc_ref, vc_ref, ks_ref, vs_ref, kw_ref, vw_ref, g_ref,
                near_ref, win_ref, far_ref, cmpl_ref, ovt_ref, eye_ref,
                o_ref, ksa, vsa, kwa, vwa, kca, vca, sc_scr, lhs_scr, m_scr, acc_scr, sa_scr, sb_scr, out_scr):
    qt = pl.program_id(2)
    first = qt * SUB
    T = ks_ref.shape[2]
    ncp = kc_ref.shape[2]

    @pl.when(qt == 0)
    def _init():
        n_chunks = ksa.shape[0] // KT
        r_io = lax.broadcasted_iota(jnp.int32, (KT, KAUG), 0)
        l_io = lax.broadcasted_iota(jnp.int32, (KT, KAUG), 1)
        lv = lax.broadcasted_iota(jnp.int32, (KT, VAUG), 1)
        vpat = jnp.where(lv == D_HEAD, 1.0, 0.0).astype(BF16)

        def fill(c, carry):
            rows = pl.ds(pl.multiple_of(c * KT, KT), KT)
            key = r_io + (c * KT - KPAD)
            blk = key // SEL_BLOCK
            real = (key >= 0) & (key < T)
            hot = real & (((l_io >= D_HEAD) & (l_io < 2 * D_HEAD) & (l_io - D_HEAD == blk))
                          | ((l_io >= 2 * D_HEAD) & (l_io < 3 * D_HEAD) & (l_io - 2 * D_HEAD == blk)))
            hot = hot | ((~real) & (l_io == 3 * D_HEAD))
            pat = jnp.where(hot, 1.0, 0.0).astype(BF16)
            ksa[rows, :] = pat
            kwa[rows, :] = pat
            vsa[rows, :] = vpat
            vwa[rows, :] = vpat
            return carry

        lax.fori_loop(0, n_chunks, fill, 0)
        ksa[KPAD:KPAD + T, 0:D_HEAD] = ks_ref[0, 0]
        kwa[KPAD:KPAD + T, 0:D_HEAD] = kw_ref[0, 0]
        vsa[KPAD:KPAD + T, 0:D_HEAD] = vs_ref[0, 0]
        vwa[KPAD:KPAD + T, 0:D_HEAD] = vw_ref[0, 0]
        lc = lax.broadcasted_iota(jnp.int32, (ncp, VAUG), 1)
        kca[...] = jnp.zeros((ncp, VAUG), BF16)
        kca[:, 0:D_HEAD] = kc_ref[0, 0]
        vca[...] = jnp.where(lc == D_HEAD, 1.0, 0.0).astype(BF16)
        vca[:, 0:D_HEAD] = vc_ref[0, 0]

    q4 = q_ref[0].reshape(ROWS, D_HEAD)

    lane_c = lax.broadcasted_iota(jnp.int32, (ncp, LANES), 1)
    n_c = lax.broadcasted_iota(jnp.int32, (ncp, LANES), 0)
    e = lane_c - D_HEAD
    base = 4 * first + (4 * SUB - 1)
    ind = (((e >= 0) & (e < CBAND) & (n_c == base - e))
           | ((e >= CBAND) & (e < 2 * CBAND) & (n_c == base - (e - CBAND)))
           | (((e == 2 * CBAND) | (e == 2 * CBAND + 1)) & (n_c < base - (CBAND - 1)))
           | ((e == 2 * CBAND + 2) & (n_c > base)))
    rhs_c = jnp.where(lane_c < D_HEAD, kca[...], jnp.where(ind, 1.0, 0.0).astype(BF16))
    lhs_c = jnp.concatenate([q4, cmpl_ref[0]], axis=1)
    s_c = _dot_nt(lhs_c, rhs_c)
    m_c = jnp.max(s_c, axis=1, keepdims=True)
    p_c = jnp.where(s_c > 0.5 * NEG, jnp.exp2(s_c - m_c), 0.0)
    l_c = jnp.sum(p_c, axis=1, keepdims=True)
    inv_c = jnp.where(l_c > 0.0, 1.0 / l_c, 0.0)
    o_c = _dot(p_c.astype(BF16), vca[...])[:, :D_HEAD] * inv_c
    pn = p_c * inv_c
    psum = pn[0:TQ] + pn[TQ:2 * TQ] + pn[2 * TQ:3 * TQ] + pn[3 * TQ:4 * TQ]
    ps_hi, ps_lo = _split_bf16(psum)
    ovt = ovt_ref[...]
    imp_t = _dot_nt(ovt, ps_hi) + _dot_nt(ovt, ps_lo)

    gates = g_ref[0, 0]

    def gated(branch, o):
        return jnp.concatenate(
            [gates[:, hh * N_BRANCH + branch:hh * N_BRANCH + branch + 1] * o[hh * TQ:(hh + 1) * TQ]
             for hh in range(NSA_GROUP)], axis=0)

    j_q = lax.broadcasted_iota(jnp.int32, (ROWS, MAX_BLOCKS), 1)
    pad_col = jnp.where(j_q == 0, NEG, 0.0).astype(BF16)
    zero_col = jnp.zeros((ROWS, MAX_BLOCKS), BF16)
    lhs_win = jnp.concatenate([q4, zero_col, zero_col, pad_col], axis=1)
    win_start = pl.multiple_of(KPAD + (first - WIN_BLOCKS) * SEL_BLOCK, KT)
    s_w = _dot_nt(lhs_win, kwa[pl.ds(win_start, 3 * KT), :]) + win_ref[0]
    _, acc_w = _softmax_pv(s_w, vwa[pl.ds(win_start, 3 * KT), :])
    o_w = acc_w[:, :D_HEAD] / acc_w[:, D_HEAD:D_HEAD + 1]
    out_scr[...] = gated(0, o_c) + gated(2, o_w)

    j_t = lax.broadcasted_iota(jnp.int32, (MAX_BLOCKS, TQ), 0)
    cur = first + lax.broadcasted_iota(jnp.int32, (MAX_BLOCKS, TQ), 1) // SEL_BLOCK
    forced = (j_t == 0) | (j_t == cur) | (j_t == cur - 1)
    score_t = jnp.where(forced, FORCED_SCORE, jnp.where(j_t <= cur, imp_t, NEG))
    sc_scr[...] = score_t

    def count(i, cnt):
        row_i = sc_scr[pl.ds(i, 1), :]
        beats = (row_i > score_t) | ((row_i == score_t) & (j_t > i))
        return cnt + beats.astype(jnp.int32)

    cnt = lax.fori_loop(0, first + SUB, count, jnp.zeros((MAX_BLOCKS, TQ), jnp.int32))
    sel_t = jnp.where((cnt < SEL_TOPK) & (j_t <= cur), 1.0, 0.0).astype(BF16)
    sel = _dot_nt(eye_ref[...], sel_t)
    sel4 = jnp.concatenate([sel] * NSA_GROUP, axis=0) > 0.5

    hi, lo = _split_bf16(jnp.where(sel4, far_ref[0], NEG))
    lhs_scr[...] = jnp.concatenate([q4, hi, lo, pad_col], axis=1)
    m_scr[...] = jnp.full((ROWS, 1), NEG, F32)
    acc_scr[...] = jnp.zeros((ROWS, VAUG), F32)

    def scores(t, dst):
        start = pl.multiple_of(KPAD + t * KT, KT)
        dst[...] = _dot_nt(lhs_scr[...], ksa[pl.ds(start, KT), :])

    def consume(t, src):
        rel = t - (qt - 1)
        col = pl.multiple_of(jnp.clip(rel, 0, 1) * KT, KT)
        s = src[...] + jnp.where(rel >= 0, near_ref[0, :, pl.ds(col, KT)], 0.0)
        m_prev = m_scr[...]
        m = jnp.maximum(m_prev, jnp.max(s, axis=1, keepdims=True))
        alpha = jnp.exp2(m_prev - m)
        p = jnp.exp2(s - m)
        start = pl.multiple_of(KPAD + t * KT, KT)
        acc_scr[...] = acc_scr[...] * alpha + _dot(p.astype(BF16), vsa[pl.ds(start, KT), :])
        m_scr[...] = m

    scores(0, sa_scr)

    def pair(u, carry):
        scores(2 * u + 1, sb_scr)
        consume(2 * u, sa_scr)
        scores(2 * u + 2, sa_scr)
        consume(2 * u + 1, sb_scr)
        return carry

    lax.fori_loop(0, qt // 2 + 1, pair, 0)
    acc_s = acc_scr[...]
    o_s = acc_s[:, :D_HEAD] / acc_s[:, D_HEAD:D_HEAD + 1]

    o = out_scr[...] + gated(1, o_s)
    for hh in range(NSA_GROUP):
        o_ref[0, :, hh * D_HEAD:(hh + 1) * D_HEAD] = o[hh * TQ:(hh + 1) * TQ].astype(BF16)


def _nsa_call(q, kcmp, vcmp, ks, vs, kw, vw, gates, near_t, win_t, far_t, cmp_lhs, ovt, eye):
    B, _, T, _ = q.shape
    G = NSA_KV_HEADS
    ncp = kcmp.shape[2]
    rows_kv = KPAD + T + 2 * KT
    per_bg =lambda a: pl.BlockSpec((1, 1) + a.shape[2:], lambda b, g, i: (b, g, 0, 0))
    per_g = lambda a: pl.BlockSpec((1,) + a.shape[1:], lambda b, g, i: (g, 0, 0))
    const2 = lambda a: pl.BlockSpec(a.shape, lambda b, g, i: (0, 0))
    return pl.pallas_call(
        _nsa_kernel,
        grid=(B, G, T // TQ),
        in_specs=[
            pl.BlockSpec((1, NSA_GROUP, TQ, D_HEAD), lambda b, g, i: (b, g, i, 0)),
            per_bg(kcmp), per_bg(vcmp), per_bg(ks), per_bg(vs), per_bg(kw), per_bg(vw),
            pl.BlockSpec((1, 1, TQ, LANES), lambda b, g, i: (b, g, i, 0)),
            per_g(near_t), per_g(win_t), per_g(far_t), per_g(cmp_lhs), const2(ovt), const2(eye),
        ],
        out_specs=pl.BlockSpec((1, TQ, NSA_GROUP * D_HEAD), lambda b, g, i: (b, i, g)),
        out_shape=jax.ShapeDtypeStruct((B, T, D_NSA), BF16),
        scratch_shapes=[
            pltpu.VMEM((rows_kv, KAUG), BF16), pltpu.VMEM((rows_kv, VAUG), BF16),
            pltpu.VMEM((rows_kv, KAUG), BF16), pltpu.VMEM((rows_kv, VAUG), BF16),
            pltpu.VMEM((ncp, VAUG), BF16), pltpu.VMEM((ncp, VAUG), BF16),
            pltpu.VMEM((MAX_BLOCKS, TQ), F32),
            pltpu.VMEM((ROWS, KAUG), BF16), pltpu.VMEM((ROWS, 1), F32), pltpu.VMEM((ROWS, VAUG), F32),
            pltpu.VMEM((ROWS, KT), F32), pltpu.VMEM((ROWS, KT), F32), pltpu.VMEM((ROWS, D_HEAD), F32),
        ],
        compiler_params=pltpu.CompilerParams(
            dimension_semantics=("arbitrary", "arbitrary", "arbitrary"), vmem_limit_bytes=VMEM_LIMIT),
        name="nsa",
    )(q, kcmp, vcmp, ks, vs, kw, vw, gates, near_t, win_t, far_t, cmp_lhs, ovt, eye)


def _causal_conv3(t, prev_ref, w, b, row):
    p1 = prev_ref[7:8, :]
    p2 = prev_ref[6:7, :]
    s1 = jnp.where(row == 0, p1, pltpu.roll(t, 1, 0))
    s2 = jnp.where(row == 0, p2, jnp.where(row == 1, p1, pltpu.roll(t, 2, 0)))
    y = t * w[2:3] + s1 * w[1:2] + s2 * w[0:1] + b
    prev_ref[...] = t[t.shape[0] - 8:, :]
    return y


def _ffn_kernel(x_ref, on_ref, os_ref, wo_n_ref, wo_s_ref, fn_ref, wup_ref, cw_ref, cb_ref, wdn_ref,
                out_ref, prev_ref, acc_ref, h_ref):
    tm = x_ref.shape[1]

    @pl.when(pl.program_id(1) == 0)
    def _start_of_sequence():
        prev_ref[...] = jnp.zeros(prev_ref.shape, F32)

    x1 = x_ref[0] + _dot(on_ref[0], wo_n_ref[...]) + _dot(os_ref[0], wo_s_ref[...])
    ms = jnp.mean(x1 * x1, axis=-1, keepdims=True)
    h_ref[...] = (x1 * lax.rsqrt(ms + EPS) * fn_ref[...]).astype(BF16)
    acc_ref[...] = x1
    row = lax.broadcasted_iota(jnp.int32, (tm, FFN_CHUNK), 0)

    def chunk(c, carry):
        h = h_ref[...]
        a = _causal_conv3(_dot(h, wup_ref[0, c]), prev_ref.at[0, c], cw_ref[0, c], cb_ref[0, c], row)
        g = _causal_conv3(_dot(h, wup_ref[1, c]), prev_ref.at[1, c], cw_ref[1, c], cb_ref[1, c], row)
        act = (g * _sigmoid(g) * a).astype(BF16)
        acc_ref[...] += _dot(act, wdn_ref[c])
        return carry

    lax.fori_loop(0, wdn_ref.shape[0], chunk, 0)
    out_ref[0] = acc_ref[...]


def _ffn_call(x, o_nsa, o_sgu, wo_n, wo_s, ffn_norm, w_up, conv_w, conv_b, w_down, tm):
    B, T, _ = x.shape
    n_chunks = w_down.shape[0]
    resident = lambda a: pl.BlockSpec(a.shape, lambda b, i: (0,) * a.ndim, pipeline_mode=pl.Buffered(1))
    tile = lambda w: pl.BlockSpec((1, tm, w), lambda b, i: (b, i, 0))
    return pl.pallas_call(
        _ffn_kernel,
        grid=(B, T // tm),
        in_specs=[tile(D_MODEL), tile(D_NSA), tile(D_GMLP), resident(wo_n), resident(wo_s),
                  resident(ffn_norm), resident(w_up), resident(conv_w), resident(conv_b), resident(w_down)],
        out_specs=tile(D_MODEL),
        out_shape=jax.ShapeDtypeStruct((B, T, D_MODEL), F32),
        scratch_shapes=[pltpu.VMEM((2, n_chunks, 8, FFN_CHUNK), F32), pltpu.VMEM((tm, D_MODEL), F32),
                        pltpu.VMEM((tm, D_MODEL), BF16)],
        compiler_params=pltpu.CompilerParams(
            dimension_semantics=("arbitrary", "arbitrary"), vmem_limit_bytes=VMEM_LIMIT),
        name="ffn",
    )(x, o_nsa, o_sgu, wo_n, wo_s, ffn_norm, w_up, conv_w, conv_b, w_down)


def _mixers(x, rel_bias, attn_norm, w_in, q_norm, k_norm_cmp, k_norm_slc, k_norm_win,
            cmp_pe_k, cmp_w1_k, cmp_b1_k, cmp_w2_k, cmp_pe_v, cmp_w1_v, cmp_b1_v, cmp_w2_v,
            sgu_norm, sgu_w, sgu_b):
    B, T, _ = x.shape
    assert T % 512 == 0 and T // SEL_BLOCK <= MAX_BLOCKS and (T // CMP_STRIDE) % LANES == 0
    ncp = T // CMP_STRIDE
    G, R = NSA_KV_HEADS, NSA_GROUP

    o_kv = D_NSA
    o_g = D_NSA + 6 * D_KV
    o_uv = o_g + N_GATES
    w_q = w_in[:, :o_kv].astype(BF16)
    w_kv = w_in[:, o_kv:o_g].astype(BF16)
    per_group = NSA_GROUP * N_BRANCH
    w_g = jnp.pad(w_in[:, o_g:o_uv].reshape(D_MODEL, G, per_group), ((0, 0), (0, 0), (0, LANES - per_group)))
    w_g = w_g.reshape(D_MODEL, G * LANES).astype(BF16)
    w_uv = w_in[:, o_uv:].astype(BF16)
    qn = (jnp.tile(q_norm, NSA_HEADS) * (D_HEAD ** -0.5 * LOG2E)).reshape(1, D_NSA)
    ksn = jnp.tile(k_norm_slc, G).reshape(1, D_KV)
    kwn = jnp.tile(k_norm_win, G).reshape(1, D_KV)
    sgu_bt = jnp.repeat(sgu_b.T, GMLP_GROUP_DIM, axis=1)
    grp = np.arange(D_NSA) // D_HEAD
    bd = jnp.asarray(grp[:, None] == grp[None, :], BF16)

    q, kc, vc, ks, vs, kw, vw, gates, o_sgu = _proj_call(
        x, attn_norm.reshape(1, D_MODEL), w_q, w_kv, w_g, w_uv, qn, ksn, kwn,
        sgu_norm.reshape(1, D_GMLP), sgu_w, sgu_bt, bd, tm=512)

    half = CMP_STRIDE * D_HEAD
    kcmp, vcmp = _compress_call(
        kc.reshape(B, G, ncp, half), vc.reshape(B, G, ncp, half),
        cmp_pe_k.reshape(2, half), cmp_w1_k.astype(BF16), cmp_b1_k.reshape(1, CMP_HIDDEN),
        cmp_w2_k.astype(BF16), k_norm_cmp.reshape(1, D_HEAD),
        cmp_pe_v.reshape(2, half), cmp_w1_v.astype(BF16), cmp_b1_v.reshape(1, CMP_HIDDEN),
        cmp_w2_v.astype(BF16))

    near_t, win_t, far_t, cmp_lhs = _bias_tables(rel_bias)
    jj = np.arange(MAX_BLOCKS)[:, None]
    nn = np.arange(ncp)[None, :]
    ovt = jnp.asarray((nn >= 4 * jj - 1) & (nn <= 4 * jj + 3), BF16)
    eye = jnp.asarray(np.eye(TQ), BF16)

    o_nsa = _nsa_call(q, kcmp, vcmp, ks, vs, kw, vw, gates, near_t, win_t, far_t, cmp_lhs, ovt, eye)
    return o_nsa, o_sgu


def _layer(x, rel_bias, attn_norm, w_in, q_norm, k_norm_cmp, k_norm_slc, k_norm_win,
           cmp_pe_k, cmp_w1_k, cmp_b1_k, cmp_w2_k, cmp_pe_v, cmp_w1_v, cmp_b1_v, cmp_w2_v,
           sgu_norm, sgu_w, sgu_b, w_out, ffn_norm, w_up, conv_w, conv_b, w_down):
    o_nsa, o_sgu = _mixers(x, rel_bias, attn_norm, w_in, q_norm, k_norm_cmp, k_norm_slc, k_norm_win,
                           cmp_pe_k, cmp_w1_k, cmp_b1_k, cmp_w2_k, cmp_pe_v, cmp_w1_v, cmp_b1_v, cmp_w2_v,
                           sgu_norm, sgu_w, sgu_b)
    wo = w_out.astype(BF16)
    n_chunks = D_FF // FFN_CHUNK
    wu = jnp.transpose(w_up.astype(BF16).reshape(D_MODEL, 2, n_chunks, FFN_CHUNK), (1, 2, 0, 3))
    cw = jnp.transpose(conv_w.reshape(conv_w.shape[0], 2, n_chunks, FFN_CHUNK), (1, 2, 0, 3))
    cb = conv_b.reshape(2, n_chunks, 1, FFN_CHUNK)
    wd = w_down.astype(BF16).reshape(n_chunks, FFN_CHUNK, D_MODEL)
    return _ffn_call(x, o_nsa, o_sgu, wo[:D_NSA], wo[D_NSA:], ffn_norm.reshape(1, D_MODEL), wu, cw, cb, wd, tm=512)


def kernel(x, rel_bias, attn_norm, w_in, q_norm, k_norm_cmp, k_norm_slc, k_norm_win,
           cmp_pe_k, cmp_w1_k, cmp_b1_k, cmp_w2_k, cmp_pe_v, cmp_w1_v, cmp_b1_v, cmp_w2_v,
           sgu_norm, sgu_w, sgu_b, w_out, ffn_norm, w_up, conv_w, conv_b, w_down):
    depth = attn_norm.shape[0]
    for l in range(depth):
        x = _layer(x, rel_bias, attn_norm[l], w_in[l], q_norm[l], k_norm_cmp[l], k_norm_slc[l], k_norm_win[l],
                   cmp_pe_k[l], cmp_w1_k[l], cmp_b1_k[l], cmp_w2_k[l], cmp_pe_v[l], cmp_w1_v[l], cmp_b1_v[l],
                   cmp_w2_v[l], sgu_norm[l], sgu_w[l], sgu_b[l], w_out[l], ffn_norm[l], w_up[l], conv_w[l],
                   conv_b[l], w_down[l])
    return x
```

```python
import functools
import math

import numpy as np
import jax
import jax.numpy as jnp
from jax import lax
from jax.experimental import pallas as pl
from jax.experimental.pallas import tpu as pltpu

F32 = jnp.float32
BF16 = jnp.bfloat16

D_MODEL = 1024
D_HEAD = 64
NSA_HEADS = 8
NSA_KV_HEADS = 2
NSA_GROUP = NSA_HEADS // NSA_KV_HEADS
D_NSA = NSA_HEADS * D_HEAD
D_KV = NSA_KV_HEADS * D_HEAD
N_BRANCH = 3
N_GATES = NSA_HEADS * N_BRANCH
CMP_BLOCK = 32
CMP_STRIDE = 16
CMP_HIDDEN = 256
SEL_BLOCK = 64
SEL_TOPK = 16
WINDOW = 512
GMLP_GROUPS = 8
GMLP_GROUP_DIM = 64
D_GMLP = GMLP_GROUPS * GMLP_GROUP_DIM
CHUNK = 128
D_MIX = D_NSA + D_GMLP
REL_BUCKETS = 32
REL_MAX_DIST = 128
D_FF = 2816
EPS = 1e-6
NEG = -1e30
FORCED_SCORE = 1e4
LOG2E = 1.4426950408889634

LANES = 128
TQ = 256
SUB = TQ // SEL_BLOCK
ROWS = NSA_GROUP * TQ
KT = 256
KS = 2 * KT
RB = 64
BLOCKS_PER_TILE = KT // SEL_BLOCK
WIN_BLOCKS = WINDOW // SEL_BLOCK
KPAD = 512
N_TOEPLITZ = 3
MAX_BLOCKS = 64
KAUG = 256
VAUG = 128
CBAND = 28
FFN_CHUNK = 256
FFN_ROWS = 128
VMEM_LIMIT = 56 * 1024 * 1024

_NT = (((1,), (1,)), ((), ()))


def _dot(a, b):
    return jnp.dot(a, b, preferred_element_type=F32)


def _dot_nt(a, b):
    return lax.dot_general(a, b, _NT, preferred_element_type=F32)


def _split_bf16(x):
    hi = x.astype(BF16)
    lo = (x - hi.astype(F32)).astype(BF16)
    return hi, lo


def _gelu_tanh(x):
    return 0.5 * x * (1.0 + jnp.tanh(0.7978845608028654 * (x + 0.044715 * (x * x * x))))


def _sigmoid(x):
    return 0.5 * (1.0 + jnp.tanh(0.5 * x))


def _group_mean_sq(t, ones_blockdiag, width):
    t2 = t * t
    hi, lo = _split_bf16(t2)
    return (_dot(hi, ones_blockdiag) + _dot(lo, ones_blockdiag)) * (1.0 / width)


def _proj_kernel(x_ref, an_ref, wq_ref, wkv_ref, wg_ref, wuv_ref, qn_ref, ksn_ref, kwn_ref,
                 sgun_ref, sguw_ref, sgub_ref, bd_ref,
                 q_out, kc_out, vc_out, ks_out, vs_out, kw_out, vw_out, g_out, sgu_out):
    tm = x_ref.shape[1]
    x = x_ref[0]
    ms = jnp.mean(x * x, axis=-1, keepdims=True)
    h = (x * lax.rsqrt(ms + EPS) * an_ref[...]).astype(BF16)

    bd = bd_ref[...]
    q = _dot(h, wq_ref[...])
    qn = q * lax.rsqrt(_group_mean_sq(q, bd, D_HEAD) + EPS) * qn_ref[...]
    for hh in range(NSA_HEADS):
        q_out[0, hh] = qn[:, hh * D_HEAD:(hh + 1) * D_HEAD].astype(BF16)

    kv = _dot(h, wkv_ref[...])
    bd_kv = bd[:D_KV, :D_KV]
    kc = kv[:, 0 * D_KV:1 * D_KV]
    vc = kv[:, 1 * D_KV:2 * D_KV]
    ks = kv[:, 2 * D_KV:3 * D_KV]
    vs = kv[:, 3 * D_KV:4 * D_KV]
    kw = kv[:, 4 * D_KV:5 * D_KV]
    vw = kv[:, 5 * D_KV:6 * D_KV]
    ks = ks * lax.rsqrt(_group_mean_sq(ks, bd_kv, D_HEAD) + EPS) * ksn_ref[...]
    kw = kw * lax.rsqrt(_group_mean_sq(kw, bd_kv, D_HEAD) + EPS) * kwn_ref[...]
    for t, o_ref in ((kc, kc_out), (vc, vc_out), (ks, ks_out), (vs, vs_out), (kw, kw_out), (vw, vw_out)):
        for g in range(NSA_KV_HEADS):
            o_ref[0, g] = t[:, g * D_HEAD:(g + 1) * D_HEAD].astype(BF16)

    gates = _sigmoid(_dot(h, wg_ref[...]))
    for g in range(NSA_KV_HEADS):
        g_out[0, g] = gates[:, g * LANES:(g + 1) * LANES]

    uv = _gelu_tanh(_dot(h, wuv_ref[...]))
    u = uv[:, :D_GMLP]
    v = uv[:, D_GMLP:]
    vms = jnp.mean(v * v, axis=-1, keepdims=True)
    vb = (v * lax.rsqrt(vms + EPS) * sgun_ref[...]).astype(BF16)

    row = lax.broadcasted_iota(jnp.int32, (CHUNK, CHUNK), 0)
    col = lax.broadcasted_iota(jnp.int32, (CHUNK, CHUNK), 1)
    tril = col <= row
    w_tril = [jnp.where(tril, sguw_ref[g], 0.0).astype(BF16) for g in range(GMLP_GROUPS)]
    first_half = lax.broadcasted_iota(jnp.int32, (CHUNK, LANES), 1) < GMLP_GROUP_DIM
    for c in range(tm // CHUNK):
        rows = slice(c * CHUNK, (c + 1) * CHUNK)
        zs = []
        for p in range(D_GMLP // LANES):
            blk = vb[rows, p * LANES:(p + 1) * LANES]
            z0 = _dot(w_tril[2 * p], blk)
            z1 = _dot(w_tril[2 * p + 1], blk)
            zs.append(jnp.where(first_half, z0, z1))
        z = jnp.concatenate(zs, axis=1) + sgub_ref[...]
        sgu_out[0, rows, :] = (u[rows, :] * z).astype(BF16)


def _proj_call(x, attn_norm, w_q, w_kv, w_g, w_uv, qn, ksn, kwn, sgu_norm, sgu_w, sgu_bt, bd, tm):
    B, T, _ = x.shape
    const2 = lambda b, i: (0, 0)
    const3 = lambda b, i: (0, 0, 0)
    head_spec = lambda nh: pl.BlockSpec((1, nh, tm, D_HEAD), lambda b, i: (b, 0, i, 0))
    kv_shape = jax.ShapeDtypeStruct((B, NSA_KV_HEADS, T, D_HEAD), BF16)
    return pl.pallas_call(
        _proj_kernel,
        grid=(B, T // tm),
        in_specs=[
            pl.BlockSpec((1, tm, D_MODEL), lambda b, i: (b, i, 0)),
            pl.BlockSpec((1, D_MODEL), const2),
            pl.BlockSpec(w_q.shape, const2),
            pl.BlockSpec(w_kv.shape, const2),
            pl.BlockSpec(w_g.shape, const2),
            pl.BlockSpec(w_uv.shape, const2),
            pl.BlockSpec((1, D_NSA), const2),
            pl.BlockSpec((1, D_KV), const2),
            pl.BlockSpec((1, D_KV), const2),
            pl.BlockSpec((1, D_GMLP), const2),
            pl.BlockSpec(sgu_w.shape, const3),
            pl.BlockSpec(sgu_bt.shape, const2),
            pl.BlockSpec(bd.shape, const2),
        ],
        out_specs=[
            head_spec(NSA_HEADS),
            head_spec(NSA_KV_HEADS), head_spec(NSA_KV_HEADS), head_spec(NSA_KV_HEADS),
            head_spec(NSA_KV_HEADS), head_spec(NSA_KV_HEADS), head_spec(NSA_KV_HEADS),
            pl.BlockSpec((1, NSA_KV_HEADS, tm, LANES), lambda b, i: (b, 0, i, 0)),
            pl.BlockSpec((1, tm, D_GMLP), lambda b, i: (b, i, 0)),
        ],
        out_shape=[
            jax.ShapeDtypeStruct((B, NSA_HEADS, T, D_HEAD), BF16),
            kv_shape, kv_shape, kv_shape, kv_shape, kv_shape, kv_shape,
            jax.ShapeDtypeStruct((B, NSA_KV_HEADS, T, LANES), F32),
            jax.ShapeDtypeStruct((B, T, D_GMLP), BF16),
        ],
        compiler_params=pltpu.CompilerParams(
            dimension_semantics=("arbitrary", "arbitrary"), vmem_limit_bytes=VMEM_LIMIT),
        name="proj",
    )(x, attn_norm, w_q, w_kv, w_g, w_uv, qn, ksn, kwn, sgu_norm, sgu_w, sgu_bt, bd)


def _compress_one(tok_ref, pe_ref, w1_ref, b1_ref, w2_ref):
    half = CMP_STRIDE * D_HEAD
    tok = tok_ref[0, 0].astype(F32)
    top = (tok + pe_ref[0:1, :]).astype(BF16)
    bot = (tok + pe_ref[1:2, :]).astype(BF16)
    a = _dot(top, w1_ref[:half, :])
    b = _dot(bot, w1_ref[half:, :])
    ncp = a.shape[0]
    pre = a + pltpu.roll(b, ncp - 1, 0) + b1_ref[...]
    hid = _gelu_tanh(pre).astype(BF16)
    return _dot(hid, w2_ref[...])


def _compress_kernel(kc_ref, vc_ref, pek_ref, w1k_ref, b1k_ref, w2k_ref, kn_ref,
                     pev_ref, w1v_ref, b1v_ref, w2v_ref, k_out, v_out):
    k = _compress_one(kc_ref, pek_ref, w1k_ref, b1k_ref, w2k_ref)
    kms = jnp.mean(k * k, axis=-1, keepdims=True)
    k_out[0, 0] = (k * lax.rsqrt(kms + EPS) * kn_ref[...]).astype(BF16)
    v_out[0, 0] = _compress_one(vc_ref, pev_ref, w1v_ref, b1v_ref, w2v_ref).astype(BF16)


def _compress_call(kc, vc, pek, w1k, b1k, w2k, kn, pev, w1v, b1v, w2v):
    B, G, ncp, width = kc.shape
    const2 = lambda b, g: (0, 0)
    tok_spec = pl.BlockSpec((1, 1, ncp, width), lambda b, g: (b, g, 0, 0))
    out_spec = pl.BlockSpec((1, 1, ncp, D_HEAD), lambda b, g: (b, g, 0, 0))
    full = lambda a: pl.BlockSpec(a.shape, const2)
    out_shape = jax.ShapeDtypeStruct((B, G, ncp, D_HEAD), BF16)
    return pl.pallas_call(
        _compress_kernel,
        grid=(B, G),
        in_specs=[tok_spec, tok_spec, full(pek), full(w1k), full(b1k), full(w2k), full(kn),
                  full(pev), full(w1v), full(b1v), full(w2v)],
        out_specs=[out_spec, out_spec],
        out_shape=[out_shape, out_shape],
        compiler_params=pltpu.CompilerParams(
            dimension_semantics=("arbitrary", "arbitrary"), vmem_limit_bytes=VMEM_LIMIT),
        name="compress",
    )(kc, vc, pek, w1k, b1k, w2k, kn, pev, w1v, b1v, w2v)


def _rel_bucket_np(dist):
    max_exact = REL_BUCKETS // 2
    d = np.maximum(dist, 1).astype(np.float32)
    log_b = max_exact + (np.log(d / np.float32(max_exact)) / np.float32(math.log(REL_MAX_DIST / max_exact))
                         * np.float32(REL_BUCKETS - max_exact)).astype(np.int32)
    log_b = np.clip(log_b, max_exact, REL_BUCKETS - 1)
    return np.where(dist < max_exact, np.maximum(dist, 0), log_b)


def _bias_tables(rel_bias):
    rb = rel_bias.astype(F32) * LOG2E
    qq = np.arange(SEL_BLOCK)[:, None]
    kp = np.arange(SEL_BLOCK)[None, :]
    tile_d = [m * SEL_BLOCK + qq - kp for m in range(N_TOEPLITZ)]
    s_ = np.arange(SUB)[:, None, None]
    e_ = np.arange(CBAND)[None, None, :]
    dist_c = qq[None, :, :] + CMP_STRIDE * (e_ - 4 * (SUB - 1) + 4 * s_) - (3 * CMP_STRIDE + CMP_BLOCK - 1)
    all_d = np.concatenate([d.reshape(-1) for d in tile_d] + [dist_c.reshape(-1)])
    onehot = np.eye(REL_BUCKETS, dtype=np.float32)[_rel_bucket_np(all_d)]
    vals = jnp.dot(jnp.asarray(onehot), rb, precision=lax.Precision.HIGHEST)
    vals = jnp.where(jnp.asarray(all_d >= 0)[:, None], vals, NEG)
    n_t = N_TOEPLITZ * SEL_BLOCK * SEL_BLOCK
    toep = jnp.transpose(vals[:n_t].reshape(N_TOEPLITZ, SEL_BLOCK, SEL_BLOCK, NSA_HEADS), (0, 3, 1, 2))
    band = jnp.transpose(vals[n_t:].reshape(SUB, SEL_BLOCK, CBAND, NSA_HEADS), (3, 0, 1, 2))
    far = rb[REL_BUCKETS - 1]
    far_tile = jnp.broadcast_to(far[:, None, None], (NSA_HEADS, SEL_BLOCK, SEL_BLOCK))
    neg_tile = jnp.full((NSA_HEADS, SEL_BLOCK, SEL_BLOCK), NEG, F32)
    edge_tile = jnp.where(jnp.asarray(kp > qq)[None], far_tile, NEG)

    def tile(m, windowed):
        if m < 0 or (windowed and m > WIN_BLOCKS):
            return neg_tile
        if windowed and m == WIN_BLOCKS:
            return edge_tile
        return toep[m] if m < N_TOEPLITZ else far_tile

    def table(first_m, n_blocks, windowed):
        rows = [jnp.concatenate([tile(first_m + s - c, windowed) for c in range(n_blocks)], axis=2)
                for s in range(SUB)]
        t = jnp.concatenate(rows, axis=1)
        return t.reshape(NSA_KV_HEADS, ROWS, n_blocks * SEL_BLOCK)

    far_rows = jnp.repeat(far, TQ).reshape(NSA_KV_HEADS, ROWS, 1)
    near_t = table(BLOCKS_PER_TILE, 2 * BLOCKS_PER_TILE, False) - far_rows
    win_t = table(WIN_BLOCKS, WIN_BLOCKS + BLOCKS_PER_TILE, True)
    far_t = jnp.broadcast_to(far_rows, (NSA_KV_HEADS, ROWS, MAX_BLOCKS))
    band_t = band.reshape(NSA_KV_HEADS, ROWS, CBAND)
    band_hi = band_t.astype(BF16)
    band_lo = (band_t - band_hi.astype(F32)).astype(BF16)
    far_hi = far_rows.astype(BF16)
    far_lo = (far_rows - far_hi.astype(F32)).astype(BF16)
    neg_col = jnp.full((NSA_KV_HEADS, ROWS, 1), NEG, BF16)
    zeros = jnp.zeros((NSA_KV_HEADS, ROWS, D_HEAD - 2 * CBAND - 3), BF16)
    cmp_lhs = jnp.concatenate([band_hi, band_lo, far_hi, far_lo, neg_col, zeros], axis=2)
    return near_t, win_t, far_t, cmp_lhs


def _nsa_kernel(q_ref, kc_ref, vc_ref, ks_ref, vs_ref, kw_ref, vw_ref, g_ref,
                near_ref, win_ref, far_ref, cmpl_ref, ovt_ref, eye_ref,
                o_ref, ksa, vsa, kwa, vwa, kca, vca, sc_scr, lhs_scr, m_scr, al_scr, acc_scr, sa_scr, sb_scr, pa_scr, pb_scr,
                ps_scr, out_scr):
    qt = pl.program_id(2)
    first = qt * SUB
    T = ks_ref.shape[2]
    ncp = kc_ref.shape[2]

    @pl.when(qt == 0)
    def _init():
        n_chunks = ksa.shape[0] // KT
        r_io = lax.broadcasted_iota(jnp.int32, (KT, KAUG), 0)
        l_io = lax.broadcasted_iota(jnp.int32, (KT, KAUG), 1)
        lv = lax.broadcasted_iota(jnp.int32, (KT, VAUG), 1)
        vpat = jnp.where(lv == D_HEAD, 1.0, 0.0).astype(BF16)

        def fill(c, carry):
            rows = pl.ds(pl.multiple_of(c * KT, KT), KT)
            key = r_io + (c * KT - KPAD)
            blk = key // SEL_BLOCK
            real = (key >= 0) & (key < T)
            hot = real & (((l_io >= D_HEAD) & (l_io < 2 * D_HEAD) & (l_io - D_HEAD == blk))
                          | ((l_io >= 2 * D_HEAD) & (l_io < 3 * D_HEAD) & (l_io - 2 * D_HEAD == blk)))
            hot = hot | ((~real) & (l_io == 3 * D_HEAD))
            pat = jnp.where(hot, 1.0, 0.0).astype(BF16)
            ksa[rows, :] = pat
            kwa[rows, :] = pat
            vsa[rows, :] = vpat
            vwa[rows, :] = vpat
            return carry

        lax.fori_loop(0, n_chunks, fill, 0)
        ksa[KPAD:KPAD + T, 0:D_HEAD] = ks_ref[0, 0]
        kwa[KPAD:KPAD + T, 0:D_HEAD] = kw_ref[0, 0]
        vsa[KPAD:KPAD + T, 0:D_HEAD] = vs_ref[0, 0]
        vwa[KPAD:KPAD + T, 0:D_HEAD] = vw_ref[0, 0]
        lc = lax.broadcasted_iota(jnp.int32, (ncp, VAUG), 1)
        kca[...] = jnp.zeros((ncp, VAUG), BF16)
        kca[:, 0:D_HEAD] = kc_ref[0, 0]
        vca[...] = jnp.where(lc == D_HEAD, 1.0, 0.0).astype(BF16)
        vca[:, 0:D_HEAD] = vc_ref[0, 0]

    q4 = q_ref[0].reshape(ROWS, D_HEAD)
    gates = g_ref[0, 0]
    row_blocks = [slice(r, r + RB) for r in range(0, ROWS, RB)]

    def gate_col(branch):
        return jnp.concatenate([gates[:, hh * N_BRANCH + branch:hh * N_BRANCH + branch + 1]
                                for hh in range(NSA_GROUP)], axis=0)

    lane_c = lax.broadcasted_iota(jnp.int32, (ncp, LANES), 1)
    n_c = lax.broadcasted_iota(jnp.int32, (ncp, LANES), 0)
    e = lane_c - D_HEAD
    base = 4 * first + (4 * SUB - 1)
    ind = (((e >= 0) & (e < CBAND) & (n_c == base - e))
           | ((e >= CBAND) & (e < 2 * CBAND) & (n_c == base - (e - CBAND)))
           | (((e == 2 * CBAND) | (e == 2 * CBAND + 1)) & (n_c < base - (CBAND - 1)))
           | ((e == 2 * CBAND + 2) & (n_c > base)))
    rhs_c = jnp.where(lane_c < D_HEAD, kca[...], jnp.where(ind, 1.0, 0.0).astype(BF16))
    lhs_c = jnp.concatenate([q4, cmpl_ref[0]], axis=1)
    sa_scr[:, 0:ncp] = _dot_nt(lhs_c, rhs_c)

    j_q = lax.broadcasted_iota(jnp.int32, (ROWS, MAX_BLOCKS), 1)
    pad_col = jnp.where(j_q == 0, NEG, 0.0).astype(BF16)
    zero_col = jnp.zeros((ROWS, MAX_BLOCKS), BF16)
    lhs_win = jnp.concatenate([q4, zero_col, zero_col, pad_col], axis=1)
    win_start = pl.multiple_of(KPAD + (first - WIN_BLOCKS) * SEL_BLOCK, KT)
    sb_scr[...] = _dot_nt(lhs_win, kwa[pl.ds(win_start, KS), :])
    sa_scr[:, KT:KS] = _dot_nt(lhs_win, kwa[pl.ds(win_start + KS, KT), :])

    for i, rows in enumerate(row_blocks):
        s = sa_scr[rows, 0:ncp]
        m = jnp.max(s, axis=1, keepdims=True)
        p = jnp.where(s > 0.5 * NEG, jnp.exp2(s - m), 0.0)
        l = jnp.sum(p, axis=1, keepdims=True)
        inv = jnp.where(l > 0.0, 1.0 / l, 0.0)
        al_scr[rows] = inv
        pb_scr[rows, 0:ncp] = p.astype(BF16)
        tok = slice((i % (TQ // RB)) * RB, (i % (TQ // RB) + 1) * RB)
        if i < TQ // RB:
            ps_scr[tok] = p * inv
        else:
            ps_scr[tok] += p * inv
    acc_c = _dot(pb_scr[:, 0:ncp], vca[...])
    out_scr[...] = acc_c[:, :D_HEAD] * (gate_col(0) * al_scr[...])
    ps_hi, ps_lo = _split_bf16(ps_scr[...])
    ovt = ovt_ref[...]
    imp_t = _dot_nt(ovt, ps_hi) + _dot_nt(ovt, ps_lo)

    for rows in row_blocks:
        s0 = sb_scr[rows, :] + win_ref[0, rows, 0:KS]
        s1 = sa_scr[rows, KT:KS] + win_ref[0, rows, KS:KS + KT]
        m = jnp.maximum(jnp.max(s0, axis=1, keepdims=True), jnp.max(s1, axis=1, keepdims=True))
        pa_scr[rows, 0:KS] = jnp.exp2(s0 - m).astype(BF16)
        pa_scr[rows, KS:KS + KT] = jnp.exp2(s1 - m).astype(BF16)
    acc_w = _dot(pa_scr[...], vwa[pl.ds(win_start, KS + KT), :])
    out_scr[...] += acc_w[:, :D_HEAD] * (gate_col(2) / acc_w[:, D_HEAD:D_HEAD + 1])

    j_t = lax.broadcasted_iota(jnp.int32, (MAX_BLOCKS, TQ), 0)
    cur = first + lax.broadcasted_iota(jnp.int32, (MAX_BLOCKS, TQ), 1) // SEL_BLOCK
    forced = (j_t == 0) | (j_t == cur) | (j_t == cur - 1)
    score_t = jnp.where(forced, FORCED_SCORE, jnp.where(j_t <= cur, imp_t, NEG))
    sc_scr[...] = score_t

    def count(u, cnt):
        for k in range(SUB):
            i = SUB * u + k
            row_i = sc_scr[pl.ds(i, 1), :]
            beats = (row_i > score_t) | ((row_i == score_t) & (j_t > i))
            cnt = cnt + beats.astype(jnp.int32)
        return cnt

    cnt = lax.fori_loop(0, qt + 1, count, jnp.zeros((MAX_BLOCKS, TQ), jnp.int32))
    sel_t = jnp.where((cnt < SEL_TOPK) & (j_t <= cur), 1.0, 0.0).astype(BF16)
    sel = _dot_nt(eye_ref[...], sel_t)
    sel4 = jnp.concatenate([sel] * NSA_GROUP, axis=0) > 0.5

    hi, lo = _split_bf16(jnp.where(sel4, far_ref[0], NEG))
    lhs_scr[...] = jnp.concatenate([q4, hi, lo, pad_col], axis=1)
    m_scr[...] = jnp.full((ROWS, 1), NEG, F32)
    acc_scr[...] = jnp.zeros((ROWS, VAUG), F32)

    def scores(w, dst):
        start = pl.multiple_of(KPAD + w * KS, KS)
        dst[...] = _dot_nt(lhs_scr[...], ksa[pl.ds(start, KS), :])

    def consume(w, src, p_scr, with_table):
        if with_table:
            rel = [2 * w + half - (qt - 1) for half in range(KS // KT)]
            col = [pl.multiple_of(jnp.clip(r, 0, 1) * KT, KT) for r in rel]
        for rows in row_blocks:
            s = [src[rows, half * KT:(half + 1) * KT] for half in range(KS // KT)]
            if with_table:
                s = [s[half] + jnp.where(rel[half] >= 0, near_ref[0, rows, pl.ds(col[half], KT)], 0.0)
                     for half in range(KS // KT)]
            m_prev = m_scr[rows]
            m = jnp.maximum(m_prev, jnp.maximum(jnp.max(s[0], axis=1, keepdims=True),
                                                jnp.max(s[1], axis=1, keepdims=True)))
            al_scr[rows] = jnp.exp2(m_prev - m)
            m_scr[rows] = m
            for half in range(KS // KT):
                p_scr[rows, half * KT:(half + 1) * KT] = jnp.exp2(s[half] - m).astype(BF16)
        start = pl.multiple_of(KPAD + w * KS, KS)
        acc_scr[...] = acc_scr[...] * al_scr[...] + _dot(p_scr[:, 0:KS], vsa[pl.ds(start, KS), :])

    scores(0, sa_scr)

    def pair(with_table, u, carry):
        scores(2 * u + 1, sb_scr)
        consume(2 * u, sa_scr, pa_scr, with_table)
        scores(2 * u + 2, sa_scr)
        consume(2 * u + 1, sb_scr, pb_scr, with_table)
        return carry

    n_trips = qt // 4 + 1
    n_plain = jnp.maximum(qt - 1, 0) // 4
    lax.fori_loop(0, n_plain, functools.partial(pair, False), 0)
    lax.fori_loop(n_plain, n_trips, functools.partial(pair, True), 0)
    acc_s = acc_scr[...]

    o = out_scr[...] + acc_s[:, :D_HEAD] * (gate_col(1) / acc_s[:, D_HEAD:D_HEAD + 1])
    for hh in range(NSA_GROUP):
        o_ref[0, :, hh * D_HEAD:(hh + 1) * D_HEAD] = o[hh * TQ:(hh + 1) * TQ].astype(BF16)


def _nsa_call(q, kcmp, vcmp, ks, vs, kw, vw, gates, near_t, win_t, far_t, cmp_lhs, ovt, eye):
    B, _, T, _ = q.shape
    G = NSA_KV_HEADS
    ncp = kcmp.shape[2]
    rows_kv = KPAD + T + KS
    per_bg =lambda a: pl.BlockSpec((1, 1) + a.shape[2:], lambda b, g, i: (b, g, 0, 0))
    per_g = lambda a: pl.BlockSpec((1,) + a.shape[1:], lambda b, g, i: (g, 0, 0))
    const2 = lambda a: pl.BlockSpec(a.shape, lambda b, g, i: (0, 0))
    return pl.pallas_call(
        _nsa_kernel,
        grid=(B, G, T // TQ),
        in_specs=[
            pl.BlockSpec((1, NSA_GROUP, TQ, D_HEAD), lambda b, g, i: (b, g, i, 0)),
            per_bg(kcmp), per_bg(vcmp), per_bg(ks), per_bg(vs), per_bg(kw), per_bg(vw),
            pl.BlockSpec((1, 1, TQ, LANES), lambda b, g, i: (b, g, i, 0)),
            per_g(near_t), per_g(win_t), per_g(far_t), per_g(cmp_lhs), const2(ovt), const2(eye),
        ],
        out_specs=pl.BlockSpec((1, TQ, NSA_GROUP * D_HEAD), lambda b, g, i: (b, i, g)),
        out_shape=jax.ShapeDtypeStruct((B, T, D_NSA), BF16),
        scratch_shapes=[
            pltpu.VMEM((rows_kv, KAUG), BF16), pltpu.VMEM((rows_kv, VAUG), BF16),
            pltpu.VMEM((rows_kv, KAUG), BF16), pltpu.VMEM((rows_kv, VAUG), BF16),
            pltpu.VMEM((ncp, VAUG), BF16), pltpu.VMEM((ncp, VAUG), BF16),
            pltpu.VMEM((MAX_BLOCKS, TQ), F32),
            pltpu.VMEM((ROWS, KAUG), BF16), pltpu.VMEM((ROWS, 1), F32), pltpu.VMEM((ROWS, 1), F32),
            pltpu.VMEM((ROWS, VAUG), F32),
            pltpu.VMEM((ROWS, KS), F32), pltpu.VMEM((ROWS, KS), F32),
            pltpu.VMEM((ROWS, KS + KT), BF16), pltpu.VMEM((ROWS, KS), BF16),
            pltpu.VMEM((TQ, ncp), F32), pltpu.VMEM((ROWS, D_HEAD), F32),
        ],
        compiler_params=pltpu.CompilerParams(
            dimension_semantics=("arbitrary", "arbitrary", "arbitrary"), vmem_limit_bytes=VMEM_LIMIT),
        name="nsa",
    )(q, kcmp, vcmp, ks, vs, kw, vw, gates, near_t, win_t, far_t, cmp_lhs, ovt, eye)


def _ffn_kernel(x_ref, on_ref, os_ref, wo_n_ref, wo_s_ref, fn_ref, wup_ref, cw_ref, cb_ref, wdn_ref,
                out_ref, prev_ref, h_ref, ua_scr, ub_scr, act_ref):
    tm = x_ref.shape[1]
    n_chunks = wup_ref.shape[1]

    @pl.when(pl.program_id(1) == 0)
    def _start_of_sequence():
        prev_ref[...] = jnp.zeros(prev_ref.shape, F32)

    x1 = x_ref[0] + _dot(on_ref[0], wo_n_ref[...]) + _dot(os_ref[0], wo_s_ref[...])
    ms = jnp.mean(x1 * x1, axis=-1, keepdims=True)
    h_ref[...] = (x1 * lax.rsqrt(ms + EPS) * fn_ref[...]).astype(BF16)
    out_ref[0] = x1

    def up(c, dst):
        h = h_ref[...]
        dst[0] = _dot(h, wup_ref[0, c])
        dst[1] = _dot(h, wup_ref[1, c])

    def conv_gate(c, src):
        for r0 in range(0, tm, FFN_ROWS):
            ys = []
            for half in range(2):
                w = cw_ref[half, c]
                if r0 == 0:
                    ext = jnp.concatenate([prev_ref[half, c], src[half, 0:FFN_ROWS]], axis=0)
                else:
                    ext = src[half, r0 - 8:r0 + FFN_ROWS]
                y = (ext * w[2:3] + pltpu.roll(ext, 1, 0) * w[1:2] + pltpu.roll(ext, 2, 0) * w[0:1]
                     + cb_ref[half, c])
                ys.append(y[8:])
            a, g = ys
            col = pl.multiple_of(c * FFN_CHUNK, FFN_CHUNK)
            act_ref[r0:r0 + FFN_ROWS, pl.ds(col, FFN_CHUNK)] = (g * _sigmoid(g) * a).astype(BF16)
        for half in range(2):
            prev_ref[half, c] = src[half, tm - 8:tm]

    up(0, ua_scr)

    def pair(u, carry):
        up(2 * u + 1, ub_scr)
        conv_gate(2 * u, ua_scr)
        up(2 * u + 2, ua_scr)
        conv_gate(2 * u + 1, ub_scr)
        return carry

    lax.fori_loop(0, (n_chunks - 1) // 2, pair, 0)
    conv_gate(n_chunks - 1, ua_scr)
    out_ref[0] += _dot(act_ref[...], wdn_ref[...])


def _ffn_call(x, o_nsa, o_sgu, wo_n, wo_s, ffn_norm, w_up, conv_w, conv_b, w_down, tm):
    B, T, _ = x.shape
    n_chunks = w_up.shape[1]
    assert n_chunks % 2 == 1
    resident =lambda a: pl.BlockSpec(a.shape, lambda b, i: (0,) * a.ndim, pipeline_mode=pl.Buffered(1))
    tile = lambda w: pl.BlockSpec((1, tm, w), lambda b, i: (b, i, 0))
    return pl.pallas_call(
        _ffn_kernel,
        grid=(B, T // tm),
        in_specs=[tile(D_MODEL), tile(D_NSA), tile(D_GMLP), resident(wo_n), resident(wo_s),
                  resident(ffn_norm), resident(w_up), resident(conv_w), resident(conv_b), resident(w_down)],
        out_specs=tile(D_MODEL),
        out_shape=jax.ShapeDtypeStruct((B, T, D_MODEL), F32),
        scratch_shapes=[pltpu.VMEM((2, n_chunks, 8, FFN_CHUNK), F32), pltpu.VMEM((tm, D_MODEL), BF16),
                        pltpu.VMEM((2, tm, FFN_CHUNK), F32), pltpu.VMEM((2, tm, FFN_CHUNK), F32),
                        pltpu.VMEM((tm, D_FF), BF16)],
        compiler_params=pltpu.CompilerParams(
            dimension_semantics=("arbitrary", "arbitrary"), vmem_limit_bytes=VMEM_LIMIT),
        name="ffn",
    )(x, o_nsa, o_sgu, wo_n, wo_s, ffn_norm, w_up, conv_w, conv_b, w_down)


def _mixers(x, rel_bias, attn_norm, w_in, q_norm, k_norm_cmp, k_norm_slc, k_norm_win,
            cmp_pe_k, cmp_w1_k, cmp_b1_k, cmp_w2_k, cmp_pe_v, cmp_w1_v, cmp_b1_v, cmp_w2_v,
            sgu_norm, sgu_w, sgu_b):
    B, T, _ = x.shape
    assert T % 512 == 0 and T // SEL_BLOCK <= MAX_BLOCKS and (T // CMP_STRIDE) % LANES == 0
    ncp = T // CMP_STRIDE
    G, R = NSA_KV_HEADS, NSA_GROUP

    o_kv = D_NSA
    o_g = D_NSA + 6 * D_KV
    o_uv = o_g + N_GATES
    w_q = w_in[:, :o_kv].astype(BF16)
    w_kv = w_in[:, o_kv:o_g].astype(BF16)
    per_group = NSA_GROUP * N_BRANCH
    w_g = jnp.pad(w_in[:, o_g:o_uv].reshape(D_MODEL, G, per_group), ((0, 0), (0, 0), (0, LANES - per_group)))
    w_g = w_g.reshape(D_MODEL, G * LANES).astype(BF16)
    w_uv = w_in[:, o_uv:].astype(BF16)
    qn = (jnp.tile(q_norm, NSA_HEADS) * (D_HEAD ** -0.5 * LOG2E)).reshape(1, D_NSA)
    ksn = jnp.tile(k_norm_slc, G).reshape(1, D_KV)
    kwn = jnp.tile(k_norm_win, G).reshape(1, D_KV)
    sgu_bt = jnp.repeat(sgu_b.T, GMLP_GROUP_DIM, axis=1)
    grp = np.arange(D_NSA) // D_HEAD
    bd = jnp.asarray(grp[:, None] == grp[None, :], BF16)

    q, kc, vc, ks, vs, kw, vw, gates, o_sgu = _proj_call(
        x, attn_norm.reshape(1, D_MODEL), w_q, w_kv, w_g, w_uv, qn, ksn, kwn,
        sgu_norm.reshape(1, D_GMLP), sgu_w, sgu_bt, bd, tm=512)

    half = CMP_STRIDE * D_HEAD
    kcmp, vcmp = _compress_call(
        kc.reshape(B, G, ncp, half), vc.reshape(B, G, ncp, half),
        cmp_pe_k.reshape(2, half), cmp_w1_k.astype(BF16), cmp_b1_k.reshape(1, CMP_HIDDEN),
        cmp_w2_k.astype(BF16), k_norm_cmp.reshape(1, D_HEAD),
        cmp_pe_v.reshape(2, half), cmp_w1_v.astype(BF16), cmp_b1_v.reshape(1, CMP_HIDDEN),
        cmp_w2_v.astype(BF16))

    near_t, win_t, far_t, cmp_lhs = _bias_tables(rel_bias)
    jj = np.arange(MAX_BLOCKS)[:, None]
    nn = np.arange(ncp)[None, :]
    ovt = jnp.asarray((nn >= 4 * jj - 1) & (nn <= 4 * jj + 3), BF16)
    eye = jnp.asarray(np.eye(TQ), BF16)

    o_nsa = _nsa_call(q, kcmp, vcmp, ks, vs, kw, vw, gates, near_t, win_t, far_t, cmp_lhs, ovt, eye)
    return o_nsa, o_sgu


def _layer(x, rel_bias, attn_norm, w_in, q_norm, k_norm_cmp, k_norm_slc, k_norm_win,
           cmp_pe_k, cmp_w1_k, cmp_b1_k, cmp_w2_k, cmp_pe_v, cmp_w1_v, cmp_b1_v, cmp_w2_v,
           sgu_norm, sgu_w, sgu_b, w_out, ffn_norm, w_up, conv_w, conv_b, w_down):
    o_nsa, o_sgu = _mixers(x, rel_bias, attn_norm, w_in, q_norm, k_norm_cmp, k_norm_slc, k_norm_win,
                           cmp_pe_k, cmp_w1_k, cmp_b1_k, cmp_w2_k, cmp_pe_v, cmp_w1_v, cmp_b1_v, cmp_w2_v,
                           sgu_norm, sgu_w, sgu_b)
    wo = w_out.astype(BF16)
    n_chunks = D_FF // FFN_CHUNK
    wu = jnp.transpose(w_up.astype(BF16).reshape(D_MODEL, 2, n_chunks, FFN_CHUNK), (1, 2, 0, 3))
    cw = jnp.transpose(conv_w.reshape(conv_w.shape[0], 2, n_chunks, FFN_CHUNK), (1, 2, 0, 3))
    cb = conv_b.reshape(2, n_chunks, 1, FFN_CHUNK)
    wd = w_down.astype(BF16)
    return _ffn_call(x, o_nsa, o_sgu, wo[:D_NSA], wo[D_NSA:], ffn_norm.reshape(1, D_MODEL), wu, cw, cb, wd, tm=512)


def kernel(x, rel_bias, attn_norm, w_in, q_norm, k_norm_cmp, k_norm_slc, k_norm_win,
           cmp_pe_k, cmp_w1_k, cmp_b1_k, cmp_w2_k, cmp_pe_v, cmp_w1_v, cmp_b1_v, cmp_w2_v,
           sgu_norm, sgu_w, sgu_b, w_out, ffn_norm, w_up, conv_w, conv_b, w_down):
    depth = attn_norm.shape[0]
    for l in range(depth):
        x = _layer(x, rel_bias, attn_norm[l], w_in[l], q_norm[l], k_norm_cmp[l], k_norm_slc[l], k_norm_win[l],
                   cmp_pe_k[l], cmp_w1_k[l], cmp_b1_k[l], cmp_w2_k[l], cmp_pe_v[l], cmp_w1_v[l], cmp_b1_v[l],
                   cmp_w2_v[l], sgu_norm[l], sgu_w[l], sgu_b[l], w_out[l], ffn_norm[l], w_up[l], conv_w[l],
                   conv_b[l], w_down[l])
    return x
```

```python
import functools
import math

import numpy as np
import jax
import jax.numpy as jnp
from jax import lax
from jax.experimental import pallas as pl
from jax.experimental.pallas import tpu as pltpu

F32 = jnp.float32
BF16 = jnp.bfloat16

D_MODEL = 1024
D_HEAD = 64
NSA_HEADS = 8
NSA_KV_HEADS = 2
NSA_GROUP = NSA_HEADS // NSA_KV_HEADS
D_NSA = NSA_HEADS * D_HEAD
D_KV = NSA_KV_HEADS * D_HEAD
N_BRANCH = 3
N_GATES = NSA_HEADS * N_BRANCH
CMP_BLOCK = 32
CMP_STRIDE = 16
CMP_HIDDEN = 256
SEL_BLOCK = 64
SEL_TOPK = 16
WINDOW = 512
GMLP_GROUPS = 8
GMLP_GROUP_DIM = 64
D_GMLP = GMLP_GROUPS * GMLP_GROUP_DIM
CHUNK = 128
D_MIX = D_NSA + D_GMLP
REL_BUCKETS = 32
REL_MAX_DIST = 128
D_FF = 2816
EPS = 1e-6
NEG = -1e30
FORCED_SCORE = 1e4
LOG2E = 1.4426950408889634

LANES = 128
TQ = 256
SUB = TQ // SEL_BLOCK
ROWS = NSA_GROUP * TQ
KT = 256
KS = 2 * KT
RB = 64
BLOCKS_PER_TILE = KT // SEL_BLOCK
WIN_BLOCKS = WINDOW // SEL_BLOCK
KPAD = 512
N_TOEPLITZ = 3
MAX_BLOCKS = 64
KAUG = 256
VAUG = 128
CBAND = 28
FFN_CHUNK = 256
FFN_ROWS = 64
VMEM_LIMIT = 56 * 1024 * 1024

_NT = (((1,), (1,)), ((), ()))


def _dot(a, b):
    return jnp.dot(a, b, preferred_element_type=F32)


def _dot_nt(a, b):
    return lax.dot_general(a, b, _NT, preferred_element_type=F32)


def _halves(dot, a, b):
    h = a.shape[0] // 2
    return jnp.concatenate([dot(a[:h], b), dot(a[h:], b)], axis=0)


def _split_bf16(x):
    hi = x.astype(BF16)
    lo = (x - hi.astype(F32)).astype(BF16)
    return hi, lo


def _gelu_tanh(x):
    return 0.5 * x * (1.0 + jnp.tanh(0.7978845608028654 * (x + 0.044715 * (x * x * x))))


def _sigmoid(x):
    return 0.5 * (1.0 + jnp.tanh(0.5 * x))


def _group_mean_sq(t, ones_blockdiag, width):
    t2 = t * t
    hi, lo = _split_bf16(t2)
    return (_dot(hi, ones_blockdiag) + _dot(lo, ones_blockdiag)) * (1.0 / width)


def _proj_kernel(x_ref, an_ref, wq_ref, wkv_ref, wg_ref, wuv_ref, qn_ref, ksn_ref, kwn_ref,
                 sgun_ref, sguw_ref, sgub_ref, bd_ref,
                 q_out, kc_out, vc_out, ks_out, vs_out, kw_out, vw_out, g_out, sgu_out):
    tm = x_ref.shape[1]
    x = x_ref[0]
    ms = jnp.mean(x * x, axis=-1, keepdims=True)
    h = (x * lax.rsqrt(ms + EPS) * an_ref[...]).astype(BF16)

    bd = bd_ref[...]
    q = _dot(h, wq_ref[...])
    qn = q * lax.rsqrt(_group_mean_sq(q, bd, D_HEAD) + EPS) * qn_ref[...]
    for hh in range(NSA_HEADS):
        q_out[0, hh] = qn[:, hh * D_HEAD:(hh + 1) * D_HEAD].astype(BF16)

    kv = _dot(h, wkv_ref[...])
    bd_kv = bd[:D_KV, :D_KV]
    kc = kv[:, 0 * D_KV:1 * D_KV]
    vc = kv[:, 1 * D_KV:2 * D_KV]
    ks = kv[:, 2 * D_KV:3 * D_KV]
    vs = kv[:, 3 * D_KV:4 * D_KV]
    kw = kv[:, 4 * D_KV:5 * D_KV]
    vw = kv[:, 5 * D_KV:6 * D_KV]
    ks = ks * lax.rsqrt(_group_mean_sq(ks, bd_kv, D_HEAD) + EPS) * ksn_ref[...]
    kw = kw * lax.rsqrt(_group_mean_sq(kw, bd_kv, D_HEAD) + EPS) * kwn_ref[...]
    for t, o_ref in ((kc, kc_out), (vc, vc_out), (ks, ks_out), (vs, vs_out), (kw, kw_out), (vw, vw_out)):
        for g in range(NSA_KV_HEADS):
            o_ref[0, g] = t[:, g * D_HEAD:(g + 1) * D_HEAD].astype(BF16)

    gates = _sigmoid(_dot(h, wg_ref[...]))
    for g in range(NSA_KV_HEADS):
        g_out[0, g] = gates[:, g * LANES:(g + 1) * LANES]

    uv = _gelu_tanh(_dot(h, wuv_ref[...]))
    u = uv[:, :D_GMLP]
    v = uv[:, D_GMLP:]
    vms = jnp.mean(v * v, axis=-1, keepdims=True)
    vb = (v * lax.rsqrt(vms + EPS) * sgun_ref[...]).astype(BF16)

    row = lax.broadcasted_iota(jnp.int32, (CHUNK, CHUNK), 0)
    col = lax.broadcasted_iota(jnp.int32, (CHUNK, CHUNK), 1)
    tril = col <= row
    w_tril = [jnp.where(tril, sguw_ref[g], 0.0).astype(BF16) for g in range(GMLP_GROUPS)]
    first_half = lax.broadcasted_iota(jnp.int32, (CHUNK, LANES), 1) < GMLP_GROUP_DIM
    for c in range(tm // CHUNK):
        rows = slice(c * CHUNK, (c + 1) * CHUNK)
        zs = []
        for p in range(D_GMLP // LANES):
            blk = vb[rows, p * LANES:(p + 1) * LANES]
            z0 = _dot(w_tril[2 * p], blk)
            z1 = _dot(w_tril[2 * p + 1], blk)
            zs.append(jnp.where(first_half, z0, z1))
        z = jnp.concatenate(zs, axis=1) + sgub_ref[...]
        sgu_out[0, rows, :] = (u[rows, :] * z).astype(BF16)


def _proj_call(x, attn_norm, w_q, w_kv, w_g, w_uv, qn, ksn, kwn, sgu_norm, sgu_w, sgu_bt, bd, tm):
    B, T, _ = x.shape
    const2 = lambda b, i: (0, 0)
    const3 = lambda b, i: (0, 0, 0)
    head_spec = lambda nh: pl.BlockSpec((1, nh, tm, D_HEAD), lambda b, i: (b, 0, i, 0))
    kv_shape = jax.ShapeDtypeStruct((B, NSA_KV_HEADS, T, D_HEAD), BF16)
    return pl.pallas_call(
        _proj_kernel,
        grid=(B, T // tm),
        in_specs=[
            pl.BlockSpec((1, tm, D_MODEL), lambda b, i: (b, i, 0)),
            pl.BlockSpec((1, D_MODEL), const2),
            pl.BlockSpec(w_q.shape, const2),
            pl.BlockSpec(w_kv.shape, const2),
            pl.BlockSpec(w_g.shape, const2),
            pl.BlockSpec(w_uv.shape, const2),
            pl.BlockSpec((1, D_NSA), const2),
            pl.BlockSpec((1, D_KV), const2),
            pl.BlockSpec((1, D_KV), const2),
            pl.BlockSpec((1, D_GMLP), const2),
            pl.BlockSpec(sgu_w.shape, const3),
            pl.BlockSpec(sgu_bt.shape, const2),
            pl.BlockSpec(bd.shape, const2),
        ],
        out_specs=[
            head_spec(NSA_HEADS),
            head_spec(NSA_KV_HEADS), head_spec(NSA_KV_HEADS), head_spec(NSA_KV_HEADS),
            head_spec(NSA_KV_HEADS), head_spec(NSA_KV_HEADS), head_spec(NSA_KV_HEADS),
            pl.BlockSpec((1, NSA_KV_HEADS, tm, LANES), lambda b, i: (b, 0, i, 0)),
            pl.BlockSpec((1, tm, D_GMLP), lambda b, i: (b, i, 0)),
        ],
        out_shape=[
            jax.ShapeDtypeStruct((B, NSA_HEADS, T, D_HEAD), BF16),
            kv_shape, kv_shape, kv_shape, kv_shape, kv_shape, kv_shape,
            jax.ShapeDtypeStruct((B, NSA_KV_HEADS, T, LANES), F32),
            jax.ShapeDtypeStruct((B, T, D_GMLP), BF16),
        ],
        compiler_params=pltpu.CompilerParams(
            dimension_semantics=("arbitrary", "arbitrary"), vmem_limit_bytes=VMEM_LIMIT),
        name="proj",
    )(x, attn_norm, w_q, w_kv, w_g, w_uv, qn, ksn, kwn, sgu_norm, sgu_w, sgu_bt, bd)


def _compress_one(tok_ref, pe_ref, w1_ref, b1_ref, w2_ref):
    half = CMP_STRIDE * D_HEAD
    tok = tok_ref[0, 0].astype(F32)
    top = (tok + pe_ref[0:1, :]).astype(BF16)
    bot = (tok + pe_ref[1:2, :]).astype(BF16)
    a = _dot(top, w1_ref[:half, :])
    b = _dot(bot, w1_ref[half:, :])
    ncp = a.shape[0]
    pre = a + pltpu.roll(b, ncp - 1, 0) + b1_ref[...]
    hid = _gelu_tanh(pre).astype(BF16)
    return _dot(hid, w2_ref[...])


def _compress_kernel(kc_ref, vc_ref, pek_ref, w1k_ref, b1k_ref, w2k_ref, kn_ref,
                     pev_ref, w1v_ref, b1v_ref, w2v_ref, k_out, v_out):
    k = _compress_one(kc_ref, pek_ref, w1k_ref, b1k_ref, w2k_ref)
    kms = jnp.mean(k * k, axis=-1, keepdims=True)
    k_out[0, 0] = (k * lax.rsqrt(kms + EPS) * kn_ref[...]).astype(BF16)
    v_out[0, 0] = _compress_one(vc_ref, pev_ref, w1v_ref, b1v_ref, w2v_ref).astype(BF16)


def _compress_call(kc, vc, pek, w1k, b1k, w2k, kn, pev, w1v, b1v, w2v):
    B, G, ncp, width = kc.shape
    const2 = lambda b, g: (0, 0)
    tok_spec = pl.BlockSpec((1, 1, ncp, width), lambda b, g: (b, g, 0, 0))
    out_spec = pl.BlockSpec((1, 1, ncp, D_HEAD), lambda b, g: (b, g, 0, 0))
    full = lambda a: pl.BlockSpec(a.shape, const2)
    out_shape = jax.ShapeDtypeStruct((B, G, ncp, D_HEAD), BF16)
    return pl.pallas_call(
        _compress_kernel,
        grid=(B, G),
        in_specs=[tok_spec, tok_spec, full(pek), full(w1k), full(b1k), full(w2k), full(kn),
                  full(pev), full(w1v), full(b1v), full(w2v)],
        out_specs=[out_spec, out_spec],
        out_shape=[out_shape, out_shape],
        compiler_params=pltpu.CompilerParams(
            dimension_semantics=("arbitrary", "arbitrary"), vmem_limit_bytes=VMEM_LIMIT),
        name="compress",
    )(kc, vc, pek, w1k, b1k, w2k, kn, pev, w1v, b1v, w2v)


def _rel_bucket_np(dist):
    max_exact = REL_BUCKETS // 2
    d = np.maximum(dist, 1).astype(np.float32)
    log_b = max_exact + (np.log(d / np.float32(max_exact)) / np.float32(math.log(REL_MAX_DIST / max_exact))
                         * np.float32(REL_BUCKETS - max_exact)).astype(np.int32)
    log_b = np.clip(log_b, max_exact, REL_BUCKETS - 1)
    return np.where(dist < max_exact, np.maximum(dist, 0), log_b)


def _bias_tables(rel_bias):
    rb = rel_bias.astype(F32) * LOG2E
    qq = np.arange(SEL_BLOCK)[:, None]
    kp = np.arange(SEL_BLOCK)[None, :]
    tile_d = [m * SEL_BLOCK + qq - kp for m in range(N_TOEPLITZ)]
    s_ = np.arange(SUB)[:, None, None]
    e_ = np.arange(CBAND)[None, None, :]
    dist_c = qq[None, :, :] + CMP_STRIDE * (e_ - 4 * (SUB - 1) + 4 * s_) - (3 * CMP_STRIDE + CMP_BLOCK - 1)
    all_d = np.concatenate([d.reshape(-1) for d in tile_d] + [dist_c.reshape(-1)])
    onehot = np.eye(REL_BUCKETS, dtype=np.float32)[_rel_bucket_np(all_d)]
    vals = jnp.dot(jnp.asarray(onehot), rb, precision=lax.Precision.HIGHEST)
    vals = jnp.where(jnp.asarray(all_d >= 0)[:, None], vals, NEG)
    n_t = N_TOEPLITZ * SEL_BLOCK * SEL_BLOCK
    toep = jnp.transpose(vals[:n_t].reshape(N_TOEPLITZ, SEL_BLOCK, SEL_BLOCK, NSA_HEADS), (0, 3, 1, 2))
    band = jnp.transpose(vals[n_t:].reshape(SUB, SEL_BLOCK, CBAND, NSA_HEADS), (3, 0, 1, 2))
    far = rb[REL_BUCKETS - 1]
    far_tile = jnp.broadcast_to(far[:, None, None], (NSA_HEADS, SEL_BLOCK, SEL_BLOCK))
    neg_tile = jnp.full((NSA_HEADS, SEL_BLOCK, SEL_BLOCK), NEG, F32)
    edge_tile = jnp.where(jnp.asarray(kp > qq)[None], far_tile, NEG)

    def tile(m, windowed):
        if m < 0 or (windowed and m > WIN_BLOCKS):
            return neg_tile
        if windowed and m == WIN_BLOCKS:
            return edge_tile
        return toep[m] if m < N_TOEPLITZ else far_tile

    def table(first_m, n_blocks, windowed):
        rows = [jnp.concatenate([tile(first_m + s - c, windowed) for c in range(n_blocks)], axis=2)
                for s in range(SUB)]
        t = jnp.concatenate(rows, axis=1)
        return t.reshape(NSA_KV_HEADS, ROWS, n_blocks * SEL_BLOCK)

    far_rows = jnp.repeat(far, TQ).reshape(NSA_KV_HEADS, ROWS, 1)
    near_t = table(BLOCKS_PER_TILE, 2 * BLOCKS_PER_TILE, False) - far_rows
    win_t = table(WIN_BLOCKS, WIN_BLOCKS + BLOCKS_PER_TILE, True)
    far_t = jnp.broadcast_to(far_rows, (NSA_KV_HEADS, ROWS, LANES))
    band_t = band.reshape(NSA_KV_HEADS, ROWS, CBAND)
    band_hi = band_t.astype(BF16)
    band_lo = (band_t - band_hi.astype(F32)).astype(BF16)
    far_hi = far_rows.astype(BF16)
    far_lo = (far_rows - far_hi.astype(F32)).astype(BF16)
    neg_col = jnp.full((NSA_KV_HEADS, ROWS, 1), NEG, BF16)
    zeros = jnp.zeros((NSA_KV_HEADS, ROWS, D_HEAD - 2 * CBAND - 3), BF16)
    q_lanes = jnp.zeros((NSA_KV_HEADS, ROWS, D_HEAD), BF16)
    cmp_lhs = jnp.concatenate([q_lanes, band_hi, band_lo, far_hi, far_lo, neg_col, zeros], axis=2)
    return near_t, win_t, far_t, cmp_lhs


def _row_max(lane_tiles):
    m = jnp.max(functools.reduce(jnp.maximum, lane_tiles), axis=1, keepdims=True)
    return jnp.broadcast_to(m, lane_tiles[0].shape)


def _nsa_kernel(q_ref, kc_ref, vc_ref, ks_ref, vs_ref, kw_ref, vw_ref, g_ref,
                near_ref, win_ref, far_ref, cmpl_ref, ovt_ref, eye_ref,
                o_ref, ksa, vsa, kwa, vwa, kca, vca, sc_scr, lhs_scr, lhsc_scr, m_scr, al_scr, inv_scr, acc_scr, sa_scr, sb_scr, pa_scr, pb_scr,
                sw_scr, pw_scr, ps_scr, out_scr):
    qt = pl.program_id(2)
    first = qt * SUB
    T = ks_ref.shape[2]
    ncp = kc_ref.shape[2]

    @pl.when(qt == 0)
    def _init():
        n_chunks = ksa.shape[0] // KT
        r_io = lax.broadcasted_iota(jnp.int32, (KT, KAUG), 0)
        l_io = lax.broadcasted_iota(jnp.int32, (KT, KAUG), 1)
        lv = lax.broadcasted_iota(jnp.int32, (KT, VAUG), 1)
        vpat = jnp.where(lv == D_HEAD, 1.0, 0.0).astype(BF16)

        def fill(c, carry):
            rows = pl.ds(pl.multiple_of(c * KT, KT), KT)
            key = r_io + (c * KT - KPAD)
            blk = key // SEL_BLOCK
            real = (key >= 0) & (key < T)
            hot = real & (((l_io < D_HEAD) & (l_io == blk))
                          | ((l_io >= D_HEAD) & (l_io < 2 * D_HEAD) & (l_io - D_HEAD == blk)))
            hot = hot | ((~real) & (l_io == 3 * D_HEAD))
            pat = jnp.where(hot, 1.0, 0.0).astype(BF16)
            ksa[rows, :] = pat
            kwa[rows, :] = pat
            vsa[rows, :] = vpat
            vwa[rows, :] = vpat
            return carry

        lax.fori_loop(0, n_chunks, fill, 0)
        ksa[KPAD:KPAD + T, 2 * D_HEAD:3 * D_HEAD] = ks_ref[0, 0]
        kwa[KPAD:KPAD + T, 2 * D_HEAD:3 * D_HEAD] = kw_ref[0, 0]
        vsa[KPAD:KPAD + T, 0:D_HEAD] = vs_ref[0, 0]
        vwa[KPAD:KPAD + T, 0:D_HEAD] = vw_ref[0, 0]
        lc = lax.broadcasted_iota(jnp.int32, (ncp, VAUG), 1)
        kca[...] = jnp.zeros((ncp, VAUG), BF16)
        kca[:, 0:D_HEAD] = kc_ref[0, 0]
        vca[...] = jnp.where(lc == D_HEAD, 1.0, 0.0).astype(BF16)
        vca[:, 0:D_HEAD] = vc_ref[0, 0]

    q4 = q_ref[0].reshape(ROWS, D_HEAD)
    gates = g_ref[0, 0]
    row_blocks = [slice(r, r + RB) for r in range(0, ROWS, RB)]

    gate_rows = jnp.concatenate(
        [gates if hh == 0 else pltpu.roll(gates, LANES - hh * N_BRANCH, 1) for hh in range(NSA_GROUP)], axis=0)

    def gate_col(branch):
        return gate_rows[:, branch:branch + 1]

    lane_c = lax.broadcasted_iota(jnp.int32, (ncp, LANES), 1)
    n_c = lax.broadcasted_iota(jnp.int32, (ncp, LANES), 0)
    e = lane_c - D_HEAD
    base = 4 * first + (4 * SUB - 1)
    ind = (((e >= 0) & (e < CBAND) & (n_c == base - e))
           | ((e >= CBAND) & (e < 2 * CBAND) & (n_c == base - (e - CBAND)))
           | (((e == 2 * CBAND) | (e == 2 * CBAND + 1)) & (n_c < base - (CBAND - 1)))
           | ((e == 2 * CBAND + 2) & (n_c > base)))
    rhs_c = jnp.where(lane_c < D_HEAD, kca[...], jnp.where(ind, 1.0, 0.0).astype(BF16))
    lhsc_scr[...] = cmpl_ref[0]
    lhsc_scr[:, 0:D_HEAD] = q4
    sa_scr[:, 0:ncp] = _halves(_dot_nt, lhsc_scr[...], rhs_c)

    lane_q = lax.broadcasted_iota(jnp.int32, (ROWS, LANES), 1)
    lhs_scr[:, LANES:2 * LANES] = jnp.where(lane_q == D_HEAD, NEG, 0.0).astype(BF16)
    lhs_scr[:, LANES:LANES + D_HEAD] = q4
    lhs_win = jnp.concatenate([jnp.zeros((ROWS, LANES), BF16), lhs_scr[:, LANES:2 * LANES]], axis=1)
    win_start = pl.multiple_of(KPAD + (first - WIN_BLOCKS) * SEL_BLOCK, KT)
    sw_scr[...] = _halves(_dot_nt, lhs_win, kwa[pl.ds(win_start, KS + KT), :])

    for i, rows in enumerate(row_blocks):
        s = [sa_scr[rows, c * LANES:(c + 1) * LANES] for c in range(ncp // LANES)]
        m = _row_max(s)
        p = [jnp.where(t > 0.5 * NEG, jnp.exp2(t - m), 0.0) for t in s]
        l = jnp.sum(functools.reduce(jnp.add, p), axis=1, keepdims=True)
        inv = jnp.where(l > 0.0, 1.0 / l, 0.0)
        inv_scr[rows] = inv
        inv_b = jnp.broadcast_to(inv, (RB, LANES))
        tok = slice((i % (TQ // RB)) * RB, (i % (TQ // RB) + 1) * RB)
        for c in range(ncp // LANES):
            lanes = slice(c * LANES, (c + 1) * LANES)
            pb_scr[rows, lanes] = p[c].astype(BF16)
            if i < TQ // RB:
                ps_scr[tok, lanes] = p[c] * inv_b
            else:
                ps_scr[tok, lanes] += p[c] * inv_b
    acc_c = _halves(_dot, pb_scr[:, 0:ncp], vca[...])
    out_scr[...] = acc_c[:, :D_HEAD] * (gate_col(0) * inv_scr[...])
    ps_hi, ps_lo = _split_bf16(ps_scr[...])
    ovt = ovt_ref[...]
    imp_t = _dot_nt(ovt, ps_hi) + _dot_nt(ovt, ps_lo)

    j_t = lax.broadcasted_iota(jnp.int32, (MAX_BLOCKS, TQ), 0)
    cur = first + lax.broadcasted_iota(jnp.int32, (MAX_BLOCKS, TQ), 1) // SEL_BLOCK
    forced = (j_t == 0) | (j_t == cur) | (j_t == cur - 1)
    score_t = jnp.where(forced, FORCED_SCORE, jnp.where(j_t <= cur, imp_t, NEG))
    sc_scr[...] = score_t

    def count(u, cnt):
        for k in range(SUB):
            i = SUB * u + k
            row_i = sc_scr[pl.ds(i, 1), :]
            beats = (row_i > score_t) | ((row_i == score_t) & (j_t > i))
            cnt = cnt + beats.astype(jnp.int32)
        return cnt

    cnt = lax.fori_loop(0, qt + 1, count, jnp.zeros((MAX_BLOCKS, TQ), jnp.int32))
    sel_t = jnp.where((cnt < SEL_TOPK) & (j_t <= cur), 1.0, 0.0).astype(BF16)
    sel = _dot_nt(eye_ref[...], jnp.concatenate([sel_t, sel_t], axis=0))
    sel4 = jnp.concatenate([sel] * NSA_GROUP, axis=0) > 0.5

    hi, lo = _split_bf16(jnp.where(sel4, far_ref[0], NEG))
    lhs_scr[:, 0:LANES] = jnp.where(lane_q < D_HEAD, hi, lo)
    m_scr[...] = jnp.full((ROWS, LANES), NEG, F32)
    acc_scr[...] = jnp.zeros((ROWS, VAUG), F32)

    def scores(w, dst):
        start = pl.multiple_of(KPAD + w * KS, KS)
        dst[...] = _halves(_dot_nt, lhs_scr[...], ksa[pl.ds(start, KS), :])

    def consume(w, src, p_scr, with_table):
        n_lt = KS // LANES
        if with_table:
            rel = [2 * w + half - (qt - 1) for half in range(KS // KT)]
            col = [jnp.clip(r, 0, 1) * KT for r in rel]
        for rows in row_blocks:
            s = [src[rows, c * LANES:(c + 1) * LANES] for c in range(n_lt)]
            if with_table:
                for c in range(n_lt):
                    half, off = divmod(c * LANES, KT)
                    at = pl.multiple_of(col[half] + off, LANES)
                    s[c] = s[c] + jnp.where(rel[half] >= 0, near_ref[0, rows, pl.ds(at, LANES)], 0.0)
            m_prev = m_scr[rows]
            m = jnp.maximum(m_prev, _row_max(s))
            al_scr[rows] = jnp.exp2(m_prev - m)
            m_scr[rows] = m
            for c in range(n_lt):
                p_scr[rows, c * LANES:(c + 1) * LANES] = jnp.exp2(s[c] - m).astype(BF16)
        start = pl.multiple_of(KPAD + w * KS, KS)
        acc_scr[...] = acc_scr[...] * al_scr[...] + _dot(p_scr[:, 0:KS], vsa[pl.ds(start, KS), :])

    scores(0, sa_scr)

    for rows in row_blocks:
        s = [sw_scr[rows, c * LANES:(c + 1) * LANES] + win_ref[0, rows, c * LANES:(c + 1) * LANES]
             for c in range((KS + KT) // LANES)]
        m = _row_max(s)
        for c in range((KS + KT) // LANES):
            pw_scr[rows, c * LANES:(c + 1) * LANES] = jnp.exp2(s[c] - m).astype(BF16)
    acc_w = _halves(_dot, pw_scr[...], vwa[pl.ds(win_start, KS + KT), :])
    out_scr[...] += acc_w[:, :D_HEAD] * (gate_col(2) / acc_w[:, D_HEAD:D_HEAD + 1])

    def pair(with_table, u, carry):
        scores(2 * u + 1, sb_scr)
        consume(2 * u, sa_scr, pa_scr, with_table)
        scores(2 * u + 2, sa_scr)
        consume(2 * u + 1, sb_scr, pb_scr, with_table)
        return carry

    n_trips = qt // 4 + 1
    n_plain = jnp.maximum(qt - 1, 0) // 4
    lax.fori_loop(0, n_plain, functools.partial(pair, False), 0)
    lax.fori_loop(n_plain, n_trips, functools.partial(pair, True), 0)
    acc_s = acc_scr[...]

    o = out_scr[...] + acc_s[:, :D_HEAD] * (gate_col(1) / acc_s[:, D_HEAD:D_HEAD + 1])
    for hh in range(NSA_GROUP):
        o_ref[0, :, hh * D_HEAD:(hh + 1) * D_HEAD] = o[hh * TQ:(hh + 1) * TQ].astype(BF16)


def _nsa_call(q, kcmp, vcmp, ks, vs, kw, vw, gates, near_t, win_t, far_t, cmp_lhs, ovt, eye):
    B, _, T, _ = q.shape
    G = NSA_KV_HEADS
    ncp = kcmp.shape[2]
    rows_kv = KPAD + T + KS
    per_bg =lambda a: pl.BlockSpec((1, 1) + a.shape[2:], lambda b, g, i: (b, g, 0, 0))
    per_g = lambda a: pl.BlockSpec((1,) + a.shape[1:], lambda b, g, i: (g, 0, 0))
    const2 = lambda a: pl.BlockSpec(a.shape, lambda b, g, i: (0, 0))
    return pl.pallas_call(
        _nsa_kernel,
        grid=(B, G, T // TQ),
        in_specs=[
            pl.BlockSpec((1, NSA_GROUP, TQ, D_HEAD), lambda b, g, i: (b, g, i, 0)),
            per_bg(kcmp), per_bg(vcmp), per_bg(ks), per_bg(vs), per_bg(kw), per_bg(vw),
            pl.BlockSpec((1, 1, TQ, LANES), lambda b, g, i: (b, g, i, 0)),
            per_g(near_t), per_g(win_t), per_g(far_t), per_g(cmp_lhs), const2(ovt), const2(eye),
        ],
        out_specs=pl.BlockSpec((1, TQ, NSA_GROUP * D_HEAD), lambda b, g, i: (b, i, g)),
        out_shape=jax.ShapeDtypeStruct((B, T, D_NSA), BF16),
        scratch_shapes=[
            pltpu.VMEM((rows_kv, KAUG), BF16), pltpu.VMEM((rows_kv, VAUG), BF16),
            pltpu.VMEM((rows_kv, KAUG), BF16), pltpu.VMEM((rows_kv, VAUG), BF16),
            pltpu.VMEM((ncp, VAUG), BF16), pltpu.VMEM((ncp, VAUG), BF16),
            pltpu.VMEM((MAX_BLOCKS, TQ), F32),
            pltpu.VMEM((ROWS, KAUG), BF16), pltpu.VMEM((ROWS, LANES), BF16),
            pltpu.VMEM((ROWS, LANES), F32), pltpu.VMEM((ROWS, LANES), F32), pltpu.VMEM((ROWS, 1), F32),
            pltpu.VMEM((ROWS, VAUG), F32),
            pltpu.VMEM((ROWS, KS), F32), pltpu.VMEM((ROWS, KS), F32),
            pltpu.VMEM((ROWS, KS), BF16), pltpu.VMEM((ROWS, KS), BF16),
            pltpu.VMEM((ROWS, KS + KT), F32), pltpu.VMEM((ROWS, KS + KT), BF16),
            pltpu.VMEM((TQ, ncp), F32), pltpu.VMEM((ROWS, D_HEAD), F32),
        ],
        compiler_params=pltpu.CompilerParams(
            dimension_semantics=("arbitrary", "arbitrary", "arbitrary"), vmem_limit_bytes=VMEM_LIMIT),
        name="nsa",
    )(q, kcmp, vcmp, ks, vs, kw, vw, gates, near_t, win_t, far_t, cmp_lhs, ovt, eye)


def _ffn_kernel(x_ref, on_ref, os_ref, wo_n_ref, wo_s_ref, fn_ref, wup_ref, cw_ref, cb_ref, wdn_ref,
                out_ref, prev_ref, h_ref, ua_scr, ub_scr, act_ref):
    tm = x_ref.shape[1]
    n_chunks = wup_ref.shape[1]

    @pl.when(pl.program_id(1) == 0)
    def _start_of_sequence():
        prev_ref[...] = jnp.zeros(prev_ref.shape, F32)

    x1 = x_ref[0] + _dot(on_ref[0], wo_n_ref[...]) + _dot(os_ref[0], wo_s_ref[...])
    ms = jnp.mean(x1 * x1, axis=-1, keepdims=True)
    h_ref[...] = (x1 * lax.rsqrt(ms + EPS) * fn_ref[...]).astype(BF16)
    out_ref[0] = x1

    def up(c, dst):
        h = h_ref[...]
        dst[0] = _dot(h, wup_ref[0, c])
        dst[1] = _dot(h, wup_ref[1, c])

    def conv_gate(c, src):
        for r0 in range(0, tm, FFN_ROWS):
            ys = []
            for half in range(2):
                w = cw_ref[half, c]
                if r0 == 0:
                    ext = jnp.concatenate([prev_ref[half, c], src[half, 0:FFN_ROWS]], axis=0)
                else:
                    ext = src[half, r0 - 8:r0 + FFN_ROWS]
                y = (ext * w[2:3] + pltpu.roll(ext, 1, 0) * w[1:2] + pltpu.roll(ext, 2, 0) * w[0:1]
                     + cb_ref[half, c])
                ys.append(y[8:])
            a, g = ys
            col = pl.multiple_of(c * FFN_CHUNK, FFN_CHUNK)
            act_ref[r0:r0 + FFN_ROWS, pl.ds(col, FFN_CHUNK)] = (g * _sigmoid(g) * a).astype(BF16)
        for half in range(2):
            prev_ref[half, c] = src[half, tm - 8:tm]

    up(0, ua_scr)

    def pair(u, carry):
        up(2 * u + 1, ub_scr)
        conv_gate(2 * u, ua_scr)
        up(2 * u + 2, ua_scr)
        conv_gate(2 * u + 1, ub_scr)
        return carry

    lax.fori_loop(0, (n_chunks - 1) // 2, pair, 0)
    done = (n_chunks - 1) * FFN_CHUNK
    out_ref[0] += _dot(act_ref[:, 0:done], wdn_ref[0:done, :])
    conv_gate(n_chunks - 1, ua_scr)
    out_ref[0] += _dot(act_ref[:, done:], wdn_ref[done:, :])


def _ffn_call(x, o_nsa, o_sgu, wo_n, wo_s, ffn_norm, w_up, conv_w, conv_b, w_down, tm):
    B, T, _ = x.shape
    n_chunks = w_up.shape[1]
    assert n_chunks % 2 == 1
    resident =lambda a: pl.BlockSpec(a.shape, lambda b, i: (0,) * a.ndim, pipeline_mode=pl.Buffered(1))
    tile = lambda w: pl.BlockSpec((1, tm, w), lambda b, i: (b, i, 0))
    return pl.pallas_call(
        _ffn_kernel,
        grid=(B, T // tm),
        in_specs=[tile(D_MODEL), tile(D_NSA), tile(D_GMLP), resident(wo_n), resident(wo_s),
                  resident(ffn_norm), resident(w_up), resident(conv_w), resident(conv_b), resident(w_down)],
        out_specs=tile(D_MODEL),
        out_shape=jax.ShapeDtypeStruct((B, T, D_MODEL), F32),
        scratch_shapes=[pltpu.VMEM((2, n_chunks, 8, FFN_CHUNK), F32), pltpu.VMEM((tm, D_MODEL), BF16),
                        pltpu.VMEM((2, tm, FFN_CHUNK), F32), pltpu.VMEM((2, tm, FFN_CHUNK), F32),
                        pltpu.VMEM((tm, D_FF), BF16)],
        compiler_params=pltpu.CompilerParams(
            dimension_semantics=("arbitrary", "arbitrary"), vmem_limit_bytes=VMEM_LIMIT),
        name="ffn",
    )(x, o_nsa, o_sgu, wo_n, wo_s, ffn_norm, w_up, conv_w, conv_b, w_down)


def _mixers(x, rel_bias, attn_norm, w_in, q_norm, k_norm_cmp, k_norm_slc, k_norm_win,
            cmp_pe_k, cmp_w1_k, cmp_b1_k, cmp_w2_k, cmp_pe_v, cmp_w1_v, cmp_b1_v, cmp_w2_v,
            sgu_norm, sgu_w, sgu_b):
    B, T, _ = x.shape
    assert T % 512 == 0 and T // SEL_BLOCK <= MAX_BLOCKS and (T // CMP_STRIDE) % LANES == 0
    ncp = T // CMP_STRIDE
    G, R = NSA_KV_HEADS, NSA_GROUP

    o_kv = D_NSA
    o_g = D_NSA + 6 * D_KV
    o_uv = o_g + N_GATES
    w_q = w_in[:, :o_kv].astype(BF16)
    w_kv = w_in[:, o_kv:o_g].astype(BF16)
    per_group = NSA_GROUP * N_BRANCH
    w_g = jnp.pad(w_in[:, o_g:o_uv].reshape(D_MODEL, G, per_group), ((0, 0), (0, 0), (0, LANES - per_group)))
    w_g = w_g.reshape(D_MODEL, G * LANES).astype(BF16)
    w_uv = w_in[:, o_uv:].astype(BF16)
    qn = (jnp.tile(q_norm, NSA_HEADS) * (D_HEAD ** -0.5 * LOG2E)).reshape(1, D_NSA)
    ksn = jnp.tile(k_norm_slc, G).reshape(1, D_KV)
    kwn = jnp.tile(k_norm_win, G).reshape(1, D_KV)
    sgu_bt = jnp.repeat(sgu_b.T, GMLP_GROUP_DIM, axis=1)
    grp = np.arange(D_NSA) // D_HEAD
    bd = jnp.asarray(grp[:, None] == grp[None, :], BF16)

    q, kc, vc, ks, vs, kw, vw, gates, o_sgu = _proj_call(
        x, attn_norm.reshape(1, D_MODEL), w_q, w_kv, w_g, w_uv, qn, ksn, kwn,
        sgu_norm.reshape(1, D_GMLP), sgu_w, sgu_bt, bd, tm=512)

    half = CMP_STRIDE * D_HEAD
    kcmp, vcmp = _compress_call(
        kc.reshape(B, G, ncp, half), vc.reshape(B, G, ncp, half),
        cmp_pe_k.reshape(2, half), cmp_w1_k.astype(BF16), cmp_b1_k.reshape(1, CMP_HIDDEN),
        cmp_w2_k.astype(BF16), k_norm_cmp.reshape(1, D_HEAD),
        cmp_pe_v.reshape(2, half), cmp_w1_v.astype(BF16), cmp_b1_v.reshape(1, CMP_HIDDEN),
        cmp_w2_v.astype(BF16))

    near_t, win_t, far_t, cmp_lhs = _bias_tables(rel_bias)
    jj = np.arange(MAX_BLOCKS)[:, None]
    nn = np.arange(ncp)[None, :]
    ovt = jnp.asarray((nn >= 4 * jj - 1) & (nn <= 4 * jj + 3), BF16)
    eye = jnp.asarray(np.eye(TQ), BF16)

    o_nsa = _nsa_call(q, kcmp, vcmp, ks, vs, kw, vw, gates, near_t, win_t, far_t, cmp_lhs, ovt, eye)
    return o_nsa, o_sgu


def _layer(x, rel_bias, attn_norm, w_in, q_norm, k_norm_cmp, k_norm_slc, k_norm_win,
           cmp_pe_k, cmp_w1_k, cmp_b1_k, cmp_w2_k, cmp_pe_v, cmp_w1_v, cmp_b1_v, cmp_w2_v,
           sgu_norm, sgu_w, sgu_b, w_out, ffn_norm, w_up, conv_w, conv_b, w_down):
    o_nsa, o_sgu = _mixers(x, rel_bias, attn_norm, w_in, q_norm, k_norm_cmp, k_norm_slc, k_norm_win,
                           cmp_pe_k, cmp_w1_k, cmp_b1_k, cmp_w2_k, cmp_pe_v, cmp_w1_v, cmp_b1_v, cmp_w2_v,
                           sgu_norm, sgu_w, sgu_b)
    wo = w_out.astype(BF16)
    n_chunks = D_FF // FFN_CHUNK
    wu = jnp.transpose(w_up.astype(BF16).reshape(D_MODEL, 2, n_chunks, FFN_CHUNK), (1, 2, 0, 3))
    cw = jnp.transpose(conv_w.reshape(conv_w.shape[0], 2, n_chunks, FFN_CHUNK), (1, 2, 0, 3))
    cb = conv_b.reshape(2, n_chunks, 1, FFN_CHUNK)
    wd = w_down.astype(BF16)
    return _ffn_call(x, o_nsa, o_sgu, wo[:D_NSA], wo[D_NSA:], ffn_norm.reshape(1, D_MODEL), wu, cw, cb, wd, tm=512)


def kernel(x, rel_bias, attn_norm, w_in, q_norm, k_norm_cmp, k_norm_slc, k_norm_win,
           cmp_pe_k, cmp_w1_k, cmp_b1_k, cmp_w2_k, cmp_pe_v, cmp_w1_v, cmp_b1_v, cmp_w2_v,
           sgu_norm, sgu_w, sgu_b, w_out, ffn_norm, w_up, conv_w, conv_b, w_down):
    depth = attn_norm.shape[0]
    for l in range(depth):
        x = _layer(x, rel_bias, attn_norm[l], w_in[l], q_norm[l], k_norm_cmp[l], k_norm_slc[l], k_norm_win[l],
                   cmp_pe_k[l], cmp_w1_k[l], cmp_b1_k[l], cmp_w2_k[l], cmp_pe_v[l], cmp_w1_v[l], cmp_b1_v[l],
                   cmp_w2_v[l], sgu_norm[l], sgu_w[l], sgu_b[l], w_out[l], ffn_norm[l], w_up[l], conv_w[l],
                   conv_b[l], w_down[l])
    return x
```

```python
import functools
import math

import numpy as np
import jax
import jax.numpy as jnp
from jax import lax
from jax.experimental import pallas as pl
from jax.experimental.pallas import tpu as pltpu

F32 = jnp.float32
BF16 = jnp.bfloat16

D_MODEL = 1024
D_HEAD = 64
NSA_HEADS = 8
NSA_KV_HEADS = 2
NSA_GROUP = NSA_HEADS // NSA_KV_HEADS
D_NSA = NSA_HEADS * D_HEAD
D_KV = NSA_KV_HEADS * D_HEAD
N_BRANCH = 3
N_GATES = NSA_HEADS * N_BRANCH
CMP_BLOCK = 32
CMP_STRIDE = 16
CMP_HIDDEN = 256
SEL_BLOCK = 64
SEL_TOPK = 16
WINDOW = 512
GMLP_GROUPS = 8
GMLP_GROUP_DIM = 64
D_GMLP = GMLP_GROUPS * GMLP_GROUP_DIM
CHUNK = 128
D_MIX = D_NSA + D_GMLP
REL_BUCKETS = 32
REL_MAX_DIST = 128
D_FF = 2816
EPS = 1e-6
NEG = -1e30
FORCED_SCORE = 1e4
LOG2E = 1.4426950408889634

LANES = 128
TQ = 256
SUB = TQ // SEL_BLOCK
ROWS = NSA_GROUP * TQ
KT = 256
KS = 2 * KT
RB = 64
BLOCKS_PER_TILE = KT // SEL_BLOCK
WIN_BLOCKS = WINDOW // SEL_BLOCK
KPAD = 512
N_TOEPLITZ = 3
MAX_BLOCKS = 64
KAUG = 256
VAUG = 128
CBAND = 28
FFN_CHUNK = 256
FFN_ROWS = 64
VMEM_LIMIT = 56 * 1024 * 1024

_NT = (((1,), (1,)), ((), ()))


def _dot(a, b):
    return jnp.dot(a, b, preferred_element_type=F32)


def _dot_nt(a, b):
    return lax.dot_general(a, b, _NT, preferred_element_type=F32)


def _halves(dot, a, b):
    h = a.shape[0] // 2
    return jnp.concatenate([dot(a[:h], b), dot(a[h:], b)], axis=0)


def _split_bf16(x):
    hi = x.astype(BF16)
    lo = (x - hi.astype(F32)).astype(BF16)
    return hi, lo


def _gelu_tanh(x):
    return 0.5 * x * (1.0 + jnp.tanh(0.7978845608028654 * (x + 0.044715 * (x * x * x))))


def _sigmoid(x):
    return 0.5 * (1.0 + jnp.tanh(0.5 * x))


def _group_mean_sq(t, ones_blockdiag, width):
    t2 = t * t
    hi, lo = _split_bf16(t2)
    return (_dot(hi, ones_blockdiag) + _dot(lo, ones_blockdiag)) * (1.0 / width)


def _proj_kernel(x_ref, an_ref, wq_ref, wkv_ref, wg_ref, wuv_ref, qn_ref, ksn_ref, kwn_ref,
                 sgun_ref, sguw_ref, sgub_ref, bd_ref,
                 q_out, kc_out, vc_out, ks_out, vs_out, kw_out, vw_out, g_out, sgu_out):
    tm = x_ref.shape[1]
    x = x_ref[0]
    ms = jnp.mean(x * x, axis=-1, keepdims=True)
    h = (x * lax.rsqrt(ms + EPS) * an_ref[...]).astype(BF16)

    bd = bd_ref[...]
    q = _dot(h, wq_ref[...])
    qn = q * lax.rsqrt(_group_mean_sq(q, bd, D_HEAD) + EPS) * qn_ref[...]
    for hh in range(NSA_HEADS):
        q_out[0, hh] = qn[:, hh * D_HEAD:(hh + 1) * D_HEAD].astype(BF16)

    kv = _dot(h, wkv_ref[...])
    bd_kv = bd[:D_KV, :D_KV]
    kc = kv[:, 0 * D_KV:1 * D_KV]
    vc = kv[:, 1 * D_KV:2 * D_KV]
    ks = kv[:, 2 * D_KV:3 * D_KV]
    vs = kv[:, 3 * D_KV:4 * D_KV]
    kw = kv[:, 4 * D_KV:5 * D_KV]
    vw = kv[:, 5 * D_KV:6 * D_KV]
    ks = ks * lax.rsqrt(_group_mean_sq(ks, bd_kv, D_HEAD) + EPS) * ksn_ref[...]
    kw = kw * lax.rsqrt(_group_mean_sq(kw, bd_kv, D_HEAD) + EPS) * kwn_ref[...]
    for t, o_ref in ((kc, kc_out), (vc, vc_out), (ks, ks_out), (vs, vs_out), (kw, kw_out), (vw, vw_out)):
        for g in range(NSA_KV_HEADS):
            o_ref[0, g] = t[:, g * D_HEAD:(g + 1) * D_HEAD].astype(BF16)

    gates = _sigmoid(_dot(h, wg_ref[...]))
    for g in range(NSA_KV_HEADS):
        g_out[0, g] = gates[:, g * LANES:(g + 1) * LANES]

    uv = _gelu_tanh(_dot(h, wuv_ref[...]))
    u = uv[:, :D_GMLP]
    v = uv[:, D_GMLP:]
    vms = jnp.mean(v * v, axis=-1, keepdims=True)
    vb = (v * lax.rsqrt(vms + EPS) * sgun_ref[...]).astype(BF16)

    row = lax.broadcasted_iota(jnp.int32, (CHUNK, CHUNK), 0)
    col = lax.broadcasted_iota(jnp.int32, (CHUNK, CHUNK), 1)
    tril = col <= row
    w_tril = [jnp.where(tril, sguw_ref[g], 0.0).astype(BF16) for g in range(GMLP_GROUPS)]
    w_pair = [jnp.concatenate([w_tril[2 * p], w_tril[2 * p + 1]], axis=1) for p in range(D_GMLP // LANES)]
    first_half = lax.broadcasted_iota(jnp.int32, (CHUNK, LANES), 1) < GMLP_GROUP_DIM
    zero = jnp.zeros((CHUNK, LANES), BF16)
    for c in range(tm // CHUNK):
        rows = slice(c * CHUNK, (c + 1) * CHUNK)
        zs = []
        for p in range(D_GMLP // LANES):
            blk = vb[rows, p * LANES:(p + 1) * LANES]
            rhs = jnp.concatenate([jnp.where(first_half, blk, zero), jnp.where(first_half, zero, blk)], axis=0)
            zs.append(_dot(w_pair[p], rhs))
        z = jnp.concatenate(zs, axis=1) + sgub_ref[...]
        sgu_out[0, rows, :] = (u[rows, :] * z).astype(BF16)


def _proj_call(x, attn_norm, w_q, w_kv, w_g, w_uv, qn, ksn, kwn, sgu_norm, sgu_w, sgu_bt, bd, tm):
    B, T, _ = x.shape
    const2 = lambda b, i: (0, 0)
    const3 = lambda b, i: (0, 0, 0)
    head_spec = lambda nh: pl.BlockSpec((1, nh, tm, D_HEAD), lambda b, i: (b, 0, i, 0))
    kv_shape = jax.ShapeDtypeStruct((B, NSA_KV_HEADS, T, D_HEAD), BF16)
    return pl.pallas_call(
        _proj_kernel,
        grid=(B, T // tm),
        in_specs=[
            pl.BlockSpec((1, tm, D_MODEL), lambda b, i: (b, i, 0)),
            pl.BlockSpec((1, D_MODEL), const2),
            pl.BlockSpec(w_q.shape, const2),
            pl.BlockSpec(w_kv.shape, const2),
            pl.BlockSpec(w_g.shape, const2),
            pl.BlockSpec(w_uv.shape, const2),
            pl.BlockSpec((1, D_NSA), const2),
            pl.BlockSpec((1, D_KV), const2),
            pl.BlockSpec((1, D_KV), const2),
            pl.BlockSpec((1, D_GMLP), const2),
            pl.BlockSpec(sgu_w.shape, const3),
            pl.BlockSpec(sgu_bt.shape, const2),
            pl.BlockSpec(bd.shape, const2),
        ],
        out_specs=[
            head_spec(NSA_HEADS),
            head_spec(NSA_KV_HEADS), head_spec(NSA_KV_HEADS), head_spec(NSA_KV_HEADS),
            head_spec(NSA_KV_HEADS), head_spec(NSA_KV_HEADS), head_spec(NSA_KV_HEADS),
            pl.BlockSpec((1, NSA_KV_HEADS, tm, LANES), lambda b, i: (b, 0, i, 0)),
            pl.BlockSpec((1, tm, D_GMLP), lambda b, i: (b, i, 0)),
        ],
        out_shape=[
            jax.ShapeDtypeStruct((B, NSA_HEADS, T, D_HEAD), BF16),
            kv_shape, kv_shape, kv_shape, kv_shape, kv_shape, kv_shape,
            jax.ShapeDtypeStruct((B, NSA_KV_HEADS, T, LANES), F32),
            jax.ShapeDtypeStruct((B, T, D_GMLP), BF16),
        ],
        compiler_params=pltpu.CompilerParams(
            dimension_semantics=("arbitrary", "arbitrary"), vmem_limit_bytes=VMEM_LIMIT),
        name="proj",
    )(x, attn_norm, w_q, w_kv, w_g, w_uv, qn, ksn, kwn, sgu_norm, sgu_w, sgu_bt, bd)


def _compress_one(tok_ref, pe_ref, w1_ref, b1_ref, w2_ref):
    half = CMP_STRIDE * D_HEAD
    tok = tok_ref[0, 0].astype(F32)
    top = (tok + pe_ref[0:1, :]).astype(BF16)
    bot = (tok + pe_ref[1:2, :]).astype(BF16)
    a = _dot(top, w1_ref[:half, :])
    b = _dot(bot, w1_ref[half:, :])
    ncp = a.shape[0]
    pre = a + pltpu.roll(b, ncp - 1, 0) + b1_ref[...]
    hid = _gelu_tanh(pre).astype(BF16)
    return _dot(hid, w2_ref[...])


def _compress_kernel(kc_ref, vc_ref, pek_ref, w1k_ref, b1k_ref, w2k_ref, kn_ref,
                     pev_ref, w1v_ref, b1v_ref, w2v_ref, k_out, v_out):
    k = _compress_one(kc_ref, pek_ref, w1k_ref, b1k_ref, w2k_ref)
    kms = jnp.mean(k * k, axis=-1, keepdims=True)
    k_out[0, 0] = (k * lax.rsqrt(kms + EPS) * kn_ref[...]).astype(BF16)
    v_out[0, 0] = _compress_one(vc_ref, pev_ref, w1v_ref, b1v_ref, w2v_ref).astype(BF16)


def _compress_call(kc, vc, pek, w1k, b1k, w2k, kn, pev, w1v, b1v, w2v):
    B, G, ncp, width = kc.shape
    const2 = lambda b, g: (0, 0)
    tok_spec = pl.BlockSpec((1, 1, ncp, width), lambda b, g: (b, g, 0, 0))
    out_spec = pl.BlockSpec((1, 1, ncp, D_HEAD), lambda b, g: (b, g, 0, 0))
    full = lambda a: pl.BlockSpec(a.shape, const2)
    out_shape = jax.ShapeDtypeStruct((B, G, ncp, D_HEAD), BF16)
    return pl.pallas_call(
        _compress_kernel,
        grid=(B, G),
        in_specs=[tok_spec, tok_spec, full(pek), full(w1k), full(b1k), full(w2k), full(kn),
                  full(pev), full(w1v), full(b1v), full(w2v)],
        out_specs=[out_spec, out_spec],
        out_shape=[out_shape, out_shape],
        compiler_params=pltpu.CompilerParams(
            dimension_semantics=("arbitrary", "arbitrary"), vmem_limit_bytes=VMEM_LIMIT),
        name="compress",
    )(kc, vc, pek, w1k, b1k, w2k, kn, pev, w1v, b1v, w2v)


def _rel_bucket_np(dist):
    max_exact = REL_BUCKETS // 2
    d = np.maximum(dist, 1).astype(np.float32)
    log_b = max_exact + (np.log(d / np.float32(max_exact)) / np.float32(math.log(REL_MAX_DIST / max_exact))
                         * np.float32(REL_BUCKETS - max_exact)).astype(np.int32)
    log_b = np.clip(log_b, max_exact, REL_BUCKETS - 1)
    return np.where(dist < max_exact, np.maximum(dist, 0), log_b)


def _bias_tables(rel_bias):
    rb = rel_bias.astype(F32) * LOG2E
    qq = np.arange(SEL_BLOCK)[:, None]
    kp = np.arange(SEL_BLOCK)[None, :]
    tile_d = [m * SEL_BLOCK + qq - kp for m in range(N_TOEPLITZ)]
    s_ = np.arange(SUB)[:, None, None]
    e_ = np.arange(CBAND)[None, None, :]
    dist_c = qq[None, :, :] + CMP_STRIDE * (e_ - 4 * (SUB - 1) + 4 * s_) - (3 * CMP_STRIDE + CMP_BLOCK - 1)
    all_d = np.concatenate([d.reshape(-1) for d in tile_d] + [dist_c.reshape(-1)])
    onehot = np.eye(REL_BUCKETS, dtype=np.float32)[_rel_bucket_np(all_d)]
    vals = jnp.dot(jnp.asarray(onehot), rb, precision=lax.Precision.HIGHEST)
    vals = jnp.where(jnp.asarray(all_d >= 0)[:, None], vals, NEG)
    n_t = N_TOEPLITZ * SEL_BLOCK * SEL_BLOCK
    toep = jnp.transpose(vals[:n_t].reshape(N_TOEPLITZ, SEL_BLOCK, SEL_BLOCK, NSA_HEADS), (0, 3, 1, 2))
    band = jnp.transpose(vals[n_t:].reshape(SUB, SEL_BLOCK, CBAND, NSA_HEADS), (3, 0, 1, 2))
    far = rb[REL_BUCKETS - 1]
    far_tile = jnp.broadcast_to(far[:, None, None], (NSA_HEADS, SEL_BLOCK, SEL_BLOCK))
    neg_tile = jnp.full((NSA_HEADS, SEL_BLOCK, SEL_BLOCK), NEG, F32)
    edge_tile = jnp.where(jnp.asarray(kp > qq)[None], far_tile, NEG)

    def tile(m, windowed):
        if m < 0 or (windowed and m > WIN_BLOCKS):
            return neg_tile
        if windowed and m == WIN_BLOCKS:
            return edge_tile
        return toep[m] if m < N_TOEPLITZ else far_tile

    def table(first_m, n_blocks, windowed):
        rows = [jnp.concatenate([tile(first_m + s - c, windowed) for c in range(n_blocks)], axis=2)
                for s in range(SUB)]
        t = jnp.concatenate(rows, axis=1)
        return t.reshape(NSA_KV_HEADS, ROWS, n_blocks * SEL_BLOCK)

    far_rows = jnp.repeat(far, TQ).reshape(NSA_KV_HEADS, ROWS, 1)
    near_t = table(BLOCKS_PER_TILE, 2 * BLOCKS_PER_TILE, False) - far_rows
    win_t = table(WIN_BLOCKS, WIN_BLOCKS + BLOCKS_PER_TILE, True)
    far_t = jnp.broadcast_to(far_rows, (NSA_KV_HEADS, ROWS, LANES))
    band_t = band.reshape(NSA_KV_HEADS, ROWS, CBAND)
    band_hi = band_t.astype(BF16)
    band_lo = (band_t - band_hi.astype(F32)).astype(BF16)
    far_hi = far_rows.astype(BF16)
    far_lo = (far_rows - far_hi.astype(F32)).astype(BF16)
    neg_col = jnp.full((NSA_KV_HEADS, ROWS, 1), NEG, BF16)
    zeros = jnp.zeros((NSA_KV_HEADS, ROWS, D_HEAD - 2 * CBAND - 3), BF16)
    q_lanes = jnp.zeros((NSA_KV_HEADS, ROWS, D_HEAD), BF16)
    cmp_lhs = jnp.concatenate([q_lanes, band_hi, band_lo, far_hi, far_lo, neg_col, zeros], axis=2)
    return near_t, win_t, far_t, cmp_lhs


def _row_max(lane_tiles):
    m = jnp.max(functools.reduce(jnp.maximum, lane_tiles), axis=1, keepdims=True)
    return jnp.broadcast_to(m, lane_tiles[0].shape)


def _nsa_kernel(q_ref, kc_ref, vc_ref, ks_ref, vs_ref, kw_ref, vw_ref, g_ref,
                near_ref, win_ref, far_ref, cmpl_ref, ovt_ref, eye_ref,
                o_ref, ksa, vsa, kwa, vwa, kca, vca, sc_scr, lhs_scr, lhsc_scr, m_scr, al_scr, inv_scr, acc_scr, sa_scr, sb_scr, pa_scr, pb_scr,
                sw_scr, pw_scr, ps_scr, out_scr):
    qt = pl.program_id(2)
    first = qt * SUB
    T = ks_ref.shape[2]
    ncp = kc_ref.shape[2]

    @pl.when(qt == 0)
    def _init():
        n_chunks = ksa.shape[0] // KT
        r_io = lax.broadcasted_iota(jnp.int32, (KT, KAUG), 0)
        l_io = lax.broadcasted_iota(jnp.int32, (KT, KAUG), 1)
        lv = lax.broadcasted_iota(jnp.int32, (KT, VAUG), 1)
        vpat = jnp.where(lv == D_HEAD, 1.0, 0.0).astype(BF16)

        def fill(c, carry):
            rows = pl.ds(pl.multiple_of(c * KT, KT), KT)
            key = r_io + (c * KT - KPAD)
            blk = key // SEL_BLOCK
            real = (key >= 0) & (key < T)
            hot = real & (((l_io < D_HEAD) & (l_io == blk))
                          | ((l_io >= D_HEAD) & (l_io < 2 * D_HEAD) & (l_io - D_HEAD == blk)))
            hot = hot | ((~real) & (l_io == 3 * D_HEAD))
            pat = jnp.where(hot, 1.0, 0.0).astype(BF16)
            ksa[rows, :] = pat
            kwa[rows, :] = pat
            vsa[rows, :] = vpat
            vwa[rows, :] = vpat
            return carry

        lax.fori_loop(0, n_chunks, fill, 0)
        ksa[KPAD:KPAD + T, 2 * D_HEAD:3 * D_HEAD] = ks_ref[0, 0]
        kwa[KPAD:KPAD + T, 2 * D_HEAD:3 * D_HEAD] = kw_ref[0, 0]
        vsa[KPAD:KPAD + T, 0:D_HEAD] = vs_ref[0, 0]
        vwa[KPAD:KPAD + T, 0:D_HEAD] = vw_ref[0, 0]
        lc = lax.broadcasted_iota(jnp.int32, (ncp, VAUG), 1)
        kca[...] = jnp.zeros((ncp, VAUG), BF16)
        kca[:, 0:D_HEAD] = kc_ref[0, 0]
        vca[...] = jnp.where(lc == D_HEAD, 1.0, 0.0).astype(BF16)
        vca[:, 0:D_HEAD] = vc_ref[0, 0]

    q4 = q_ref[0].reshape(ROWS, D_HEAD)
    gates = g_ref[0, 0]
    row_blocks = [slice(r, r + RB) for r in range(0, ROWS, RB)]

    gate_rows = jnp.concatenate(
        [gates if hh == 0 else pltpu.roll(gates, LANES - hh * N_BRANCH, 1) for hh in range(NSA_GROUP)], axis=0)

    def gate_col(branch):
        return gate_rows[:, branch:branch + 1]

    lane_c = lax.broadcasted_iota(jnp.int32, (ncp, LANES), 1)
    n_c = lax.broadcasted_iota(jnp.int32, (ncp, LANES), 0)
    e = lane_c - D_HEAD
    base = 4 * first + (4 * SUB - 1)
    ind = (((e >= 0) & (e < CBAND) & (n_c == base - e))
           | ((e >= CBAND) & (e < 2 * CBAND) & (n_c == base - (e - CBAND)))
           | (((e == 2 * CBAND) | (e == 2 * CBAND + 1)) & (n_c < base - (CBAND - 1)))
           | ((e == 2 * CBAND + 2) & (n_c > base)))
    rhs_c = jnp.where(lane_c < D_HEAD, kca[...], jnp.where(ind, 1.0, 0.0).astype(BF16))
    lhsc_scr[...] = cmpl_ref[0]
    lhsc_scr[:, 0:D_HEAD] = q4
    sa_scr[:, 0:ncp] = _halves(_dot_nt, lhsc_scr[...], rhs_c)

    lane_q = lax.broadcasted_iota(jnp.int32, (ROWS, LANES), 1)
    lhs_scr[:, LANES:2 * LANES] = jnp.where(lane_q == D_HEAD, NEG, 0.0).astype(BF16)
    lhs_scr[:, LANES:LANES + D_HEAD] = q4
    lhs_win = jnp.concatenate([jnp.zeros((ROWS, LANES), BF16), lhs_scr[:, LANES:2 * LANES]], axis=1)
    win_start = pl.multiple_of(KPAD + (first - WIN_BLOCKS) * SEL_BLOCK, KT)
    sw_scr[...] = _halves(_dot_nt, lhs_win, kwa[pl.ds(win_start, KS + KT), :])

    for i, rows in enumerate(row_blocks):
        s = [sa_scr[rows, c * LANES:(c + 1) * LANES] for c in range(ncp // LANES)]
        m = _row_max(s)
        p = [jnp.where(t > 0.5 * NEG, jnp.exp2(t - m), 0.0) for t in s]
        l = jnp.sum(functools.reduce(jnp.add, p), axis=1, keepdims=True)
        inv = jnp.where(l > 0.0, 1.0 / l, 0.0)
        inv_scr[rows] = inv
        inv_b = jnp.broadcast_to(inv, (RB, LANES))
        tok = slice((i % (TQ // RB)) * RB, (i % (TQ // RB) + 1) * RB)
        for c in range(ncp // LANES):
            lanes = slice(c * LANES, (c + 1) * LANES)
            pb_scr[rows, lanes] = p[c].astype(BF16)
            if i < TQ // RB:
                ps_scr[tok, lanes] = p[c] * inv_b
            else:
                ps_scr[tok, lanes] += p[c] * inv_b
    acc_c = _halves(_dot, pb_scr[:, 0:ncp], vca[...])
    out_scr[...] = acc_c[:, :D_HEAD] * (gate_col(0) * inv_scr[...])
    ps_hi, ps_lo = _split_bf16(ps_scr[...])
    ovt = ovt_ref[...]
    imp_t = _dot_nt(ovt, ps_hi) + _dot_nt(ovt, ps_lo)

    j_t = lax.broadcasted_iota(jnp.int32, (MAX_BLOCKS, TQ), 0)
    cur = first + lax.broadcasted_iota(jnp.int32, (MAX_BLOCKS, TQ), 1) // SEL_BLOCK
    forced = (j_t == 0) | (j_t == cur) | (j_t == cur - 1)
    score_t = jnp.where(forced, FORCED_SCORE, jnp.where(j_t <= cur, imp_t, NEG))
    sc_scr[...] = score_t

    def count(u, cnt):
        for k in range(SUB):
            i = SUB * u + k
            row_i = sc_scr[pl.ds(i, 1), :]
            beats = (row_i > score_t) | ((row_i == score_t) & (j_t > i))
            cnt = cnt + beats.astype(jnp.int32)
        return cnt

    cnt = lax.fori_loop(0, qt + 1, count, jnp.zeros((MAX_BLOCKS, TQ), jnp.int32))
    sel_t = jnp.where((cnt < SEL_TOPK) & (j_t <= cur), 1.0, 0.0).astype(BF16)
    sel = _dot_nt(eye_ref[...], jnp.concatenate([sel_t, sel_t], axis=0))
    sel4 = jnp.concatenate([sel] * NSA_GROUP, axis=0) > 0.5

    hi, lo = _split_bf16(jnp.where(sel4, far_ref[0], NEG))
    lhs_scr[:, 0:LANES] = jnp.where(lane_q < D_HEAD, hi, lo)
    m_scr[...] = jnp.full((ROWS, LANES), NEG, F32)
    acc_scr[...] = jnp.zeros((ROWS, VAUG), F32)

    def scores(w, dst):
        start = pl.multiple_of(KPAD + w * KS, KS)
        dst[...] = _halves(_dot_nt, lhs_scr[...], ksa[pl.ds(start, KS), :])

    def consume(w, src, p_scr, with_table):
        n_lt = KS // LANES
        if with_table:
            rel = [2 * w + half - (qt - 1) for half in range(KS // KT)]
            col = [jnp.clip(r, 0, 1) * KT for r in rel]
        for rows in row_blocks:
            s = [src[rows, c * LANES:(c + 1) * LANES] for c in range(n_lt)]
            if with_table:
                for c in range(n_lt):
                    half, off = divmod(c * LANES, KT)
                    at = pl.multiple_of(col[half] + off, LANES)
                    s[c] = s[c] + jnp.where(rel[half] >= 0, near_ref[0, rows, pl.ds(at, LANES)], 0.0)
            m_prev = m_scr[rows]
            m = jnp.maximum(m_prev, _row_max(s))
            al_scr[rows] = jnp.exp2(m_prev - m)
            m_scr[rows] = m
            for c in range(n_lt):
                p_scr[rows, c * LANES:(c + 1) * LANES] = jnp.exp2(s[c] - m).astype(BF16)
        start = pl.multiple_of(KPAD + w * KS, KS)
        acc_scr[...] = acc_scr[...] * al_scr[...] + _dot(p_scr[:, 0:KS], vsa[pl.ds(start, KS), :])

    scores(0, sa_scr)

    for rows in row_blocks:
        s = [sw_scr[rows, c * LANES:(c + 1) * LANES] + win_ref[0, rows, c * LANES:(c + 1) * LANES]
             for c in range((KS + KT) // LANES)]
        m = _row_max(s)
        for c in range((KS + KT) // LANES):
            pw_scr[rows, c * LANES:(c + 1) * LANES] = jnp.exp2(s[c] - m).astype(BF16)
    acc_w = _halves(_dot, pw_scr[...], vwa[pl.ds(win_start, KS + KT), :])
    out_scr[...] += acc_w[:, :D_HEAD] * (gate_col(2) / acc_w[:, D_HEAD:D_HEAD + 1])

    def pair(with_table, u, carry):
        scores(2 * u + 1, sb_scr)
        consume(2 * u, sa_scr, pa_scr, with_table)
        scores(2 * u + 2, sa_scr)
        consume(2 * u + 1, sb_scr, pb_scr, with_table)
        return carry

    n_trips = qt // 4 + 1
    n_plain = jnp.maximum(qt - 1, 0) // 4
    lax.fori_loop(0, n_plain, functools.partial(pair, False), 0)
    lax.fori_loop(n_plain, n_trips, functools.partial(pair, True), 0)
    acc_s = acc_scr[...]

    o = out_scr[...] + acc_s[:, :D_HEAD] * (gate_col(1) / acc_s[:, D_HEAD:D_HEAD + 1])
    for hh in range(NSA_GROUP):
        o_ref[0, :, hh * D_HEAD:(hh + 1) * D_HEAD] = o[hh * TQ:(hh + 1) * TQ].astype(BF16)


def _nsa_call(q, kcmp, vcmp, ks, vs, kw, vw, gates, near_t, win_t, far_t, cmp_lhs, ovt, eye):
    B, _, T, _ = q.shape
    G = NSA_KV_HEADS
    ncp = kcmp.shape[2]
    rows_kv = KPAD + T + KS
    per_bg =lambda a: pl.BlockSpec((1, 1) + a.shape[2:], lambda b, g, i: (b, g, 0, 0))
    per_g = lambda a: pl.BlockSpec((1,) + a.shape[1:], lambda b, g, i: (g, 0, 0))
    const2 = lambda a: pl.BlockSpec(a.shape, lambda b, g, i: (0, 0))
    return pl.pallas_call(
        _nsa_kernel,
        grid=(B, G, T // TQ),
        in_specs=[
            pl.BlockSpec((1, NSA_GROUP, TQ, D_HEAD), lambda b, g, i: (b, g, i, 0)),
            per_bg(kcmp), per_bg(vcmp), per_bg(ks), per_bg(vs), per_bg(kw), per_bg(vw),
            pl.BlockSpec((1, 1, TQ, LANES), lambda b, g, i: (b, g, i, 0)),
            per_g(near_t), per_g(win_t), per_g(far_t), per_g(cmp_lhs), const2(ovt), const2(eye),
        ],
        out_specs=pl.BlockSpec((1, TQ, NSA_GROUP * D_HEAD), lambda b, g, i: (b, i, g)),
        out_shape=jax.ShapeDtypeStruct((B, T, D_NSA), BF16),
        scratch_shapes=[
            pltpu.VMEM((rows_kv, KAUG), BF16), pltpu.VMEM((rows_kv, VAUG), BF16),
            pltpu.VMEM((rows_kv, KAUG), BF16), pltpu.VMEM((rows_kv, VAUG), BF16),
            pltpu.VMEM((ncp, VAUG), BF16), pltpu.VMEM((ncp, VAUG), BF16),
            pltpu.VMEM((MAX_BLOCKS, TQ), F32),
            pltpu.VMEM((ROWS, KAUG), BF16), pltpu.VMEM((ROWS, LANES), BF16),
            pltpu.VMEM((ROWS, LANES), F32), pltpu.VMEM((ROWS, LANES), F32), pltpu.VMEM((ROWS, 1), F32),
            pltpu.VMEM((ROWS, VAUG), F32),
            pltpu.VMEM((ROWS, KS), F32), pltpu.VMEM((ROWS, KS), F32),
            pltpu.VMEM((ROWS, KS), BF16), pltpu.VMEM((ROWS, KS), BF16),
            pltpu.VMEM((ROWS, KS + KT), F32), pltpu.VMEM((ROWS, KS + KT), BF16),
            pltpu.VMEM((TQ, ncp), F32), pltpu.VMEM((ROWS, D_HEAD), F32),
        ],
        compiler_params=pltpu.CompilerParams(
            dimension_semantics=("arbitrary", "arbitrary", "arbitrary"), vmem_limit_bytes=VMEM_LIMIT),
        name="nsa",
    )(q, kcmp, vcmp, ks, vs, kw, vw, gates, near_t, win_t, far_t, cmp_lhs, ovt, eye)


def _ffn_kernel(x_ref, on_ref, os_ref, wo_n_ref, wo_s_ref, fn_ref, wup_ref, cw_ref, cb_ref, wdn_ref,
                out_ref, prev_ref, perm_ref, h_ref, ua_scr, ub_scr, act_ref):
    tm = x_ref.shape[1]
    seg = tm // 8
    n_chunks = wup_ref.shape[1]

    @pl.when(pl.program_id(1) == 0)
    def _start_of_sequence():
        prev_ref[...] = jnp.zeros(prev_ref.shape, F32)

    x1 = x_ref[0] + _dot(on_ref[0], wo_n_ref[...]) + _dot(os_ref[0], wo_s_ref[...])
    ms = jnp.mean(x1 * x1, axis=-1, keepdims=True)
    hn = x1 * lax.rsqrt(ms + EPS) * fn_ref[...]
    out_ref[0] = x1
    n_lt = hn.shape[1] // LANES
    for c in range(n_lt):
        perm_ref[c] = hn[:, c * LANES:(c + 1) * LANES]
    for u in range(seg // 2):
        groups = [jnp.concatenate([perm_ref[c, pl.ds(2 * u + k, 8, stride=seg), :] for c in range(n_lt)], axis=1)
                  for k in range(2)]
        h_ref[16 * u:16 * u + 16, :] = jnp.concatenate(groups, axis=0).astype(BF16)

    def up(c, dst):
        h = h_ref[...]
        for half in range(2):
            dst[half] = _dot(h, wup_ref[half, c])

    first_sublane = lax.broadcasted_iota(jnp.int32, (8, FFN_CHUNK), 0) == 0

    def conv_gate(c, src):
        for r0 in range(0, tm, FFN_ROWS):
            ys = []
            for half in range(2):
                w = cw_ref[half, c]
                x0 = src[half, r0:r0 + FFN_ROWS]
                if r0 == 0:
                    wrap = [jnp.where(first_sublane, pltpu.roll(prev_ref[half, c, 8 * k:8 * k + 8], 1, 0),
                                      pltpu.roll(src[half, tm - 16 + 8 * k:tm - 8 + 8 * k], 1, 0)) for k in range(2)]
                    x1_ = jnp.concatenate([wrap[1], src[half, 0:FFN_ROWS - 8]], axis=0)
                    x2_ = jnp.concatenate([wrap[0], wrap[1], src[half, 0:FFN_ROWS - 16]], axis=0)
                else:
                    x1_ = src[half, r0 - 8:r0 + FFN_ROWS - 8]
                    x2_ = src[half, r0 - 16:r0 + FFN_ROWS - 16]
                ys.append(x0 * w[2:3] + x1_ * w[1:2] + x2_ * w[0:1] + cb_ref[half, c])
            a, g = ys
            col = pl.multiple_of(c * FFN_CHUNK, FFN_CHUNK)
            act_ref[r0:r0 + FFN_ROWS, pl.ds(col, FFN_CHUNK)] = (g * _sigmoid(g) * a).astype(BF16)
        for half in range(2):
            prev_ref[half, c] = src[half, tm - 16:tm]

    up(0, ua_scr)

    def pair(u, carry):
        up(2 * u + 1, ub_scr)
        conv_gate(2 * u, ua_scr)
        up(2 * u + 2, ua_scr)
        conv_gate(2 * u + 1, ub_scr)
        return carry

    lax.fori_loop(0, (n_chunks - 1) // 2, pair, 0)
    conv_gate(n_chunks - 1, ua_scr)
    y = _halves(_dot, act_ref[...], wdn_ref[...])
    for c in range(n_lt):
        perm_ref[c] = y[:, c * LANES:(c + 1) * LANES]
    for sgm in range(8):
        natural = jnp.concatenate([perm_ref[c, pl.ds(sgm, seg, stride=8), :] for c in range(n_lt)], axis=1)
        out_ref[0, sgm * seg:(sgm + 1) * seg, :] += natural


def _ffn_call(x, o_nsa, o_sgu, wo_n, wo_s, ffn_norm, w_up, conv_w, conv_b, w_down, tm):
    B, T, _ = x.shape
    n_chunks = w_up.shape[1]
    assert n_chunks % 2 == 1
    resident =lambda a: pl.BlockSpec(a.shape, lambda b, i: (0,) * a.ndim, pipeline_mode=pl.Buffered(1))
    tile = lambda w: pl.BlockSpec((1, tm, w), lambda b, i: (b, i, 0))
    return pl.pallas_call(
        _ffn_kernel,
        grid=(B, T // tm),
        in_specs=[tile(D_MODEL), tile(D_NSA), tile(D_GMLP), resident(wo_n), resident(wo_s),
                  resident(ffn_norm), resident(w_up), resident(conv_w), resident(conv_b), resident(w_down)],
        out_specs=tile(D_MODEL),
        out_shape=jax.ShapeDtypeStruct((B, T, D_MODEL), F32),
        scratch_shapes=[pltpu.VMEM((2, n_chunks, 16, FFN_CHUNK), F32), pltpu.VMEM((D_MODEL // LANES, tm, LANES), F32),
                        pltpu.VMEM((tm, D_MODEL), BF16),
                        pltpu.VMEM((2, tm, FFN_CHUNK), F32), pltpu.VMEM((2, tm, FFN_CHUNK), F32),
                        pltpu.VMEM((tm, D_FF), BF16)],
        compiler_params=pltpu.CompilerParams(
            dimension_semantics=("arbitrary", "arbitrary"), vmem_limit_bytes=VMEM_LIMIT),
        name="ffn",
    )(x, o_nsa, o_sgu, wo_n, wo_s, ffn_norm, w_up, conv_w, conv_b, w_down)


def _mixers(x, rel_bias, attn_norm, w_in, q_norm, k_norm_cmp, k_norm_slc, k_norm_win,
            cmp_pe_k, cmp_w1_k, cmp_b1_k, cmp_w2_k, cmp_pe_v, cmp_w1_v, cmp_b1_v, cmp_w2_v,
            sgu_norm, sgu_w, sgu_b):
    B, T, _ = x.shape
    assert T % 512 == 0 and T // SEL_BLOCK <= MAX_BLOCKS and (T // CMP_STRIDE) % LANES == 0
    ncp = T // CMP_STRIDE
    G, R = NSA_KV_HEADS, NSA_GROUP

    o_kv = D_NSA
    o_g = D_NSA + 6 * D_KV
    o_uv = o_g + N_GATES
    w_q = w_in[:, :o_kv].astype(BF16)
    w_kv = w_in[:, o_kv:o_g].astype(BF16)
    per_group = NSA_GROUP * N_BRANCH
    w_g = jnp.pad(w_in[:, o_g:o_uv].reshape(D_MODEL, G, per_group), ((0, 0), (0, 0), (0, LANES - per_group)))
    w_g = w_g.reshape(D_MODEL, G * LANES).astype(BF16)
    w_uv = w_in[:, o_uv:].astype(BF16)
    qn = (jnp.tile(q_norm, NSA_HEADS) * (D_HEAD ** -0.5 * LOG2E)).reshape(1, D_NSA)
    ksn = jnp.tile(k_norm_slc, G).reshape(1, D_KV)
    kwn = jnp.tile(k_norm_win, G).reshape(1, D_KV)
    sgu_bt = jnp.repeat(sgu_b.T, GMLP_GROUP_DIM, axis=1)
    grp = np.arange(D_NSA) // D_HEAD
    bd = jnp.asarray(grp[:, None] == grp[None, :], BF16)

    q, kc, vc, ks, vs, kw, vw, gates, o_sgu = _proj_call(
        x, attn_norm.reshape(1, D_MODEL), w_q, w_kv, w_g, w_uv, qn, ksn, kwn,
        sgu_norm.reshape(1, D_GMLP), sgu_w, sgu_bt, bd, tm=512)

    half = CMP_STRIDE * D_HEAD
    kcmp, vcmp = _compress_call(
        kc.reshape(B, G, ncp, half), vc.reshape(B, G, ncp, half),
        cmp_pe_k.reshape(2, half), cmp_w1_k.astype(BF16), cmp_b1_k.reshape(1, CMP_HIDDEN),
        cmp_w2_k.astype(BF16), k_norm_cmp.reshape(1, D_HEAD),
        cmp_pe_v.reshape(2, half), cmp_w1_v.astype(BF16), cmp_b1_v.reshape(1, CMP_HIDDEN),
        cmp_w2_v.astype(BF16))

    near_t, win_t, far_t, cmp_lhs = _bias_tables(rel_bias)
    jj = np.arange(MAX_BLOCKS)[:, None]
    nn = np.arange(ncp)[None, :]
    ovt = jnp.asarray((nn >= 4 * jj - 1) & (nn <= 4 * jj + 3), BF16)
    eye = jnp.asarray(np.eye(TQ), BF16)

    o_nsa = _nsa_call(q, kcmp, vcmp, ks, vs, kw, vw, gates, near_t, win_t, far_t, cmp_lhs, ovt, eye)
    return o_nsa, o_sgu


def _layer(x, rel_bias, attn_norm, w_in, q_norm, k_norm_cmp, k_norm_slc, k_norm_win,
           cmp_pe_k, cmp_w1_k, cmp_b1_k, cmp_w2_k, cmp_pe_v, cmp_w1_v, cmp_b1_v, cmp_w2_v,
           sgu_norm, sgu_w, sgu_b, w_out, ffn_norm, w_up, conv_w, conv_b, w_down):
    o_nsa, o_sgu = _mixers(x, rel_bias, attn_norm, w_in, q_norm, k_norm_cmp, k_norm_slc, k_norm_win,
                           cmp_pe_k, cmp_w1_k, cmp_b1_k, cmp_w2_k, cmp_pe_v, cmp_w1_v, cmp_b1_v, cmp_w2_v,
                           sgu_norm, sgu_w, sgu_b)
    wo = w_out.astype(BF16)
    n_chunks = D_FF // FFN_CHUNK
    wu = jnp.transpose(w_up.astype(BF16).reshape(D_MODEL, 2, n_chunks, FFN_CHUNK), (1, 2, 0, 3))
    cw = jnp.transpose(conv_w.reshape(conv_w.shape[0], 2, n_chunks, FFN_CHUNK), (1, 2, 0, 3))
    cb = conv_b.reshape(2, n_chunks, 1, FFN_CHUNK)
    wd = w_down.astype(BF16)
    return _ffn_call(x, o_nsa, o_sgu, wo[:D_NSA], wo[D_NSA:], ffn_norm.reshape(1, D_MODEL), wu, cw, cb, wd, tm=512)


def kernel(x, rel_bias, attn_norm, w_in, q_norm, k_norm_cmp, k_norm_slc, k_norm_win,
           cmp_pe_k, cmp_w1_k, cmp_b1_k, cmp_w2_k, cmp_pe_v, cmp_w1_v, cmp_b1_v, cmp_w2_v,
           sgu_norm, sgu_w, sgu_b, w_out, ffn_norm, w_up, conv_w, conv_b, w_down):
    depth = attn_norm.shape[0]
    for l in range(depth):
        x = _layer(x, rel_bias, attn_norm[l], w_in[l], q_norm[l], k_norm_cmp[l], k_norm_slc[l], k_norm_win[l],
                   cmp_pe_k[l], cmp_w1_k[l], cmp_b1_k[l], cmp_w2_k[l], cmp_pe_v[l], cmp_w1_v[l], cmp_b1_v[l],
                   cmp_w2_v[l], sgu_norm[l], sgu_w[l], sgu_b[l], w_out[l], ffn_norm[l], w_up[l], conv_w[l],
                   conv_b[l], w_down[l])
    return x
```

```python
import functools
import math

import numpy as np
import jax
import jax.numpy as jnp
from jax import lax
from jax.experimental import pallas as pl
from jax.experimental.pallas import tpu as pltpu

F32 = jnp.float32
BF16 = jnp.bfloat16

D_MODEL = 1024
D_HEAD = 64
NSA_HEADS = 8
NSA_KV_HEADS = 2
NSA_GROUP = NSA_HEADS // NSA_KV_HEADS
D_NSA = NSA_HEADS * D_HEAD
D_KV = NSA_KV_HEADS * D_HEAD
N_BRANCH = 3
N_GATES = NSA_HEADS * N_BRANCH
CMP_BLOCK = 32
CMP_STRIDE = 16
CMP_HIDDEN = 256
SEL_BLOCK = 64
SEL_TOPK = 16
WINDOW = 512
GMLP_GROUPS = 8
GMLP_GROUP_DIM = 64
D_GMLP = GMLP_GROUPS * GMLP_GROUP_DIM
CHUNK = 128
D_MIX = D_NSA + D_GMLP
REL_BUCKETS = 32
REL_MAX_DIST = 128
D_FF = 2816
EPS = 1e-6
NEG = -1e30
FORCED_SCORE = 1e4
LOG2E = 1.4426950408889634

LANES = 128
TQ = 256
SUB = TQ // SEL_BLOCK
ROWS = NSA_GROUP * TQ
KT = 256
KS = 2 * KT
RB = 64
BLOCKS_PER_TILE = KT // SEL_BLOCK
WIN_BLOCKS = WINDOW // SEL_BLOCK
KPAD = 512
N_TOEPLITZ = 3
MAX_BLOCKS = 64
KAUG = 256
VAUG = 128
CBAND = 28
FFN_CHUNK = 256
FFN_ROWS = 64
VMEM_LIMIT = 56 * 1024 * 1024

_NT = (((1,), (1,)), ((), ()))


def _dot(a, b):
    return jnp.dot(a, b, preferred_element_type=F32)


def _dot_nt(a, b):
    return lax.dot_general(a, b, _NT, preferred_element_type=F32)


def _halves(dot, a, b):
    h = a.shape[0] // 2
    return jnp.concatenate([dot(a[:h], b), dot(a[h:], b)], axis=0)


def _split_bf16(x):
    hi = x.astype(BF16)
    lo = (x - hi.astype(F32)).astype(BF16)
    return hi, lo


def _gelu_tanh(x):
    return 0.5 * x * (1.0 + jnp.tanh(0.7978845608028654 * (x + 0.044715 * (x * x * x))))


def _sigmoid(x):
    return 0.5 * (1.0 + jnp.tanh(0.5 * x))


def _group_mean_sq(t, ones_blockdiag, width):
    t2 = t * t
    hi, lo = _split_bf16(t2)
    return (_dot(hi, ones_blockdiag) + _dot(lo, ones_blockdiag)) * (1.0 / width)


def _proj_kernel(x_ref, an_ref, wq_ref, wkv_ref, wg_ref, wuv_ref, qn_ref, ksn_ref, kwn_ref,
                 sgun_ref, sguw_ref, sgub_ref, bd_ref,
                 q_out, kc_out, vc_out, ks_out, vs_out, kw_out, vw_out, g_out, sgu_out):
    tm = x_ref.shape[1]
    x = x_ref[0]
    ms = jnp.mean(x * x, axis=-1, keepdims=True)
    h = (x * lax.rsqrt(ms + EPS) * an_ref[...]).astype(BF16)

    bd = bd_ref[...]
    q = _dot(h, wq_ref[...])
    qn = q * lax.rsqrt(_group_mean_sq(q, bd, D_HEAD) + EPS) * qn_ref[...]
    for hh in range(NSA_HEADS):
        q_out[0, hh] = qn[:, hh * D_HEAD:(hh + 1) * D_HEAD].astype(BF16)

    kv = _dot(h, wkv_ref[...])
    bd_kv = bd[:D_KV, :D_KV]
    kc = kv[:, 0 * D_KV:1 * D_KV]
    vc = kv[:, 1 * D_KV:2 * D_KV]
    ks = kv[:, 2 * D_KV:3 * D_KV]
    vs = kv[:, 3 * D_KV:4 * D_KV]
    kw = kv[:, 4 * D_KV:5 * D_KV]
    vw = kv[:, 5 * D_KV:6 * D_KV]
    ks = ks * lax.rsqrt(_group_mean_sq(ks, bd_kv, D_HEAD) + EPS) * ksn_ref[...]
    kw = kw * lax.rsqrt(_group_mean_sq(kw, bd_kv, D_HEAD) + EPS) * kwn_ref[...]
    for t, o_ref in ((kc, kc_out), (vc, vc_out), (ks, ks_out), (vs, vs_out), (kw, kw_out), (vw, vw_out)):
        for g in range(NSA_KV_HEADS):
            o_ref[0, g] = t[:, g * D_HEAD:(g + 1) * D_HEAD].astype(BF16)

    gates = _sigmoid(_dot(h, wg_ref[...]))
    for g in range(NSA_KV_HEADS):
        g_out[0, g] = gates[:, g * LANES:(g + 1) * LANES]

    uv = _gelu_tanh(_dot(h, wuv_ref[...]))
    u = uv[:, :D_GMLP]
    v = uv[:, D_GMLP:]
    vms = jnp.mean(v * v, axis=-1, keepdims=True)
    vb = (v * lax.rsqrt(vms + EPS) * sgun_ref[...]).astype(BF16)

    row = lax.broadcasted_iota(jnp.int32, (CHUNK, CHUNK), 0)
    col = lax.broadcasted_iota(jnp.int32, (CHUNK, CHUNK), 1)
    tril = col <= row
    w_tril = [jnp.where(tril, sguw_ref[g], 0.0).astype(BF16) for g in range(GMLP_GROUPS)]
    w_pair = [jnp.concatenate([w_tril[2 * p], w_tril[2 * p + 1]], axis=1) for p in range(D_GMLP // LANES)]
    first_half = lax.broadcasted_iota(jnp.int32, (CHUNK, LANES), 1) < GMLP_GROUP_DIM
    zero = jnp.zeros((CHUNK, LANES), BF16)
    for c in range(tm // CHUNK):
        rows = slice(c * CHUNK, (c + 1) * CHUNK)
        zs = []
        for p in range(D_GMLP // LANES):
            blk = vb[rows, p * LANES:(p + 1) * LANES]
            rhs = jnp.concatenate([jnp.where(first_half, blk, zero), jnp.where(first_half, zero, blk)], axis=0)
            zs.append(_dot(w_pair[p], rhs))
        z = jnp.concatenate(zs, axis=1) + sgub_ref[...]
        sgu_out[0, rows, :] = (u[rows, :] * z).astype(BF16)


def _proj_call(x, attn_norm, w_q, w_kv, w_g, w_uv, qn, ksn, kwn, sgu_norm, sgu_w, sgu_bt, bd, tm):
    B, T, _ = x.shape
    const2 = lambda b, i: (0, 0)
    const3 = lambda b, i: (0, 0, 0)
    head_spec = lambda nh: pl.BlockSpec((1, nh, tm, D_HEAD), lambda b, i: (b, 0, i, 0))
    kv_shape = jax.ShapeDtypeStruct((B, NSA_KV_HEADS, T, D_HEAD), BF16)
    return pl.pallas_call(
        _proj_kernel,
        grid=(B, T // tm),
        in_specs=[
            pl.BlockSpec((1, tm, D_MODEL), lambda b, i: (b, i, 0)),
            pl.BlockSpec((1, D_MODEL), const2),
            pl.BlockSpec(w_q.shape, const2),
            pl.BlockSpec(w_kv.shape, const2),
            pl.BlockSpec(w_g.shape, const2),
            pl.BlockSpec(w_uv.shape, const2),
            pl.BlockSpec((1, D_NSA), const2),
            pl.BlockSpec((1, D_KV), const2),
            pl.BlockSpec((1, D_KV), const2),
            pl.BlockSpec((1, D_GMLP), const2),
            pl.BlockSpec(sgu_w.shape, const3),
            pl.BlockSpec(sgu_bt.shape, const2),
            pl.BlockSpec(bd.shape, const2),
        ],
        out_specs=[
            head_spec(NSA_HEADS),
            head_spec(NSA_KV_HEADS), head_spec(NSA_KV_HEADS), head_spec(NSA_KV_HEADS),
            head_spec(NSA_KV_HEADS), head_spec(NSA_KV_HEADS), head_spec(NSA_KV_HEADS),
            pl.BlockSpec((1, NSA_KV_HEADS, tm, LANES), lambda b, i: (b, 0, i, 0)),
            pl.BlockSpec((1, tm, D_GMLP), lambda b, i: (b, i, 0)),
        ],
        out_shape=[
            jax.ShapeDtypeStruct((B, NSA_HEADS, T, D_HEAD), BF16),
            kv_shape, kv_shape, kv_shape, kv_shape, kv_shape, kv_shape,
            jax.ShapeDtypeStruct((B, NSA_KV_HEADS, T, LANES), F32),
            jax.ShapeDtypeStruct((B, T, D_GMLP), BF16),
        ],
        compiler_params=pltpu.CompilerParams(
            dimension_semantics=("arbitrary", "arbitrary"), vmem_limit_bytes=VMEM_LIMIT),
        name="proj",
    )(x, attn_norm, w_q, w_kv, w_g, w_uv, qn, ksn, kwn, sgu_norm, sgu_w, sgu_bt, bd)


def _compress_one(tok_ref, pe_ref, w1_ref, b1_ref, w2_ref):
    half = CMP_STRIDE * D_HEAD
    tok = tok_ref[0, 0].astype(F32)
    top = (tok + pe_ref[0:1, :]).astype(BF16)
    bot = (tok + pe_ref[1:2, :]).astype(BF16)
    a = _dot(top, w1_ref[:half, :])
    b = _dot(bot, w1_ref[half:, :])
    ncp = a.shape[0]
    pre = a + pltpu.roll(b, ncp - 1, 0) + b1_ref[...]
    hid = _gelu_tanh(pre).astype(BF16)
    return _dot(hid, w2_ref[...])


def _compress_kernel(kc_ref, vc_ref, pek_ref, w1k_ref, b1k_ref, w2k_ref, kn_ref,
                     pev_ref, w1v_ref, b1v_ref, w2v_ref, k_out, v_out):
    k = _compress_one(kc_ref, pek_ref, w1k_ref, b1k_ref, w2k_ref)
    kms = jnp.mean(k * k, axis=-1, keepdims=True)
    k_out[0, 0] = (k * lax.rsqrt(kms + EPS) * kn_ref[...]).astype(BF16)
    v_out[0, 0] = _compress_one(vc_ref, pev_ref, w1v_ref, b1v_ref, w2v_ref).astype(BF16)


def _compress_call(kc, vc, pek, w1k, b1k, w2k, kn, pev, w1v, b1v, w2v):
    B, G, ncp, width = kc.shape
    const2 = lambda b, g: (0, 0)
    tok_spec = pl.BlockSpec((1, 1, ncp, width), lambda b, g: (b, g, 0, 0))
    out_spec = pl.BlockSpec((1, 1, ncp, D_HEAD), lambda b, g: (b, g, 0, 0))
    full = lambda a: pl.BlockSpec(a.shape, const2)
    out_shape = jax.ShapeDtypeStruct((B, G, ncp, D_HEAD), BF16)
    return pl.pallas_call(
        _compress_kernel,
        grid=(B, G),
        in_specs=[tok_spec, tok_spec, full(pek), full(w1k), full(b1k), full(w2k), full(kn),
                  full(pev), full(w1v), full(b1v), full(w2v)],
        out_specs=[out_spec, out_spec],
        out_shape=[out_shape, out_shape],
        compiler_params=pltpu.CompilerParams(
            dimension_semantics=("arbitrary", "arbitrary"), vmem_limit_bytes=VMEM_LIMIT),
        name="compress",
    )(kc, vc, pek, w1k, b1k, w2k, kn, pev, w1v, b1v, w2v)


def _rel_bucket_np(dist):
    max_exact = REL_BUCKETS // 2
    d = np.maximum(dist, 1).astype(np.float32)
    log_b = max_exact + (np.log(d / np.float32(max_exact)) / np.float32(math.log(REL_MAX_DIST / max_exact))
                         * np.float32(REL_BUCKETS - max_exact)).astype(np.int32)
    log_b = np.clip(log_b, max_exact, REL_BUCKETS - 1)
    return np.where(dist < max_exact, np.maximum(dist, 0), log_b)


def _bias_tables(rel_bias):
    rb = rel_bias.astype(F32) * LOG2E
    qq = np.arange(SEL_BLOCK)[:, None]
    kp = np.arange(SEL_BLOCK)[None, :]
    tile_d = [m * SEL_BLOCK + qq - kp for m in range(N_TOEPLITZ)]
    s_ = np.arange(SUB)[:, None, None]
    e_ = np.arange(CBAND)[None, None, :]
    dist_c = qq[None, :, :] + CMP_STRIDE * (e_ - 4 * (SUB - 1) + 4 * s_) - (3 * CMP_STRIDE + CMP_BLOCK - 1)
    all_d = np.concatenate([d.reshape(-1) for d in tile_d] + [dist_c.reshape(-1)])
    onehot = np.eye(REL_BUCKETS, dtype=np.float32)[_rel_bucket_np(all_d)]
    vals = jnp.dot(jnp.asarray(onehot), rb, precision=lax.Precision.HIGHEST)
    vals = jnp.where(jnp.asarray(all_d >= 0)[:, None], vals, NEG)
    n_t = N_TOEPLITZ * SEL_BLOCK * SEL_BLOCK
    toep = jnp.transpose(vals[:n_t].reshape(N_TOEPLITZ, SEL_BLOCK, SEL_BLOCK, NSA_HEADS), (0, 3, 1, 2))
    band = jnp.transpose(vals[n_t:].reshape(SUB, SEL_BLOCK, CBAND, NSA_HEADS), (3, 0, 1, 2))
    far = rb[REL_BUCKETS - 1]
    far_tile = jnp.broadcast_to(far[:, None, None], (NSA_HEADS, SEL_BLOCK, SEL_BLOCK))
    neg_tile = jnp.full((NSA_HEADS, SEL_BLOCK, SEL_BLOCK), NEG, F32)
    edge_tile = jnp.where(jnp.asarray(kp > qq)[None], far_tile, NEG)

    def tile(m, windowed):
        if m < 0 or (windowed and m > WIN_BLOCKS):
            return neg_tile
        if windowed and m == WIN_BLOCKS:
            return edge_tile
        return toep[m] if m < N_TOEPLITZ else far_tile

    def table(first_m, n_blocks, windowed):
        rows = [jnp.concatenate([tile(first_m + s - c, windowed) for c in range(n_blocks)], axis=2)
                for s in range(SUB)]
        t = jnp.concatenate(rows, axis=1)
        return t.reshape(NSA_KV_HEADS, ROWS, n_blocks * SEL_BLOCK)

    far_rows = jnp.repeat(far, TQ).reshape(NSA_KV_HEADS, ROWS, 1)
    near_t = table(BLOCKS_PER_TILE, 2 * BLOCKS_PER_TILE, False) - far_rows
    win_t = table(WIN_BLOCKS, WIN_BLOCKS + BLOCKS_PER_TILE, True)
    far_t = jnp.broadcast_to(far_rows, (NSA_KV_HEADS, ROWS, LANES))
    band_t = band.reshape(NSA_KV_HEADS, ROWS, CBAND)
    band_hi = band_t.astype(BF16)
    band_lo = (band_t - band_hi.astype(F32)).astype(BF16)
    far_hi = far_rows.astype(BF16)
    far_lo = (far_rows - far_hi.astype(F32)).astype(BF16)
    neg_col = jnp.full((NSA_KV_HEADS, ROWS, 1), NEG, BF16)
    zeros = jnp.zeros((NSA_KV_HEADS, ROWS, D_HEAD - 2 * CBAND - 3), BF16)
    q_lanes = jnp.zeros((NSA_KV_HEADS, ROWS, D_HEAD), BF16)
    cmp_lhs = jnp.concatenate([q_lanes, band_hi, band_lo, far_hi, far_lo, neg_col, zeros], axis=2)
    return near_t, win_t, far_t, cmp_lhs


def _row_max(lane_tiles):
    m = jnp.max(functools.reduce(jnp.maximum, lane_tiles), axis=1, keepdims=True)
    return jnp.broadcast_to(m, lane_tiles[0].shape)


def _nsa_kernel(q_ref, kc_ref, vc_ref, ks_ref, vs_ref, kw_ref, vw_ref, g_ref,
                near_ref, win_ref, far_ref, cmpl_ref, ovt_ref, eye_ref,
                o_ref, ksa, vsa, kwa, vwa, kca, vca, sc_scr, lhs_scr, lhsc_scr, m_scr, al_scr, inv_scr, acc_scr, sa_scr, sb_scr, pa_scr, pb_scr,
                sw_scr, pw_scr, ps_scr, out_scr, cnt_scr):
    qt = pl.program_id(2)
    first = qt * SUB
    T = ks_ref.shape[2]
    ncp = kc_ref.shape[2]

    @pl.when(qt == 0)
    def _init():
        n_chunks = ksa.shape[0] // KT
        r_io = lax.broadcasted_iota(jnp.int32, (KT, KAUG), 0)
        l_io = lax.broadcasted_iota(jnp.int32, (KT, KAUG), 1)
        lv = lax.broadcasted_iota(jnp.int32, (KT, VAUG), 1)
        vpat = jnp.where(lv == D_HEAD, 1.0, 0.0).astype(BF16)

        def fill(c, carry):
            rows = pl.ds(pl.multiple_of(c * KT, KT), KT)
            key = r_io + (c * KT - KPAD)
            blk = key // SEL_BLOCK
            real = (key >= 0) & (key < T)
            hot = real & (((l_io < D_HEAD) & (l_io == blk))
                          | ((l_io >= D_HEAD) & (l_io < 2 * D_HEAD) & (l_io - D_HEAD == blk)))
            hot = hot | ((~real) & (l_io == 3 * D_HEAD))
            pat = jnp.where(hot, 1.0, 0.0).astype(BF16)
            ksa[rows, :] = pat
            kwa[rows, :] = pat
            vsa[rows, :] = vpat
            vwa[rows, :] = vpat
            return carry

        lax.fori_loop(0, n_chunks, fill, 0)
        ksa[KPAD:KPAD + T, 2 * D_HEAD:3 * D_HEAD] = ks_ref[0, 0]
        kwa[KPAD:KPAD + T, 2 * D_HEAD:3 * D_HEAD] = kw_ref[0, 0]
        vsa[KPAD:KPAD + T, 0:D_HEAD] = vs_ref[0, 0]
        vwa[KPAD:KPAD + T, 0:D_HEAD] = vw_ref[0, 0]
        lc = lax.broadcasted_iota(jnp.int32, (ncp, VAUG), 1)
        kca[...] = jnp.zeros((ncp, VAUG), BF16)
        kca[:, 0:D_HEAD] = kc_ref[0, 0]
        vca[...] = jnp.where(lc == D_HEAD, 1.0, 0.0).astype(BF16)
        vca[:, 0:D_HEAD] = vc_ref[0, 0]

    q4 = q_ref[0].reshape(ROWS, D_HEAD)
    gates = g_ref[0, 0]
    row_blocks = [slice(r, r + RB) for r in range(0, ROWS, RB)]

    gate_rows = jnp.concatenate(
        [gates if hh == 0 else pltpu.roll(gates, LANES - hh * N_BRANCH, 1) for hh in range(NSA_GROUP)], axis=0)

    def gate_col(branch):
        return gate_rows[:, branch:branch + 1]

    lane_c = lax.broadcasted_iota(jnp.int32, (ncp, LANES), 1)
    n_c = lax.broadcasted_iota(jnp.int32, (ncp, LANES), 0)
    e = lane_c - D_HEAD
    base = 4 * first + (4 * SUB - 1)
    ind = (((e >= 0) & (e < CBAND) & (n_c == base - e))
           | ((e >= CBAND) & (e < 2 * CBAND) & (n_c == base - (e - CBAND)))
           | (((e == 2 * CBAND) | (e == 2 * CBAND + 1)) & (n_c < base - (CBAND - 1)))
           | ((e == 2 * CBAND + 2) & (n_c > base)))
    rhs_c = jnp.where(lane_c < D_HEAD, kca[...], jnp.where(ind, 1.0, 0.0).astype(BF16))
    lhsc_scr[...] = cmpl_ref[0]
    lhsc_scr[:, 0:D_HEAD] = q4
    sa_scr[:, 0:ncp] = _halves(_dot_nt, lhsc_scr[...], rhs_c)

    lane_q = lax.broadcasted_iota(jnp.int32, (ROWS, LANES), 1)
    lhs_scr[:, LANES:2 * LANES] = jnp.where(lane_q == D_HEAD, NEG, 0.0).astype(BF16)
    lhs_scr[:, LANES:LANES + D_HEAD] = q4
    lhs_win = jnp.concatenate([jnp.zeros((ROWS, LANES), BF16), lhs_scr[:, LANES:2 * LANES]], axis=1)
    win_start = pl.multiple_of(KPAD + (first - WIN_BLOCKS) * SEL_BLOCK, KT)
    sw_scr[...] = _halves(_dot_nt, lhs_win, kwa[pl.ds(win_start, KS + KT), :])

    for i, rows in enumerate(row_blocks):
        s = [sa_scr[rows, c * LANES:(c + 1) * LANES] for c in range(ncp // LANES)]
        m = _row_max(s)
        p = [jnp.where(t > 0.5 * NEG, jnp.exp2(t - m), 0.0) for t in s]
        l = jnp.sum(functools.reduce(jnp.add, p), axis=1, keepdims=True)
        inv = jnp.where(l > 0.0, 1.0 / l, 0.0)
        inv_scr[rows] = inv
        inv_b = jnp.broadcast_to(inv, (RB, LANES))
        tok = slice((i % (TQ // RB)) * RB, (i % (TQ // RB) + 1) * RB)
        for c in range(ncp // LANES):
            lanes = slice(c * LANES, (c + 1) * LANES)
            pb_scr[rows, lanes] = p[c].astype(BF16)
            if i < TQ // RB:
                ps_scr[tok, lanes] = p[c] * inv_b
            else:
                ps_scr[tok, lanes] += p[c] * inv_b
    acc_c = _halves(_dot, pb_scr[:, 0:ncp], vca[...])
    out_scr[...] = acc_c[:, :D_HEAD] * (gate_col(0) * inv_scr[...])
    ps_hi, ps_lo = _split_bf16(ps_scr[...])
    ovt = ovt_ref[...]
    imp_t = _dot_nt(ovt, ps_hi) + _dot_nt(ovt, ps_lo)

    j_t = lax.broadcasted_iota(jnp.int32, (MAX_BLOCKS, TQ), 0)
    cur = first + lax.broadcasted_iota(jnp.int32, (MAX_BLOCKS, TQ), 1) // SEL_BLOCK
    forced = (j_t == 0) | (j_t == cur) | (j_t == cur - 1)
    score_t = jnp.where(forced, FORCED_SCORE, jnp.where(j_t <= cur, imp_t, NEG))
    sc_scr[...] = score_t

    cnt_scr[...] = jnp.zeros((MAX_BLOCKS, TQ), jnp.int32)
    sub_io = lax.broadcasted_iota(jnp.int32, (8, TQ), 0)
    n_groups = MAX_BLOCKS // 8
    for grp in range(n_groups):
        @pl.when(grp * 8 < first + SUB)
        def _count_group():
            targets = [score_t[8 * jg:8 * jg + 8] for jg in range(n_groups)]
            acc = [jnp.zeros((8, TQ), jnp.int32) for _ in range(n_groups)]
            for k in range(8):
                row_i = jnp.broadcast_to(sc_scr[grp * 8 + k:grp * 8 + k + 1, :], (8, TQ))
                for jg in range(n_groups):
                    if jg > grp:
                        beats = row_i >= targets[jg]
                    elif jg < grp:
                        beats = row_i > targets[jg]
                    else:
                        beats = (row_i > targets[jg]) | ((row_i == targets[jg]) & (sub_io > k))
                    acc[jg] = acc[jg] + beats.astype(jnp.int32)
            cnt_scr[...] += jnp.concatenate(acc, axis=0)

    cnt = cnt_scr[...]
    sel_t = jnp.where((cnt < SEL_TOPK) & (j_t <= cur), 1.0, 0.0).astype(BF16)
    sel = _dot_nt(eye_ref[...], jnp.concatenate([sel_t, sel_t], axis=0))
    sel4 = jnp.concatenate([sel] * NSA_GROUP, axis=0) > 0.5

    hi, lo = _split_bf16(jnp.where(sel4, far_ref[0], NEG))
    lhs_scr[:, 0:LANES] = jnp.where(lane_q < D_HEAD, hi, lo)
    m_scr[...] = jnp.full((ROWS, LANES), NEG, F32)
    acc_scr[...] = jnp.zeros((ROWS, VAUG), F32)

    def scores(w, dst):
        start = pl.multiple_of(KPAD + w * KS, KS)
        dst[...] = _halves(_dot_nt, lhs_scr[...], ksa[pl.ds(start, KS), :])

    def consume(w, src, p_scr, with_table):
        n_lt = KS // LANES
        if with_table:
            rel = [2 * w + half - (qt - 1) for half in range(KS // KT)]
            col = [jnp.clip(r, 0, 1) * KT for r in rel]
        for rows in row_blocks:
            s = [src[rows, c * LANES:(c + 1) * LANES] for c in range(n_lt)]
            if with_table:
                for c in range(n_lt):
                    half, off = divmod(c * LANES, KT)
                    at = pl.multiple_of(col[half] + off, LANES)
                    s[c] = s[c] + jnp.where(rel[half] >= 0, near_ref[0, rows, pl.ds(at, LANES)], 0.0)
            m_prev = m_scr[rows]
            m = jnp.maximum(m_prev, _row_max(s))
            al_scr[rows] = jnp.exp2(m_prev - m)
            m_scr[rows] = m
            for c in range(n_lt):
                p_scr[rows, c * LANES:(c + 1) * LANES] = jnp.exp2((s[c] - m).astype(BF16))
        start = pl.multiple_of(KPAD + w * KS, KS)
        acc_scr[...] = acc_scr[...] * al_scr[...] + _dot(p_scr[:, 0:KS], vsa[pl.ds(start, KS), :])

    scores(0, sa_scr)

    for rows in row_blocks:
        s = [sw_scr[rows, c * LANES:(c + 1) * LANES] + win_ref[0, rows, c * LANES:(c + 1) * LANES]
             for c in range((KS + KT) // LANES)]
        m = _row_max(s)
        for c in range((KS + KT) // LANES):
            pw_scr[rows, c * LANES:(c + 1) * LANES] = jnp.exp2((s[c] - m).astype(BF16))
    acc_w = _halves(_dot, pw_scr[...], vwa[pl.ds(win_start, KS + KT), :])
    out_scr[...] += acc_w[:, :D_HEAD] * (gate_col(2) / acc_w[:, D_HEAD:D_HEAD + 1])

    def pair(with_table, u, carry):
        scores(2 * u + 1, sb_scr)
        consume(2 * u, sa_scr, pa_scr, with_table)
        scores(2 * u + 2, sa_scr)
        consume(2 * u + 1, sb_scr, pb_scr, with_table)
        return carry

    n_trips = qt // 4 + 1
    n_plain = jnp.maximum(qt - 1, 0) // 4
    lax.fori_loop(0, n_plain, functools.partial(pair, False), 0)
    lax.fori_loop(n_plain, n_trips, functools.partial(pair, True), 0)
    acc_s = acc_scr[...]

    o = out_scr[...] + acc_s[:, :D_HEAD] * (gate_col(1) / acc_s[:, D_HEAD:D_HEAD + 1])
    for hh in range(NSA_GROUP):
        o_ref[0, :, hh * D_HEAD:(hh + 1) * D_HEAD] = o[hh * TQ:(hh + 1) * TQ].astype(BF16)


def _nsa_call(q, kcmp, vcmp, ks, vs, kw, vw, gates, near_t, win_t, far_t, cmp_lhs, ovt, eye):
    B, _, T, _ = q.shape
    G = NSA_KV_HEADS
    ncp = kcmp.shape[2]
    rows_kv = KPAD + T + KS
    per_bg =lambda a: pl.BlockSpec((1, 1) + a.shape[2:], lambda b, g, i: (b, g, 0, 0))
    per_g = lambda a: pl.BlockSpec((1,) + a.shape[1:], lambda b, g, i: (g, 0, 0))
    const2 = lambda a: pl.BlockSpec(a.shape, lambda b, g, i: (0, 0))
    return pl.pallas_call(
        _nsa_kernel,
        grid=(B, G, T // TQ),
        in_specs=[
            pl.BlockSpec((1, NSA_GROUP, TQ, D_HEAD), lambda b, g, i: (b, g, i, 0)),
            per_bg(kcmp), per_bg(vcmp), per_bg(ks), per_bg(vs), per_bg(kw), per_bg(vw),
            pl.BlockSpec((1, 1, TQ, LANES), lambda b, g, i: (b, g, i, 0)),
            per_g(near_t), per_g(win_t), per_g(far_t), per_g(cmp_lhs), const2(ovt), const2(eye),
        ],
        out_specs=pl.BlockSpec((1, TQ, NSA_GROUP * D_HEAD), lambda b, g, i: (b, i, g)),
        out_shape=jax.ShapeDtypeStruct((B, T, D_NSA), BF16),
        scratch_shapes=[
            pltpu.VMEM((rows_kv, KAUG), BF16), pltpu.VMEM((rows_kv, VAUG), BF16),
            pltpu.VMEM((rows_kv, KAUG), BF16), pltpu.VMEM((rows_kv, VAUG), BF16),
            pltpu.VMEM((ncp, VAUG), BF16), pltpu.VMEM((ncp, VAUG), BF16),
            pltpu.VMEM((MAX_BLOCKS, TQ), F32),
            pltpu.VMEM((ROWS, KAUG), BF16), pltpu.VMEM((ROWS, LANES), BF16),
            pltpu.VMEM((ROWS, LANES), F32), pltpu.VMEM((ROWS, LANES), F32), pltpu.VMEM((ROWS, 1), F32),
            pltpu.VMEM((ROWS, VAUG), F32),
            pltpu.VMEM((ROWS, KS), F32), pltpu.VMEM((ROWS, KS), F32),
            pltpu.VMEM((ROWS, KS), BF16), pltpu.VMEM((ROWS, KS), BF16),
            pltpu.VMEM((ROWS, KS + KT), F32), pltpu.VMEM((ROWS, KS + KT), BF16),
            pltpu.VMEM((TQ, ncp), F32), pltpu.VMEM((ROWS, D_HEAD), F32), pltpu.VMEM((MAX_BLOCKS, TQ), jnp.int32),
        ],
        compiler_params=pltpu.CompilerParams(
            dimension_semantics=("arbitrary", "arbitrary", "arbitrary"), vmem_limit_bytes=VMEM_LIMIT),
        name="nsa",
    )(q, kcmp, vcmp, ks, vs, kw, vw, gates, near_t, win_t, far_t, cmp_lhs, ovt, eye)


def _ffn_kernel(x_ref, on_ref, os_ref, wo_n_ref, wo_s_ref, fn_ref, wup_ref, cw_ref, cb_ref, wdn_ref,
                out_ref, prev_ref, perm_ref, h_ref, ua_scr, ub_scr, act_ref):
    tm = x_ref.shape[1]
    seg = tm // 8
    n_chunks = wup_ref.shape[1]

    @pl.when(pl.program_id(1) == 0)
    def _start_of_sequence():
        prev_ref[...] = jnp.zeros(prev_ref.shape, F32)

    x1 = x_ref[0] + _dot(on_ref[0], wo_n_ref[...]) + _dot(os_ref[0], wo_s_ref[...])
    ms = jnp.mean(x1 * x1, axis=-1, keepdims=True)
    hn = x1 * lax.rsqrt(ms + EPS) * fn_ref[...]
    out_ref[0] = x1
    n_lt = hn.shape[1] // LANES
    for c in range(n_lt):
        perm_ref[c] = hn[:, c * LANES:(c + 1) * LANES]
    for u in range(seg // 2):
        groups = [jnp.concatenate([perm_ref[c, pl.ds(2 * u + k, 8, stride=seg), :] for c in range(n_lt)], axis=1)
                  for k in range(2)]
        h_ref[16 * u:16 * u + 16, :] = jnp.concatenate(groups, axis=0).astype(BF16)

    def up(c, dst):
        h = h_ref[...]
        for half in range(2):
            dst[half] = _dot(h, wup_ref[half, c])

    first_sublane = lax.broadcasted_iota(jnp.int32, (8, FFN_CHUNK), 0) == 0

    def conv_gate(c, src):
        for r0 in range(0, tm, FFN_ROWS):
            ys = []
            for half in range(2):
                w = cw_ref[half, c]
                x0 = src[half, r0:r0 + FFN_ROWS]
                if r0 == 0:
                    wrap = [jnp.where(first_sublane, pltpu.roll(prev_ref[half, c, 8 * k:8 * k + 8], 1, 0),
                                      pltpu.roll(src[half, tm - 16 + 8 * k:tm - 8 + 8 * k], 1, 0)) for k in range(2)]
                    x1_ = jnp.concatenate([wrap[1], src[half, 0:FFN_ROWS - 8]], axis=0)
                    x2_ = jnp.concatenate([wrap[0], wrap[1], src[half, 0:FFN_ROWS - 16]], axis=0)
                else:
                    x1_ = src[half, r0 - 8:r0 + FFN_ROWS - 8]
                    x2_ = src[half, r0 - 16:r0 + FFN_ROWS - 16]
                ys.append(x0 * w[2:3] + x1_ * w[1:2] + x2_ * w[0:1] + cb_ref[half, c])
            a, g = ys
            col = pl.multiple_of(c * FFN_CHUNK, FFN_CHUNK)
            act_ref[r0:r0 + FFN_ROWS, pl.ds(col, FFN_CHUNK)] = (g * _sigmoid(g) * a).astype(BF16)
        for half in range(2):
            prev_ref[half, c] = src[half, tm - 16:tm]

    up(0, ua_scr)

    def pair(u, carry):
        up(2 * u + 1, ub_scr)
        conv_gate(2 * u, ua_scr)
        up(2 * u + 2, ua_scr)
        conv_gate(2 * u + 1, ub_scr)
        return carry

    lax.fori_loop(0, (n_chunks - 1) // 2, pair, 0)
    conv_gate(n_chunks - 1, ua_scr)
    y = _halves(_dot, act_ref[...], wdn_ref[...])
    for c in range(n_lt):
        perm_ref[c] = y[:, c * LANES:(c + 1) * LANES]
    for sgm in range(8):
        natural = jnp.concatenate([perm_ref[c, pl.ds(sgm, seg, stride=8), :] for c in range(n_lt)], axis=1)
        out_ref[0, sgm * seg:(sgm + 1) * seg, :] += natural


def _ffn_call(x, o_nsa, o_sgu, wo_n, wo_s, ffn_norm, w_up, conv_w, conv_b, w_down, tm):
    B, T, _ = x.shape
    n_chunks = w_up.shape[1]
    assert n_chunks % 2 == 1
    resident =lambda a: pl.BlockSpec(a.shape, lambda b, i: (0,) * a.ndim, pipeline_mode=pl.Buffered(1))
    tile = lambda w: pl.BlockSpec((1, tm, w), lambda b, i: (b, i, 0))
    return pl.pallas_call(
        _ffn_kernel,
        grid=(B, T // tm),
        in_specs=[tile(D_MODEL), tile(D_NSA), tile(D_GMLP), resident(wo_n), resident(wo_s),
                  resident(ffn_norm), resident(w_up), resident(conv_w), resident(conv_b), resident(w_down)],
        out_specs=tile(D_MODEL),
        out_shape=jax.ShapeDtypeStruct((B, T, D_MODEL), F32),
        scratch_shapes=[pltpu.VMEM((2, n_chunks, 16, FFN_CHUNK), F32), pltpu.VMEM((D_MODEL // LANES, tm, LANES), F32),
                        pltpu.VMEM((tm, D_MODEL), BF16),
                        pltpu.VMEM((2, tm, FFN_CHUNK), F32), pltpu.VMEM((2, tm, FFN_CHUNK), F32),
                        pltpu.VMEM((tm, D_FF), BF16)],
        compiler_params=pltpu.CompilerParams(
            dimension_semantics=("arbitrary", "arbitrary"), vmem_limit_bytes=VMEM_LIMIT),
        name="ffn",
    )(x, o_nsa, o_sgu, wo_n, wo_s, ffn_norm, w_up, conv_w, conv_b, w_down)


def _mixers(x, rel_bias, attn_norm, w_in, q_norm, k_norm_cmp, k_norm_slc, k_norm_win,
            cmp_pe_k, cmp_w1_k, cmp_b1_k, cmp_w2_k, cmp_pe_v, cmp_w1_v, cmp_b1_v, cmp_w2_v,
            sgu_norm, sgu_w, sgu_b):
    B, T, _ = x.shape
    assert T % 512 == 0 and T // SEL_BLOCK <= MAX_BLOCKS and (T // CMP_STRIDE) % LANES == 0
    ncp = T // CMP_STRIDE
    G, R = NSA_KV_HEADS, NSA_GROUP

    o_kv = D_NSA
    o_g = D_NSA + 6 * D_KV
    o_uv = o_g + N_GATES
    w_q = w_in[:, :o_kv].astype(BF16)
    w_kv = w_in[:, o_kv:o_g].astype(BF16)
    per_group = NSA_GROUP * N_BRANCH
    w_g = jnp.pad(w_in[:, o_g:o_uv].reshape(D_MODEL, G, per_group), ((0, 0), (0, 0), (0, LANES - per_group)))
    w_g = w_g.reshape(D_MODEL, G * LANES).astype(BF16)
    w_uv = w_in[:, o_uv:].astype(BF16)
    qn = (jnp.tile(q_norm, NSA_HEADS) * (D_HEAD ** -0.5 * LOG2E)).reshape(1, D_NSA)
    ksn = jnp.tile(k_norm_slc, G).reshape(1, D_KV)
    kwn = jnp.tile(k_norm_win, G).reshape(1, D_KV)
    sgu_bt = jnp.repeat(sgu_b.T, GMLP_GROUP_DIM, axis=1)
    grp = np.arange(D_NSA) // D_HEAD
    bd = jnp.asarray(grp[:, None] == grp[None, :], BF16)

    q, kc, vc, ks, vs, kw, vw, gates, o_sgu = _proj_call(
        x, attn_norm.reshape(1, D_MODEL), w_q, w_kv, w_g, w_uv, qn, ksn, kwn,
        sgu_norm.reshape(1, D_GMLP), sgu_w, sgu_bt, bd, tm=512)

    half = CMP_STRIDE * D_HEAD
    kcmp, vcmp = _compress_call(
        kc.reshape(B, G, ncp, half), vc.reshape(B, G, ncp, half),
        cmp_pe_k.reshape(2, half), cmp_w1_k.astype(BF16), cmp_b1_k.reshape(1, CMP_HIDDEN),
        cmp_w2_k.astype(BF16), k_norm_cmp.reshape(1, D_HEAD),
        cmp_pe_v.reshape(2, half), cmp_w1_v.astype(BF16), cmp_b1_v.reshape(1, CMP_HIDDEN),
        cmp_w2_v.astype(BF16))

    near_t, win_t, far_t, cmp_lhs = _bias_tables(rel_bias)
    jj = np.arange(MAX_BLOCKS)[:, None]
    nn = np.arange(ncp)[None, :]
    ovt = jnp.asarray((nn >= 4 * jj - 1) & (nn <= 4 * jj + 3), BF16)
    eye = jnp.asarray(np.eye(TQ), BF16)

    o_nsa = _nsa_call(q, kcmp, vcmp, ks, vs, kw, vw, gates, near_t, win_t, far_t, cmp_lhs, ovt, eye)
    return o_nsa, o_sgu


def _layer(x, rel_bias, attn_norm, w_in, q_norm, k_norm_cmp, k_norm_slc, k_norm_win,
           cmp_pe_k, cmp_w1_k, cmp_b1_k, cmp_w2_k, cmp_pe_v, cmp_w1_v, cmp_b1_v, cmp_w2_v,
           sgu_norm, sgu_w, sgu_b, w_out, ffn_norm, w_up, conv_w, conv_b, w_down):
    o_nsa, o_sgu = _mixers(x, rel_bias, attn_norm, w_in, q_norm, k_norm_cmp, k_norm_slc, k_norm_win,
                           cmp_pe_k, cmp_w1_k, cmp_b1_k, cmp_w2_k, cmp_pe_v, cmp_w1_v, cmp_b1_v, cmp_w2_v,
                           sgu_norm, sgu_w, sgu_b)
    wo = w_out.astype(BF16)
    n_chunks = D_FF // FFN_CHUNK
    wu = jnp.transpose(w_up.astype(BF16).reshape(D_MODEL, 2, n_chunks, FFN_CHUNK), (1, 2, 0, 3))
    cw = jnp.transpose(conv_w.reshape(conv_w.shape[0], 2, n_chunks, FFN_CHUNK), (1, 2, 0, 3))
    cb = conv_b.reshape(2, n_chunks, 1, FFN_CHUNK)
    wd = w_down.astype(BF16)
    return _ffn_call(x, o_nsa, o_sgu, wo[:D_NSA], wo[D_NSA:], ffn_norm.reshape(1, D_MODEL), wu, cw, cb, wd, tm=512)


def kernel(x, rel_bias, attn_norm, w_in, q_norm, k_norm_cmp, k_norm_slc, k_norm_win,
           cmp_pe_k, cmp_w1_k, cmp_b1_k, cmp_w2_k, cmp_pe_v, cmp_w1_v, cmp_b1_v, cmp_w2_v,
           sgu_norm, sgu_w, sgu_b, w_out, ffn_norm, w_up, conv_w, conv_b, w_down):
    depth = attn_norm.shape[0]
    for l in range(depth):
        x = _layer(x, rel_bias, attn_norm[l], w_in[l], q_norm[l], k_norm_cmp[l], k_norm_slc[l], k_norm_win[l],
                   cmp_pe_k[l], cmp_w1_k[l], cmp_b1_k[l], cmp_w2_k[l], cmp_pe_v[l], cmp_w1_v[l], cmp_b1_v[l],
                   cmp_w2_v[l], sgu_norm[l], sgu_w[l], sgu_b[l], w_out[l], ffn_norm[l], w_up[l], conv_w[l],
                   conv_b[l], w_down[l])
    return x
```

```python
import functools
import math

import numpy as np
import jax
import jax.numpy as jnp
from jax import lax
from jax.experimental import pallas as pl
from jax.experimental.pallas import tpu as pltpu

F32 = jnp.float32
BF16 = jnp.bfloat16

D_MODEL = 1024
D_HEAD = 64
NSA_HEADS = 8
NSA_KV_HEADS = 2
NSA_GROUP = NSA_HEADS // NSA_KV_HEADS
D_NSA = NSA_HEADS * D_HEAD
D_KV = NSA_KV_HEADS * D_HEAD
N_BRANCH = 3
N_GATES = NSA_HEADS * N_BRANCH
CMP_BLOCK = 32
CMP_STRIDE = 16
CMP_HIDDEN = 256
SEL_BLOCK = 64
SEL_TOPK = 16
WINDOW = 512
GMLP_GROUPS = 8
GMLP_GROUP_DIM = 64
D_GMLP = GMLP_GROUPS * GMLP_GROUP_DIM
CHUNK = 128
D_MIX = D_NSA + D_GMLP
REL_BUCKETS = 32
REL_MAX_DIST = 128
D_FF = 2816
EPS = 1e-6
NEG = -1e30
FORCED_SCORE = 1e4
LOG2E = 1.4426950408889634

LANES = 128
TQ = 256
SUB = TQ // SEL_BLOCK
ROWS = NSA_GROUP * TQ
KT = 256
KS = 2 * KT
RB = 64
BLOCKS_PER_TILE = KT // SEL_BLOCK
WIN_BLOCKS = WINDOW // SEL_BLOCK
KPAD = 1280
N_TOEPLITZ = 3
MAX_BLOCKS = 64
KAUG = 256
VAUG = 128
CBAND = 28
FFN_CHUNK = 256
FFN_ROWS = 64
VMEM_LIMIT = 56 * 1024 * 1024

_NT = (((1,), (1,)), ((), ()))


def _dot(a, b):
    return jnp.dot(a, b, preferred_element_type=F32)


def _dot_nt(a, b):
    return lax.dot_general(a, b, _NT, preferred_element_type=F32)


def _halves(dot, a, b):
    h = a.shape[0] // 2
    return jnp.concatenate([dot(a[:h], b), dot(a[h:], b)], axis=0)


def _split_bf16(x):
    hi = x.astype(BF16)
    lo = (x - hi.astype(F32)).astype(BF16)
    return hi, lo


def _gelu_tanh(x):
    return 0.5 * x * (1.0 + jnp.tanh(0.7978845608028654 * (x + 0.044715 * (x * x * x))))


def _sigmoid(x):
    return 0.5 * (1.0 + jnp.tanh(0.5 * x))


def _group_mean_sq(t, ones_blockdiag, width):
    t2 = t * t
    hi, lo = _split_bf16(t2)
    return (_dot(hi, ones_blockdiag) + _dot(lo, ones_blockdiag)) * (1.0 / width)


def _proj_kernel(x_ref, an_ref, wq_ref, wkv_ref, wg_ref, wuv_ref, qn_ref, ksn_ref, kwn_ref,
                 sgun_ref, sguw_ref, sgub_ref, bd_ref,
                 q_out, kc_out, vc_out, ks_out, vs_out, kw_out, vw_out, g_out, sgu_out):
    tm = x_ref.shape[1]
    x = x_ref[0]
    ms = jnp.mean(x * x, axis=-1, keepdims=True)
    h = (x * lax.rsqrt(ms + EPS) * an_ref[...]).astype(BF16)

    bd = bd_ref[...]
    q = _dot(h, wq_ref[...])
    qn = q * lax.rsqrt(_group_mean_sq(q, bd, D_HEAD) + EPS) * qn_ref[...]
    for hh in range(NSA_HEADS):
        q_out[0, hh] = qn[:, hh * D_HEAD:(hh + 1) * D_HEAD].astype(BF16)

    kv = _dot(h, wkv_ref[...])
    bd_kv = bd[:D_KV, :D_KV]
    kc = kv[:, 0 * D_KV:1 * D_KV]
    vc = kv[:, 1 * D_KV:2 * D_KV]
    ks = kv[:, 2 * D_KV:3 * D_KV]
    vs = kv[:, 3 * D_KV:4 * D_KV]
    kw = kv[:, 4 * D_KV:5 * D_KV]
    vw = kv[:, 5 * D_KV:6 * D_KV]
    ks = ks * lax.rsqrt(_group_mean_sq(ks, bd_kv, D_HEAD) + EPS) * ksn_ref[...]
    kw = kw * lax.rsqrt(_group_mean_sq(kw, bd_kv, D_HEAD) + EPS) * kwn_ref[...]
    for t, o_ref in ((kc, kc_out), (vc, vc_out), (ks, ks_out), (vs, vs_out), (kw, kw_out), (vw, vw_out)):
        for g in range(NSA_KV_HEADS):
            o_ref[0, g] = t[:, g * D_HEAD:(g + 1) * D_HEAD].astype(BF16)

    gates = _sigmoid(_dot(h, wg_ref[...]))
    for g in range(NSA_KV_HEADS):
        g_out[0, g] = gates[:, g * LANES:(g + 1) * LANES]

    uv = _gelu_tanh(_dot(h, wuv_ref[...]))
    u = uv[:, :D_GMLP]
    v = uv[:, D_GMLP:]
    vms = jnp.mean(v * v, axis=-1, keepdims=True)
    vb = (v * lax.rsqrt(vms + EPS) * sgun_ref[...]).astype(BF16)

    row = lax.broadcasted_iota(jnp.int32, (CHUNK, CHUNK), 0)
    col = lax.broadcasted_iota(jnp.int32, (CHUNK, CHUNK), 1)
    tril = col <= row
    w_tril = [jnp.where(tril, sguw_ref[g], 0.0).astype(BF16) for g in range(GMLP_GROUPS)]
    w_pair = [jnp.concatenate([w_tril[2 * p], w_tril[2 * p + 1]], axis=1) for p in range(D_GMLP // LANES)]
    first_half = lax.broadcasted_iota(jnp.int32, (CHUNK, LANES), 1) < GMLP_GROUP_DIM
    zero = jnp.zeros((CHUNK, LANES), BF16)
    for c in range(tm // CHUNK):
        rows = slice(c * CHUNK, (c + 1) * CHUNK)
        zs = []
        for p in range(D_GMLP // LANES):
            blk = vb[rows, p * LANES:(p + 1) * LANES]
            rhs = jnp.concatenate([jnp.where(first_half, blk, zero), jnp.where(first_half, zero, blk)], axis=0)
            zs.append(_dot(w_pair[p], rhs))
        z = jnp.concatenate(zs, axis=1) + sgub_ref[...]
        sgu_out[0, rows, :] = (u[rows, :] * z).astype(BF16)


def _proj_call(x, attn_norm, w_q, w_kv, w_g, w_uv, qn, ksn, kwn, sgu_norm, sgu_w, sgu_bt, bd, tm):
    B, T, _ = x.shape
    const2 = lambda b, i: (0, 0)
    const3 = lambda b, i: (0, 0, 0)
    head_spec = lambda nh: pl.BlockSpec((1, nh, tm, D_HEAD), lambda b, i: (b, 0, i, 0))
    kv_shape = jax.ShapeDtypeStruct((B, NSA_KV_HEADS, T, D_HEAD), BF16)
    return pl.pallas_call(
        _proj_kernel,
        grid=(B, T // tm),
        in_specs=[
            pl.BlockSpec((1, tm, D_MODEL), lambda b, i: (b, i, 0)),
            pl.BlockSpec((1, D_MODEL), const2),
            pl.BlockSpec(w_q.shape, const2),
            pl.BlockSpec(w_kv.shape, const2),
            pl.BlockSpec(w_g.shape, const2),
            pl.BlockSpec(w_uv.shape, const2),
            pl.BlockSpec((1, D_NSA), const2),
            pl.BlockSpec((1, D_KV), const2),
            pl.BlockSpec((1, D_KV), const2),
            pl.BlockSpec((1, D_GMLP), const2),
            pl.BlockSpec(sgu_w.shape, const3),
            pl.BlockSpec(sgu_bt.shape, const2),
            pl.BlockSpec(bd.shape, const2),
        ],
        out_specs=[
            head_spec(NSA_HEADS),
            head_spec(NSA_KV_HEADS), head_spec(NSA_KV_HEADS), head_spec(NSA_KV_HEADS),
            head_spec(NSA_KV_HEADS), head_spec(NSA_KV_HEADS), head_spec(NSA_KV_HEADS),
            pl.BlockSpec((1, NSA_KV_HEADS, tm, LANES), lambda b, i: (b, 0, i, 0)),
            pl.BlockSpec((1, tm, D_GMLP), lambda b, i: (b, i, 0)),
        ],
        out_shape=[
            jax.ShapeDtypeStruct((B, NSA_HEADS, T, D_HEAD), BF16),
            kv_shape, kv_shape, kv_shape, kv_shape, kv_shape, kv_shape,
            jax.ShapeDtypeStruct((B, NSA_KV_HEADS, T, LANES), F32),
            jax.ShapeDtypeStruct((B, T, D_GMLP), BF16),
        ],
        compiler_params=pltpu.CompilerParams(
            dimension_semantics=("arbitrary", "arbitrary"), vmem_limit_bytes=VMEM_LIMIT),
        name="proj",
    )(x, attn_norm, w_q, w_kv, w_g, w_uv, qn, ksn, kwn, sgu_norm, sgu_w, sgu_bt, bd)


def _compress_one(tok_ref, pe_ref, w1_ref, b1_ref, w2_ref):
    half = CMP_STRIDE * D_HEAD
    tok = tok_ref[0, 0].astype(F32)
    top = (tok + pe_ref[0:1, :]).astype(BF16)
    bot = (tok + pe_ref[1:2, :]).astype(BF16)
    a = _dot(top, w1_ref[:half, :])
    b = _dot(bot, w1_ref[half:, :])
    ncp = a.shape[0]
    pre = a + pltpu.roll(b, ncp - 1, 0) + b1_ref[...]
    hid = _gelu_tanh(pre).astype(BF16)
    return _dot(hid, w2_ref[...])


def _compress_kernel(kc_ref, vc_ref, pek_ref, w1k_ref, b1k_ref, w2k_ref, kn_ref,
                     pev_ref, w1v_ref, b1v_ref, w2v_ref, k_out, v_out):
    k = _compress_one(kc_ref, pek_ref, w1k_ref, b1k_ref, w2k_ref)
    kms = jnp.mean(k * k, axis=-1, keepdims=True)
    k_out[0, 0] = (k * lax.rsqrt(kms + EPS) * kn_ref[...]).astype(BF16)
    v_out[0, 0] = _compress_one(vc_ref, pev_ref, w1v_ref, b1v_ref, w2v_ref).astype(BF16)


def _compress_call(kc, vc, pek, w1k, b1k, w2k, kn, pev, w1v, b1v, w2v):
    B, G, ncp, width = kc.shape
    const2 = lambda b, g: (0, 0)
    tok_spec = pl.BlockSpec((1, 1, ncp, width), lambda b, g: (b, g, 0, 0))
    out_spec = pl.BlockSpec((1, 1, ncp, D_HEAD), lambda b, g: (b, g, 0, 0))
    full = lambda a: pl.BlockSpec(a.shape, const2)
    out_shape = jax.ShapeDtypeStruct((B, G, ncp, D_HEAD), BF16)
    return pl.pallas_call(
        _compress_kernel,
        grid=(B, G),
        in_specs=[tok_spec, tok_spec, full(pek), full(w1k), full(b1k), full(w2k), full(kn),
                  full(pev), full(w1v), full(b1v), full(w2v)],
        out_specs=[out_spec, out_spec],
        out_shape=[out_shape, out_shape],
        compiler_params=pltpu.CompilerParams(
            dimension_semantics=("arbitrary", "arbitrary"), vmem_limit_bytes=VMEM_LIMIT),
        name="compress",
    )(kc, vc, pek, w1k, b1k, w2k, kn, pev, w1v, b1v, w2v)


def _rel_bucket_np(dist):
    max_exact = REL_BUCKETS // 2
    d = np.maximum(dist, 1).astype(np.float32)
    log_b = max_exact + (np.log(d / np.float32(max_exact)) / np.float32(math.log(REL_MAX_DIST / max_exact))
                         * np.float32(REL_BUCKETS - max_exact)).astype(np.int32)
    log_b = np.clip(log_b, max_exact, REL_BUCKETS - 1)
    return np.where(dist < max_exact, np.maximum(dist, 0), log_b)


def _bias_tables(rel_bias):
    rb = rel_bias.astype(F32) * LOG2E
    qq = np.arange(SEL_BLOCK)[:, None]
    kp = np.arange(SEL_BLOCK)[None, :]
    tile_d = [m * SEL_BLOCK + qq - kp for m in range(N_TOEPLITZ)]
    s_ = np.arange(SUB)[:, None, None]
    e_ = np.arange(CBAND)[None, None, :]
    dist_c = qq[None, :, :] + CMP_STRIDE * (e_ - 4 * (SUB - 1) + 4 * s_) - (3 * CMP_STRIDE + CMP_BLOCK - 1)
    all_d = np.concatenate([d.reshape(-1) for d in tile_d] + [dist_c.reshape(-1)])
    onehot = np.eye(REL_BUCKETS, dtype=np.float32)[_rel_bucket_np(all_d)]
    vals = jnp.dot(jnp.asarray(onehot), rb, precision=lax.Precision.HIGHEST)
    vals = jnp.where(jnp.asarray(all_d >= 0)[:, None], vals, NEG)
    n_t = N_TOEPLITZ * SEL_BLOCK * SEL_BLOCK
    toep = jnp.transpose(vals[:n_t].reshape(N_TOEPLITZ, SEL_BLOCK, SEL_BLOCK, NSA_HEADS), (0, 3, 1, 2))
    band = jnp.transpose(vals[n_t:].reshape(SUB, SEL_BLOCK, CBAND, NSA_HEADS), (3, 0, 1, 2))
    far = rb[REL_BUCKETS - 1]
    far_tile = jnp.broadcast_to(far[:, None, None], (NSA_HEADS, SEL_BLOCK, SEL_BLOCK))
    neg_tile = jnp.full((NSA_HEADS, SEL_BLOCK, SEL_BLOCK), NEG, F32)
    edge_tile = jnp.where(jnp.asarray(kp > qq)[None], far_tile, NEG)

    def tile(m, windowed):
        if m < 0 or (windowed and m > WIN_BLOCKS):
            return neg_tile
        if windowed and m == WIN_BLOCKS:
            return edge_tile
        return toep[m] if m < N_TOEPLITZ else far_tile

    def table(first_m, n_blocks, windowed):
        rows = [jnp.concatenate([tile(first_m + s - c, windowed) for c in range(n_blocks)], axis=2)
                for s in range(SUB)]
        t = jnp.concatenate(rows, axis=1)
        return t.reshape(NSA_KV_HEADS, ROWS, n_blocks * SEL_BLOCK)

    far_rows = jnp.repeat(far, TQ).reshape(NSA_KV_HEADS, ROWS, 1)
    near_t = table(BLOCKS_PER_TILE, 2 * BLOCKS_PER_TILE, False) - far_rows
    win_t = table(WIN_BLOCKS, WIN_BLOCKS + BLOCKS_PER_TILE, True)
    far_t = jnp.broadcast_to(far_rows, (NSA_KV_HEADS, ROWS, LANES))
    band_t = band.reshape(NSA_KV_HEADS, ROWS, CBAND)
    band_hi = band_t.astype(BF16)
    band_lo = (band_t - band_hi.astype(F32)).astype(BF16)
    far_hi = far_rows.astype(BF16)
    far_lo = (far_rows - far_hi.astype(F32)).astype(BF16)
    neg_col = jnp.full((NSA_KV_HEADS, ROWS, 1), NEG, BF16)
    zeros = jnp.zeros((NSA_KV_HEADS, ROWS, D_HEAD - 2 * CBAND - 3), BF16)
    q_lanes = jnp.zeros((NSA_KV_HEADS, ROWS, D_HEAD), BF16)
    cmp_lhs = jnp.concatenate([q_lanes, band_hi, band_lo, far_hi, far_lo, neg_col, zeros], axis=2)
    return near_t, win_t, far_t, cmp_lhs


def _row_max(lane_tiles):
    m = jnp.max(functools.reduce(jnp.maximum, lane_tiles), axis=1, keepdims=True)
    return jnp.broadcast_to(m, lane_tiles[0].shape)


def _nsa_kernel(q_ref, kc_ref, vc_ref, ks_ref, vs_ref, kw_ref, vw_ref, g_ref,
                near_ref, win_ref, far_ref, cmpl_ref, ovt_ref, eye_ref,
                o_ref, ksa, vsa, kwa, vwa, kca, vca, sc_scr, lhs_scr, lhsc_scr, m_scr, al_scr, inv_scr, acc_scr, sa_scr, sb_scr, pa_scr, pb_scr,
                sw_scr, pw_scr, ps_scr, out_scr, cnt_scr):
    qt = pl.program_id(2)
    first = qt * SUB
    T = ks_ref.shape[2]
    ncp = kc_ref.shape[2]

    @pl.when(qt == 0)
    def _init():
        n_chunks = ksa.shape[0] // KT
        r_io = lax.broadcasted_iota(jnp.int32, (KT, KAUG), 0)
        l_io = lax.broadcasted_iota(jnp.int32, (KT, KAUG), 1)
        lv = lax.broadcasted_iota(jnp.int32, (KT, VAUG), 1)
        vpat = jnp.where(lv == D_HEAD, 1.0, 0.0).astype(BF16)

        def fill(c, carry):
            rows = pl.ds(pl.multiple_of(c * KT, KT), KT)
            key = r_io + (c * KT - KPAD)
            blk = key // SEL_BLOCK
            real = (key >= 0) & (key < T)
            hot = real & (((l_io < D_HEAD) & (l_io == blk))
                          | ((l_io >= D_HEAD) & (l_io < 2 * D_HEAD) & (l_io - D_HEAD == blk)))
            hot = hot | ((~real) & (l_io == 3 * D_HEAD))
            pat = jnp.where(hot, 1.0, 0.0).astype(BF16)
            ksa[rows, :] = pat
            kwa[rows, :] = pat
            vsa[rows, :] = vpat
            vwa[rows, :] = vpat
            return carry

        lax.fori_loop(0, n_chunks, fill, 0)
        ksa[KPAD:KPAD + T, 2 * D_HEAD:3 * D_HEAD] = ks_ref[0, 0]
        kwa[KPAD:KPAD + T, 2 * D_HEAD:3 * D_HEAD] = kw_ref[0, 0]
        vsa[KPAD:KPAD + T, 0:D_HEAD] = vs_ref[0, 0]
        vwa[KPAD:KPAD + T, 0:D_HEAD] = vw_ref[0, 0]
        lc = lax.broadcasted_iota(jnp.int32, (ncp, VAUG), 1)
        kca[...] = jnp.zeros((ncp, VAUG), BF16)
        kca[:, 0:D_HEAD] = kc_ref[0, 0]
        vca[...] = jnp.where(lc == D_HEAD, 1.0, 0.0).astype(BF16)
        vca[:, 0:D_HEAD] = vc_ref[0, 0]

    q4 = q_ref[0].reshape(ROWS, D_HEAD)
    gates = g_ref[0, 0]
    row_blocks = [slice(r, r + RB) for r in range(0, ROWS, RB)]

    gate_rows = jnp.concatenate(
        [gates if hh == 0 else pltpu.roll(gates, LANES - hh * N_BRANCH, 1) for hh in range(NSA_GROUP)], axis=0)

    def gate_col(branch):
        return gate_rows[:, branch:branch + 1]

    lane_c = lax.broadcasted_iota(jnp.int32, (ncp, LANES), 1)
    n_c = lax.broadcasted_iota(jnp.int32, (ncp, LANES), 0)
    e = lane_c - D_HEAD
    base = 4 * first + (4 * SUB - 1)
    ind = (((e >= 0) & (e < CBAND) & (n_c == base - e))
           | ((e >= CBAND) & (e < 2 * CBAND) & (n_c == base - (e - CBAND)))
           | (((e == 2 * CBAND) | (e == 2 * CBAND + 1)) & (n_c < base - (CBAND - 1)))
           | ((e == 2 * CBAND + 2) & (n_c > base)))
    rhs_c = jnp.where(lane_c < D_HEAD, kca[...], jnp.where(ind, 1.0, 0.0).astype(BF16))
    lhsc_scr[...] = cmpl_ref[0]
    lhsc_scr[:, 0:D_HEAD] = q4
    sa_scr[:, 0:ncp] = _halves(_dot_nt, lhsc_scr[...], rhs_c)

    lane_q = lax.broadcasted_iota(jnp.int32, (ROWS, LANES), 1)
    lhs_scr[:, LANES:2 * LANES] = jnp.where(lane_q == D_HEAD, NEG, 0.0).astype(BF16)
    lhs_scr[:, LANES:LANES + D_HEAD] = q4
    lhs_win = jnp.concatenate([jnp.zeros((ROWS, LANES), BF16), lhs_scr[:, LANES:2 * LANES]], axis=1)
    win_start = pl.multiple_of(KPAD + (first - WIN_BLOCKS) * SEL_BLOCK, KT)
    sw_scr[...] = _halves(_dot_nt, lhs_win, kwa[pl.ds(win_start, KS + KT), :])

    for i, rows in enumerate(row_blocks):
        s = [sa_scr[rows, c * LANES:(c + 1) * LANES] for c in range(ncp // LANES)]
        m = _row_max(s)
        p = [jnp.where(t > 0.5 * NEG, jnp.exp2(t - m), 0.0) for t in s]
        l = jnp.sum(functools.reduce(jnp.add, p), axis=1, keepdims=True)
        inv = jnp.where(l > 0.0, 1.0 / l, 0.0)
        inv_scr[rows] = inv
        inv_b = jnp.broadcast_to(inv, (RB, LANES))
        tok = slice((i % (TQ // RB)) * RB, (i % (TQ // RB) + 1) * RB)
        for c in range(ncp // LANES):
            lanes = slice(c * LANES, (c + 1) * LANES)
            pb_scr[rows, lanes] = p[c].astype(BF16)
            if i < TQ // RB:
                ps_scr[tok, lanes] = p[c] * inv_b
            else:
                ps_scr[tok, lanes] += p[c] * inv_b
    acc_c = _halves(_dot, pb_scr[:, 0:ncp], vca[...])
    out_scr[...] = acc_c[:, :D_HEAD] * (gate_col(0) * inv_scr[...])
    ps_hi, ps_lo = _split_bf16(ps_scr[...])
    ovt = ovt_ref[...]
    imp_t = _dot_nt(ovt, ps_hi) + _dot_nt(ovt, ps_lo)

    j_t = lax.broadcasted_iota(jnp.int32, (MAX_BLOCKS, TQ), 0)
    cur = first + lax.broadcasted_iota(jnp.int32, (MAX_BLOCKS, TQ), 1) // SEL_BLOCK
    forced = (j_t == 0) | (j_t == cur) | (j_t == cur - 1)
    score_t = jnp.where(forced, FORCED_SCORE, jnp.where(j_t <= cur, imp_t, NEG))
    sc_scr[...] = score_t

    cnt_scr[...] = jnp.zeros((MAX_BLOCKS, TQ), jnp.int32)
    sub_io = lax.broadcasted_iota(jnp.int32, (8, TQ), 0)
    n_groups = MAX_BLOCKS // 8
    for grp in range(n_groups):
        @pl.when(grp * 8 < first + SUB)
        def _count_group():
            targets = [score_t[8 * jg:8 * jg + 8] for jg in range(n_groups)]
            acc = [jnp.zeros((8, TQ), jnp.int32) for _ in range(n_groups)]
            for k in range(8):
                row_i = jnp.broadcast_to(sc_scr[grp * 8 + k:grp * 8 + k + 1, :], (8, TQ))
                for jg in range(n_groups):
                    if jg > grp:
                        beats = row_i >= targets[jg]
                    elif jg < grp:
                        beats = row_i > targets[jg]
                    else:
                        beats = (row_i > targets[jg]) | ((row_i == targets[jg]) & (sub_io > k))
                    acc[jg] = acc[jg] + beats.astype(jnp.int32)
            cnt_scr[...] += jnp.concatenate(acc, axis=0)

    cnt = cnt_scr[...]
    sel_t = jnp.where((cnt < SEL_TOPK) & (j_t <= cur), 1.0, 0.0).astype(BF16)
    sel = _dot_nt(eye_ref[...], jnp.concatenate([sel_t, sel_t], axis=0))
    sel4 = jnp.concatenate([sel] * NSA_GROUP, axis=0) > 0.5

    hi, lo = _split_bf16(jnp.where(sel4, far_ref[0], NEG))
    lhs_scr[:, 0:LANES] = jnp.where(lane_q < D_HEAD, hi, lo)
    m_scr[...] = jnp.full((ROWS, LANES), NEG, F32)
    acc_scr[...] = jnp.zeros((ROWS, VAUG), F32)

    def tile_start(k):
        return pl.multiple_of(KPAD + (first - BLOCKS_PER_TILE) * SEL_BLOCK - k * KS, KT)

    def scores(k, dst):
        dst[...] = _halves(_dot_nt, lhs_scr[...], ksa[pl.ds(tile_start(k), KS), :])

    def consume(k, src, p_scr, with_table):
        n_lt = KS // LANES
        for rows in row_blocks:
            s = [src[rows, c * LANES:(c + 1) * LANES] for c in range(n_lt)]
            if with_table:
                s = [s[c] + near_ref[0, rows, c * LANES:(c + 1) * LANES] for c in range(n_lt)]
            m_prev = m_scr[rows]
            m = jnp.maximum(m_prev, _row_max(s))
            al_scr[rows] = jnp.exp2(m_prev - m)
            m_scr[rows] = m
            for c in range(n_lt):
                p_scr[rows, c * LANES:(c + 1) * LANES] = jnp.exp2((s[c] - m).astype(BF16))
        acc_scr[...] = acc_scr[...] * al_scr[...] + _dot(p_scr[:, 0:KS], vsa[pl.ds(tile_start(k), KS), :])

    scores(0, sa_scr)

    for rows in row_blocks:
        s = [sw_scr[rows, c * LANES:(c + 1) * LANES] + win_ref[0, rows, c * LANES:(c + 1) * LANES]
             for c in range((KS + KT) // LANES)]
        m = _row_max(s)
        for c in range((KS + KT) // LANES):
            pw_scr[rows, c * LANES:(c + 1) * LANES] = jnp.exp2((s[c] - m).astype(BF16))
    acc_w = _halves(_dot, pw_scr[...], vwa[pl.ds(win_start, KS + KT), :])
    out_scr[...] += acc_w[:, :D_HEAD] * (gate_col(2) / acc_w[:, D_HEAD:D_HEAD + 1])

    def pair(with_table, u, carry):
        scores(2 * u + 1, sb_scr)
        consume(2 * u, sa_scr, pa_scr, with_table)
        scores(2 * u + 2, sa_scr)
        consume(2 * u + 1, sb_scr, pb_scr, False)
        return carry

    pair(True, 0, 0)
    lax.fori_loop(1, qt // 4 + 1, functools.partial(pair, False), 0)
    acc_s = acc_scr[...]

    o = out_scr[...] + acc_s[:, :D_HEAD] * (gate_col(1) / acc_s[:, D_HEAD:D_HEAD + 1])
    for hh in range(NSA_GROUP):
        o_ref[0, :, hh * D_HEAD:(hh + 1) * D_HEAD] = o[hh * TQ:(hh + 1) * TQ].astype(BF16)


def _nsa_call(q, kcmp, vcmp, ks, vs, kw, vw, gates, near_t, win_t, far_t, cmp_lhs, ovt, eye):
    B, _, T, _ = q.shape
    G = NSA_KV_HEADS
    ncp = kcmp.shape[2]
    rows_kv = KPAD + T
    per_bg =lambda a: pl.BlockSpec((1, 1) + a.shape[2:], lambda b, g, i: (b, g, 0, 0))
    per_g = lambda a: pl.BlockSpec((1,) + a.shape[1:], lambda b, g, i: (g, 0, 0))
    const2 = lambda a: pl.BlockSpec(a.shape, lambda b, g, i: (0, 0))
    return pl.pallas_call(
        _nsa_kernel,
        grid=(B, G, T // TQ),
        in_specs=[
            pl.BlockSpec((1, NSA_GROUP, TQ, D_HEAD), lambda b, g, i: (b, g, i, 0)),
            per_bg(kcmp), per_bg(vcmp), per_bg(ks), per_bg(vs), per_bg(kw), per_bg(vw),
            pl.BlockSpec((1, 1, TQ, LANES), lambda b, g, i: (b, g, i, 0)),
            per_g(near_t), per_g(win_t), per_g(far_t), per_g(cmp_lhs), const2(ovt), const2(eye),
        ],
        out_specs=pl.BlockSpec((1, TQ, NSA_GROUP * D_HEAD), lambda b, g, i: (b, i, g)),
        out_shape=jax.ShapeDtypeStruct((B, T, D_NSA), BF16),
        scratch_shapes=[
            pltpu.VMEM((rows_kv, KAUG), BF16), pltpu.VMEM((rows_kv, VAUG), BF16),
            pltpu.VMEM((rows_kv, KAUG), BF16), pltpu.VMEM((rows_kv, VAUG), BF16),
            pltpu.VMEM((ncp, VAUG), BF16), pltpu.VMEM((ncp, VAUG), BF16),
            pltpu.VMEM((MAX_BLOCKS, TQ), F32),
            pltpu.VMEM((ROWS, KAUG), BF16), pltpu.VMEM((ROWS, LANES), BF16),
            pltpu.VMEM((ROWS, LANES), F32), pltpu.VMEM((ROWS, LANES), F32), pltpu.VMEM((ROWS, 1), F32),
            pltpu.VMEM((ROWS, VAUG), F32),
            pltpu.VMEM((ROWS, KS), F32), pltpu.VMEM((ROWS, KS), F32),
            pltpu.VMEM((ROWS, KS), BF16), pltpu.VMEM((ROWS, KS), BF16),
            pltpu.VMEM((ROWS, KS + KT), F32), pltpu.VMEM((ROWS, KS + KT), BF16),
            pltpu.VMEM((TQ, ncp), F32), pltpu.VMEM((ROWS, D_HEAD), F32), pltpu.VMEM((MAX_BLOCKS, TQ), jnp.int32),
        ],
        compiler_params=pltpu.CompilerParams(
            dimension_semantics=("arbitrary", "arbitrary", "arbitrary"), vmem_limit_bytes=VMEM_LIMIT),
        name="nsa",
    )(q, kcmp, vcmp, ks, vs, kw, vw, gates, near_t, win_t, far_t, cmp_lhs, ovt, eye)


def _ffn_kernel(x_ref, on_ref, os_ref, wo_n_ref, wo_s_ref, fn_ref, wup_ref, cw_ref, cb_ref, wdn_ref,
                out_ref, prev_ref, perm_ref, h_ref, ua_scr, ub_scr, act_ref):
    tm = x_ref.shape[1]
    seg = tm // 8
    n_chunks = wup_ref.shape[1]

    @pl.when(pl.program_id(1) == 0)
    def _start_of_sequence():
        prev_ref[...] = jnp.zeros(prev_ref.shape, F32)

    x1 = x_ref[0] + _dot(on_ref[0], wo_n_ref[...]) + _dot(os_ref[0], wo_s_ref[...])
    ms = jnp.mean(x1 * x1, axis=-1, keepdims=True)
    hn = x1 * lax.rsqrt(ms + EPS) * fn_ref[...]
    out_ref[0] = x1
    n_lt = hn.shape[1] // LANES
    for c in range(n_lt):
        perm_ref[c] = hn[:, c * LANES:(c + 1) * LANES]
    for u in range(seg // 2):
        groups = [jnp.concatenate([perm_ref[c, pl.ds(2 * u + k, 8, stride=seg), :] for c in range(n_lt)], axis=1)
                  for k in range(2)]
        h_ref[16 * u:16 * u + 16, :] = jnp.concatenate(groups, axis=0).astype(BF16)

    def up(c, dst):
        h = h_ref[...]
        for half in range(2):
            dst[half] = _dot(h, wup_ref[half, c])

    first_sublane = lax.broadcasted_iota(jnp.int32, (8, FFN_CHUNK), 0) == 0

    def conv_gate(c, src):
        for r0 in range(0, tm, FFN_ROWS):
            ys = []
            for half in range(2):
                w = cw_ref[half, c]
                x0 = src[half, r0:r0 + FFN_ROWS]
                if r0 == 0:
                    wrap = [jnp.where(first_sublane, pltpu.roll(prev_ref[half, c, 8 * k:8 * k + 8], 1, 0),
                                      pltpu.roll(src[half, tm - 16 + 8 * k:tm - 8 + 8 * k], 1, 0)) for k in range(2)]
                    x1_ = jnp.concatenate([wrap[1], src[half, 0:FFN_ROWS - 8]], axis=0)
                    x2_ = jnp.concatenate([wrap[0], wrap[1], src[half, 0:FFN_ROWS - 16]], axis=0)
                else:
                    x1_ = src[half, r0 - 8:r0 + FFN_ROWS - 8]
                    x2_ = src[half, r0 - 16:r0 + FFN_ROWS - 16]
                ys.append(x0 * w[2:3] + x1_ * w[1:2] + x2_ * w[0:1] + cb_ref[half, c])
            a, g = ys
            col = pl.multiple_of(c * FFN_CHUNK, FFN_CHUNK)
            act_ref[r0:r0 + FFN_ROWS, pl.ds(col, FFN_CHUNK)] = (g * _sigmoid(g) * a).astype(BF16)
        for half in range(2):
            prev_ref[half, c] = src[half, tm - 16:tm]

    up(0, ua_scr)

    def pair(u, carry):
        up(2 * u + 1, ub_scr)
        conv_gate(2 * u, ua_scr)
        up(2 * u + 2, ua_scr)
        conv_gate(2 * u + 1, ub_scr)
        return carry

    lax.fori_loop(0, (n_chunks - 1) // 2, pair, 0)
    conv_gate(n_chunks - 1, ua_scr)
    y = _halves(_dot, act_ref[...], wdn_ref[...])
    for c in range(n_lt):
        perm_ref[c] = y[:, c * LANES:(c + 1) * LANES]
    for sgm in range(8):
        natural = jnp.concatenate([perm_ref[c, pl.ds(sgm, seg, stride=8), :] for c in range(n_lt)], axis=1)
        out_ref[0, sgm * seg:(sgm + 1) * seg, :] += natural


def _ffn_call(x, o_nsa, o_sgu, wo_n, wo_s, ffn_norm, w_up, conv_w, conv_b, w_down, tm):
    B, T, _ = x.shape
    n_chunks = w_up.shape[1]
    assert n_chunks % 2 == 1
    resident =lambda a: pl.BlockSpec(a.shape, lambda b, i: (0,) * a.ndim, pipeline_mode=pl.Buffered(1))
    tile = lambda w: pl.BlockSpec((1, tm, w), lambda b, i: (b, i, 0))
    return pl.pallas_call(
        _ffn_kernel,
        grid=(B, T // tm),
        in_specs=[tile(D_MODEL), tile(D_NSA), tile(D_GMLP), resident(wo_n), resident(wo_s),
                  resident(ffn_norm), resident(w_up), resident(conv_w), resident(conv_b), resident(w_down)],
        out_specs=tile(D_MODEL),
        out_shape=jax.ShapeDtypeStruct((B, T, D_MODEL), F32),
        scratch_shapes=[pltpu.VMEM((2, n_chunks, 16, FFN_CHUNK), F32), pltpu.VMEM((D_MODEL // LANES, tm, LANES), F32),
                        pltpu.VMEM((tm, D_MODEL), BF16),
                        pltpu.VMEM((2, tm, FFN_CHUNK), F32), pltpu.VMEM((2, tm, FFN_CHUNK), F32),
                        pltpu.VMEM((tm, D_FF), BF16)],
        compiler_params=pltpu.CompilerParams(
            dimension_semantics=("arbitrary", "arbitrary"), vmem_limit_bytes=VMEM_LIMIT),
        name="ffn",
    )(x, o_nsa, o_sgu, wo_n, wo_s, ffn_norm, w_up, conv_w, conv_b, w_down)


def _mixers(x, rel_bias, attn_norm, w_in, q_norm, k_norm_cmp, k_norm_slc, k_norm_win,
            cmp_pe_k, cmp_w1_k, cmp_b1_k, cmp_w2_k, cmp_pe_v, cmp_w1_v, cmp_b1_v, cmp_w2_v,
            sgu_norm, sgu_w, sgu_b):
    B, T, _ = x.shape
    assert T % 512 == 0 and T // SEL_BLOCK <= MAX_BLOCKS and (T // CMP_STRIDE) % LANES == 0
    ncp = T // CMP_STRIDE
    G, R = NSA_KV_HEADS, NSA_GROUP

    o_kv = D_NSA
    o_g = D_NSA + 6 * D_KV
    o_uv = o_g + N_GATES
    w_q = w_in[:, :o_kv].astype(BF16)
    w_kv = w_in[:, o_kv:o_g].astype(BF16)
    per_group = NSA_GROUP * N_BRANCH
    w_g = jnp.pad(w_in[:, o_g:o_uv].reshape(D_MODEL, G, per_group), ((0, 0), (0, 0), (0, LANES - per_group)))
    w_g = w_g.reshape(D_MODEL, G * LANES).astype(BF16)
    w_uv = w_in[:, o_uv:].astype(BF16)
    qn = (jnp.tile(q_norm, NSA_HEADS) * (D_HEAD ** -0.5 * LOG2E)).reshape(1, D_NSA)
    ksn = jnp.tile(k_norm_slc, G).reshape(1, D_KV)
    kwn = jnp.tile(k_norm_win, G).reshape(1, D_KV)
    sgu_bt = jnp.repeat(sgu_b.T, GMLP_GROUP_DIM, axis=1)
    grp = np.arange(D_NSA) // D_HEAD
    bd = jnp.asarray(grp[:, None] == grp[None, :], BF16)

    q, kc, vc, ks, vs, kw, vw, gates, o_sgu = _proj_call(
        x, attn_norm.reshape(1, D_MODEL), w_q, w_kv, w_g, w_uv, qn, ksn, kwn,
        sgu_norm.reshape(1, D_GMLP), sgu_w, sgu_bt, bd, tm=512)

    half = CMP_STRIDE * D_HEAD
    kcmp, vcmp = _compress_call(
        kc.reshape(B, G, ncp, half), vc.reshape(B, G, ncp, half),
        cmp_pe_k.reshape(2, half), cmp_w1_k.astype(BF16), cmp_b1_k.reshape(1, CMP_HIDDEN),
        cmp_w2_k.astype(BF16), k_norm_cmp.reshape(1, D_HEAD),
        cmp_pe_v.reshape(2, half), cmp_w1_v.astype(BF16), cmp_b1_v.reshape(1, CMP_HIDDEN),
        cmp_w2_v.astype(BF16))

    near_t, win_t, far_t, cmp_lhs = _bias_tables(rel_bias)
    jj = np.arange(MAX_BLOCKS)[:, None]
    nn = np.arange(ncp)[None, :]
    ovt = jnp.asarray((nn >= 4 * jj - 1) & (nn <= 4 * jj + 3), BF16)
    eye = jnp.asarray(np.eye(TQ), BF16)

    o_nsa = _nsa_call(q, kcmp, vcmp, ks, vs, kw, vw, gates, near_t, win_t, far_t, cmp_lhs, ovt, eye)
    return o_nsa, o_sgu


def _layer(x, rel_bias, attn_norm, w_in, q_norm, k_norm_cmp, k_norm_slc, k_norm_win,
           cmp_pe_k, cmp_w1_k, cmp_b1_k, cmp_w2_k, cmp_pe_v, cmp_w1_v, cmp_b1_v, cmp_w2_v,
           sgu_norm, sgu_w, sgu_b, w_out, ffn_norm, w_up, conv_w, conv_b, w_down):
    o_nsa, o_sgu = _mixers(x, rel_bias, attn_norm, w_in, q_norm, k_norm_cmp, k_norm_slc, k_norm_win,
                           cmp_pe_k, cmp_w1_k, cmp_b1_k, cmp_w2_k, cmp_pe_v, cmp_w1_v, cmp_b1_v, cmp_w2_v,
                           sgu_norm, sgu_w, sgu_b)
    wo = w_out.astype(BF16)
    n_chunks = D_FF // FFN_CHUNK
    wu = jnp.transpose(w_up.astype(BF16).reshape(D_MODEL, 2, n_chunks, FFN_CHUNK), (1, 2, 0, 3))
    cw = jnp.transpose(conv_w.reshape(conv_w.shape[0], 2, n_chunks, FFN_CHUNK), (1, 2, 0, 3))
    cb = conv_b.reshape(2, n_chunks, 1, FFN_CHUNK)
    wd = w_down.astype(BF16)
    return _ffn_call(x, o_nsa, o_sgu, wo[:D_NSA], wo[D_NSA:], ffn_norm.reshape(1, D_MODEL), wu, cw, cb, wd, tm=512)


def kernel(x, rel_bias, attn_norm, w_in, q_norm, k_norm_cmp, k_norm_slc, k_norm_win,
           cmp_pe_k, cmp_w1_k, cmp_b1_k, cmp_w2_k, cmp_pe_v, cmp_w1_v, cmp_b1_v, cmp_w2_v,
           sgu_norm, sgu_w, sgu_b, w_out, ffn_norm, w_up, conv_w, conv_b, w_down):
    depth = attn_norm.shape[0]
    for l in range(depth):
        x = _layer(x, rel_bias, attn_norm[l], w_in[l], q_norm[l], k_norm_cmp[l], k_norm_slc[l], k_norm_win[l],
                   cmp_pe_k[l], cmp_w1_k[l], cmp_b1_k[l], cmp_w2_k[l], cmp_pe_v[l], cmp_w1_v[l], cmp_b1_v[l],
                   cmp_w2_v[l], sgu_norm[l], sgu_w[l], sgu_b[l], w_out[l], ffn_norm[l], w_up[l], conv_w[l],
                   conv_b[l], w_down[l])
    return x
```

```python
import functools
import math

import numpy as np
import jax
import jax.numpy as jnp
from jax import lax
from jax.experimental import pallas as pl
from jax.experimental.pallas import tpu as pltpu

F32 = jnp.float32
BF16 = jnp.bfloat16

D_MODEL = 1024
D_HEAD = 64
NSA_HEADS = 8
NSA_KV_HEADS = 2
NSA_GROUP = NSA_HEADS // NSA_KV_HEADS
D_NSA = NSA_HEADS * D_HEAD
D_KV = NSA_KV_HEADS * D_HEAD
N_BRANCH = 3
N_GATES = NSA_HEADS * N_BRANCH
CMP_BLOCK = 32
CMP_STRIDE = 16
CMP_HIDDEN = 256
SEL_BLOCK = 64
SEL_TOPK = 16
WINDOW = 512
GMLP_GROUPS = 8
GMLP_GROUP_DIM = 64
D_GMLP = GMLP_GROUPS * GMLP_GROUP_DIM
CHUNK = 128
D_MIX = D_NSA + D_GMLP
REL_BUCKETS = 32
REL_MAX_DIST = 128
D_FF = 2816
EPS = 1e-6
NEG = -1e30
FORCED_SCORE = 1e4
LOG2E = 1.4426950408889634

LANES = 128
TQ = 256
SUB = TQ // SEL_BLOCK
ROWS = NSA_GROUP * TQ
KT = 256
KS = 2 * KT
RB = 64
BLOCKS_PER_TILE = KT // SEL_BLOCK
WIN_BLOCKS = WINDOW // SEL_BLOCK
KPAD = 1280
N_TOEPLITZ = 3
MAX_BLOCKS = 64
KAUG = 256
VAUG = 128
CBAND = 28
FFN_CHUNK = 256
FFN_ROWS = 64
VMEM_LIMIT = 56 * 1024 * 1024

_NT = (((1,), (1,)), ((), ()))


def _dot(a, b):
    return jnp.dot(a, b, preferred_element_type=F32)


def _dot_nt(a, b):
    return lax.dot_general(a, b, _NT, preferred_element_type=F32)


def _halves(dot, a, b):
    h = a.shape[0] // 2
    return jnp.concatenate([dot(a[:h], b), dot(a[h:], b)], axis=0)


def _split_bf16(x):
    hi = x.astype(BF16)
    lo = (x - hi.astype(F32)).astype(BF16)
    return hi, lo


def _gelu_tanh(x):
    return 0.5 * x * (1.0 + jnp.tanh(0.7978845608028654 * (x + 0.044715 * (x * x * x))))


def _sigmoid(x):
    return 0.5 * (1.0 + jnp.tanh(0.5 * x))


def _group_mean_sq(t, ones_blockdiag, width):
    t2 = t * t
    hi, lo = _split_bf16(t2)
    return (_dot(hi, ones_blockdiag) + _dot(lo, ones_blockdiag)) * (1.0 / width)


def _proj_kernel(x_ref, an_ref, wq_ref, wkv_ref, wg_ref, wuv_ref, qn_ref, ksn_ref, kwn_ref,
                 sgun_ref, sguw_ref, sgub_ref, bd_ref,
                 q_out, kc_out, vc_out, ks_out, vs_out, kw_out, vw_out, g_out, sgu_out):
    tm = x_ref.shape[1]
    x = x_ref[0]
    ms = jnp.mean(x * x, axis=-1, keepdims=True)
    h = (x * lax.rsqrt(ms + EPS) * an_ref[...]).astype(BF16)

    bd = bd_ref[...]
    q = _dot(h, wq_ref[...])
    qn = q * lax.rsqrt(_group_mean_sq(q, bd, D_HEAD) + EPS) * qn_ref[...]
    for hh in range(NSA_HEADS):
        q_out[0, hh] = qn[:, hh * D_HEAD:(hh + 1) * D_HEAD].astype(BF16)

    kv = _dot(h, wkv_ref[...])
    bd_kv = bd[:D_KV, :D_KV]
    kc = kv[:, 0 * D_KV:1 * D_KV]
    vc = kv[:, 1 * D_KV:2 * D_KV]
    ks = kv[:, 2 * D_KV:3 * D_KV]
    vs = kv[:, 3 * D_KV:4 * D_KV]
    kw = kv[:, 4 * D_KV:5 * D_KV]
    vw = kv[:, 5 * D_KV:6 * D_KV]
    ks = ks * lax.rsqrt(_group_mean_sq(ks, bd_kv, D_HEAD) + EPS) * ksn_ref[...]
    kw = kw * lax.rsqrt(_group_mean_sq(kw, bd_kv, D_HEAD) + EPS) * kwn_ref[...]
    for t, o_ref in ((kc, kc_out), (vc, vc_out), (ks, ks_out), (vs, vs_out), (kw, kw_out), (vw, vw_out)):
        for g in range(NSA_KV_HEADS):
            o_ref[0, g] = t[:, g * D_HEAD:(g + 1) * D_HEAD].astype(BF16)

    gates = _sigmoid(_dot(h, wg_ref[...]))
    for g in range(NSA_KV_HEADS):
        g_out[0, g] = gates[:, g * LANES:(g + 1) * LANES]

    uv = _gelu_tanh(_dot(h, wuv_ref[...]))
    u = uv[:, :D_GMLP]
    v = uv[:, D_GMLP:]
    vms = jnp.mean(v * v, axis=-1, keepdims=True)
    vb = (v * lax.rsqrt(vms + EPS) * sgun_ref[...]).astype(BF16)

    row = lax.broadcasted_iota(jnp.int32, (CHUNK, CHUNK), 0)
    col = lax.broadcasted_iota(jnp.int32, (CHUNK, CHUNK), 1)
    tril = col <= row
    w_tril = [jnp.where(tril, sguw_ref[g], 0.0).astype(BF16) for g in range(GMLP_GROUPS)]
    w_pair = [jnp.concatenate([w_tril[2 * p], w_tril[2 * p + 1]], axis=1) for p in range(D_GMLP // LANES)]
    first_half = lax.broadcasted_iota(jnp.int32, (CHUNK, LANES), 1) < GMLP_GROUP_DIM
    zero = jnp.zeros((CHUNK, LANES), BF16)
    for c in range(tm // CHUNK):
        rows = slice(c * CHUNK, (c + 1) * CHUNK)
        zs = []
        for p in range(D_GMLP // LANES):
            blk = vb[rows, p * LANES:(p + 1) * LANES]
            rhs = jnp.concatenate([jnp.where(first_half, blk, zero), jnp.where(first_half, zero, blk)], axis=0)
            zs.append(_dot(w_pair[p], rhs))
        z = jnp.concatenate(zs, axis=1) + sgub_ref[...]
        sgu_out[0, rows, :] = (u[rows, :] * z).astype(BF16)


def _proj_call(x, attn_norm, w_q, w_kv, w_g, w_uv, qn, ksn, kwn, sgu_norm, sgu_w, sgu_bt, bd, tm):
    B, T, _ = x.shape
    const2 = lambda b, i: (0, 0)
    const3 = lambda b, i: (0, 0, 0)
    head_spec = lambda nh: pl.BlockSpec((1, nh, tm, D_HEAD), lambda b, i: (b, 0, i, 0))
    kv_shape = jax.ShapeDtypeStruct((B, NSA_KV_HEADS, T, D_HEAD), BF16)
    return pl.pallas_call(
        _proj_kernel,
        grid=(B, T // tm),
        in_specs=[
            pl.BlockSpec((1, tm, D_MODEL), lambda b, i: (b, i, 0)),
            pl.BlockSpec((1, D_MODEL), const2),
            pl.BlockSpec(w_q.shape, const2),
            pl.BlockSpec(w_kv.shape, const2),
            pl.BlockSpec(w_g.shape, const2),
            pl.BlockSpec(w_uv.shape, const2),
            pl.BlockSpec((1, D_NSA), const2),
            pl.BlockSpec((1, D_KV), const2),
            pl.BlockSpec((1, D_KV), const2),
            pl.BlockSpec((1, D_GMLP), const2),
            pl.BlockSpec(sgu_w.shape, const3),
            pl.BlockSpec(sgu_bt.shape, const2),
            pl.BlockSpec(bd.shape, const2),
        ],
        out_specs=[
            head_spec(NSA_HEADS),
            head_spec(NSA_KV_HEADS), head_spec(NSA_KV_HEADS), head_spec(NSA_KV_HEADS),
            head_spec(NSA_KV_HEADS), head_spec(NSA_KV_HEADS), head_spec(NSA_KV_HEADS),
            pl.BlockSpec((1, NSA_KV_HEADS, tm, LANES), lambda b, i: (b, 0, i, 0)),
            pl.BlockSpec((1, tm, D_GMLP), lambda b, i: (b, i, 0)),
        ],
        out_shape=[
            jax.ShapeDtypeStruct((B, NSA_HEADS, T, D_HEAD), BF16),
            kv_shape, kv_shape, kv_shape, kv_shape, kv_shape, kv_shape,
            jax.ShapeDtypeStruct((B, NSA_KV_HEADS, T, LANES), F32),
            jax.ShapeDtypeStruct((B, T, D_GMLP), BF16),
        ],
        compiler_params=pltpu.CompilerParams(
            dimension_semantics=("arbitrary", "arbitrary"), vmem_limit_bytes=VMEM_LIMIT),
        name="proj",
    )(x, attn_norm, w_q, w_kv, w_g, w_uv, qn, ksn, kwn, sgu_norm, sgu_w, sgu_bt, bd)


def _compress_one(tok_ref, pe_ref, w1_ref, b1_ref, w2_ref):
    half = CMP_STRIDE * D_HEAD
    tok = tok_ref[0, 0].astype(F32)
    top = (tok + pe_ref[0:1, :]).astype(BF16)
    bot = (tok + pe_ref[1:2, :]).astype(BF16)
    a = _dot(top, w1_ref[:half, :])
    b = _dot(bot, w1_ref[half:, :])
    ncp = a.shape[0]
    pre = a + pltpu.roll(b, ncp - 1, 0) + b1_ref[...]
    hid = _gelu_tanh(pre).astype(BF16)
    return _dot(hid, w2_ref[...])


def _compress_kernel(kc_ref, vc_ref, pek_ref, w1k_ref, b1k_ref, w2k_ref, kn_ref,
                     pev_ref, w1v_ref, b1v_ref, w2v_ref, k_out, v_out):
    k = _compress_one(kc_ref, pek_ref, w1k_ref, b1k_ref, w2k_ref)
    kms = jnp.mean(k * k, axis=-1, keepdims=True)
    k_out[0, 0] = (k * lax.rsqrt(kms + EPS) * kn_ref[...]).astype(BF16)
    v_out[0, 0] = _compress_one(vc_ref, pev_ref, w1v_ref, b1v_ref, w2v_ref).astype(BF16)


def _compress_call(kc, vc, pek, w1k, b1k, w2k, kn, pev, w1v, b1v, w2v):
    B, G, ncp, width = kc.shape
    const2 = lambda b, g: (0, 0)
    tok_spec = pl.BlockSpec((1, 1, ncp, width), lambda b, g: (b, g, 0, 0))
    out_spec = pl.BlockSpec((1, 1, ncp, D_HEAD), lambda b, g: (b, g, 0, 0))
    full = lambda a: pl.BlockSpec(a.shape, const2)
    out_shape = jax.ShapeDtypeStruct((B, G, ncp, D_HEAD), BF16)
    return pl.pallas_call(
        _compress_kernel,
        grid=(B, G),
        in_specs=[tok_spec, tok_spec, full(pek), full(w1k), full(b1k), full(w2k), full(kn),
                  full(pev), full(w1v), full(b1v), full(w2v)],
        out_specs=[out_spec, out_spec],
        out_shape=[out_shape, out_shape],
        compiler_params=pltpu.CompilerParams(
            dimension_semantics=("arbitrary", "arbitrary"), vmem_limit_bytes=VMEM_LIMIT),
        name="compress",
    )(kc, vc, pek, w1k, b1k, w2k, kn, pev, w1v, b1v, w2v)


def _rel_bucket_np(dist):
    max_exact = REL_BUCKETS // 2
    d = np.maximum(dist, 1).astype(np.float32)
    log_b = max_exact + (np.log(d / np.float32(max_exact)) / np.float32(math.log(REL_MAX_DIST / max_exact))
                         * np.float32(REL_BUCKETS - max_exact)).astype(np.int32)
    log_b = np.clip(log_b, max_exact, REL_BUCKETS - 1)
    return np.where(dist < max_exact, np.maximum(dist, 0), log_b)


def _bias_tables(rel_bias):
    rb = rel_bias.astype(F32) * LOG2E
    qq = np.arange(SEL_BLOCK)[:, None]
    kp = np.arange(SEL_BLOCK)[None, :]
    tile_d = [m * SEL_BLOCK + qq - kp for m in range(N_TOEPLITZ)]
    s_ = np.arange(SUB)[:, None, None]
    e_ = np.arange(CBAND)[None, None, :]
    dist_c = qq[None, :, :] + CMP_STRIDE * (e_ - 4 * (SUB - 1) + 4 * s_) - (3 * CMP_STRIDE + CMP_BLOCK - 1)
    all_d = np.concatenate([d.reshape(-1) for d in tile_d] + [dist_c.reshape(-1)])
    onehot = np.eye(REL_BUCKETS, dtype=np.float32)[_rel_bucket_np(all_d)]
    vals = jnp.dot(jnp.asarray(onehot), rb, precision=lax.Precision.HIGHEST)
    vals = jnp.where(jnp.asarray(all_d >= 0)[:, None], vals, NEG)
    n_t = N_TOEPLITZ * SEL_BLOCK * SEL_BLOCK
    toep = jnp.transpose(vals[:n_t].reshape(N_TOEPLITZ, SEL_BLOCK, SEL_BLOCK, NSA_HEADS), (0, 3, 1, 2))
    band = jnp.transpose(vals[n_t:].reshape(SUB, SEL_BLOCK, CBAND, NSA_HEADS), (3, 0, 1, 2))
    far = rb[REL_BUCKETS - 1]
    far_tile = jnp.broadcast_to(far[:, None, None], (NSA_HEADS, SEL_BLOCK, SEL_BLOCK))
    neg_tile = jnp.full((NSA_HEADS, SEL_BLOCK, SEL_BLOCK), NEG, F32)
    edge_tile = jnp.where(jnp.asarray(kp > qq)[None], far_tile, NEG)

    def tile(m, windowed):
        if m < 0 or (windowed and m > WIN_BLOCKS):
            return neg_tile
        if windowed and m == WIN_BLOCKS:
            return edge_tile
        return toep[m] if m < N_TOEPLITZ else far_tile

    def table(first_m, n_blocks, windowed):
        rows = [jnp.concatenate([tile(first_m + s - c, windowed) for c in range(n_blocks)], axis=2)
                for s in range(SUB)]
        t = jnp.concatenate(rows, axis=1)
        return t.reshape(NSA_KV_HEADS, ROWS, n_blocks * SEL_BLOCK)

    far_rows = jnp.repeat(far, TQ).reshape(NSA_KV_HEADS, ROWS, 1)
    near_t = table(BLOCKS_PER_TILE, 2 * BLOCKS_PER_TILE, False) - far_rows
    win_t = table(WIN_BLOCKS, WIN_BLOCKS + BLOCKS_PER_TILE, True)
    far_t = jnp.broadcast_to(far_rows, (NSA_KV_HEADS, ROWS, LANES))
    band_t = band.reshape(NSA_KV_HEADS, ROWS, CBAND)
    band_hi = band_t.astype(BF16)
    band_lo = (band_t - band_hi.astype(F32)).astype(BF16)
    far_hi = far_rows.astype(BF16)
    far_lo = (far_rows - far_hi.astype(F32)).astype(BF16)
    neg_col = jnp.full((NSA_KV_HEADS, ROWS, 1), NEG, BF16)
    zeros = jnp.zeros((NSA_KV_HEADS, ROWS, D_HEAD - 2 * CBAND - 3), BF16)
    q_lanes = jnp.zeros((NSA_KV_HEADS, ROWS, D_HEAD), BF16)
    cmp_lhs = jnp.concatenate([q_lanes, band_hi, band_lo, far_hi, far_lo, neg_col, zeros], axis=2)
    return near_t, win_t, far_t, cmp_lhs


def _row_max(lane_tiles):
    m = jnp.max(functools.reduce(jnp.maximum, lane_tiles), axis=1, keepdims=True)
    return jnp.broadcast_to(m, lane_tiles[0].shape)


def _nsa_kernel(q_ref, kc_ref, vc_ref, ks_ref, vs_ref, kw_ref, vw_ref, g_ref,
                near_ref, win_ref, far_ref, cmpl_ref, ovt_ref, eye_ref,
                o_ref, ksa, vsa, kwa, vwa, kca, vca, sc_scr, lhs_scr, lhsc_scr, m_scr, al_scr, inv_scr, acc_scr, sa_scr, sb_scr, pa_scr, pb_scr,
                sw_scr, pw_scr, ps_scr, out_scr, cnt_scr):
    qt = pl.program_id(2)
    first = qt * SUB
    T = ks_ref.shape[2]
    ncp = kc_ref.shape[2]

    @pl.when(qt == 0)
    def _init():
        n_chunks = ksa.shape[0] // KT
        r_io = lax.broadcasted_iota(jnp.int32, (KT, KAUG), 0)
        l_io = lax.broadcasted_iota(jnp.int32, (KT, KAUG), 1)
        lv = lax.broadcasted_iota(jnp.int32, (KT, VAUG), 1)
        vpat = jnp.where(lv == D_HEAD, 1.0, 0.0).astype(BF16)

        def fill(c, carry):
            rows = pl.ds(pl.multiple_of(c * KT, KT), KT)
            key = r_io + (c * KT - KPAD)
            blk = key // SEL_BLOCK
            real = (key >= 0) & (key < T)
            hot = real & (((l_io < D_HEAD) & (l_io == blk))
                          | ((l_io >= D_HEAD) & (l_io < 2 * D_HEAD) & (l_io - D_HEAD == blk)))
            hot = hot | ((~real) & (l_io == 3 * D_HEAD))
            pat = jnp.where(hot, 1.0, 0.0).astype(BF16)
            ksa[rows, :] = pat
            kwa[rows, :] = pat
            vsa[rows, :] = vpat
            vwa[rows, :] = vpat
            return carry

        lax.fori_loop(0, n_chunks, fill, 0)
        ksa[KPAD:KPAD + T, 2 * D_HEAD:3 * D_HEAD] = ks_ref[0, 0]
        kwa[KPAD:KPAD + T, 2 * D_HEAD:3 * D_HEAD] = kw_ref[0, 0]
        vsa[KPAD:KPAD + T, 0:D_HEAD] = vs_ref[0, 0]
        vwa[KPAD:KPAD + T, 0:D_HEAD] = vw_ref[0, 0]
        lc = lax.broadcasted_iota(jnp.int32, (ncp, VAUG), 1)
        kca[...] = jnp.zeros((ncp, VAUG), BF16)
        kca[:, 0:D_HEAD] = kc_ref[0, 0]
        vca[...] = jnp.where(lc == D_HEAD, 1.0, 0.0).astype(BF16)
        vca[:, 0:D_HEAD] = vc_ref[0, 0]

    q4 = q_ref[0].reshape(ROWS, D_HEAD)
    gates = g_ref[0, 0]
    row_blocks = [slice(r, r + RB) for r in range(0, ROWS, RB)]

    gate_rows = jnp.concatenate(
        [gates if hh == 0 else pltpu.roll(gates, LANES - hh * N_BRANCH, 1) for hh in range(NSA_GROUP)], axis=0)

    def gate_col(branch):
        return gate_rows[:, branch:branch + 1]

    lane_c = lax.broadcasted_iota(jnp.int32, (ncp, LANES), 1)
    n_c = lax.broadcasted_iota(jnp.int32, (ncp, LANES), 0)
    e = lane_c - D_HEAD
    base = 4 * first + (4 * SUB - 1)
    ind = (((e >= 0) & (e < CBAND) & (n_c == base - e))
           | ((e >= CBAND) & (e < 2 * CBAND) & (n_c == base - (e - CBAND)))
           | (((e == 2 * CBAND) | (e == 2 * CBAND + 1)) & (n_c < base - (CBAND - 1)))
           | ((e == 2 * CBAND + 2) & (n_c > base)))
    rhs_c = jnp.where(lane_c < D_HEAD, kca[...], jnp.where(ind, 1.0, 0.0).astype(BF16))
    lhsc_scr[...] = cmpl_ref[0]
    lhsc_scr[:, 0:D_HEAD] = q4
    sa_scr[:, 0:ncp] = _halves(_dot_nt, lhsc_scr[...], rhs_c)

    lane_q = lax.broadcasted_iota(jnp.int32, (ROWS, LANES), 1)
    lhs_scr[:, LANES:2 * LANES] = jnp.where(lane_q == D_HEAD, NEG, 0.0).astype(BF16)
    lhs_scr[:, LANES:LANES + D_HEAD] = q4
    lhs_win = jnp.concatenate([jnp.zeros((ROWS, LANES), BF16), lhs_scr[:, LANES:2 * LANES]], axis=1)
    win_start = pl.multiple_of(KPAD + (first - WIN_BLOCKS) * SEL_BLOCK, KT)
    sw_scr[...] = _halves(_dot_nt, lhs_win, kwa[pl.ds(win_start, KS + KT), :])

    for i, rows in enumerate(row_blocks):
        s = [sa_scr[rows, c * LANES:(c + 1) * LANES] for c in range(ncp // LANES)]
        m = _row_max(s)
        p = [jnp.where(t > 0.5 * NEG, jnp.exp2(t - m), 0.0) for t in s]
        l = jnp.sum(functools.reduce(jnp.add, p), axis=1, keepdims=True)
        inv = jnp.where(l > 0.0, 1.0 / l, 0.0)
        inv_scr[rows] = inv
        inv_b = jnp.broadcast_to(inv, (RB, LANES))
        tok = slice((i % (TQ // RB)) * RB, (i % (TQ // RB) + 1) * RB)
        for c in range(ncp // LANES):
            lanes = slice(c * LANES, (c + 1) * LANES)
            pb_scr[rows, lanes] = p[c].astype(BF16)
            if i < TQ // RB:
                ps_scr[tok, lanes] = p[c] * inv_b
            else:
                ps_scr[tok, lanes] += p[c] * inv_b
    acc_c = _halves(_dot, pb_scr[:, 0:ncp], vca[...])
    out_scr[...] = acc_c[:, :D_HEAD] * (gate_col(0) * inv_scr[...])
    ps_hi, ps_lo = _split_bf16(ps_scr[...])
    ovt = ovt_ref[...]
    imp_t = _dot_nt(ovt, ps_hi) + _dot_nt(ovt, ps_lo)

    j_t = lax.broadcasted_iota(jnp.int32, (MAX_BLOCKS, TQ), 0)
    cur = first + lax.broadcasted_iota(jnp.int32, (MAX_BLOCKS, TQ), 1) // SEL_BLOCK
    forced = (j_t == 0) | (j_t == cur) | (j_t == cur - 1)
    score_t = jnp.where(forced, FORCED_SCORE, jnp.where(j_t <= cur, imp_t, NEG))
    sc_scr[...] = score_t

    cnt_scr[...] = jnp.zeros((MAX_BLOCKS, TQ), jnp.int32)
    sub_io = lax.broadcasted_iota(jnp.int32, (8, TQ), 0)
    n_groups = MAX_BLOCKS // 8
    for grp in range(n_groups):
        @pl.when(grp * 8 < first + SUB)
        def _count_group():
            targets = [score_t[8 * jg:8 * jg + 8] for jg in range(n_groups)]
            acc = [jnp.zeros((8, TQ), jnp.int32) for _ in range(n_groups)]
            for k in range(8):
                row_i = jnp.broadcast_to(sc_scr[grp * 8 + k:grp * 8 + k + 1, :], (8, TQ))
                for jg in range(n_groups):
                    if jg > grp:
                        beats = row_i >= targets[jg]
                    elif jg < grp:
                        beats = row_i > targets[jg]
                    else:
                        beats = (row_i > targets[jg]) | ((row_i == targets[jg]) & (sub_io > k))
                    acc[jg] = acc[jg] + beats.astype(jnp.int32)
            cnt_scr[...] += jnp.concatenate(acc, axis=0)

    cnt = cnt_scr[...]
    sel_t = jnp.where((cnt < SEL_TOPK) & (j_t <= cur), 1.0, 0.0).astype(BF16)
    sel = _dot_nt(eye_ref[...], jnp.concatenate([sel_t, sel_t], axis=0))
    sel4 = jnp.concatenate([sel] * NSA_GROUP, axis=0) > 0.5

    hi, lo = _split_bf16(jnp.where(sel4, far_ref[0], NEG))
    lhs_scr[:, 0:LANES] = jnp.where(lane_q < D_HEAD, hi, lo)
    m_scr[...] = jnp.full((ROWS, LANES), NEG, F32)
    acc_scr[...] = jnp.zeros((ROWS, VAUG), F32)

    def tile_start(k):
        return pl.multiple_of(KPAD + (first - BLOCKS_PER_TILE) * SEL_BLOCK - k * KS, KT)

    def scores(k, dst):
        dst[...] = _halves(_dot_nt, lhs_scr[...], ksa[pl.ds(tile_start(k), KS), :])

    def consume(k, src, p_scr, with_table):
        n_lt = KS // LANES
        for rows in row_blocks:
            s = [src[rows, c * LANES:(c + 1) * LANES] for c in range(n_lt)]
            if with_table:
                s = [s[c] + near_ref[0, rows, c * LANES:(c + 1) * LANES] for c in range(n_lt)]
            m_prev = m_scr[rows]
            m = jnp.maximum(m_prev, _row_max(s))
            al_scr[rows] = jnp.exp2(m_prev - m)
            m_scr[rows] = m
            for c in range(n_lt):
                p_scr[rows, c * LANES:(c + 1) * LANES] = jnp.exp2((s[c] - m).astype(BF16))
        acc_scr[...] = acc_scr[...] * al_scr[...] + _dot(p_scr[:, 0:KS], vsa[pl.ds(tile_start(k), KS), :])

    scores(0, sa_scr)

    for rows in row_blocks:
        s = [sw_scr[rows, c * LANES:(c + 1) * LANES] + win_ref[0, rows, c * LANES:(c + 1) * LANES]
             for c in range((KS + KT) // LANES)]
        m = _row_max(s)
        for c in range((KS + KT) // LANES):
            pw_scr[rows, c * LANES:(c + 1) * LANES] = jnp.exp2((s[c] - m).astype(BF16))
    acc_w = _halves(_dot, pw_scr[...], vwa[pl.ds(win_start, KS + KT), :])
    out_scr[...] += acc_w[:, :D_HEAD] * (gate_col(2) / acc_w[:, D_HEAD:D_HEAD + 1])

    def pair(with_table, u, carry):
        scores(2 * u + 1, sb_scr)
        consume(2 * u, sa_scr, pa_scr, with_table)
        scores(2 * u + 2, sa_scr)
        consume(2 * u + 1, sb_scr, pb_scr, False)
        return carry

    n_tiles = (qt + 2) // 2
    pair(True, 0, 0)
    lax.fori_loop(1, jnp.maximum(n_tiles // 2, 1), functools.partial(pair, False), 0)

    @pl.when((n_tiles % 2 == 1) & (n_tiles >= 3))
    def _odd_tile():
        consume(n_tiles - 1, sa_scr, pa_scr, False)

    acc_s = acc_scr[...]

    o = out_scr[...] + acc_s[:, :D_HEAD] * (gate_col(1) / acc_s[:, D_HEAD:D_HEAD + 1])
    for hh in range(NSA_GROUP):
        o_ref[0, :, hh * D_HEAD:(hh + 1) * D_HEAD] = o[hh * TQ:(hh + 1) * TQ].astype(BF16)


def _nsa_call(q, kcmp, vcmp, ks, vs, kw, vw, gates, near_t, win_t, far_t, cmp_lhs, ovt, eye):
    B, _, T, _ = q.shape
    G = NSA_KV_HEADS
    ncp = kcmp.shape[2]
    rows_kv = KPAD + T
    per_bg =lambda a: pl.BlockSpec((1, 1) + a.shape[2:], lambda b, g, i: (b, g, 0, 0))
    per_g = lambda a: pl.BlockSpec((1,) + a.shape[1:], lambda b, g, i: (g, 0, 0))
    const2 = lambda a: pl.BlockSpec(a.shape, lambda b, g, i: (0, 0))
    return pl.pallas_call(
        _nsa_kernel,
        grid=(B, G, T // TQ),
        in_specs=[
            pl.BlockSpec((1, NSA_GROUP, TQ, D_HEAD), lambda b, g, i: (b, g, i, 0)),
            per_bg(kcmp), per_bg(vcmp), per_bg(ks), per_bg(vs), per_bg(kw), per_bg(vw),
            pl.BlockSpec((1, 1, TQ, LANES), lambda b, g, i: (b, g, i, 0)),
            per_g(near_t), per_g(win_t), per_g(far_t), per_g(cmp_lhs), const2(ovt), const2(eye),
        ],
        out_specs=pl.BlockSpec((1, TQ, NSA_GROUP * D_HEAD), lambda b, g, i: (b, i, g)),
        out_shape=jax.ShapeDtypeStruct((B, T, D_NSA), BF16),
        scratch_shapes=[
            pltpu.VMEM((rows_kv, KAUG), BF16), pltpu.VMEM((rows_kv, VAUG), BF16),
            pltpu.VMEM((rows_kv, KAUG), BF16), pltpu.VMEM((rows_kv, VAUG), BF16),
            pltpu.VMEM((ncp, VAUG), BF16), pltpu.VMEM((ncp, VAUG), BF16),
            pltpu.VMEM((MAX_BLOCKS, TQ), F32),
            pltpu.VMEM((ROWS, KAUG), BF16), pltpu.VMEM((ROWS, LANES), BF16),
            pltpu.VMEM((ROWS, LANES), F32), pltpu.VMEM((ROWS, LANES), F32), pltpu.VMEM((ROWS, 1), F32),
            pltpu.VMEM((ROWS, VAUG), F32),
            pltpu.VMEM((ROWS, KS), F32), pltpu.VMEM((ROWS, KS), F32),
            pltpu.VMEM((ROWS, KS), BF16), pltpu.VMEM((ROWS, KS), BF16),
            pltpu.VMEM((ROWS, KS + KT), F32), pltpu.VMEM((ROWS, KS + KT), BF16),
            pltpu.VMEM((TQ, ncp), F32), pltpu.VMEM((ROWS, D_HEAD), F32), pltpu.VMEM((MAX_BLOCKS, TQ), jnp.int32),
        ],
        compiler_params=pltpu.CompilerParams(
            dimension_semantics=("arbitrary", "arbitrary", "arbitrary"), vmem_limit_bytes=VMEM_LIMIT),
        name="nsa",
    )(q, kcmp, vcmp, ks, vs, kw, vw, gates, near_t, win_t, far_t, cmp_lhs, ovt, eye)


def _ffn_kernel(x_ref, on_ref, os_ref, wo_n_ref, wo_s_ref, fn_ref, wup_ref, cw_ref, cb_ref, wdn_ref,
                out_ref, prev_ref, perm_ref, h_ref, ua_scr, ub_scr, act_ref):
    tm = x_ref.shape[1]
    seg = tm // 8
    n_chunks = wup_ref.shape[1]

    @pl.when(pl.program_id(1) == 0)
    def _start_of_sequence():
        prev_ref[...] = jnp.zeros(prev_ref.shape, F32)

    x1 = x_ref[0] + _dot(on_ref[0], wo_n_ref[...]) + _dot(os_ref[0], wo_s_ref[...])
    ms = jnp.mean(x1 * x1, axis=-1, keepdims=True)
    hn = x1 * lax.rsqrt(ms + EPS) * fn_ref[...]
    out_ref[0] = x1
    n_lt = hn.shape[1] // LANES
    for c in range(n_lt):
        perm_ref[c] = hn[:, c * LANES:(c + 1) * LANES]
    for u in range(seg // 2):
        groups = [jnp.concatenate([perm_ref[c, pl.ds(2 * u + k, 8, stride=seg), :] for c in range(n_lt)], axis=1)
                  for k in range(2)]
        h_ref[16 * u:16 * u + 16, :] = jnp.concatenate(groups, axis=0).astype(BF16)

    def up(c, dst):
        h = h_ref[...]
        for half in range(2):
            dst[half] = _dot(h, wup_ref[half, c])

    first_sublane = lax.broadcasted_iota(jnp.int32, (8, FFN_CHUNK), 0) == 0

    def conv_gate(c, src):
        for r0 in range(0, tm, FFN_ROWS):
            ys = []
            for half in range(2):
                w = cw_ref[half, c]
                x0 = src[half, r0:r0 + FFN_ROWS]
                if r0 == 0:
                    wrap = [jnp.where(first_sublane, pltpu.roll(prev_ref[half, c, 8 * k:8 * k + 8], 1, 0),
                                      pltpu.roll(src[half, tm - 16 + 8 * k:tm - 8 + 8 * k], 1, 0)) for k in range(2)]
                    x1_ = jnp.concatenate([wrap[1], src[half, 0:FFN_ROWS - 8]], axis=0)
                    x2_ = jnp.concatenate([wrap[0], wrap[1], src[half, 0:FFN_ROWS - 16]], axis=0)
                else:
                    x1_ = src[half, r0 - 8:r0 + FFN_ROWS - 8]
                    x2_ = src[half, r0 - 16:r0 + FFN_ROWS - 16]
                ys.append(x0 * w[2:3] + x1_ * w[1:2] + x2_ * w[0:1] + cb_ref[half, c])
            a, g = ys
            col = pl.multiple_of(c * FFN_CHUNK, FFN_CHUNK)
            act_ref[r0:r0 + FFN_ROWS, pl.ds(col, FFN_CHUNK)] = ((g + g * jnp.tanh(g)) * a).astype(BF16)
        for half in range(2):
            prev_ref[half, c] = src[half, tm - 16:tm]

    up(0, ua_scr)

    def pair(u, carry):
        up(2 * u + 1, ub_scr)
        conv_gate(2 * u, ua_scr)
        up(2 * u + 2, ua_scr)
        conv_gate(2 * u + 1, ub_scr)
        return carry

    lax.fori_loop(0, (n_chunks - 1) // 2, pair, 0)
    conv_gate(n_chunks - 1, ua_scr)
    y = _halves(_dot, act_ref[...], wdn_ref[...])
    for c in range(n_lt):
        perm_ref[c] = y[:, c * LANES:(c + 1) * LANES]
    for sgm in range(8):
        natural = jnp.concatenate([perm_ref[c, pl.ds(sgm, seg, stride=8), :] for c in range(n_lt)], axis=1)
        out_ref[0, sgm * seg:(sgm + 1) * seg, :] += natural


def _ffn_call(x, o_nsa, o_sgu, wo_n, wo_s, ffn_norm, w_up, conv_w, conv_b, w_down, tm):
    B, T, _ = x.shape
    n_chunks = w_up.shape[1]
    assert n_chunks % 2 == 1
    resident =lambda a: pl.BlockSpec(a.shape, lambda b, i: (0,) * a.ndim, pipeline_mode=pl.Buffered(1))
    tile = lambda w: pl.BlockSpec((1, tm, w), lambda b, i: (b, i, 0))
    return pl.pallas_call(
        _ffn_kernel,
        grid=(B, T // tm),
        in_specs=[tile(D_MODEL), tile(D_NSA), tile(D_GMLP), resident(wo_n), resident(wo_s),
                  resident(ffn_norm), resident(w_up), resident(conv_w), resident(conv_b), resident(w_down)],
        out_specs=tile(D_MODEL),
        out_shape=jax.ShapeDtypeStruct((B, T, D_MODEL), F32),
        scratch_shapes=[pltpu.VMEM((2, n_chunks, 16, FFN_CHUNK), F32), pltpu.VMEM((D_MODEL // LANES, tm, LANES), F32),
                        pltpu.VMEM((tm, D_MODEL), BF16),
                        pltpu.VMEM((2, tm, FFN_CHUNK), F32), pltpu.VMEM((2, tm, FFN_CHUNK), F32),
                        pltpu.VMEM((tm, D_FF), BF16)],
        compiler_params=pltpu.CompilerParams(
            dimension_semantics=("arbitrary", "arbitrary"), vmem_limit_bytes=VMEM_LIMIT),
        name="ffn",
    )(x, o_nsa, o_sgu, wo_n, wo_s, ffn_norm, w_up, conv_w, conv_b, w_down)


def _mixers(x, rel_bias, attn_norm, w_in, q_norm, k_norm_cmp, k_norm_slc, k_norm_win,
            cmp_pe_k, cmp_w1_k, cmp_b1_k, cmp_w2_k, cmp_pe_v, cmp_w1_v, cmp_b1_v, cmp_w2_v,
            sgu_norm, sgu_w, sgu_b):
    B, T, _ = x.shape
    assert T % 512 == 0 and T // SEL_BLOCK <= MAX_BLOCKS and (T // CMP_STRIDE) % LANES == 0
    ncp = T // CMP_STRIDE
    G, R = NSA_KV_HEADS, NSA_GROUP

    o_kv = D_NSA
    o_g = D_NSA + 6 * D_KV
    o_uv = o_g + N_GATES
    w_q = w_in[:, :o_kv].astype(BF16)
    w_kv = w_in[:, o_kv:o_g].astype(BF16)
    per_group = NSA_GROUP * N_BRANCH
    w_g = jnp.pad(w_in[:, o_g:o_uv].reshape(D_MODEL, G, per_group), ((0, 0), (0, 0), (0, LANES - per_group)))
    w_g = w_g.reshape(D_MODEL, G * LANES).astype(BF16)
    w_uv = w_in[:, o_uv:].astype(BF16)
    qn = (jnp.tile(q_norm, NSA_HEADS) * (D_HEAD ** -0.5 * LOG2E)).reshape(1, D_NSA)
    ksn = jnp.tile(k_norm_slc, G).reshape(1, D_KV)
    kwn = jnp.tile(k_norm_win, G).reshape(1, D_KV)
    sgu_bt = jnp.repeat(sgu_b.T, GMLP_GROUP_DIM, axis=1)
    grp = np.arange(D_NSA) // D_HEAD
    bd = jnp.asarray(grp[:, None] == grp[None, :], BF16)

    q, kc, vc, ks, vs, kw, vw, gates, o_sgu = _proj_call(
        x, attn_norm.reshape(1, D_MODEL), w_q, w_kv, w_g, w_uv, qn, ksn, kwn,
        sgu_norm.reshape(1, D_GMLP), sgu_w, sgu_bt, bd, tm=512)

    half = CMP_STRIDE * D_HEAD
    kcmp, vcmp = _compress_call(
        kc.reshape(B, G, ncp, half), vc.reshape(B, G, ncp, half),
        cmp_pe_k.reshape(2, half), cmp_w1_k.astype(BF16), cmp_b1_k.reshape(1, CMP_HIDDEN),
        cmp_w2_k.astype(BF16), k_norm_cmp.reshape(1, D_HEAD),
        cmp_pe_v.reshape(2, half), cmp_w1_v.astype(BF16), cmp_b1_v.reshape(1, CMP_HIDDEN),
        cmp_w2_v.astype(BF16))

    near_t, win_t, far_t, cmp_lhs = _bias_tables(rel_bias)
    jj = np.arange(MAX_BLOCKS)[:, None]
    nn = np.arange(ncp)[None, :]
    ovt = jnp.asarray((nn >= 4 * jj - 1) & (nn <= 4 * jj + 3), BF16)
    eye = jnp.asarray(np.eye(TQ), BF16)

    o_nsa = _nsa_call(q, kcmp, vcmp, ks, vs, kw, vw, gates, near_t, win_t, far_t, cmp_lhs, ovt, eye)
    return o_nsa, o_sgu


def _layer(x, rel_bias, attn_norm, w_in, q_norm, k_norm_cmp, k_norm_slc, k_norm_win,
           cmp_pe_k, cmp_w1_k, cmp_b1_k, cmp_w2_k, cmp_pe_v, cmp_w1_v, cmp_b1_v, cmp_w2_v,
           sgu_norm, sgu_w, sgu_b, w_out, ffn_norm, w_up, conv_w, conv_b, w_down):
    o_nsa, o_sgu = _mixers(x, rel_bias, attn_norm, w_in, q_norm, k_norm_cmp, k_norm_slc, k_norm_win,
                           cmp_pe_k, cmp_w1_k, cmp_b1_k, cmp_w2_k, cmp_pe_v, cmp_w1_v, cmp_b1_v, cmp_w2_v,
                           sgu_norm, sgu_w, sgu_b)
    wo = w_out.astype(BF16)
    n_chunks = D_FF // FFN_CHUNK
    wu = jnp.transpose(w_up.astype(BF16).reshape(D_MODEL, 2, n_chunks, FFN_CHUNK), (1, 2, 0, 3))
    half_gate = jnp.asarray([1.0, 0.5], F32).reshape(2, 1, 1, 1)
    cw = jnp.transpose(conv_w.reshape(conv_w.shape[0], 2, n_chunks, FFN_CHUNK), (1, 2, 0, 3)) * half_gate
    cb = conv_b.reshape(2, n_chunks, 1, FFN_CHUNK) * half_gate
    wd = w_down.astype(BF16)
    return _ffn_call(x, o_nsa, o_sgu, wo[:D_NSA], wo[D_NSA:], ffn_norm.reshape(1, D_MODEL), wu, cw, cb, wd, tm=512)


def kernel(x, rel_bias, attn_norm, w_in, q_norm, k_norm_cmp, k_norm_slc, k_norm_win,
           cmp_pe_k, cmp_w1_k, cmp_b1_k, cmp_w2_k, cmp_pe_v, cmp_w1_v, cmp_b1_v, cmp_w2_v,
           sgu_norm, sgu_w, sgu_b, w_out, ffn_norm, w_up, conv_w, conv_b, w_down):
    depth = attn_norm.shape[0]
    for l in range(depth):
        x = _layer(x, rel_bias, attn_norm[l], w_in[l], q_norm[l], k_norm_cmp[l], k_norm_slc[l], k_norm_win[l],
                   cmp_pe_k[l], cmp_w1_k[l], cmp_b1_k[l], cmp_w2_k[l], cmp_pe_v[l], cmp_w1_v[l], cmp_b1_v[l],
                   cmp_w2_v[l], sgu_norm[l], sgu_w[l], sgu_b[l], w_out[l], ffn_norm[l], w_up[l], conv_w[l],
                   conv_b[l], w_down[l])
    return x
```

```python
import functools
import math

import numpy as np
import jax
import jax.numpy as jnp
from jax import lax
from jax.experimental import pallas as pl
from jax.experimental.pallas import tpu as pltpu

F32 = jnp.float32
BF16 = jnp.bfloat16

D_MODEL = 1024
D_HEAD = 64
NSA_HEADS = 8
NSA_KV_HEADS = 2
NSA_GROUP = NSA_HEADS // NSA_KV_HEADS
D_NSA = NSA_HEADS * D_HEAD
D_KV = NSA_KV_HEADS * D_HEAD
N_BRANCH = 3
N_GATES = NSA_HEADS * N_BRANCH
CMP_BLOCK = 32
CMP_STRIDE = 16
CMP_HIDDEN = 256
SEL_BLOCK = 64
SEL_TOPK = 16
WINDOW = 512
GMLP_GROUPS = 8
GMLP_GROUP_DIM = 64
D_GMLP = GMLP_GROUPS * GMLP_GROUP_DIM
CHUNK = 128
D_MIX = D_NSA + D_GMLP
REL_BUCKETS = 32
REL_MAX_DIST = 128
D_FF = 2816
EPS = 1e-6
NEG = -1e30
FORCED_SCORE = 1e4
LOG2E = 1.4426950408889634

LANES = 128
TQ = 256
SUB = TQ // SEL_BLOCK
ROWS = NSA_GROUP * TQ
KT = 256
KS = 2 * KT
RB = 64
BLOCKS_PER_TILE = KT // SEL_BLOCK
WIN_BLOCKS = WINDOW // SEL_BLOCK
KPAD = 1280
N_TOEPLITZ = 3
MAX_BLOCKS = 64
KAUG = 256
VAUG = 128
CBAND = 28
FFN_CHUNK = 256
FFN_ROWS = 64
VMEM_LIMIT = 56 * 1024 * 1024

_NT = (((1,), (1,)), ((), ()))


def _dot(a, b):
    return jnp.dot(a, b, preferred_element_type=F32)


def _dot_nt(a, b):
    return lax.dot_general(a, b, _NT, preferred_element_type=F32)


def _halves(dot, a, b):
    h = a.shape[0] // 2
    return jnp.concatenate([dot(a[:h], b), dot(a[h:], b)], axis=0)


def _split_bf16(x):
    hi = x.astype(BF16)
    lo = (x - hi.astype(F32)).astype(BF16)
    return hi, lo


def _gelu_tanh(x):
    return 0.5 * x * (1.0 + jnp.tanh(0.7978845608028654 * (x + 0.044715 * (x * x * x))))


def _sigmoid(x):
    return 0.5 * (1.0 + jnp.tanh(0.5 * x))


def _group_mean_sq(t, ones_blockdiag, width):
    t2 = t * t
    hi, lo = _split_bf16(t2)
    return (_dot(hi, ones_blockdiag) + _dot(lo, ones_blockdiag)) * (1.0 / width)


def _proj_kernel(x_ref, an_ref, wq_ref, wkv_ref, wg_ref, wuv_ref, qn_ref, ksn_ref, kwn_ref,
                 sgun_ref, sguw_ref, sgub_ref, bd_ref,
                 q_out, kc_out, vc_out, ks_out, vs_out, kw_out, vw_out, g_out, sgu_out):
    tm = x_ref.shape[1]
    x = x_ref[0]
    ms = jnp.mean(x * x, axis=-1, keepdims=True)
    h = (x * lax.rsqrt(ms + EPS) * an_ref[...]).astype(BF16)

    bd = bd_ref[...]
    q = _dot(h, wq_ref[...])
    qn = q * lax.rsqrt(_group_mean_sq(q, bd, D_HEAD) + EPS) * qn_ref[...]
    for hh in range(NSA_HEADS):
        q_out[0, hh] = qn[:, hh * D_HEAD:(hh + 1) * D_HEAD].astype(BF16)

    kv = _dot(h, wkv_ref[...])
    bd_kv = bd[:D_KV, :D_KV]
    kc = kv[:, 0 * D_KV:1 * D_KV]
    vc = kv[:, 1 * D_KV:2 * D_KV]
    ks = kv[:, 2 * D_KV:3 * D_KV]
    vs = kv[:, 3 * D_KV:4 * D_KV]
    kw = kv[:, 4 * D_KV:5 * D_KV]
    vw = kv[:, 5 * D_KV:6 * D_KV]
    ks = ks * lax.rsqrt(_group_mean_sq(ks, bd_kv, D_HEAD) + EPS) * ksn_ref[...]
    kw = kw * lax.rsqrt(_group_mean_sq(kw, bd_kv, D_HEAD) + EPS) * kwn_ref[...]
    for t, o_ref in ((kc, kc_out), (vc, vc_out), (ks, ks_out), (vs, vs_out), (kw, kw_out), (vw, vw_out)):
        for g in range(NSA_KV_HEADS):
            o_ref[0, g] = t[:, g * D_HEAD:(g + 1) * D_HEAD].astype(BF16)

    gates = _sigmoid(_dot(h, wg_ref[...]))
    for g in range(NSA_KV_HEADS):
        g_out[0, g] = gates[:, g * LANES:(g + 1) * LANES]

    uv = _gelu_tanh(_dot(h, wuv_ref[...]))
    u = uv[:, :D_GMLP]
    v = uv[:, D_GMLP:]
    vms = jnp.mean(v * v, axis=-1, keepdims=True)
    vb = (v * lax.rsqrt(vms + EPS) * sgun_ref[...]).astype(BF16)

    row = lax.broadcasted_iota(jnp.int32, (CHUNK, CHUNK), 0)
    col = lax.broadcasted_iota(jnp.int32, (CHUNK, CHUNK), 1)
    tril = col <= row
    w_tril = [jnp.where(tril, sguw_ref[g], 0.0).astype(BF16) for g in range(GMLP_GROUPS)]
    w_pair = [jnp.concatenate([w_tril[2 * p], w_tril[2 * p + 1]], axis=1) for p in range(D_GMLP // LANES)]
    first_half = lax.broadcasted_iota(jnp.int32, (CHUNK, LANES), 1) < GMLP_GROUP_DIM
    zero = jnp.zeros((CHUNK, LANES), BF16)
    for c in range(tm // CHUNK):
        rows = slice(c * CHUNK, (c + 1) * CHUNK)
        zs = []
        for p in range(D_GMLP // LANES):
            blk = vb[rows, p * LANES:(p + 1) * LANES]
            rhs = jnp.concatenate([jnp.where(first_half, blk, zero), jnp.where(first_half, zero, blk)], axis=0)
            zs.append(_dot(w_pair[p], rhs))
        z = jnp.concatenate(zs, axis=1) + sgub_ref[...]
        sgu_out[0, rows, :] = (u[rows, :] * z).astype(BF16)


def _proj_call(x, attn_norm, w_q, w_kv, w_g, w_uv, qn, ksn, kwn, sgu_norm, sgu_w, sgu_bt, bd, tm):
    B, T, _ = x.shape
    const2 = lambda b, i: (0, 0)
    const3 = lambda b, i: (0, 0, 0)
    head_spec = lambda nh: pl.BlockSpec((1, nh, tm, D_HEAD), lambda b, i: (b, 0, i, 0))
    kv_shape = jax.ShapeDtypeStruct((B, NSA_KV_HEADS, T, D_HEAD), BF16)
    return pl.pallas_call(
        _proj_kernel,
        grid=(B, T // tm),
        in_specs=[
            pl.BlockSpec((1, tm, D_MODEL), lambda b, i: (b, i, 0)),
            pl.BlockSpec((1, D_MODEL), const2),
            pl.BlockSpec(w_q.shape, const2),
            pl.BlockSpec(w_kv.shape, const2),
            pl.BlockSpec(w_g.shape, const2),
            pl.BlockSpec(w_uv.shape, const2),
            pl.BlockSpec((1, D_NSA), const2),
            pl.BlockSpec((1, D_KV), const2),
            pl.BlockSpec((1, D_KV), const2),
            pl.BlockSpec((1, D_GMLP), const2),
            pl.BlockSpec(sgu_w.shape, const3),
            pl.BlockSpec(sgu_bt.shape, const2),
            pl.BlockSpec(bd.shape, const2),
        ],
        out_specs=[
            head_spec(NSA_HEADS),
            head_spec(NSA_KV_HEADS), head_spec(NSA_KV_HEADS), head_spec(NSA_KV_HEADS),
            head_spec(NSA_KV_HEADS), head_spec(NSA_KV_HEADS), head_spec(NSA_KV_HEADS),
            pl.BlockSpec((1, NSA_KV_HEADS, tm, LANES), lambda b, i: (b, 0, i, 0)),
            pl.BlockSpec((1, tm, D_GMLP), lambda b, i: (b, i, 0)),
        ],
        out_shape=[
            jax.ShapeDtypeStruct((B, NSA_HEADS, T, D_HEAD), BF16),
            kv_shape, kv_shape, kv_shape, kv_shape, kv_shape, kv_shape,
            jax.ShapeDtypeStruct((B, NSA_KV_HEADS, T, LANES), F32),
            jax.ShapeDtypeStruct((B, T, D_GMLP), BF16),
        ],
        compiler_params=pltpu.CompilerParams(
            dimension_semantics=("arbitrary", "arbitrary"), vmem_limit_bytes=VMEM_LIMIT),
        name="proj",
    )(x, attn_norm, w_q, w_kv, w_g, w_uv, qn, ksn, kwn, sgu_norm, sgu_w, sgu_bt, bd)


def _compress_one(tok_ref, pe_ref, w1_ref, b1_ref, w2_ref):
    half = CMP_STRIDE * D_HEAD
    tok = tok_ref[0, 0].astype(F32)
    top = (tok + pe_ref[0:1, :]).astype(BF16)
    bot = (tok + pe_ref[1:2, :]).astype(BF16)
    a = _dot(top, w1_ref[:half, :])
    b = _dot(bot, w1_ref[half:, :])
    ncp = a.shape[0]
    pre = a + pltpu.roll(b, ncp - 1, 0) + b1_ref[...]
    hid = _gelu_tanh(pre).astype(BF16)
    return _dot(hid, w2_ref[...])


def _compress_kernel(kc_ref, vc_ref, pek_ref, w1k_ref, b1k_ref, w2k_ref, kn_ref,
                     pev_ref, w1v_ref, b1v_ref, w2v_ref, k_out, v_out):
    k = _compress_one(kc_ref, pek_ref, w1k_ref, b1k_ref, w2k_ref)
    kms = jnp.mean(k * k, axis=-1, keepdims=True)
    k_out[0, 0] = (k * lax.rsqrt(kms + EPS) * kn_ref[...]).astype(BF16)
    v_out[0, 0] = _compress_one(vc_ref, pev_ref, w1v_ref, b1v_ref, w2v_ref).astype(BF16)


def _compress_call(kc, vc, pek, w1k, b1k, w2k, kn, pev, w1v, b1v, w2v):
    B, G, ncp, width = kc.shape
    const2 = lambda b, g: (0, 0)
    tok_spec = pl.BlockSpec((1, 1, ncp, width), lambda b, g: (b, g, 0, 0))
    out_spec = pl.BlockSpec((1, 1, ncp, D_HEAD), lambda b, g: (b, g, 0, 0))
    full = lambda a: pl.BlockSpec(a.shape, const2)
    out_shape = jax.ShapeDtypeStruct((B, G, ncp, D_HEAD), BF16)
    return pl.pallas_call(
        _compress_kernel,
        grid=(B, G),
        in_specs=[tok_spec, tok_spec, full(pek), full(w1k), full(b1k), full(w2k), full(kn),
                  full(pev), full(w1v), full(b1v), full(w2v)],
        out_specs=[out_spec, out_spec],
        out_shape=[out_shape, out_shape],
        compiler_params=pltpu.CompilerParams(
            dimension_semantics=("arbitrary", "arbitrary"), vmem_limit_bytes=VMEM_LIMIT),
        name="compress",
    )(kc, vc, pek, w1k, b1k, w2k, kn, pev, w1v, b1v, w2v)


def _rel_bucket_np(dist):
    max_exact = REL_BUCKETS // 2
    d = np.maximum(dist, 1).astype(np.float32)
    log_b = max_exact + (np.log(d / np.float32(max_exact)) / np.float32(math.log(REL_MAX_DIST / max_exact))
                         * np.float32(REL_BUCKETS - max_exact)).astype(np.int32)
    log_b = np.clip(log_b, max_exact, REL_BUCKETS - 1)
    return np.where(dist < max_exact, np.maximum(dist, 0), log_b)


def _bias_tables(rel_bias):
    rb = rel_bias.astype(F32) * LOG2E
    qq = np.arange(SEL_BLOCK)[:, None]
    kp = np.arange(SEL_BLOCK)[None, :]
    tile_d = [m * SEL_BLOCK + qq - kp for m in range(N_TOEPLITZ)]
    s_ = np.arange(SUB)[:, None, None]
    e_ = np.arange(CBAND)[None, None, :]
    dist_c = qq[None, :, :] + CMP_STRIDE * (e_ - 4 * (SUB - 1) + 4 * s_) - (3 * CMP_STRIDE + CMP_BLOCK - 1)
    all_d = np.concatenate([d.reshape(-1) for d in tile_d] + [dist_c.reshape(-1)])
    onehot = np.eye(REL_BUCKETS, dtype=np.float32)[_rel_bucket_np(all_d)]
    vals = jnp.dot(jnp.asarray(onehot), rb, precision=lax.Precision.HIGHEST)
    vals = jnp.where(jnp.asarray(all_d >= 0)[:, None], vals, NEG)
    n_t = N_TOEPLITZ * SEL_BLOCK * SEL_BLOCK
    toep = jnp.transpose(vals[:n_t].reshape(N_TOEPLITZ, SEL_BLOCK, SEL_BLOCK, NSA_HEADS), (0, 3, 1, 2))
    band = jnp.transpose(vals[n_t:].reshape(SUB, SEL_BLOCK, CBAND, NSA_HEADS), (3, 0, 1, 2))
    far = rb[REL_BUCKETS - 1]
    far_tile = jnp.broadcast_to(far[:, None, None], (NSA_HEADS, SEL_BLOCK, SEL_BLOCK))
    neg_tile = jnp.full((NSA_HEADS, SEL_BLOCK, SEL_BLOCK), NEG, F32)
    edge_tile = jnp.where(jnp.asarray(kp > qq)[None], far_tile, NEG)

    def tile(m, windowed):
        if m < 0 or (windowed and m > WIN_BLOCKS):
            return neg_tile
        if windowed and m == WIN_BLOCKS:
            return edge_tile
        return toep[m] if m < N_TOEPLITZ else far_tile

    def table(first_m, n_blocks, windowed):
        rows = [jnp.concatenate([tile(first_m + s - c, windowed) for c in range(n_blocks)], axis=2)
                for s in range(SUB)]
        t = jnp.concatenate(rows, axis=1)
        return t.reshape(NSA_KV_HEADS, ROWS, n_blocks * SEL_BLOCK)

    far_rows = jnp.repeat(far, TQ).reshape(NSA_KV_HEADS, ROWS, 1)
    near_t = table(BLOCKS_PER_TILE, 2 * BLOCKS_PER_TILE, False) - far_rows
    win_t = table(WIN_BLOCKS, WIN_BLOCKS + BLOCKS_PER_TILE, True)
    far_t = jnp.broadcast_to(far_rows, (NSA_KV_HEADS, ROWS, LANES))
    band_t = band.reshape(NSA_KV_HEADS, ROWS, CBAND)
    band_hi = band_t.astype(BF16)
    band_lo = (band_t - band_hi.astype(F32)).astype(BF16)
    far_hi = far_rows.astype(BF16)
    far_lo = (far_rows - far_hi.astype(F32)).astype(BF16)
    neg_col = jnp.full((NSA_KV_HEADS, ROWS, 1), NEG, BF16)
    zeros = jnp.zeros((NSA_KV_HEADS, ROWS, D_HEAD - 2 * CBAND - 3), BF16)
    q_lanes = jnp.zeros((NSA_KV_HEADS, ROWS, D_HEAD), BF16)
    cmp_lhs = jnp.concatenate([q_lanes, band_hi, band_lo, far_hi, far_lo, neg_col, zeros], axis=2)
    return near_t, win_t, far_t, cmp_lhs


def _row_max(lane_tiles):
    m = jnp.max(functools.reduce(jnp.maximum, lane_tiles), axis=1, keepdims=True)
    return jnp.broadcast_to(m, lane_tiles[0].shape)


def _nsa_kernel(q_ref, kc_ref, vc_ref, ks_ref, vs_ref, kw_ref, vw_ref, g_ref,
                near_ref, win_ref, far_ref, cmpl_ref, ovt_ref, eye_ref,
                o_ref, ksa, vsa, kwa, vwa, kca, vca, sc_scr, lhs_scr, lhsc_scr, m_scr, al_scr, inv_scr, acc_scr, sa_scr, sb_scr, pa_scr, pb_scr,
                sw_scr, pw_scr, ps_scr, out_scr, cnt_scr):
    qt = pl.program_id(2)
    first = qt * SUB
    T = ks_ref.shape[2]
    ncp = kc_ref.shape[2]

    @pl.when(qt == 0)
    def _init():
        n_chunks = ksa.shape[0] // KT
        r_io = lax.broadcasted_iota(jnp.int32, (KT, KAUG), 0)
        l_io = lax.broadcasted_iota(jnp.int32, (KT, KAUG), 1)
        lv = lax.broadcasted_iota(jnp.int32, (KT, VAUG), 1)
        vpat = jnp.where(lv == D_HEAD, 1.0, 0.0).astype(BF16)

        def fill(c, carry):
            rows = pl.ds(pl.multiple_of(c * KT, KT), KT)
            key = r_io + (c * KT - KPAD)
            blk = key // SEL_BLOCK
            real = (key >= 0) & (key < T)
            hot = real & (((l_io < D_HEAD) & (l_io == blk))
                          | ((l_io >= D_HEAD) & (l_io < 2 * D_HEAD) & (l_io - D_HEAD == blk)))
            hot = hot | ((~real) & (l_io == 3 * D_HEAD))
            pat = jnp.where(hot, 1.0, 0.0).astype(BF16)
            ksa[rows, :] = pat
            kwa[rows, :] = pat
            vsa[rows, :] = vpat
            vwa[rows, :] = vpat
            return carry

        lax.fori_loop(0, n_chunks, fill, 0)
        ksa[KPAD:KPAD + T, 2 * D_HEAD:3 * D_HEAD] = ks_ref[0, 0]
        kwa[KPAD:KPAD + T, 2 * D_HEAD:3 * D_HEAD] = kw_ref[0, 0]
        vsa[KPAD:KPAD + T, 0:D_HEAD] = vs_ref[0, 0]
        vwa[KPAD:KPAD + T, 0:D_HEAD] = vw_ref[0, 0]
        lc = lax.broadcasted_iota(jnp.int32, (ncp, VAUG), 1)
        kca[...] = jnp.zeros((ncp, VAUG), BF16)
        kca[:, 0:D_HEAD] = kc_ref[0, 0]
        vca[...] = jnp.where(lc == D_HEAD, 1.0, 0.0).astype(BF16)
        vca[:, 0:D_HEAD] = vc_ref[0, 0]

    q4 = q_ref[0].reshape(ROWS, D_HEAD)
    gates = g_ref[0, 0]
    row_blocks = [slice(r, r + RB) for r in range(0, ROWS, RB)]

    gate_rows = jnp.concatenate(
        [gates if hh == 0 else pltpu.roll(gates, LANES - hh * N_BRANCH, 1) for hh in range(NSA_GROUP)], axis=0)

    def gate_col(branch):
        return gate_rows[:, branch:branch + 1]

    lane_c = lax.broadcasted_iota(jnp.int32, (ncp, LANES), 1)
    n_c = lax.broadcasted_iota(jnp.int32, (ncp, LANES), 0)
    e = lane_c - D_HEAD
    base = 4 * first + (4 * SUB - 1)
    ind = (((e >= 0) & (e < CBAND) & (n_c == base - e))
           | ((e >= CBAND) & (e < 2 * CBAND) & (n_c == base - (e - CBAND)))
           | (((e == 2 * CBAND) | (e == 2 * CBAND + 1)) & (n_c < base - (CBAND - 1)))
           | ((e == 2 * CBAND + 2) & (n_c > base)))
    rhs_c = jnp.where(lane_c < D_HEAD, kca[...], jnp.where(ind, 1.0, 0.0).astype(BF16))
    lhsc_scr[...] = cmpl_ref[0]
    lhsc_scr[:, 0:D_HEAD] = q4
    sa_scr[:, 0:ncp] = _halves(_dot_nt, lhsc_scr[...], rhs_c)

    lane_q = lax.broadcasted_iota(jnp.int32, (ROWS, LANES), 1)
    lhs_scr[:, LANES:2 * LANES] = jnp.where(lane_q == D_HEAD, NEG, 0.0).astype(BF16)
    lhs_scr[:, LANES:LANES + D_HEAD] = q4
    lhs_win = jnp.concatenate([jnp.zeros((ROWS, LANES), BF16), lhs_scr[:, LANES:2 * LANES]], axis=1)
    win_start = pl.multiple_of(KPAD + (first - WIN_BLOCKS) * SEL_BLOCK, KT)
    sw_scr[...] = _halves(_dot_nt, lhs_win, kwa[pl.ds(win_start, KS + KT), :])

    for i, rows in enumerate(row_blocks):
        s = [sa_scr[rows, c * LANES:(c + 1) * LANES] for c in range(ncp // LANES)]
        m = _row_max(s)
        p = [jnp.where(t > 0.5 * NEG, jnp.exp2(t - m), 0.0) for t in s]
        l = jnp.sum(functools.reduce(jnp.add, p), axis=1, keepdims=True)
        inv = jnp.where(l > 0.0, 1.0 / l, 0.0)
        inv_scr[rows] = inv
        inv_b = jnp.broadcast_to(inv, (RB, LANES))
        tok = slice((i % (TQ // RB)) * RB, (i % (TQ // RB) + 1) * RB)
        for c in range(ncp // LANES):
            lanes = slice(c * LANES, (c + 1) * LANES)
            pb_scr[rows, lanes] = p[c].astype(BF16)
            if i < TQ // RB:
                ps_scr[tok, lanes] = p[c] * inv_b
            else:
                ps_scr[tok, lanes] += p[c] * inv_b
    acc_c = _halves(_dot, pb_scr[:, 0:ncp], vca[...])
    out_scr[...] = acc_c[:, :D_HEAD] * (gate_col(0) * inv_scr[...])
    ps_hi, ps_lo = _split_bf16(ps_scr[...])
    ovt = ovt_ref[...]
    imp_t = _dot_nt(ovt, ps_hi) + _dot_nt(ovt, ps_lo)

    j_t = lax.broadcasted_iota(jnp.int32, (MAX_BLOCKS, TQ), 0)
    cur = first + lax.broadcasted_iota(jnp.int32, (MAX_BLOCKS, TQ), 1) // SEL_BLOCK
    forced = (j_t == 0) | (j_t == cur) | (j_t == cur - 1)
    score_t = jnp.where(forced, FORCED_SCORE, jnp.where(j_t <= cur, imp_t, NEG))
    sc_scr[...] = score_t

    cnt_scr[...] = jnp.zeros((MAX_BLOCKS, TQ), jnp.int32)
    sub_io = lax.broadcasted_iota(jnp.int32, (8, TQ), 0)
    n_groups = MAX_BLOCKS // 8
    for grp in range(n_groups):
        @pl.when(grp * 8 < first + SUB)
        def _count_group():
            targets = [score_t[8 * jg:8 * jg + 8] for jg in range(n_groups)]
            acc = [jnp.zeros((8, TQ), jnp.int32) for _ in range(n_groups)]
            for k in range(8):
                row_i = jnp.broadcast_to(sc_scr[grp * 8 + k:grp * 8 + k + 1, :], (8, TQ))
                for jg in range(n_groups):
                    if jg > grp:
                        beats = row_i >= targets[jg]
                    elif jg < grp:
                        beats = row_i > targets[jg]
                    else:
                        beats = (row_i > targets[jg]) | ((row_i == targets[jg]) & (sub_io > k))
                    acc[jg] = acc[jg] + beats.astype(jnp.int32)
            cnt_scr[...] += jnp.concatenate(acc, axis=0)

    cnt = cnt_scr[...]
    sel_t = jnp.where((cnt < SEL_TOPK) & (j_t <= cur), 1.0, 0.0).astype(BF16)
    sel = _dot_nt(eye_ref[...], jnp.concatenate([sel_t, sel_t], axis=0))
    sel4 = jnp.concatenate([sel] * NSA_GROUP, axis=0) > 0.5

    hi, lo = _split_bf16(jnp.where(sel4, far_ref[0], NEG))
    lhs_scr[:, 0:LANES] = jnp.where(lane_q < D_HEAD, hi, lo)
    m_scr[...] = jnp.full((ROWS, LANES), NEG, F32)
    acc_scr[...] = jnp.zeros((ROWS, VAUG), F32)

    def tile_start(k):
        return pl.multiple_of(KPAD + (first - BLOCKS_PER_TILE) * SEL_BLOCK - k * KS, KT)

    def scores(k, dst):
        dst[...] = _halves(_dot_nt, lhs_scr[...], ksa[pl.ds(tile_start(k), KS), :])

    def consume(k, src, p_scr, with_table):
        n_lt = KS // LANES
        for rows in row_blocks:
            s = [src[rows, c * LANES:(c + 1) * LANES] for c in range(n_lt)]
            if with_table:
                s = [s[c] + near_ref[0, rows, c * LANES:(c + 1) * LANES] for c in range(n_lt)]
            m_prev = m_scr[rows]
            m = jnp.maximum(m_prev, _row_max(s))
            al_scr[rows] = jnp.exp2(m_prev - m)
            m_scr[rows] = m
            for c in range(n_lt):
                p_scr[rows, c * LANES:(c + 1) * LANES] = jnp.exp2((s[c] - m).astype(BF16))
        acc_scr[...] = acc_scr[...] * al_scr[...] + _dot(p_scr[:, 0:KS], vsa[pl.ds(tile_start(k), KS), :])

    scores(0, sa_scr)

    for rows in row_blocks:
        s = [sw_scr[rows, c * LANES:(c + 1) * LANES] + win_ref[0, rows, c * LANES:(c + 1) * LANES]
             for c in range((KS + KT) // LANES)]
        m = _row_max(s)
        for c in range((KS + KT) // LANES):
            pw_scr[rows, c * LANES:(c + 1) * LANES] = jnp.exp2((s[c] - m).astype(BF16))
    acc_w = _halves(_dot, pw_scr[...], vwa[pl.ds(win_start, KS + KT), :])
    out_scr[...] += acc_w[:, :D_HEAD] * (gate_col(2) / acc_w[:, D_HEAD:D_HEAD + 1])

    def pair(with_table, u, carry):
        scores(2 * u + 1, sb_scr)
        consume(2 * u, sa_scr, pa_scr, with_table)
        scores(2 * u + 2, sa_scr)
        consume(2 * u + 1, sb_scr, pb_scr, False)
        return carry

    n_tiles = (qt + 2) // 2
    pair(True, 0, 0)
    lax.fori_loop(1, jnp.maximum(n_tiles // 2, 1), functools.partial(pair, False), 0)

    @pl.when((n_tiles % 2 == 1) & (n_tiles >= 3))
    def _odd_tile():
        consume(n_tiles - 1, sa_scr, pa_scr, False)

    acc_s = acc_scr[...]

    o = out_scr[...] + acc_s[:, :D_HEAD] * (gate_col(1) / acc_s[:, D_HEAD:D_HEAD + 1])
    for hh in range(NSA_GROUP):
        o_ref[0, :, hh * D_HEAD:(hh + 1) * D_HEAD] = o[hh * TQ:(hh + 1) * TQ].astype(BF16)


def _nsa_call(q, kcmp, vcmp, ks, vs, kw, vw, gates, near_t, win_t, far_t, cmp_lhs, ovt, eye):
    B, _, T, _ = q.shape
    G = NSA_KV_HEADS
    ncp = kcmp.shape[2]
    rows_kv = KPAD + T
    per_bg =lambda a: pl.BlockSpec((1, 1) + a.shape[2:], lambda b, g, i: (b, g, 0, 0))
    per_g = lambda a: pl.BlockSpec((1,) + a.shape[1:], lambda b, g, i: (g, 0, 0))
    const2 = lambda a: pl.BlockSpec(a.shape, lambda b, g, i: (0, 0))
    return pl.pallas_call(
        _nsa_kernel,
        grid=(B, G, T // TQ),
        in_specs=[
            pl.BlockSpec((1, NSA_GROUP, TQ, D_HEAD), lambda b, g, i: (b, g, i, 0)),
            per_bg(kcmp), per_bg(vcmp), per_bg(ks), per_bg(vs), per_bg(kw), per_bg(vw),
            pl.BlockSpec((1, 1, TQ, LANES), lambda b, g, i: (b, g, i, 0)),
            per_g(near_t), per_g(win_t), per_g(far_t), per_g(cmp_lhs), const2(ovt), const2(eye),
        ],
        out_specs=pl.BlockSpec((1, TQ, NSA_GROUP * D_HEAD), lambda b, g, i: (b, i, g)),
        out_shape=jax.ShapeDtypeStruct((B, T, D_NSA), BF16),
        scratch_shapes=[
            pltpu.VMEM((rows_kv, KAUG), BF16), pltpu.VMEM((rows_kv, VAUG), BF16),
            pltpu.VMEM((rows_kv, KAUG), BF16), pltpu.VMEM((rows_kv, VAUG), BF16),
            pltpu.VMEM((ncp, VAUG), BF16), pltpu.VMEM((ncp, VAUG), BF16),
            pltpu.VMEM((MAX_BLOCKS, TQ), F32),
            pltpu.VMEM((ROWS, KAUG), BF16), pltpu.VMEM((ROWS, LANES), BF16),
            pltpu.VMEM((ROWS, LANES), F32), pltpu.VMEM((ROWS, LANES), F32), pltpu.VMEM((ROWS, 1), F32),
            pltpu.VMEM((ROWS, VAUG), F32),
            pltpu.VMEM((ROWS, KS), F32), pltpu.VMEM((ROWS, KS), F32),
            pltpu.VMEM((ROWS, KS), BF16), pltpu.VMEM((ROWS, KS), BF16),
            pltpu.VMEM((ROWS, KS + KT), F32), pltpu.VMEM((ROWS, KS + KT), BF16),
            pltpu.VMEM((TQ, ncp), F32), pltpu.VMEM((ROWS, D_HEAD), F32), pltpu.VMEM((MAX_BLOCKS, TQ), jnp.int32),
        ],
        compiler_params=pltpu.CompilerParams(
            dimension_semantics=("arbitrary", "arbitrary", "arbitrary"), vmem_limit_bytes=VMEM_LIMIT),
        name="nsa",
    )(q, kcmp, vcmp, ks, vs, kw, vw, gates, near_t, win_t, far_t, cmp_lhs, ovt, eye)


def _ffn_kernel(x_ref, on_ref, os_ref, wo_n_ref, wo_s_ref, fn_ref, wup_ref, cw_ref, cb_ref, wdn_ref, pm_ref,
                out_ref, prev_ref, perm_ref, h_ref, ua_scr, ub_scr, act_ref):
    tm = x_ref.shape[1]
    seg = tm // 8
    n_chunks = wup_ref.shape[1]

    @pl.when(pl.program_id(1) == 0)
    def _start_of_sequence():
        prev_ref[...] = jnp.zeros(prev_ref.shape, F32)

    x1 = x_ref[0] + _dot(on_ref[0], wo_n_ref[...]) + _dot(os_ref[0], wo_s_ref[...])
    ms = jnp.mean(x1 * x1, axis=-1, keepdims=True)
    hn = (x1 * lax.rsqrt(ms + EPS) * fn_ref[...]).astype(BF16)
    out_ref[0] = x1
    n_lt = x1.shape[1] // LANES
    h_ref[...] = _halves(_dot, pm_ref[...], hn).astype(BF16)

    def up(c, dst):
        h = h_ref[...]
        for half in range(2):
            dst[half] = _dot(h, wup_ref[half, c])

    first_sublane = lax.broadcasted_iota(jnp.int32, (8, FFN_CHUNK), 0) == 0

    def conv_gate(c, src):
        for r0 in range(0, tm, FFN_ROWS):
            ys = []
            for half in range(2):
                w = cw_ref[half, c]
                x0 = src[half, r0:r0 + FFN_ROWS]
                if r0 == 0:
                    wrap = [jnp.where(first_sublane, pltpu.roll(prev_ref[half, c, 8 * k:8 * k + 8], 1, 0),
                                      pltpu.roll(src[half, tm - 16 + 8 * k:tm - 8 + 8 * k], 1, 0)) for k in range(2)]
                    x1_ = jnp.concatenate([wrap[1], src[half, 0:FFN_ROWS - 8]], axis=0)
                    x2_ = jnp.concatenate([wrap[0], wrap[1], src[half, 0:FFN_ROWS - 16]], axis=0)
                else:
                    x1_ = src[half, r0 - 8:r0 + FFN_ROWS - 8]
                    x2_ = src[half, r0 - 16:r0 + FFN_ROWS - 16]
                ys.append(x0 * w[2:3] + x1_ * w[1:2] + x2_ * w[0:1] + cb_ref[half, c])
            a, g = ys
            col = pl.multiple_of(c * FFN_CHUNK, FFN_CHUNK)
            act_ref[r0:r0 + FFN_ROWS, pl.ds(col, FFN_CHUNK)] = ((g + g * jnp.tanh(g)) * a).astype(BF16)
        for half in range(2):
            prev_ref[half, c] = src[half, tm - 16:tm]

    up(0, ua_scr)

    def pair(u, carry):
        up(2 * u + 1, ub_scr)
        conv_gate(2 * u, ua_scr)
        up(2 * u + 2, ua_scr)
        conv_gate(2 * u + 1, ub_scr)
        return carry

    lax.fori_loop(0, (n_chunks - 1) // 2, pair, 0)
    conv_gate(n_chunks - 1, ua_scr)
    y = _halves(_dot, act_ref[...], wdn_ref[...])
    for c in range(n_lt):
        perm_ref[c] = y[:, c * LANES:(c + 1) * LANES]
    for sgm in range(8):
        natural = jnp.concatenate([perm_ref[c, pl.ds(sgm, seg, stride=8), :] for c in range(n_lt)], axis=1)
        out_ref[0, sgm * seg:(sgm + 1) * seg, :] += natural


def _ffn_call(x, o_nsa, o_sgu, wo_n, wo_s, ffn_norm, w_up, conv_w, conv_b, w_down, tm):
    B, T, _ = x.shape
    r = np.arange(tm)
    perm = jnp.asarray(((r % 8) * (tm // 8) + r // 8)[:, None] == r[None, :], BF16)
    n_chunks = w_up.shape[1]
    assert n_chunks % 2 == 1
    resident =lambda a: pl.BlockSpec(a.shape, lambda b, i: (0,) * a.ndim, pipeline_mode=pl.Buffered(1))
    tile = lambda w: pl.BlockSpec((1, tm, w), lambda b, i: (b, i, 0))
    return pl.pallas_call(
        _ffn_kernel,
        grid=(B, T // tm),
        in_specs=[tile(D_MODEL), tile(D_NSA), tile(D_GMLP), resident(wo_n), resident(wo_s),
                  resident(ffn_norm), resident(w_up), resident(conv_w), resident(conv_b), resident(w_down),
                  resident(perm)],
        out_specs=tile(D_MODEL),
        out_shape=jax.ShapeDtypeStruct((B, T, D_MODEL), F32),
        scratch_shapes=[pltpu.VMEM((2, n_chunks, 16, FFN_CHUNK), F32), pltpu.VMEM((D_MODEL // LANES, tm, LANES), F32),
                        pltpu.VMEM((tm, D_MODEL), BF16),
                        pltpu.VMEM((2, tm, FFN_CHUNK), F32), pltpu.VMEM((2, tm, FFN_CHUNK), F32),
                        pltpu.VMEM((tm, D_FF), BF16)],
        compiler_params=pltpu.CompilerParams(
            dimension_semantics=("arbitrary", "arbitrary"), vmem_limit_bytes=VMEM_LIMIT),
        name="ffn",
    )(x, o_nsa, o_sgu, wo_n, wo_s, ffn_norm, w_up, conv_w, conv_b, w_down, perm)


def _mixers(x, rel_bias, attn_norm, w_in, q_norm, k_norm_cmp, k_norm_slc, k_norm_win,
            cmp_pe_k, cmp_w1_k, cmp_b1_k, cmp_w2_k, cmp_pe_v, cmp_w1_v, cmp_b1_v, cmp_w2_v,
            sgu_norm, sgu_w, sgu_b):
    B, T, _ = x.shape
    assert T % 512 == 0 and T // SEL_BLOCK <= MAX_BLOCKS and (T // CMP_STRIDE) % LANES == 0
    ncp = T // CMP_STRIDE
    G, R = NSA_KV_HEADS, NSA_GROUP

    o_kv = D_NSA
    o_g = D_NSA + 6 * D_KV
    o_uv = o_g + N_GATES
    w_q = w_in[:, :o_kv].astype(BF16)
    w_kv = w_in[:, o_kv:o_g].astype(BF16)
    per_group = NSA_GROUP * N_BRANCH
    w_g = jnp.pad(w_in[:, o_g:o_uv].reshape(D_MODEL, G, per_group), ((0, 0), (0, 0), (0, LANES - per_group)))
    w_g = w_g.reshape(D_MODEL, G * LANES).astype(BF16)
    w_uv = w_in[:, o_uv:].astype(BF16)
    qn = (jnp.tile(q_norm, NSA_HEADS) * (D_HEAD ** -0.5 * LOG2E)).reshape(1, D_NSA)
    ksn = jnp.tile(k_norm_slc, G).reshape(1, D_KV)
    kwn = jnp.tile(k_norm_win, G).reshape(1, D_KV)
    sgu_bt = jnp.repeat(sgu_b.T, GMLP_GROUP_DIM, axis=1)
    grp = np.arange(D_NSA) // D_HEAD
    bd = jnp.asarray(grp[:, None] == grp[None, :], BF16)

    q, kc, vc, ks, vs, kw, vw, gates, o_sgu = _proj_call(
        x, attn_norm.reshape(1, D_MODEL), w_q, w_kv, w_g, w_uv, qn, ksn, kwn,
        sgu_norm.reshape(1, D_GMLP), sgu_w, sgu_bt, bd, tm=512)

    half = CMP_STRIDE * D_HEAD
    kcmp, vcmp = _compress_call(
        kc.reshape(B, G, ncp, half), vc.reshape(B, G, ncp, half),
        cmp_pe_k.reshape(2, half), cmp_w1_k.astype(BF16), cmp_b1_k.reshape(1, CMP_HIDDEN),
        cmp_w2_k.astype(BF16), k_norm_cmp.reshape(1, D_HEAD),
        cmp_pe_v.reshape(2, half), cmp_w1_v.astype(BF16), cmp_b1_v.reshape(1, CMP_HIDDEN),
        cmp_w2_v.astype(BF16))

    near_t, win_t, far_t, cmp_lhs = _bias_tables(rel_bias)
    jj = np.arange(MAX_BLOCKS)[:, None]
    nn = np.arange(ncp)[None, :]
    ovt = jnp.asarray((nn >= 4 * jj - 1) & (nn <= 4 * jj + 3), BF16)
    eye = jnp.asarray(np.eye(TQ), BF16)

    o_nsa = _nsa_call(q, kcmp, vcmp, ks, vs, kw, vw, gates, near_t, win_t, far_t, cmp_lhs, ovt, eye)
    return o_nsa, o_sgu


def _layer(x, rel_bias, attn_norm, w_in, q_norm, k_norm_cmp, k_norm_slc, k_norm_win,
           cmp_pe_k, cmp_w1_k, cmp_b1_k, cmp_w2_k, cmp_pe_v, cmp_w1_v, cmp_b1_v, cmp_w2_v,
           sgu_norm, sgu_w, sgu_b, w_out, ffn_norm, w_up, conv_w, conv_b, w_down):
    o_nsa, o_sgu = _mixers(x, rel_bias, attn_norm, w_in, q_norm, k_norm_cmp, k_norm_slc, k_norm_win,
                           cmp_pe_k, cmp_w1_k, cmp_b1_k, cmp_w2_k, cmp_pe_v, cmp_w1_v, cmp_b1_v, cmp_w2_v,
                           sgu_norm, sgu_w, sgu_b)
    wo = w_out.astype(BF16)
    n_chunks = D_FF // FFN_CHUNK
    wu = jnp.transpose(w_up.astype(BF16).reshape(D_MODEL, 2, n_chunks, FFN_CHUNK), (1, 2, 0, 3))
    half_gate = jnp.asarray([1.0, 0.5], F32).reshape(2, 1, 1, 1)
    cw = jnp.transpose(conv_w.reshape(conv_w.shape[0], 2, n_chunks, FFN_CHUNK), (1, 2, 0, 3)) * half_gate
    cb = conv_b.reshape(2, n_chunks, 1, FFN_CHUNK) * half_gate
    wd = w_down.astype(BF16)
    return _ffn_call(x, o_nsa, o_sgu, wo[:D_NSA], wo[D_NSA:], ffn_norm.reshape(1, D_MODEL), wu, cw, cb, wd, tm=512)


def kernel(x, rel_bias, attn_norm, w_in, q_norm, k_norm_cmp, k_norm_slc, k_norm_win,
           cmp_pe_k, cmp_w1_k, cmp_b1_k, cmp_w2_k, cmp_pe_v, cmp_w1_v, cmp_b1_v, cmp_w2_v,
           sgu_norm, sgu_w, sgu_b, w_out, ffn_norm, w_up, conv_w, conv_b, w_down):
    depth = attn_norm.shape[0]
    for l in range(depth):
        x = _layer(x, rel_bias, attn_norm[l], w_in[l], q_norm[l], k_norm_cmp[l], k_norm_slc[l], k_norm_win[l],
                   cmp_pe_k[l], cmp_w1_k[l], cmp_b1_k[l], cmp_w2_k[l], cmp_pe_v[l], cmp_w1_v[l], cmp_b1_v[l],
                   cmp_w2_v[l], sgu_norm[l], sgu_w[l], sgu_b[l], w_out[l], ffn_norm[l], w_up[l], conv_w[l],
                   conv_b[l], w_down[l])
    return x
```

```python
import functools
import math

import numpy as np
import jax
import jax.numpy as jnp
from jax import lax
from jax.experimental import pallas as pl
from jax.experimental.pallas import tpu as pltpu

F32 = jnp.float32
BF16 = jnp.bfloat16

D_MODEL = 1024
D_HEAD = 64
NSA_HEADS = 8
NSA_KV_HEADS = 2
NSA_GROUP = NSA_HEADS // NSA_KV_HEADS
D_NSA = NSA_HEADS * D_HEAD
D_KV = NSA_KV_HEADS * D_HEAD
N_BRANCH = 3
N_GATES = NSA_HEADS * N_BRANCH
CMP_BLOCK = 32
CMP_STRIDE = 16
CMP_HIDDEN = 256
SEL_BLOCK = 64
SEL_TOPK = 16
WINDOW = 512
GMLP_GROUPS = 8
GMLP_GROUP_DIM = 64
D_GMLP = GMLP_GROUPS * GMLP_GROUP_DIM
CHUNK = 128
D_MIX = D_NSA + D_GMLP
REL_BUCKETS = 32
REL_MAX_DIST = 128
D_FF = 2816
EPS = 1e-6
NEG = -1e30
FORCED_SCORE = 1e4
LOG2E = 1.4426950408889634

LANES = 128
TQ = 256
SUB = TQ // SEL_BLOCK
ROWS = NSA_GROUP * TQ
KT = 256
KS = 2 * KT
RB = 64
BLOCKS_PER_TILE = KT // SEL_BLOCK
WIN_BLOCKS = WINDOW // SEL_BLOCK
KPAD = 1280
N_TOEPLITZ = 3
MAX_BLOCKS = 64
KAUG = 256
VAUG = 128
CBAND = 28
FFN_CHUNK = 256
FFN_ROWS = 64
VMEM_LIMIT = 56 * 1024 * 1024

_NT = (((1,), (1,)), ((), ()))


def _dot(a, b):
    return jnp.dot(a, b, preferred_element_type=F32)


def _dot_nt(a, b):
    return lax.dot_general(a, b, _NT, preferred_element_type=F32)


def _halves(dot, a, b):
    h = a.shape[0] // 2
    return jnp.concatenate([dot(a[:h], b), dot(a[h:], b)], axis=0)


def _split_bf16(x):
    hi = x.astype(BF16)
    lo = (x - hi.astype(F32)).astype(BF16)
    return hi, lo


def _gelu_tanh(x):
    return 0.5 * x * (1.0 + jnp.tanh(0.7978845608028654 * (x + 0.044715 * (x * x * x))))


def _sigmoid(x):
    return 0.5 * (1.0 + jnp.tanh(0.5 * x))


def _group_mean_sq(t, ones_blockdiag, width):
    t2 = t * t
    hi, lo = _split_bf16(t2)
    return (_dot(hi, ones_blockdiag) + _dot(lo, ones_blockdiag)) * (1.0 / width)


def _proj_kernel(x_ref, an_ref, wq_ref, wkv_ref, wg_ref, wuv_ref, qn_ref, ksn_ref, kwn_ref,
                 sgun_ref, sguw_ref, sgub_ref, bd_ref,
                 q_out, kc_out, vc_out, ks_out, vs_out, kw_out, vw_out, g_out, sgu_out):
    tm = x_ref.shape[1]
    x = x_ref[0]
    ms = jnp.mean(x * x, axis=-1, keepdims=True)
    h = (x * lax.rsqrt(ms + EPS) * an_ref[...]).astype(BF16)

    bd = bd_ref[...]
    q = _dot(h, wq_ref[...])
    qn = q * lax.rsqrt(_group_mean_sq(q, bd, D_HEAD) + EPS) * qn_ref[...]
    for hh in range(NSA_HEADS):
        q_out[0, hh] = qn[:, hh * D_HEAD:(hh + 1) * D_HEAD].astype(BF16)

    kv = _dot(h, wkv_ref[...])
    bd_kv = bd[:D_KV, :D_KV]
    kc = kv[:, 0 * D_KV:1 * D_KV]
    vc = kv[:, 1 * D_KV:2 * D_KV]
    ks = kv[:, 2 * D_KV:3 * D_KV]
    vs = kv[:, 3 * D_KV:4 * D_KV]
    kw = kv[:, 4 * D_KV:5 * D_KV]
    vw = kv[:, 5 * D_KV:6 * D_KV]
    ks = ks * lax.rsqrt(_group_mean_sq(ks, bd_kv, D_HEAD) + EPS) * ksn_ref[...]
    kw = kw * lax.rsqrt(_group_mean_sq(kw, bd_kv, D_HEAD) + EPS) * kwn_ref[...]
    for t, o_ref in ((kc, kc_out), (vc, vc_out), (ks, ks_out), (vs, vs_out), (kw, kw_out), (vw, vw_out)):
        for g in range(NSA_KV_HEADS):
            o_ref[0, g] = t[:, g * D_HEAD:(g + 1) * D_HEAD].astype(BF16)

    gates = _sigmoid(_dot(h, wg_ref[...]))
    for g in range(NSA_KV_HEADS):
        g_out[0, g] = gates[:, g * LANES:(g + 1) * LANES]

    uv = _gelu_tanh(_dot(h, wuv_ref[...]))
    u = uv[:, :D_GMLP]
    v = uv[:, D_GMLP:]
    vms = jnp.mean(v * v, axis=-1, keepdims=True)
    vb = (v * lax.rsqrt(vms + EPS) * sgun_ref[...]).astype(BF16)

    row = lax.broadcasted_iota(jnp.int32, (CHUNK, CHUNK), 0)
    col = lax.broadcasted_iota(jnp.int32, (CHUNK, CHUNK), 1)
    tril = col <= row
    w_tril = [jnp.where(tril, sguw_ref[g], 0.0).astype(BF16) for g in range(GMLP_GROUPS)]
    w_pair = [jnp.concatenate([w_tril[2 * p], w_tril[2 * p + 1]], axis=1) for p in range(D_GMLP // LANES)]
    first_half = lax.broadcasted_iota(jnp.int32, (CHUNK, LANES), 1) < GMLP_GROUP_DIM
    zero = jnp.zeros((CHUNK, LANES), BF16)
    for c in range(tm // CHUNK):
        rows = slice(c * CHUNK, (c + 1) * CHUNK)
        zs = []
        for p in range(D_GMLP // LANES):
            blk = vb[rows, p * LANES:(p + 1) * LANES]
            rhs = jnp.concatenate([jnp.where(first_half, blk, zero), jnp.where(first_half, zero, blk)], axis=0)
            zs.append(_dot(w_pair[p], rhs))
        z = jnp.concatenate(zs, axis=1) + sgub_ref[...]
        sgu_out[0, rows, :] = (u[rows, :] * z).astype(BF16)


def _proj_call(x, attn_norm, w_q, w_kv, w_g, w_uv, qn, ksn, kwn, sgu_norm, sgu_w, sgu_bt, bd, tm):
    B, T, _ = x.shape
    const2 = lambda b, i: (0, 0)
    const3 = lambda b, i: (0, 0, 0)
    head_spec = lambda nh: pl.BlockSpec((1, nh, tm, D_HEAD), lambda b, i: (b, 0, i, 0))
    kv_shape = jax.ShapeDtypeStruct((B, NSA_KV_HEADS, T, D_HEAD), BF16)
    return pl.pallas_call(
        _proj_kernel,
        grid=(B, T // tm),
        in_specs=[
            pl.BlockSpec((1, tm, D_MODEL), lambda b, i: (b, i, 0)),
            pl.BlockSpec((1, D_MODEL), const2),
            pl.BlockSpec(w_q.shape, const2),
            pl.BlockSpec(w_kv.shape, const2),
            pl.BlockSpec(w_g.shape, const2),
            pl.BlockSpec(w_uv.shape, const2),
            pl.BlockSpec((1, D_NSA), const2),
            pl.BlockSpec((1, D_KV), const2),
            pl.BlockSpec((1, D_KV), const2),
            pl.BlockSpec((1, D_GMLP), const2),
            pl.BlockSpec(sgu_w.shape, const3),
            pl.BlockSpec(sgu_bt.shape, const2),
            pl.BlockSpec(bd.shape, const2),
        ],
        out_specs=[
            head_spec(NSA_HEADS),
            head_spec(NSA_KV_HEADS), head_spec(NSA_KV_HEADS), head_spec(NSA_KV_HEADS),
            head_spec(NSA_KV_HEADS), head_spec(NSA_KV_HEADS), head_spec(NSA_KV_HEADS),
            pl.BlockSpec((1, NSA_KV_HEADS, tm, LANES), lambda b, i: (b, 0, i, 0)),
            pl.BlockSpec((1, tm, D_GMLP), lambda b, i: (b, i, 0)),
        ],
        out_shape=[
            jax.ShapeDtypeStruct((B, NSA_HEADS, T, D_HEAD), BF16),
            kv_shape, kv_shape, kv_shape, kv_shape, kv_shape, kv_shape,
            jax.ShapeDtypeStruct((B, NSA_KV_HEADS, T, LANES), F32),
            jax.ShapeDtypeStruct((B, T, D_GMLP), BF16),
        ],
        compiler_params=pltpu.CompilerParams(
            dimension_semantics=("arbitrary", "arbitrary"), vmem_limit_bytes=VMEM_LIMIT),
        name="proj",
    )(x, attn_norm, w_q, w_kv, w_g, w_uv, qn, ksn, kwn, sgu_norm, sgu_w, sgu_bt, bd)


def _compress_one(tok_ref, pe_ref, w1_ref, b1_ref, w2_ref):
    half = CMP_STRIDE * D_HEAD
    tok = tok_ref[0, 0].astype(F32)
    top = (tok + pe_ref[0:1, :]).astype(BF16)
    bot = (tok + pe_ref[1:2, :]).astype(BF16)
    a = _dot(top, w1_ref[:half, :])
    b = _dot(bot, w1_ref[half:, :])
    ncp = a.shape[0]
    pre = a + pltpu.roll(b, ncp - 1, 0) + b1_ref[...]
    hid = _gelu_tanh(pre).astype(BF16)
    return _dot(hid, w2_ref[...])


def _compress_kernel(kc_ref, vc_ref, pek_ref, w1k_ref, b1k_ref, w2k_ref, kn_ref,
                     pev_ref, w1v_ref, b1v_ref, w2v_ref, k_out, v_out):
    k = _compress_one(kc_ref, pek_ref, w1k_ref, b1k_ref, w2k_ref)
    kms = jnp.mean(k * k, axis=-1, keepdims=True)
    k_out[0, 0] = (k * lax.rsqrt(kms + EPS) * kn_ref[...]).astype(BF16)
    v_out[0, 0] = _compress_one(vc_ref, pev_ref, w1v_ref, b1v_ref, w2v_ref).astype(BF16)


def _compress_call(kc, vc, pek, w1k, b1k, w2k, kn, pev, w1v, b1v, w2v):
    B, G, ncp, width = kc.shape
    const2 = lambda b, g: (0, 0)
    tok_spec = pl.BlockSpec((1, 1, ncp, width), lambda b, g: (b, g, 0, 0))
    out_spec = pl.BlockSpec((1, 1, ncp, D_HEAD), lambda b, g: (b, g, 0, 0))
    full = lambda a: pl.BlockSpec(a.shape, const2)
    out_shape = jax.ShapeDtypeStruct((B, G, ncp, D_HEAD), BF16)
    return pl.pallas_call(
        _compress_kernel,
        grid=(B, G),
        in_specs=[tok_spec, tok_spec, full(pek), full(w1k), full(b1k), full(w2k), full(kn),
                  full(pev), full(w1v), full(b1v), full(w2v)],
        out_specs=[out_spec, out_spec],
        out_shape=[out_shape, out_shape],
        compiler_params=pltpu.CompilerParams(
            dimension_semantics=("arbitrary", "arbitrary"), vmem_limit_bytes=VMEM_LIMIT),
        name="compress",
    )(kc, vc, pek, w1k, b1k, w2k, kn, pev, w1v, b1v, w2v)


def _rel_bucket_np(dist):
    max_exact = REL_BUCKETS // 2
    d = np.maximum(dist, 1).astype(np.float32)
    log_b = max_exact + (np.log(d / np.float32(max_exact)) / np.float32(math.log(REL_MAX_DIST / max_exact))
                         * np.float32(REL_BUCKETS - max_exact)).astype(np.int32)
    log_b = np.clip(log_b, max_exact, REL_BUCKETS - 1)
    return np.where(dist < max_exact, np.maximum(dist, 0), log_b)


def _bias_tables(rel_bias):
    rb = rel_bias.astype(F32) * LOG2E
    qq = np.arange(SEL_BLOCK)[:, None]
    kp = np.arange(SEL_BLOCK)[None, :]
    tile_d = [m * SEL_BLOCK + qq - kp for m in range(N_TOEPLITZ)]
    s_ = np.arange(SUB)[:, None, None]
    e_ = np.arange(CBAND)[None, None, :]
    dist_c = qq[None, :, :] + CMP_STRIDE * (e_ - 4 * (SUB - 1) + 4 * s_) - (3 * CMP_STRIDE + CMP_BLOCK - 1)
    all_d = np.concatenate([d.reshape(-1) for d in tile_d] + [dist_c.reshape(-1)])
    onehot = np.eye(REL_BUCKETS, dtype=np.float32)[_rel_bucket_np(all_d)]
    vals = jnp.dot(jnp.asarray(onehot), rb, precision=lax.Precision.HIGHEST)
    vals = jnp.where(jnp.asarray(all_d >= 0)[:, None], vals, NEG)
    n_t = N_TOEPLITZ * SEL_BLOCK * SEL_BLOCK
    toep = jnp.transpose(vals[:n_t].reshape(N_TOEPLITZ, SEL_BLOCK, SEL_BLOCK, NSA_HEADS), (0, 3, 1, 2))
    band = jnp.transpose(vals[n_t:].reshape(SUB, SEL_BLOCK, CBAND, NSA_HEADS), (3, 0, 1, 2))
    far = rb[REL_BUCKETS - 1]
    far_tile = jnp.broadcast_to(far[:, None, None], (NSA_HEADS, SEL_BLOCK, SEL_BLOCK))
    neg_tile = jnp.full((NSA_HEADS, SEL_BLOCK, SEL_BLOCK), NEG, F32)
    edge_tile = jnp.where(jnp.asarray(kp > qq)[None], far_tile, NEG)

    def tile(m, windowed):
        if m < 0 or (windowed and m > WIN_BLOCKS):
            return neg_tile
        if windowed and m == WIN_BLOCKS:
            return edge_tile
        return toep[m] if m < N_TOEPLITZ else far_tile

    def table(first_m, n_blocks, windowed):
        rows = [jnp.concatenate([tile(first_m + s - c, windowed) for c in range(n_blocks)], axis=2)
                for s in range(SUB)]
        t = jnp.concatenate(rows, axis=1)
        return t.reshape(NSA_KV_HEADS, ROWS, n_blocks * SEL_BLOCK)

    far_rows = jnp.repeat(far, TQ).reshape(NSA_KV_HEADS, ROWS, 1)
    near_t = table(BLOCKS_PER_TILE, 2 * BLOCKS_PER_TILE, False) - far_rows
    win_t = table(WIN_BLOCKS, WIN_BLOCKS + BLOCKS_PER_TILE, True) - far_rows
    band_t = band.reshape(NSA_KV_HEADS, ROWS, CBAND)
    band_hi = band_t.astype(BF16)
    band_lo = (band_t - band_hi.astype(F32)).astype(BF16)
    far_hi = far_rows.astype(BF16)
    far_lo = (far_rows - far_hi.astype(F32)).astype(BF16)
    q_half = jnp.concatenate([jnp.zeros((NSA_KV_HEADS, ROWS, D_HEAD), BF16),
                              jnp.full((NSA_KV_HEADS, ROWS, 1), NEG, BF16), far_hi, far_lo,
                              jnp.zeros((NSA_KV_HEADS, ROWS, LANES - D_HEAD - 3), BF16)], axis=2)
    neg_col = jnp.full((NSA_KV_HEADS, ROWS, 1), NEG, BF16)
    zeros = jnp.zeros((NSA_KV_HEADS, ROWS, D_HEAD - 2 * CBAND - 3), BF16)
    q_lanes = jnp.zeros((NSA_KV_HEADS, ROWS, D_HEAD), BF16)
    cmp_lhs = jnp.concatenate([q_lanes, band_hi, band_lo, far_hi, far_lo, neg_col, zeros], axis=2)
    return near_t, win_t, q_half, cmp_lhs


def _row_max(lane_tiles):
    m = jnp.max(functools.reduce(jnp.maximum, lane_tiles), axis=1, keepdims=True)
    return jnp.broadcast_to(m, lane_tiles[0].shape)


def _nsa_kernel(q_ref, kc_ref, vc_ref, ks_ref, vs_ref, kw_ref, vw_ref, g_ref,
                near_ref, win_ref, qhalf_ref, cmpl_ref, ovt_ref, eye_ref,
                o_ref, ksa, vsa, kwa, vwa, kca, vca, sc_scr, lhs_scr, lhsc_scr, m_scr, al_scr, inv_scr, acc_scr, sa_scr, sb_scr, pa_scr, pb_scr,
                sw_scr, pw_scr, ps_scr, out_scr, cnt_scr):
    qt = pl.program_id(2)
    first = qt * SUB
    T = ks_ref.shape[2]
    ncp = kc_ref.shape[2]

    @pl.when(qt == 0)
    def _init():
        n_chunks = ksa.shape[0] // KT
        r_io = lax.broadcasted_iota(jnp.int32, (KT, KAUG), 0)
        l_io = lax.broadcasted_iota(jnp.int32, (KT, KAUG), 1)
        lv = lax.broadcasted_iota(jnp.int32, (KT, VAUG), 1)
        vpat = jnp.where(lv == D_HEAD, 1.0, 0.0).astype(BF16)

        def fill(c, carry):
            rows = pl.ds(pl.multiple_of(c * KT, KT), KT)
            key = r_io + (c * KT - KPAD)
            blk = key // SEL_BLOCK
            real = (key >= 0) & (key < T)
            hot = real & (((l_io < D_HEAD) & (l_io == blk))
                          | ((l_io >= D_HEAD) & (l_io < 2 * D_HEAD) & (l_io - D_HEAD == blk)))
            hot = hot | ((~real) & (l_io == 3 * D_HEAD)) | (real & (l_io > 3 * D_HEAD) & (l_io <= 3 * D_HEAD + 2))
            pat = jnp.where(hot, 1.0, 0.0).astype(BF16)
            ksa[rows, :] = pat
            kwa[rows, :] = pat
            vsa[rows, :] = vpat
            vwa[rows, :] = vpat
            return carry

        lax.fori_loop(0, n_chunks, fill, 0)
        ksa[KPAD:KPAD + T, 2 * D_HEAD:3 * D_HEAD] = ks_ref[0, 0]
        kwa[KPAD:KPAD + T, 2 * D_HEAD:3 * D_HEAD] = kw_ref[0, 0]
        vsa[KPAD:KPAD + T, 0:D_HEAD] = vs_ref[0, 0]
        vwa[KPAD:KPAD + T, 0:D_HEAD] = vw_ref[0, 0]
        lhs_scr[:, 0:LANES] = jnp.zeros((ROWS, LANES), BF16)
        lc = lax.broadcasted_iota(jnp.int32, (ncp, VAUG), 1)
        kca[...] = jnp.zeros((ncp, VAUG), BF16)
        kca[:, 0:D_HEAD] = kc_ref[0, 0]
        vca[...] = jnp.where(lc == D_HEAD, 1.0, 0.0).astype(BF16)
        vca[:, 0:D_HEAD] = vc_ref[0, 0]

    q4 = q_ref[0].reshape(ROWS, D_HEAD)
    gates = g_ref[0, 0]
    row_blocks = [slice(r, r + RB) for r in range(0, ROWS, RB)]

    gate_rows = jnp.concatenate(
        [gates if hh == 0 else pltpu.roll(gates, LANES - hh * N_BRANCH, 1) for hh in range(NSA_GROUP)], axis=0)

    def gate_col(branch):
        return gate_rows[:, branch:branch + 1]

    lane_c = lax.broadcasted_iota(jnp.int32, (ncp, LANES), 1)
    n_c = lax.broadcasted_iota(jnp.int32, (ncp, LANES), 0)
    e = lane_c - D_HEAD
    base = 4 * first + (4 * SUB - 1)
    ind = (((e >= 0) & (e < CBAND) & (n_c == base - e))
           | ((e >= CBAND) & (e < 2 * CBAND) & (n_c == base - (e - CBAND)))
           | (((e == 2 * CBAND) | (e == 2 * CBAND + 1)) & (n_c < base - (CBAND - 1)))
           | ((e == 2 * CBAND + 2) & (n_c > base)))
    rhs_c = jnp.where(lane_c < D_HEAD, kca[...], jnp.where(ind, 1.0, 0.0).astype(BF16))
    lhsc_scr[...] = cmpl_ref[0]
    lhsc_scr[:, 0:D_HEAD] = q4
    sa_scr[:, 0:ncp] = _halves(_dot_nt, lhsc_scr[...], rhs_c)

    lhs_scr[:, LANES:2 * LANES] = qhalf_ref[0]
    lhs_scr[:, LANES:LANES + D_HEAD] = q4
    lhs_win = jnp.concatenate([jnp.zeros((ROWS, LANES), BF16), lhs_scr[:, LANES:2 * LANES]], axis=1)
    win_start = pl.multiple_of(KPAD + (first - WIN_BLOCKS) * SEL_BLOCK, KT)
    sw_scr[...] = _halves(_dot_nt, lhs_win, kwa[pl.ds(win_start, KS + KT), :])

    for i, rows in enumerate(row_blocks):
        s = [sa_scr[rows, c * LANES:(c + 1) * LANES] for c in range(ncp // LANES)]
        m = _row_max(s)
        p = [jnp.where(t > 0.5 * NEG, jnp.exp2(t - m), 0.0) for t in s]
        l = jnp.sum(functools.reduce(jnp.add, p), axis=1, keepdims=True)
        inv = jnp.where(l > 0.0, 1.0 / l, 0.0)
        inv_scr[rows] = inv
        inv_b = jnp.broadcast_to(inv, (RB, LANES))
        tok = slice((i % (TQ // RB)) * RB, (i % (TQ // RB) + 1) * RB)
        for c in range(ncp // LANES):
            lanes = slice(c * LANES, (c + 1) * LANES)
            pb_scr[rows, lanes] = p[c].astype(BF16)
            if i < TQ // RB:
                ps_scr[tok, lanes] = p[c] * inv_b
            else:
                ps_scr[tok, lanes] += p[c] * inv_b
    acc_c = _halves(_dot, pb_scr[:, 0:ncp], vca[...])
    out_scr[...] = acc_c[:, :D_HEAD] * (gate_col(0) * inv_scr[...])
    ps_hi, ps_lo = _split_bf16(ps_scr[...])
    ovt = ovt_ref[...]
    imp_t = _dot_nt(ovt, ps_hi) + _dot_nt(ovt, ps_lo)

    j_t = lax.broadcasted_iota(jnp.int32, (MAX_BLOCKS, TQ), 0)
    cur = first + lax.broadcasted_iota(jnp.int32, (MAX_BLOCKS, TQ), 1) // SEL_BLOCK
    forced = (j_t == 0) | (j_t == cur) | (j_t == cur - 1)
    score_t = jnp.where(forced, FORCED_SCORE, jnp.where(j_t <= cur, imp_t, NEG))
    sc_scr[...] = score_t

    cnt_scr[...] = jnp.zeros((MAX_BLOCKS, TQ), jnp.int32)
    sub_io = lax.broadcasted_iota(jnp.int32, (8, TQ), 0)
    n_groups = MAX_BLOCKS // 8
    for grp in range(n_groups):
        @pl.when(grp * 8 < first + SUB)
        def _count_group():
            targets = [score_t[8 * jg:8 * jg + 8] for jg in range(n_groups)]
            acc = [jnp.zeros((8, TQ), jnp.int32) for _ in range(n_groups)]
            for k in range(8):
                row_i = jnp.broadcast_to(sc_scr[grp * 8 + k:grp * 8 + k + 1, :], (8, TQ))
                for jg in range(n_groups):
                    if jg > grp:
                        beats = row_i >= targets[jg]
                    elif jg < grp:
                        beats = row_i > targets[jg]
                    else:
                        beats = (row_i > targets[jg]) | ((row_i == targets[jg]) & (sub_io > k))
                    acc[jg] = acc[jg] + beats.astype(jnp.int32)
            cnt_scr[...] += jnp.concatenate(acc, axis=0)

    cnt = cnt_scr[...]
    mask_t = jnp.where((cnt < SEL_TOPK) & (j_t <= cur), 0.0, NEG).astype(BF16)
    mask = _dot_nt(eye_ref[...], mask_t).astype(BF16)

    for hh in range(NSA_GROUP):
        lhs_scr[hh * TQ:(hh + 1) * TQ, 0:MAX_BLOCKS] = mask
    m_scr[...] = jnp.full((ROWS, LANES), NEG, F32)
    acc_scr[...] = jnp.zeros((ROWS, VAUG), F32)

    def tile_start(k):
        return pl.multiple_of(KPAD + (first - BLOCKS_PER_TILE) * SEL_BLOCK - k * KS, KT)

    def scores(k, dst):
        dst[...] = _halves(_dot_nt, lhs_scr[...], ksa[pl.ds(tile_start(k), KS), :])

    def consume(k, src, p_scr, with_table):
        n_lt = KS // LANES
        for rows in row_blocks:
            s = [src[rows, c * LANES:(c + 1) * LANES] for c in range(n_lt)]
            if with_table:
                s = [s[c] + near_ref[0, rows, c * LANES:(c + 1) * LANES] for c in range(n_lt)]
            m_prev = m_scr[rows]
            m = jnp.maximum(m_prev, _row_max(s))
            al_scr[rows] = jnp.exp2(m_prev - m)
            m_scr[rows] = m
            for c in range(n_lt):
                p_scr[rows, c * LANES:(c + 1) * LANES] = jnp.exp2((s[c] - m).astype(BF16))
        acc_scr[...] = acc_scr[...] * al_scr[...] + _dot(p_scr[:, 0:KS], vsa[pl.ds(tile_start(k), KS), :])

    scores(0, sa_scr)

    for rows in row_blocks:
        s = [sw_scr[rows, c * LANES:(c + 1) * LANES] + win_ref[0, rows, c * LANES:(c + 1) * LANES]
             for c in range((KS + KT) // LANES)]
        m = _row_max(s)
        for c in range((KS + KT) // LANES):
            pw_scr[rows, c * LANES:(c + 1) * LANES] = jnp.exp2((s[c] - m).astype(BF16))
    acc_w = _halves(_dot, pw_scr[...], vwa[pl.ds(win_start, KS + KT), :])
    out_scr[...] += acc_w[:, :D_HEAD] * (gate_col(2) / acc_w[:, D_HEAD:D_HEAD + 1])

    def pair(with_table, u, carry):
        scores(2 * u + 1, sb_scr)
        consume(2 * u, sa_scr, pa_scr, with_table)
        scores(2 * u + 2, sa_scr)
        consume(2 * u + 1, sb_scr, pb_scr, False)
        return carry

    n_tiles = (qt + 2) // 2
    pair(True, 0, 0)
    lax.fori_loop(1, jnp.maximum(n_tiles // 2, 1), functools.partial(pair, False), 0)

    @pl.when((n_tiles % 2 == 1) & (n_tiles >= 3))
    def _odd_tile():
        consume(n_tiles - 1, sa_scr, pa_scr, False)

    acc_s = acc_scr[...]

    o = out_scr[...] + acc_s[:, :D_HEAD] * (gate_col(1) / acc_s[:, D_HEAD:D_HEAD + 1])
    for hh in range(NSA_GROUP):
        o_ref[0, :, hh * D_HEAD:(hh + 1) * D_HEAD] = o[hh * TQ:(hh + 1) * TQ].astype(BF16)


def _nsa_call(q, kcmp, vcmp, ks, vs, kw, vw, gates, near_t, win_t, q_half, cmp_lhs, ovt, eye):
    B, _, T, _ = q.shape
    G = NSA_KV_HEADS
    ncp = kcmp.shape[2]
    rows_kv = KPAD + T
    per_bg =lambda a: pl.BlockSpec((1, 1) + a.shape[2:], lambda b, g, i: (b, g, 0, 0))
    per_g = lambda a: pl.BlockSpec((1,) + a.shape[1:], lambda b, g, i: (g, 0, 0))
    const2 = lambda a: pl.BlockSpec(a.shape, lambda b, g, i: (0, 0))
    return pl.pallas_call(
        _nsa_kernel,
        grid=(B, G, T // TQ),
        in_specs=[
            pl.BlockSpec((1, NSA_GROUP, TQ, D_HEAD), lambda b, g, i: (b, g, i, 0)),
            per_bg(kcmp), per_bg(vcmp), per_bg(ks), per_bg(vs), per_bg(kw), per_bg(vw),
            pl.BlockSpec((1, 1, TQ, LANES), lambda b, g, i: (b, g, i, 0)),
            per_g(near_t), per_g(win_t), per_g(q_half), per_g(cmp_lhs), const2(ovt), const2(eye),
        ],
        out_specs=pl.BlockSpec((1, TQ, NSA_GROUP * D_HEAD), lambda b, g, i: (b, i, g)),
        out_shape=jax.ShapeDtypeStruct((B, T, D_NSA), BF16),
        scratch_shapes=[
            pltpu.VMEM((rows_kv, KAUG), BF16), pltpu.VMEM((rows_kv, VAUG), BF16),
            pltpu.VMEM((rows_kv, KAUG), BF16), pltpu.VMEM((rows_kv, VAUG), BF16),
            pltpu.VMEM((ncp, VAUG), BF16), pltpu.VMEM((ncp, VAUG), BF16),
            pltpu.VMEM((MAX_BLOCKS, TQ), F32),
            pltpu.VMEM((ROWS, KAUG), BF16), pltpu.VMEM((ROWS, LANES), BF16),
            pltpu.VMEM((ROWS, LANES), F32), pltpu.VMEM((ROWS, LANES), F32), pltpu.VMEM((ROWS, 1), F32),
            pltpu.VMEM((ROWS, VAUG), F32),
            pltpu.VMEM((ROWS, KS), F32), pltpu.VMEM((ROWS, KS), F32),
            pltpu.VMEM((ROWS, KS), BF16), pltpu.VMEM((ROWS, KS), BF16),
            pltpu.VMEM((ROWS, KS + KT), F32), pltpu.VMEM((ROWS, KS + KT), BF16),
            pltpu.VMEM((TQ, ncp), F32), pltpu.VMEM((ROWS, D_HEAD), F32), pltpu.VMEM((MAX_BLOCKS, TQ), jnp.int32),
        ],
        compiler_params=pltpu.CompilerParams(
            dimension_semantics=("arbitrary", "arbitrary", "arbitrary"), vmem_limit_bytes=VMEM_LIMIT),
        name="nsa",
    )(q, kcmp, vcmp, ks, vs, kw, vw, gates, near_t, win_t, q_half, cmp_lhs, ovt, eye)


def _ffn_kernel(x_ref, on_ref, os_ref, wo_n_ref, wo_s_ref, fn_ref, wup_ref, cw_ref, cb_ref, wdn_ref, pm_ref,
                out_ref, prev_ref, perm_ref, h_ref, ua_scr, ub_scr, act_ref):
    tm = x_ref.shape[1]
    seg = tm // 8
    n_chunks = wup_ref.shape[1]

    @pl.when(pl.program_id(1) == 0)
    def _start_of_sequence():
        prev_ref[...] = jnp.zeros(prev_ref.shape, F32)

    x1 = x_ref[0] + _dot(on_ref[0], wo_n_ref[...]) + _dot(os_ref[0], wo_s_ref[...])
    ms = jnp.mean(x1 * x1, axis=-1, keepdims=True)
    hn = (x1 * lax.rsqrt(ms + EPS) * fn_ref[...]).astype(BF16)
    out_ref[0] = x1
    n_lt = x1.shape[1] // LANES
    h_ref[...] = _halves(_dot, pm_ref[...], hn).astype(BF16)

    def up(c, dst):
        h = h_ref[...]
        for half in range(2):
            dst[half] = _dot(h, wup_ref[half, c])

    first_sublane = lax.broadcasted_iota(jnp.int32, (8, FFN_CHUNK), 0) == 0

    def conv_gate(c, src):
        for r0 in range(0, tm, FFN_ROWS):
            ys = []
            for half in range(2):
                w = cw_ref[half, c]
                x0 = src[half, r0:r0 + FFN_ROWS]
                if r0 == 0:
                    wrap = [jnp.where(first_sublane, pltpu.roll(prev_ref[half, c, 8 * k:8 * k + 8], 1, 0),
                                      pltpu.roll(src[half, tm - 16 + 8 * k:tm - 8 + 8 * k], 1, 0)) for k in range(2)]
                    x1_ = jnp.concatenate([wrap[1], src[half, 0:FFN_ROWS - 8]], axis=0)
                    x2_ = jnp.concatenate([wrap[0], wrap[1], src[half, 0:FFN_ROWS - 16]], axis=0)
                else:
                    x1_ = src[half, r0 - 8:r0 + FFN_ROWS - 8]
                    x2_ = src[half, r0 - 16:r0 + FFN_ROWS - 16]
                ys.append(x0 * w[2:3] + x1_ * w[1:2] + x2_ * w[0:1] + cb_ref[half, c])
            a, g = ys
            col = pl.multiple_of(c * FFN_CHUNK, FFN_CHUNK)
            act_ref[r0:r0 + FFN_ROWS, pl.ds(col, FFN_CHUNK)] = ((g + g * jnp.tanh(g)) * a).astype(BF16)
        for half in range(2):
            prev_ref[half, c] = src[half, tm - 16:tm]

    up(0, ua_scr)

    def pair(u, carry):
        up(2 * u + 1, ub_scr)
        conv_gate(2 * u, ua_scr)
        up(2 * u + 2, ua_scr)
        conv_gate(2 * u + 1, ub_scr)
        return carry

    lax.fori_loop(0, (n_chunks - 1) // 2, pair, 0)
    conv_gate(n_chunks - 1, ua_scr)
    y = _halves(_dot, act_ref[...], wdn_ref[...])
    for c in range(n_lt):
        perm_ref[c] = y[:, c * LANES:(c + 1) * LANES]
    for sgm in range(8):
        natural = jnp.concatenate([perm_ref[c, pl.ds(sgm, seg, stride=8), :] for c in range(n_lt)], axis=1)
        out_ref[0, sgm * seg:(sgm + 1) * seg, :] += natural


def _ffn_call(x, o_nsa, o_sgu, wo_n, wo_s, ffn_norm, w_up, conv_w, conv_b, w_down, tm):
    B, T, _ = x.shape
    r = np.arange(tm)
    perm = jnp.asarray(((r % 8) * (tm // 8) + r // 8)[:, None] == r[None, :], BF16)
    n_chunks = w_up.shape[1]
    assert n_chunks % 2 == 1
    resident =lambda a: pl.BlockSpec(a.shape, lambda b, i: (0,) * a.ndim, pipeline_mode=pl.Buffered(1))
    tile = lambda w: pl.BlockSpec((1, tm, w), lambda b, i: (b, i, 0))
    return pl.pallas_call(
        _ffn_kernel,
        grid=(B, T // tm),
        in_specs=[tile(D_MODEL), tile(D_NSA), tile(D_GMLP), resident(wo_n), resident(wo_s),
                  resident(ffn_norm), resident(w_up), resident(conv_w), resident(conv_b), resident(w_down),
                  resident(perm)],
        out_specs=tile(D_MODEL),
        out_shape=jax.ShapeDtypeStruct((B, T, D_MODEL), F32),
        scratch_shapes=[pltpu.VMEM((2, n_chunks, 16, FFN_CHUNK), F32), pltpu.VMEM((D_MODEL // LANES, tm, LANES), F32),
                        pltpu.VMEM((tm, D_MODEL), BF16),
                        pltpu.VMEM((2, tm, FFN_CHUNK), F32), pltpu.VMEM((2, tm, FFN_CHUNK), F32),
                        pltpu.VMEM((tm, D_FF), BF16)],
        compiler_params=pltpu.CompilerParams(
            dimension_semantics=("arbitrary", "arbitrary"), vmem_limit_bytes=VMEM_LIMIT),
        name="ffn",
    )(x, o_nsa, o_sgu, wo_n, wo_s, ffn_norm, w_up, conv_w, conv_b, w_down, perm)


def _mixers(x, rel_bias, attn_norm, w_in, q_norm, k_norm_cmp, k_norm_slc, k_norm_win,
            cmp_pe_k, cmp_w1_k, cmp_b1_k, cmp_w2_k, cmp_pe_v, cmp_w1_v, cmp_b1_v, cmp_w2_v,
            sgu_norm, sgu_w, sgu_b):
    B, T, _ = x.shape
    assert T % 512 == 0 and T // SEL_BLOCK <= MAX_BLOCKS and (T // CMP_STRIDE) % LANES == 0
    ncp = T // CMP_STRIDE
    G, R = NSA_KV_HEADS, NSA_GROUP

    o_kv = D_NSA
    o_g = D_NSA + 6 * D_KV
    o_uv = o_g + N_GATES
    w_q = w_in[:, :o_kv].astype(BF16)
    w_kv = w_in[:, o_kv:o_g].astype(BF16)
    per_group = NSA_GROUP * N_BRANCH
    w_g = jnp.pad(w_in[:, o_g:o_uv].reshape(D_MODEL, G, per_group), ((0, 0), (0, 0), (0, LANES - per_group)))
    w_g = w_g.reshape(D_MODEL, G * LANES).astype(BF16)
    w_uv = w_in[:, o_uv:].astype(BF16)
    qn = (jnp.tile(q_norm, NSA_HEADS) * (D_HEAD ** -0.5 * LOG2E)).reshape(1, D_NSA)
    ksn = jnp.tile(k_norm_slc, G).reshape(1, D_KV)
    kwn = jnp.tile(k_norm_win, G).reshape(1, D_KV)
    sgu_bt = jnp.repeat(sgu_b.T, GMLP_GROUP_DIM, axis=1)
    grp = np.arange(D_NSA) // D_HEAD
    bd = jnp.asarray(grp[:, None] == grp[None, :], BF16)

    q, kc, vc, ks, vs, kw, vw, gates, o_sgu = _proj_call(
        x, attn_norm.reshape(1, D_MODEL), w_q, w_kv, w_g, w_uv, qn, ksn, kwn,
        sgu_norm.reshape(1, D_GMLP), sgu_w, sgu_bt, bd, tm=512)

    half = CMP_STRIDE * D_HEAD
    kcmp, vcmp = _compress_call(
        kc.reshape(B, G, ncp, half), vc.reshape(B, G, ncp, half),
        cmp_pe_k.reshape(2, half), cmp_w1_k.astype(BF16), cmp_b1_k.reshape(1, CMP_HIDDEN),
        cmp_w2_k.astype(BF16), k_norm_cmp.reshape(1, D_HEAD),
        cmp_pe_v.reshape(2, half), cmp_w1_v.astype(BF16), cmp_b1_v.reshape(1, CMP_HIDDEN),
        cmp_w2_v.astype(BF16))

    near_t, win_t, q_half, cmp_lhs = _bias_tables(rel_bias)
    jj = np.arange(MAX_BLOCKS)[:, None]
    nn = np.arange(ncp)[None, :]
    ovt = jnp.asarray((nn >= 4 * jj - 1) & (nn <= 4 * jj + 3), BF16)
    eye = jnp.asarray(np.eye(TQ), BF16)

    o_nsa = _nsa_call(q, kcmp, vcmp, ks, vs, kw, vw, gates, near_t, win_t, q_half, cmp_lhs, ovt, eye)
    return o_nsa, o_sgu


def _layer(x, rel_bias, attn_norm, w_in, q_norm, k_norm_cmp, k_norm_slc, k_norm_win,
           cmp_pe_k, cmp_w1_k, cmp_b1_k, cmp_w2_k, cmp_pe_v, cmp_w1_v, cmp_b1_v, cmp_w2_v,
           sgu_norm, sgu_w, sgu_b, w_out, ffn_norm, w_up, conv_w, conv_b, w_down):
    o_nsa, o_sgu = _mixers(x, rel_bias, attn_norm, w_in, q_norm, k_norm_cmp, k_norm_slc, k_norm_win,
                           cmp_pe_k, cmp_w1_k, cmp_b1_k, cmp_w2_k, cmp_pe_v, cmp_w1_v, cmp_b1_v, cmp_w2_v,
                           sgu_norm, sgu_w, sgu_b)
    wo = w_out.astype(BF16)
    n_chunks = D_FF // FFN_CHUNK
    wu = jnp.transpose(w_up.astype(BF16).reshape(D_MODEL, 2, n_chunks, FFN_CHUNK), (1, 2, 0, 3))
    half_gate = jnp.asarray([1.0, 0.5], F32).reshape(2, 1, 1, 1)
    cw = jnp.transpose(conv_w.reshape(conv_w.shape[0], 2, n_chunks, FFN_CHUNK), (1, 2, 0, 3)) * half_gate
    cb = conv_b.reshape(2, n_chunks, 1, FFN_CHUNK) * half_gate
    wd = w_down.astype(BF16)
    return _ffn_call(x, o_nsa, o_sgu, wo[:D_NSA], wo[D_NSA:], ffn_norm.reshape(1, D_MODEL), wu, cw, cb, wd, tm=512)


def kernel(x, rel_bias, attn_norm, w_in, q_norm, k_norm_cmp, k_norm_slc, k_norm_win,
           cmp_pe_k, cmp_w1_k, cmp_b1_k, cmp_w2_k, cmp_pe_v, cmp_w1_v, cmp_b1_v, cmp_w2_v,
           sgu_norm, sgu_w, sgu_b, w_out, ffn_norm, w_up, conv_w, conv_b, w_down):
    depth = attn_norm.shape[0]
    for l in range(depth):
        x = _layer(x, rel_bias, attn_norm[l], w_in[l], q_norm[l], k_norm_cmp[l], k_norm_slc[l], k_norm_win[l],
                   cmp_pe_k[l], cmp_w1_k[l], cmp_b1_k[l], cmp_w2_k[l], cmp_pe_v[l], cmp_w1_v[l], cmp_b1_v[l],
                   cmp_w2_v[l], sgu_norm[l], sgu_w[l], sgu_b[l], w_out[l], ffn_norm[l], w_up[l], conv_w[l],
                   conv_b[l], w_down[l])
    return x
```

```python
import functools
import math

import numpy as np
import jax
import jax.numpy as jnp
from jax import lax
from jax.experimental import pallas as pl
from jax.experimental.pallas import tpu as pltpu

F32 = jnp.float32
BF16 = jnp.bfloat16

D_MODEL = 1024
D_HEAD = 64
NSA_HEADS = 8
NSA_KV_HEADS = 2
NSA_GROUP = NSA_HEADS // NSA_KV_HEADS
D_NSA = NSA_HEADS * D_HEAD
D_KV = NSA_KV_HEADS * D_HEAD
N_BRANCH = 3
N_GATES = NSA_HEADS * N_BRANCH
CMP_BLOCK = 32
CMP_STRIDE = 16
CMP_HIDDEN = 256
SEL_BLOCK = 64
SEL_TOPK = 16
WINDOW = 512
GMLP_GROUPS = 8
GMLP_GROUP_DIM = 64
D_GMLP = GMLP_GROUPS * GMLP_GROUP_DIM
CHUNK = 128
D_MIX = D_NSA + D_GMLP
REL_BUCKETS = 32
REL_MAX_DIST = 128
D_FF = 2816
EPS = 1e-6
NEG = -1e30
FORCED_SCORE = 1e4
LOG2E = 1.4426950408889634

LANES = 128
TQ = 256
SUB = TQ // SEL_BLOCK
ROWS = NSA_GROUP * TQ
KT = 256
KS = 2 * KT
RB = 64
BLOCKS_PER_TILE = KT // SEL_BLOCK
WIN_BLOCKS = WINDOW // SEL_BLOCK
KPAD = 1280
N_TOEPLITZ = 3
MAX_BLOCKS = 64
KAUG = 256
VAUG = 128
CBAND = 28
FFN_CHUNK = 256
FFN_ROWS = 64
VMEM_LIMIT = 56 * 1024 * 1024

_NT = (((1,), (1,)), ((), ()))


def _dot(a, b):
    return jnp.dot(a, b, preferred_element_type=F32)


def _dot_nt(a, b):
    return lax.dot_general(a, b, _NT, preferred_element_type=F32)


def _halves(dot, a, b):
    h = a.shape[0] // 2
    return jnp.concatenate([dot(a[:h], b), dot(a[h:], b)], axis=0)


def _split_bf16(x):
    hi = x.astype(BF16)
    lo = (x - hi.astype(F32)).astype(BF16)
    return hi, lo


def _gelu_tanh(x):
    return 0.5 * x * (1.0 + jnp.tanh(0.7978845608028654 * (x + 0.044715 * (x * x * x))))


def _sigmoid(x):
    return 0.5 * (1.0 + jnp.tanh(0.5 * x))


def _group_mean_sq(t, ones_blockdiag, width):
    t2 = t * t
    hi, lo = _split_bf16(t2)
    return (_dot(hi, ones_blockdiag) + _dot(lo, ones_blockdiag)) * (1.0 / width)


def _proj_kernel(x_ref, an_ref, wq_ref, wkv_ref, wg_ref, wuv_ref, qn_ref, ksn_ref, kwn_ref,
                 sgun_ref, sguw_ref, sgub_ref, bd_ref,
                 q_out, kc_out, vc_out, ks_out, vs_out, kw_out, vw_out, g_out, sgu_out):
    tm = x_ref.shape[1]
    x = x_ref[0]
    ms = jnp.mean(x * x, axis=-1, keepdims=True)
    h = (x * lax.rsqrt(ms + EPS) * an_ref[...]).astype(BF16)

    bd = bd_ref[...]
    q = _halves(_dot, h, wq_ref[...])
    qn = q * lax.rsqrt(_group_mean_sq(q, bd, D_HEAD) + EPS) * qn_ref[...]
    for hh in range(NSA_HEADS):
        q_out[0, hh] = qn[:, hh * D_HEAD:(hh + 1) * D_HEAD].astype(BF16)

    kv = _halves(_dot, h, wkv_ref[...])
    bd_kv = bd[:D_KV, :D_KV]
    kc = kv[:, 0 * D_KV:1 * D_KV]
    vc = kv[:, 1 * D_KV:2 * D_KV]
    ks = kv[:, 2 * D_KV:3 * D_KV]
    vs = kv[:, 3 * D_KV:4 * D_KV]
    kw = kv[:, 4 * D_KV:5 * D_KV]
    vw = kv[:, 5 * D_KV:6 * D_KV]
    ks = ks * lax.rsqrt(_group_mean_sq(ks, bd_kv, D_HEAD) + EPS) * ksn_ref[...]
    kw = kw * lax.rsqrt(_group_mean_sq(kw, bd_kv, D_HEAD) + EPS) * kwn_ref[...]
    for t, o_ref in ((kc, kc_out), (vc, vc_out), (ks, ks_out), (vs, vs_out), (kw, kw_out), (vw, vw_out)):
        for g in range(NSA_KV_HEADS):
            o_ref[0, g] = t[:, g * D_HEAD:(g + 1) * D_HEAD].astype(BF16)

    gates = _sigmoid(_dot(h, wg_ref[...]))
    for g in range(NSA_KV_HEADS):
        g_out[0, g] = gates[:, g * LANES:(g + 1) * LANES]

    uv = _gelu_tanh(_halves(_dot, h, wuv_ref[...]))
    u = uv[:, :D_GMLP]
    v = uv[:, D_GMLP:]
    vms = jnp.mean(v * v, axis=-1, keepdims=True)
    vb = (v * lax.rsqrt(vms + EPS) * sgun_ref[...]).astype(BF16)

    row = lax.broadcasted_iota(jnp.int32, (CHUNK, CHUNK), 0)
    col = lax.broadcasted_iota(jnp.int32, (CHUNK, CHUNK), 1)
    tril = col <= row
    w_tril = [jnp.where(tril, sguw_ref[g], 0.0).astype(BF16) for g in range(GMLP_GROUPS)]
    w_pair = [jnp.concatenate([w_tril[2 * p], w_tril[2 * p + 1]], axis=1) for p in range(D_GMLP // LANES)]
    first_half = lax.broadcasted_iota(jnp.int32, (CHUNK, LANES), 1) < GMLP_GROUP_DIM
    zero = jnp.zeros((CHUNK, LANES), BF16)
    for c in range(tm // CHUNK):
        rows = slice(c * CHUNK, (c + 1) * CHUNK)
        zs = []
        for p in range(D_GMLP // LANES):
            blk = vb[rows, p * LANES:(p + 1) * LANES]
            rhs = jnp.concatenate([jnp.where(first_half, blk, zero), jnp.where(first_half, zero, blk)], axis=0)
            zs.append(_dot(w_pair[p], rhs))
        z = jnp.concatenate(zs, axis=1) + sgub_ref[...]
        sgu_out[0, rows, :] = (u[rows, :] * z).astype(BF16)


def _proj_call(x, attn_norm, w_q, w_kv, w_g, w_uv, qn, ksn, kwn, sgu_norm, sgu_w, sgu_bt, bd, tm):
    B, T, _ = x.shape
    const2 = lambda b, i: (0, 0)
    const3 = lambda b, i: (0, 0, 0)
    head_spec = lambda nh: pl.BlockSpec((1, nh, tm, D_HEAD), lambda b, i: (b, 0, i, 0))
    kv_shape = jax.ShapeDtypeStruct((B, NSA_KV_HEADS, T, D_HEAD), BF16)
    return pl.pallas_call(
        _proj_kernel,
        grid=(B, T // tm),
        in_specs=[
            pl.BlockSpec((1, tm, D_MODEL), lambda b, i: (b, i, 0)),
            pl.BlockSpec((1, D_MODEL), const2),
            pl.BlockSpec(w_q.shape, const2),
            pl.BlockSpec(w_kv.shape, const2),
            pl.BlockSpec(w_g.shape, const2),
            pl.BlockSpec(w_uv.shape, const2),
            pl.BlockSpec((1, D_NSA), const2),
            pl.BlockSpec((1, D_KV), const2),
            pl.BlockSpec((1, D_KV), const2),
            pl.BlockSpec((1, D_GMLP), const2),
            pl.BlockSpec(sgu_w.shape, const3),
            pl.BlockSpec(sgu_bt.shape, const2),
            pl.BlockSpec(bd.shape, const2),
        ],
        out_specs=[
            head_spec(NSA_HEADS),
            head_spec(NSA_KV_HEADS), head_spec(NSA_KV_HEADS), head_spec(NSA_KV_HEADS),
            head_spec(NSA_KV_HEADS), head_spec(NSA_KV_HEADS), head_spec(NSA_KV_HEADS),
            pl.BlockSpec((1, NSA_KV_HEADS, tm, LANES), lambda b, i: (b, 0, i, 0)),
            pl.BlockSpec((1, tm, D_GMLP), lambda b, i: (b, i, 0)),
        ],
        out_shape=[
            jax.ShapeDtypeStruct((B, NSA_HEADS, T, D_HEAD), BF16),
            kv_shape, kv_shape, kv_shape, kv_shape, kv_shape, kv_shape,
            jax.ShapeDtypeStruct((B, NSA_KV_HEADS, T, LANES), F32),
            jax.ShapeDtypeStruct((B, T, D_GMLP), BF16),
        ],
        compiler_params=pltpu.CompilerParams(
            dimension_semantics=("arbitrary", "arbitrary"), vmem_limit_bytes=VMEM_LIMIT),
        name="proj",
    )(x, attn_norm, w_q, w_kv, w_g, w_uv, qn, ksn, kwn, sgu_norm, sgu_w, sgu_bt, bd)


def _compress_one(tok_ref, pe_ref, w1_ref, b1_ref, w2_ref):
    half = CMP_STRIDE * D_HEAD
    tok = tok_ref[0, 0].astype(F32)
    top = (tok + pe_ref[0:1, :]).astype(BF16)
    bot = (tok + pe_ref[1:2, :]).astype(BF16)
    a = _dot(top, w1_ref[:half, :])
    b = _dot(bot, w1_ref[half:, :])
    ncp = a.shape[0]
    pre = a + pltpu.roll(b, ncp - 1, 0) + b1_ref[...]
    hid = _gelu_tanh(pre).astype(BF16)
    return _dot(hid, w2_ref[...])


def _compress_kernel(kc_ref, vc_ref, pek_ref, w1k_ref, b1k_ref, w2k_ref, kn_ref,
                     pev_ref, w1v_ref, b1v_ref, w2v_ref, k_out, v_out):
    k = _compress_one(kc_ref, pek_ref, w1k_ref, b1k_ref, w2k_ref)
    kms = jnp.mean(k * k, axis=-1, keepdims=True)
    k_out[0, 0] = (k * lax.rsqrt(kms + EPS) * kn_ref[...]).astype(BF16)
    v_out[0, 0] = _compress_one(vc_ref, pev_ref, w1v_ref, b1v_ref, w2v_ref).astype(BF16)


def _compress_call(kc, vc, pek, w1k, b1k, w2k, kn, pev, w1v, b1v, w2v):
    B, G, ncp, width = kc.shape
    const2 = lambda b, g: (0, 0)
    tok_spec = pl.BlockSpec((1, 1, ncp, width), lambda b, g: (b, g, 0, 0))
    out_spec = pl.BlockSpec((1, 1, ncp, D_HEAD), lambda b, g: (b, g, 0, 0))
    full = lambda a: pl.BlockSpec(a.shape, const2)
    out_shape = jax.ShapeDtypeStruct((B, G, ncp, D_HEAD), BF16)
    return pl.pallas_call(
        _compress_kernel,
        grid=(B, G),
        in_specs=[tok_spec, tok_spec, full(pek), full(w1k), full(b1k), full(w2k), full(kn),
                  full(pev), full(w1v), full(b1v), full(w2v)],
        out_specs=[out_spec, out_spec],
        out_shape=[out_shape, out_shape],
        compiler_params=pltpu.CompilerParams(
            dimension_semantics=("arbitrary", "arbitrary"), vmem_limit_bytes=VMEM_LIMIT),
        name="compress",
    )(kc, vc, pek, w1k, b1k, w2k, kn, pev, w1v, b1v, w2v)


def _rel_bucket_np(dist):
    max_exact = REL_BUCKETS // 2
    d = np.maximum(dist, 1).astype(np.float32)
    log_b = max_exact + (np.log(d / np.float32(max_exact)) / np.float32(math.log(REL_MAX_DIST / max_exact))
                         * np.float32(REL_BUCKETS - max_exact)).astype(np.int32)
    log_b = np.clip(log_b, max_exact, REL_BUCKETS - 1)
    return np.where(dist < max_exact, np.maximum(dist, 0), log_b)


def _bias_tables(rel_bias):
    rb = rel_bias.astype(F32) * LOG2E
    qq = np.arange(SEL_BLOCK)[:, None]
    kp = np.arange(SEL_BLOCK)[None, :]
    tile_d = [m * SEL_BLOCK + qq - kp for m in range(N_TOEPLITZ)]
    s_ = np.arange(SUB)[:, None, None]
    e_ = np.arange(CBAND)[None, None, :]
    dist_c = qq[None, :, :] + CMP_STRIDE * (e_ - 4 * (SUB - 1) + 4 * s_) - (3 * CMP_STRIDE + CMP_BLOCK - 1)
    all_d = np.concatenate([d.reshape(-1) for d in tile_d] + [dist_c.reshape(-1)])
    onehot = np.eye(REL_BUCKETS, dtype=np.float32)[_rel_bucket_np(all_d)]
    vals = jnp.dot(jnp.asarray(onehot), rb, precision=lax.Precision.HIGHEST)
    vals = jnp.where(jnp.asarray(all_d >= 0)[:, None], vals, NEG)
    n_t = N_TOEPLITZ * SEL_BLOCK * SEL_BLOCK
    toep = jnp.transpose(vals[:n_t].reshape(N_TOEPLITZ, SEL_BLOCK, SEL_BLOCK, NSA_HEADS), (0, 3, 1, 2))
    band = jnp.transpose(vals[n_t:].reshape(SUB, SEL_BLOCK, CBAND, NSA_HEADS), (3, 0, 1, 2))
    far = rb[REL_BUCKETS - 1]
    far_tile = jnp.broadcast_to(far[:, None, None], (NSA_HEADS, SEL_BLOCK, SEL_BLOCK))
    neg_tile = jnp.full((NSA_HEADS, SEL_BLOCK, SEL_BLOCK), NEG, F32)
    edge_tile = jnp.where(jnp.asarray(kp > qq)[None], far_tile, NEG)

    def tile(m, windowed):
        if m < 0 or (windowed and m > WIN_BLOCKS):
            return neg_tile
        if windowed and m == WIN_BLOCKS:
            return edge_tile
        return toep[m] if m < N_TOEPLITZ else far_tile

    def table(first_m, n_blocks, windowed):
        rows = [jnp.concatenate([tile(first_m + s - c, windowed) for c in range(n_blocks)], axis=2)
                for s in range(SUB)]
        t = jnp.concatenate(rows, axis=1)
        return t.reshape(NSA_KV_HEADS, ROWS, n_blocks * SEL_BLOCK)

    far_rows = jnp.repeat(far, TQ).reshape(NSA_KV_HEADS, ROWS, 1)
    near_t = table(BLOCKS_PER_TILE, 2 * BLOCKS_PER_TILE, False) - far_rows
    win_t = table(WIN_BLOCKS, WIN_BLOCKS + BLOCKS_PER_TILE, True) - far_rows
    band_t = band.reshape(NSA_KV_HEADS, ROWS, CBAND)
    band_hi = band_t.astype(BF16)
    band_lo = (band_t - band_hi.astype(F32)).astype(BF16)
    far_hi = far_rows.astype(BF16)
    far_lo = (far_rows - far_hi.astype(F32)).astype(BF16)
    q_half = jnp.concatenate([jnp.zeros((NSA_KV_HEADS, ROWS, D_HEAD), BF16),
                              jnp.full((NSA_KV_HEADS, ROWS, 1), NEG, BF16), far_hi, far_lo,
                              jnp.zeros((NSA_KV_HEADS, ROWS, LANES - D_HEAD - 3), BF16)], axis=2)
    neg_col = jnp.full((NSA_KV_HEADS, ROWS, 1), NEG, BF16)
    zeros = jnp.zeros((NSA_KV_HEADS, ROWS, D_HEAD - 2 * CBAND - 3), BF16)
    q_lanes = jnp.zeros((NSA_KV_HEADS, ROWS, D_HEAD), BF16)
    cmp_lhs = jnp.concatenate([q_lanes, band_hi, band_lo, far_hi, far_lo, neg_col, zeros], axis=2)
    return near_t, win_t, q_half, cmp_lhs


def _row_max(lane_tiles):
    m = jnp.max(functools.reduce(jnp.maximum, lane_tiles), axis=1, keepdims=True)
    return jnp.broadcast_to(m, lane_tiles[0].shape)


def _nsa_kernel(q_ref, kc_ref, vc_ref, ks_ref, vs_ref, kw_ref, vw_ref, g_ref,
                near_ref, win_ref, qhalf_ref, cmpl_ref, ovt_ref, eye_ref,
                o_ref, ksa, vsa, kwa, vwa, kca, vca, sc_scr, lhs_scr, lhsc_scr, m_scr, al_scr, inv_scr, acc_scr, sa_scr, sb_scr, pa_scr, pb_scr,
                sw_scr, pw_scr, ps_scr, out_scr, cnt_scr):
    qt = pl.program_id(2)
    first = qt * SUB
    T = ks_ref.shape[2]
    ncp = kc_ref.shape[2]

    @pl.when(qt == 0)
    def _init():
        n_chunks = ksa.shape[0] // KT
        r_io = lax.broadcasted_iota(jnp.int32, (KT, KAUG), 0)
        l_io = lax.broadcasted_iota(jnp.int32, (KT, KAUG), 1)
        lv = lax.broadcasted_iota(jnp.int32, (KT, VAUG), 1)
        vpat = jnp.where(lv == D_HEAD, 1.0, 0.0).astype(BF16)

        def fill(c, carry):
            rows = pl.ds(pl.multiple_of(c * KT, KT), KT)
            key = r_io + (c * KT - KPAD)
            blk = key // SEL_BLOCK
            real = (key >= 0) & (key < T)
            hot = real & (((l_io < D_HEAD) & (l_io == blk))
                          | ((l_io >= D_HEAD) & (l_io < 2 * D_HEAD) & (l_io - D_HEAD == blk)))
            hot = hot | ((~real) & (l_io == 3 * D_HEAD)) | (real & (l_io > 3 * D_HEAD) & (l_io <= 3 * D_HEAD + 2))
            pat = jnp.where(hot, 1.0, 0.0).astype(BF16)
            ksa[rows, :] = pat
            kwa[rows, :] = pat
            vsa[rows, :] = vpat
            vwa[rows, :] = vpat
            return carry

        lax.fori_loop(0, n_chunks, fill, 0)
        ksa[KPAD:KPAD + T, 2 * D_HEAD:3 * D_HEAD] = ks_ref[0, 0]
        kwa[KPAD:KPAD + T, 2 * D_HEAD:3 * D_HEAD] = kw_ref[0, 0]
        vsa[KPAD:KPAD + T, 0:D_HEAD] = vs_ref[0, 0]
        vwa[KPAD:KPAD + T, 0:D_HEAD] = vw_ref[0, 0]
        lhs_scr[:, 0:LANES] = jnp.zeros((ROWS, LANES), BF16)
        lc = lax.broadcasted_iota(jnp.int32, (ncp, VAUG), 1)
        kca[...] = jnp.zeros((ncp, VAUG), BF16)
        kca[:, 0:D_HEAD] = kc_ref[0, 0]
        vca[...] = jnp.where(lc == D_HEAD, 1.0, 0.0).astype(BF16)
        vca[:, 0:D_HEAD] = vc_ref[0, 0]

    q4 = q_ref[0].reshape(ROWS, D_HEAD)
    gates = g_ref[0, 0]
    row_blocks = [slice(r, r + RB) for r in range(0, ROWS, RB)]

    gate_rows = jnp.concatenate(
        [gates if hh == 0 else pltpu.roll(gates, LANES - hh * N_BRANCH, 1) for hh in range(NSA_GROUP)], axis=0)

    def gate_col(branch):
        return gate_rows[:, branch:branch + 1]

    lane_c = lax.broadcasted_iota(jnp.int32, (ncp, LANES), 1)
    n_c = lax.broadcasted_iota(jnp.int32, (ncp, LANES), 0)
    e = lane_c - D_HEAD
    base = 4 * first + (4 * SUB - 1)
    ind = (((e >= 0) & (e < CBAND) & (n_c == base - e))
           | ((e >= CBAND) & (e < 2 * CBAND) & (n_c == base - (e - CBAND)))
           | (((e == 2 * CBAND) | (e == 2 * CBAND + 1)) & (n_c < base - (CBAND - 1)))
           | ((e == 2 * CBAND + 2) & (n_c > base)))
    rhs_c = jnp.where(lane_c < D_HEAD, kca[...], jnp.where(ind, 1.0, 0.0).astype(BF16))
    lhsc_scr[...] = cmpl_ref[0]
    lhsc_scr[:, 0:D_HEAD] = q4
    sa_scr[:, 0:ncp] = _halves(_dot_nt, lhsc_scr[...], rhs_c)

    lhs_scr[:, LANES:2 * LANES] = qhalf_ref[0]
    lhs_scr[:, LANES:LANES + D_HEAD] = q4
    lhs_win = jnp.concatenate([jnp.zeros((ROWS, LANES), BF16), lhs_scr[:, LANES:2 * LANES]], axis=1)
    win_start = pl.multiple_of(KPAD + (first - WIN_BLOCKS) * SEL_BLOCK, KT)
    sw_scr[...] = _halves(_dot_nt, lhs_win, kwa[pl.ds(win_start, KS + KT), :])

    for i, rows in enumerate(row_blocks):
        s = [sa_scr[rows, c * LANES:(c + 1) * LANES] for c in range(ncp // LANES)]
        m = _row_max(s)
        p = [jnp.where(t > 0.5 * NEG, jnp.exp2(t - m), 0.0) for t in s]
        l = jnp.sum(functools.reduce(jnp.add, p), axis=1, keepdims=True)
        inv = jnp.where(l > 0.0, 1.0 / l, 0.0)
        inv_scr[rows] = inv
        inv_b = jnp.broadcast_to(inv, (RB, LANES))
        tok = slice((i % (TQ // RB)) * RB, (i % (TQ // RB) + 1) * RB)
        for c in range(ncp // LANES):
            lanes = slice(c * LANES, (c + 1) * LANES)
            pb_scr[rows, lanes] = p[c].astype(BF16)
            if i < TQ // RB:
                ps_scr[tok, lanes] = p[c] * inv_b
            else:
                ps_scr[tok, lanes] += p[c] * inv_b
    acc_c = _halves(_dot, pb_scr[:, 0:ncp], vca[...])
    out_scr[...] = acc_c[:, :D_HEAD] * (gate_col(0) * inv_scr[...])
    ps_hi, ps_lo = _split_bf16(ps_scr[...])
    ovt = ovt_ref[...]
    imp_t = _dot_nt(ovt, ps_hi) + _dot_nt(ovt, ps_lo)

    j_t = lax.broadcasted_iota(jnp.int32, (MAX_BLOCKS, TQ), 0)
    cur = first + lax.broadcasted_iota(jnp.int32, (MAX_BLOCKS, TQ), 1) // SEL_BLOCK
    forced = (j_t == 0) | (j_t == cur) | (j_t == cur - 1)
    score_t = jnp.where(forced, FORCED_SCORE, jnp.where(j_t <= cur, imp_t, NEG))
    sc_scr[...] = score_t

    cnt_scr[...] = jnp.zeros((MAX_BLOCKS, TQ), jnp.int32)
    sub_io = lax.broadcasted_iota(jnp.int32, (8, TQ), 0)
    n_groups = MAX_BLOCKS // 8
    for grp in range(n_groups):
        @pl.when(grp * 8 < first + SUB)
        def _count_group():
            targets = [score_t[8 * jg:8 * jg + 8] for jg in range(n_groups)]
            acc = [jnp.zeros((8, TQ), jnp.int32) for _ in range(n_groups)]
            for k in range(8):
                row_i = jnp.broadcast_to(sc_scr[grp * 8 + k:grp * 8 + k + 1, :], (8, TQ))
                for jg in range(n_groups):
                    if jg > grp:
                        beats = row_i >= targets[jg]
                    elif jg < grp:
                        beats = row_i > targets[jg]
                    else:
                        beats = (row_i > targets[jg]) | ((row_i == targets[jg]) & (sub_io > k))
                    acc[jg] = acc[jg] + beats.astype(jnp.int32)
            cnt_scr[...] += jnp.concatenate(acc, axis=0)

    cnt = cnt_scr[...]
    mask_t = jnp.where((cnt < SEL_TOPK) & (j_t <= cur), 0.0, NEG).astype(BF16)
    mask = _dot_nt(eye_ref[...], mask_t).astype(BF16)

    for hh in range(NSA_GROUP):
        lhs_scr[hh * TQ:(hh + 1) * TQ, 0:MAX_BLOCKS] = mask
    m_scr[...] = jnp.full((ROWS, LANES), NEG, F32)
    acc_scr[...] = jnp.zeros((ROWS, VAUG), F32)

    def tile_start(k):
        return pl.multiple_of(KPAD + (first - BLOCKS_PER_TILE) * SEL_BLOCK - k * KS, KT)

    def scores(k, dst):
        dst[...] = _halves(_dot_nt, lhs_scr[...], ksa[pl.ds(tile_start(k), KS), :])

    def consume(k, src, p_scr, with_table):
        n_lt = KS // LANES
        for rows in row_blocks:
            s = [src[rows, c * LANES:(c + 1) * LANES] for c in range(n_lt)]
            if with_table:
                s = [s[c] + near_ref[0, rows, c * LANES:(c + 1) * LANES] for c in range(n_lt)]
            m_prev = m_scr[rows]
            m = jnp.maximum(m_prev, _row_max(s))
            al_scr[rows] = jnp.exp2(m_prev - m)
            m_scr[rows] = m
            for c in range(n_lt):
                p_scr[rows, c * LANES:(c + 1) * LANES] = jnp.exp2((s[c] - m).astype(BF16))
        acc_scr[...] = acc_scr[...] * al_scr[...] + _dot(p_scr[:, 0:KS], vsa[pl.ds(tile_start(k), KS), :])

    scores(0, sa_scr)

    for rows in row_blocks:
        s = [sw_scr[rows, c * LANES:(c + 1) * LANES] + win_ref[0, rows, c * LANES:(c + 1) * LANES]
             for c in range((KS + KT) // LANES)]
        m = _row_max(s)
        for c in range((KS + KT) // LANES):
            pw_scr[rows, c * LANES:(c + 1) * LANES] = jnp.exp2((s[c] - m).astype(BF16))
    acc_w = _halves(_dot, pw_scr[...], vwa[pl.ds(win_start, KS + KT), :])
    out_scr[...] += acc_w[:, :D_HEAD] * (gate_col(2) / acc_w[:, D_HEAD:D_HEAD + 1])

    def pair(with_table, u, carry):
        scores(2 * u + 1, sb_scr)
        consume(2 * u, sa_scr, pa_scr, with_table)
        scores(2 * u + 2, sa_scr)
        consume(2 * u + 1, sb_scr, pb_scr, False)
        return carry

    n_tiles = (qt + 2) // 2
    pair(True, 0, 0)
    lax.fori_loop(1, jnp.maximum(n_tiles // 2, 1), functools.partial(pair, False), 0)

    @pl.when((n_tiles % 2 == 1) & (n_tiles >= 3))
    def _odd_tile():
        consume(n_tiles - 1, sa_scr, pa_scr, False)

    acc_s = acc_scr[...]

    o = out_scr[...] + acc_s[:, :D_HEAD] * (gate_col(1) / acc_s[:, D_HEAD:D_HEAD + 1])
    for hh in range(NSA_GROUP):
        o_ref[0, :, hh * D_HEAD:(hh + 1) * D_HEAD] = o[hh * TQ:(hh + 1) * TQ].astype(BF16)


def _nsa_call(q, kcmp, vcmp, ks, vs, kw, vw, gates, near_t, win_t, q_half, cmp_lhs, ovt, eye):
    B, _, T, _ = q.shape
    G = NSA_KV_HEADS
    ncp = kcmp.shape[2]
    rows_kv = KPAD + T
    per_bg =lambda a: pl.BlockSpec((1, 1) + a.shape[2:], lambda b, g, i: (b, g, 0, 0))
    per_g = lambda a: pl.BlockSpec((1,) + a.shape[1:], lambda b, g, i: (g, 0, 0))
    const2 = lambda a: pl.BlockSpec(a.shape, lambda b, g, i: (0, 0))
    return pl.pallas_call(
        _nsa_kernel,
        grid=(B, G, T // TQ),
        in_specs=[
            pl.BlockSpec((1, NSA_GROUP, TQ, D_HEAD), lambda b, g, i: (b, g, i, 0)),
            per_bg(kcmp), per_bg(vcmp), per_bg(ks), per_bg(vs), per_bg(kw), per_bg(vw),
            pl.BlockSpec((1, 1, TQ, LANES), lambda b, g, i: (b, g, i, 0)),
            per_g(near_t), per_g(win_t), per_g(q_half), per_g(cmp_lhs), const2(ovt), const2(eye),
        ],
        out_specs=pl.BlockSpec((1, TQ, NSA_GROUP * D_HEAD), lambda b, g, i: (b, i, g)),
        out_shape=jax.ShapeDtypeStruct((B, T, D_NSA), BF16),
        scratch_shapes=[
            pltpu.VMEM((rows_kv, KAUG), BF16), pltpu.VMEM((rows_kv, VAUG), BF16),
            pltpu.VMEM((rows_kv, KAUG), BF16), pltpu.VMEM((rows_kv, VAUG), BF16),
            pltpu.VMEM((ncp, VAUG), BF16), pltpu.VMEM((ncp, VAUG), BF16),
            pltpu.VMEM((MAX_BLOCKS, TQ), F32),
            pltpu.VMEM((ROWS, KAUG), BF16), pltpu.VMEM((ROWS, LANES), BF16),
            pltpu.VMEM((ROWS, LANES), F32), pltpu.VMEM((ROWS, LANES), F32), pltpu.VMEM((ROWS, 1), F32),
            pltpu.VMEM((ROWS, VAUG), F32),
            pltpu.VMEM((ROWS, KS), F32), pltpu.VMEM((ROWS, KS), F32),
            pltpu.VMEM((ROWS, KS), BF16), pltpu.VMEM((ROWS, KS), BF16),
            pltpu.VMEM((ROWS, KS + KT), F32), pltpu.VMEM((ROWS, KS + KT), BF16),
            pltpu.VMEM((TQ, ncp), F32), pltpu.VMEM((ROWS, D_HEAD), F32), pltpu.VMEM((MAX_BLOCKS, TQ), jnp.int32),
        ],
        compiler_params=pltpu.CompilerParams(
            dimension_semantics=("arbitrary", "arbitrary", "arbitrary"), vmem_limit_bytes=VMEM_LIMIT),
        name="nsa",
    )(q, kcmp, vcmp, ks, vs, kw, vw, gates, near_t, win_t, q_half, cmp_lhs, ovt, eye)


def _ffn_kernel(x_ref, on_ref, os_ref, wo_n_ref, wo_s_ref, fn_ref, wup_ref, cw_ref, cb_ref, wdn_ref, pm_ref,
                out_ref, prev_ref, perm_ref, h_ref, ua_scr, ub_scr, act_ref):
    tm = x_ref.shape[1]
    seg = tm // 8
    n_chunks = wup_ref.shape[1]

    @pl.when(pl.program_id(1) == 0)
    def _start_of_sequence():
        prev_ref[...] = jnp.zeros(prev_ref.shape, F32)

    x1 = x_ref[0] + _dot(on_ref[0], wo_n_ref[...]) + _dot(os_ref[0], wo_s_ref[...])
    ms = jnp.mean(x1 * x1, axis=-1, keepdims=True)
    hn = (x1 * lax.rsqrt(ms + EPS) * fn_ref[...]).astype(BF16)
    out_ref[0] = x1
    n_lt = x1.shape[1] // LANES
    h_ref[...] = _halves(_dot, pm_ref[...], hn).astype(BF16)

    def up(c, dst):
        h = h_ref[...]
        for half in range(2):
            dst[half] = _dot(h, wup_ref[half, c])

    first_sublane = lax.broadcasted_iota(jnp.int32, (8, FFN_CHUNK), 0) == 0

    def conv_gate(c, src):
        for r0 in range(0, tm, FFN_ROWS):
            ys = []
            for half in range(2):
                w = cw_ref[half, c]
                x0 = src[half, r0:r0 + FFN_ROWS]
                if r0 == 0:
                    wrap = [jnp.where(first_sublane, pltpu.roll(prev_ref[half, c, 8 * k:8 * k + 8], 1, 0),
                                      pltpu.roll(src[half, tm - 16 + 8 * k:tm - 8 + 8 * k], 1, 0)) for k in range(2)]
                    x1_ = jnp.concatenate([wrap[1], src[half, 0:FFN_ROWS - 8]], axis=0)
                    x2_ = jnp.concatenate([wrap[0], wrap[1], src[half, 0:FFN_ROWS - 16]], axis=0)
                else:
                    x1_ = src[half, r0 - 8:r0 + FFN_ROWS - 8]
                    x2_ = src[half, r0 - 16:r0 + FFN_ROWS - 16]
                ys.append(x0 * w[2:3] + x1_ * w[1:2] + x2_ * w[0:1] + cb_ref[half, c])
            a, g = ys
            col = pl.multiple_of(c * FFN_CHUNK, FFN_CHUNK)
            act_ref[r0:r0 + FFN_ROWS, pl.ds(col, FFN_CHUNK)] = ((g + g * jnp.tanh(g)) * a).astype(BF16)
        for half in range(2):
            prev_ref[half, c] = src[half, tm - 16:tm]

    up(0, ua_scr)

    def pair(u, carry):
        up(2 * u + 1, ub_scr)
        conv_gate(2 * u, ua_scr)
        up(2 * u + 2, ua_scr)
        conv_gate(2 * u + 1, ub_scr)
        return carry

    lax.fori_loop(0, (n_chunks - 1) // 2, pair, 0)
    conv_gate(n_chunks - 1, ua_scr)
    y = _halves(_dot, act_ref[...], wdn_ref[...])
    for c in range(n_lt):
        perm_ref[c] = y[:, c * LANES:(c + 1) * LANES]
    for sgm in range(8):
        natural = jnp.concatenate([perm_ref[c, pl.ds(sgm, seg, stride=8), :] for c in range(n_lt)], axis=1)
        out_ref[0, sgm * seg:(sgm + 1) * seg, :] += natural


def _ffn_call(x, o_nsa, o_sgu, wo_n, wo_s, ffn_norm, w_up, conv_w, conv_b, w_down, tm):
    B, T, _ = x.shape
    r = np.arange(tm)
    perm = jnp.asarray(((r % 8) * (tm // 8) + r // 8)[:, None] == r[None, :], BF16)
    n_chunks = w_up.shape[1]
    assert n_chunks % 2 == 1
    resident =lambda a: pl.BlockSpec(a.shape, lambda b, i: (0,) * a.ndim, pipeline_mode=pl.Buffered(1))
    tile = lambda w: pl.BlockSpec((1, tm, w), lambda b, i: (b, i, 0))
    return pl.pallas_call(
        _ffn_kernel,
        grid=(B, T // tm),
        in_specs=[tile(D_MODEL), tile(D_NSA), tile(D_GMLP), resident(wo_n), resident(wo_s),
                  resident(ffn_norm), resident(w_up), resident(conv_w), resident(conv_b), resident(w_down),
                  resident(perm)],
        out_specs=tile(D_MODEL),
        out_shape=jax.ShapeDtypeStruct((B, T, D_MODEL), F32),
        scratch_shapes=[pltpu.VMEM((2, n_chunks, 16, FFN_CHUNK), F32), pltpu.VMEM((D_MODEL // LANES, tm, LANES), F32),
                        pltpu.VMEM((tm, D_MODEL), BF16),
                        pltpu.VMEM((2, tm, FFN_CHUNK), F32), pltpu.VMEM((2, tm, FFN_CHUNK), F32),
                        pltpu.VMEM((tm, D_FF), BF16)],
        compiler_params=pltpu.CompilerParams(
            dimension_semantics=("arbitrary", "arbitrary"), vmem_limit_bytes=VMEM_LIMIT),
        name="ffn",
    )(x, o_nsa, o_sgu, wo_n, wo_s, ffn_norm, w_up, conv_w, conv_b, w_down, perm)


def _mixers(x, rel_bias, attn_norm, w_in, q_norm, k_norm_cmp, k_norm_slc, k_norm_win,
            cmp_pe_k, cmp_w1_k, cmp_b1_k, cmp_w2_k, cmp_pe_v, cmp_w1_v, cmp_b1_v, cmp_w2_v,
            sgu_norm, sgu_w, sgu_b):
    B, T, _ = x.shape
    assert T % 512 == 0 and T // SEL_BLOCK <= MAX_BLOCKS and (T // CMP_STRIDE) % LANES == 0
    ncp = T // CMP_STRIDE
    G, R = NSA_KV_HEADS, NSA_GROUP

    o_kv = D_NSA
    o_g = D_NSA + 6 * D_KV
    o_uv = o_g + N_GATES
    w_q = w_in[:, :o_kv].astype(BF16)
    w_kv = w_in[:, o_kv:o_g].astype(BF16)
    per_group = NSA_GROUP * N_BRANCH
    w_g = jnp.pad(w_in[:, o_g:o_uv].reshape(D_MODEL, G, per_group), ((0, 0), (0, 0), (0, LANES - per_group)))
    w_g = w_g.reshape(D_MODEL, G * LANES).astype(BF16)
    w_uv = w_in[:, o_uv:].astype(BF16)
    qn = (jnp.tile(q_norm, NSA_HEADS) * (D_HEAD ** -0.5 * LOG2E)).reshape(1, D_NSA)
    ksn = jnp.tile(k_norm_slc, G).reshape(1, D_KV)
    kwn = jnp.tile(k_norm_win, G).reshape(1, D_KV)
    sgu_bt = jnp.repeat(sgu_b.T, GMLP_GROUP_DIM, axis=1)
    grp = np.arange(D_NSA) // D_HEAD
    bd = jnp.asarray(grp[:, None] == grp[None, :], BF16)

    q, kc, vc, ks, vs, kw, vw, gates, o_sgu = _proj_call(
        x, attn_norm.reshape(1, D_MODEL), w_q, w_kv, w_g, w_uv, qn, ksn, kwn,
        sgu_norm.reshape(1, D_GMLP), sgu_w, sgu_bt, bd, tm=512)

    half = CMP_STRIDE * D_HEAD
    kcmp, vcmp = _compress_call(
        kc.reshape(B, G, ncp, half), vc.reshape(B, G, ncp, half),
        cmp_pe_k.reshape(2, half), cmp_w1_k.astype(BF16), cmp_b1_k.reshape(1, CMP_HIDDEN),
        cmp_w2_k.astype(BF16), k_norm_cmp.reshape(1, D_HEAD),
        cmp_pe_v.reshape(2, half), cmp_w1_v.astype(BF16), cmp_b1_v.reshape(1, CMP_HIDDEN),
        cmp_w2_v.astype(BF16))

    near_t, win_t, q_half, cmp_lhs = _bias_tables(rel_bias)
    jj = np.arange(MAX_BLOCKS)[:, None]
    nn = np.arange(ncp)[None, :]
    ovt = jnp.asarray((nn >= 4 * jj - 1) & (nn <= 4 * jj + 3), BF16)
    eye = jnp.asarray(np.eye(TQ), BF16)

    o_nsa = _nsa_call(q, kcmp, vcmp, ks, vs, kw, vw, gates, near_t, win_t, q_half, cmp_lhs, ovt, eye)
    return o_nsa, o_sgu


def _layer(x, rel_bias, attn_norm, w_in, q_norm, k_norm_cmp, k_norm_slc, k_norm_win,
           cmp_pe_k, cmp_w1_k, cmp_b1_k, cmp_w2_k, cmp_pe_v, cmp_w1_v, cmp_b1_v, cmp_w2_v,
           sgu_norm, sgu_w, sgu_b, w_out, ffn_norm, w_up, conv_w, conv_b, w_down):
    o_nsa, o_sgu = _mixers(x, rel_bias, attn_norm, w_in, q_norm, k_norm_cmp, k_norm_slc, k_norm_win,
                           cmp_pe_k, cmp_w1_k, cmp_b1_k, cmp_w2_k, cmp_pe_v, cmp_w1_v, cmp_b1_v, cmp_w2_v,
                           sgu_norm, sgu_w, sgu_b)
    wo = w_out.astype(BF16)
    n_chunks = D_FF // FFN_CHUNK
    wu = jnp.transpose(w_up.astype(BF16).reshape(D_MODEL, 2, n_chunks, FFN_CHUNK), (1, 2, 0, 3))
    half_gate = jnp.asarray([1.0, 0.5], F32).reshape(2, 1, 1, 1)
    cw = jnp.transpose(conv_w.reshape(conv_w.shape[0], 2, n_chunks, FFN_CHUNK), (1, 2, 0, 3)) * half_gate
    cb = conv_b.reshape(2, n_chunks, 1, FFN_CHUNK) * half_gate
    wd = w_down.astype(BF16)
    return _ffn_call(x, o_nsa, o_sgu, wo[:D_NSA], wo[D_NSA:], ffn_norm.reshape(1, D_MODEL), wu, cw, cb, wd, tm=512)


def kernel(x, rel_bias, attn_norm, w_in, q_norm, k_norm_cmp, k_norm_slc, k_norm_win,
           cmp_pe_k, cmp_w1_k, cmp_b1_k, cmp_w2_k, cmp_pe_v, cmp_w1_v, cmp_b1_v, cmp_w2_v,
           sgu_norm, sgu_w, sgu_b, w_out, ffn_norm, w_up, conv_w, conv_b, w_down):
    depth = attn_norm.shape[0]
    for l in range(depth):
        x = _layer(x, rel_bias, attn_norm[l], w_in[l], q_norm[l], k_norm_cmp[l], k_norm_slc[l], k_norm_win[l],
                   cmp_pe_k[l], cmp_w1_k[l], cmp_b1_k[l], cmp_w2_k[l], cmp_pe_v[l], cmp_w1_v[l], cmp_b1_v[l],
                   cmp_w2_v[l], sgu_norm[l], sgu_w[l], sgu_b[l], w_out[l], ffn_norm[l], w_up[l], conv_w[l],
                   conv_b[l], w_down[l])
    return x
```

```python
import functools
import math

import numpy as np
import jax
import jax.numpy as jnp
from jax import lax
from jax.experimental import pallas as pl
from jax.experimental.pallas import tpu as pltpu

F32 = jnp.float32
BF16 = jnp.bfloat16

D_MODEL = 1024
D_HEAD = 64
NSA_HEADS = 8
NSA_KV_HEADS = 2
NSA_GROUP = NSA_HEADS // NSA_KV_HEADS
D_NSA = NSA_HEADS * D_HEAD
D_KV = NSA_KV_HEADS * D_HEAD
N_BRANCH = 3
N_GATES = NSA_HEADS * N_BRANCH
CMP_BLOCK = 32
CMP_STRIDE = 16
CMP_HIDDEN = 256
SEL_BLOCK = 64
SEL_TOPK = 16
WINDOW = 512
GMLP_GROUPS = 8
GMLP_GROUP_DIM = 64
D_GMLP = GMLP_GROUPS * GMLP_GROUP_DIM
CHUNK = 128
D_MIX = D_NSA + D_GMLP
REL_BUCKETS = 32
REL_MAX_DIST = 128
D_FF = 2816
EPS = 1e-6
NEG = -1e30
FORCED_SCORE = 1e4
LOG2E = 1.4426950408889634

LANES = 128
TQ = 256
SUB = TQ // SEL_BLOCK
ROWS = NSA_GROUP * TQ
KT = 256
KS = 2 * KT
RB = 64
BLOCKS_PER_TILE = KT // SEL_BLOCK
WIN_BLOCKS = WINDOW // SEL_BLOCK
KPAD = 1280
N_TOEPLITZ = 3
MAX_BLOCKS = 64
KAUG = 256
VAUG = 128
CBAND = 28
FFN_CHUNK = 256
FFN_ROWS = 64
VMEM_LIMIT = 56 * 1024 * 1024

_NT = (((1,), (1,)), ((), ()))


def _dot(a, b):
    return jnp.dot(a, b, preferred_element_type=F32)


def _dot_nt(a, b):
    return lax.dot_general(a, b, _NT, preferred_element_type=F32)


def _halves(dot, a, b):
    h = a.shape[0] // 2
    return jnp.concatenate([dot(a[:h], b), dot(a[h:], b)], axis=0)


def _split_bf16(x):
    hi = x.astype(BF16)
    lo = (x - hi.astype(F32)).astype(BF16)
    return hi, lo


def _gelu_tanh(x):
    return 0.5 * x * (1.0 + jnp.tanh(0.7978845608028654 * (x + 0.044715 * (x * x * x))))


def _sigmoid(x):
    return 0.5 * (1.0 + jnp.tanh(0.5 * x))


def _group_mean_sq(t, ones_blockdiag, width):
    t2 = t * t
    hi, lo = _split_bf16(t2)
    return (_dot(hi, ones_blockdiag) + _dot(lo, ones_blockdiag)) * (1.0 / width)


def _proj_kernel(x_ref, an_ref, wq_ref, wkv_ref, wg_ref, wuv_ref, qn_ref, ksn_ref, kwn_ref,
                 sgun_ref, sguw_ref, sgub_ref, bd_ref,
                 q_out, kc_out, vc_out, ks_out, vs_out, kw_out, vw_out, g_out, sgu_out):
    tm = x_ref.shape[1]
    x = x_ref[0]
    ms = jnp.mean(x * x, axis=-1, keepdims=True)
    h = (x * lax.rsqrt(ms + EPS) * an_ref[...]).astype(BF16)

    bd = bd_ref[...]
    q = _halves(_dot, h, wq_ref[...])
    qn = q * lax.rsqrt(_group_mean_sq(q, bd, D_HEAD) + EPS) * qn_ref[...]
    for hh in range(NSA_HEADS):
        q_out[0, hh] = qn[:, hh * D_HEAD:(hh + 1) * D_HEAD].astype(BF16)

    kv = _halves(_dot, h, wkv_ref[...])
    bd_kv = bd[:D_KV, :D_KV]
    kc = kv[:, 0 * D_KV:1 * D_KV]
    vc = kv[:, 1 * D_KV:2 * D_KV]
    ks = kv[:, 2 * D_KV:3 * D_KV]
    vs = kv[:, 3 * D_KV:4 * D_KV]
    kw = kv[:, 4 * D_KV:5 * D_KV]
    vw = kv[:, 5 * D_KV:6 * D_KV]
    ks = ks * lax.rsqrt(_group_mean_sq(ks, bd_kv, D_HEAD) + EPS) * ksn_ref[...]
    kw = kw * lax.rsqrt(_group_mean_sq(kw, bd_kv, D_HEAD) + EPS) * kwn_ref[...]
    for t, o_ref in ((kc, kc_out), (vc, vc_out), (ks, ks_out), (vs, vs_out), (kw, kw_out), (vw, vw_out)):
        for g in range(NSA_KV_HEADS):
            o_ref[0, g] = t[:, g * D_HEAD:(g + 1) * D_HEAD].astype(BF16)

    gates = _sigmoid(_dot(h, wg_ref[...]))
    for g in range(NSA_KV_HEADS):
        g_out[0, g] = gates[:, g * LANES:(g + 1) * LANES]

    uv = _gelu_tanh(_halves(_dot, h, wuv_ref[...]))
    u = uv[:, :D_GMLP]
    v = uv[:, D_GMLP:]
    vms = jnp.mean(v * v, axis=-1, keepdims=True)
    vb = (v * lax.rsqrt(vms + EPS) * sgun_ref[...]).astype(BF16)

    row = lax.broadcasted_iota(jnp.int32, (CHUNK, CHUNK), 0)
    col = lax.broadcasted_iota(jnp.int32, (CHUNK, CHUNK), 1)
    tril = col <= row
    w_tril = [jnp.where(tril, sguw_ref[g], 0.0).astype(BF16) for g in range(GMLP_GROUPS)]
    w_pair = [jnp.concatenate([w_tril[2 * p], w_tril[2 * p + 1]], axis=1) for p in range(D_GMLP // LANES)]
    first_half = lax.broadcasted_iota(jnp.int32, (CHUNK, LANES), 1) < GMLP_GROUP_DIM
    zero = jnp.zeros((CHUNK, LANES), BF16)
    for c in range(tm // CHUNK):
        rows = slice(c * CHUNK, (c + 1) * CHUNK)
        zs = []
        for p in range(D_GMLP // LANES):
            blk = vb[rows, p * LANES:(p + 1) * LANES]
            rhs = jnp.concatenate([jnp.where(first_half, blk, zero), jnp.where(first_half, zero, blk)], axis=0)
            zs.append(_dot(w_pair[p], rhs))
        z = jnp.concatenate(zs, axis=1) + sgub_ref[...]
        sgu_out[0, rows, :] = (u[rows, :] * z).astype(BF16)


def _proj_call(x, attn_norm, w_q, w_kv, w_g, w_uv, qn, ksn, kwn, sgu_norm, sgu_w, sgu_bt, bd, tm):
    B, T, _ = x.shape
    const2 = lambda b, i: (0, 0)
    const3 = lambda b, i: (0, 0, 0)
    head_spec = lambda nh: pl.BlockSpec((1, nh, tm, D_HEAD), lambda b, i: (b, 0, i, 0))
    kv_shape = jax.ShapeDtypeStruct((B, NSA_KV_HEADS, T, D_HEAD), BF16)
    return pl.pallas_call(
        _proj_kernel,
        grid=(B, T // tm),
        in_specs=[
            pl.BlockSpec((1, tm, D_MODEL), lambda b, i: (b, i, 0)),
            pl.BlockSpec((1, D_MODEL), const2),
            pl.BlockSpec(w_q.shape, const2),
            pl.BlockSpec(w_kv.shape, const2),
            pl.BlockSpec(w_g.shape, const2),
            pl.BlockSpec(w_uv.shape, const2),
            pl.BlockSpec((1, D_NSA), const2),
            pl.BlockSpec((1, D_KV), const2),
            pl.BlockSpec((1, D_KV), const2),
            pl.BlockSpec((1, D_GMLP), const2),
            pl.BlockSpec(sgu_w.shape, const3),
            pl.BlockSpec(sgu_bt.shape, const2),
            pl.BlockSpec(bd.shape, const2),
        ],
        out_specs=[
            head_spec(NSA_HEADS),
            head_spec(NSA_KV_HEADS), head_spec(NSA_KV_HEADS), head_spec(NSA_KV_HEADS),
            head_spec(NSA_KV_HEADS), head_spec(NSA_KV_HEADS), head_spec(NSA_KV_HEADS),
            pl.BlockSpec((1, NSA_KV_HEADS, tm, LANES), lambda b, i: (b, 0, i, 0)),
            pl.BlockSpec((1, tm, D_GMLP), lambda b, i: (b, i, 0)),
        ],
        out_shape=[
            jax.ShapeDtypeStruct((B, NSA_HEADS, T, D_HEAD), BF16),
            kv_shape, kv_shape, kv_shape, kv_shape, kv_shape, kv_shape,
            jax.ShapeDtypeStruct((B, NSA_KV_HEADS, T, LANES), F32),
            jax.ShapeDtypeStruct((B, T, D_GMLP), BF16),
        ],
        compiler_params=pltpu.CompilerParams(
            dimension_semantics=("arbitrary", "arbitrary"), vmem_limit_bytes=VMEM_LIMIT),
        name="proj",
    )(x, attn_norm, w_q, w_kv, w_g, w_uv, qn, ksn, kwn, sgu_norm, sgu_w, sgu_bt, bd)


def _compress_one(tok_ref, pe_ref, w1_ref, b1_ref, w2_ref):
    half = CMP_STRIDE * D_HEAD
    tok = tok_ref[0, 0].astype(F32)
    top = (tok + pe_ref[0:1, :]).astype(BF16)
    bot = (tok + pe_ref[1:2, :]).astype(BF16)
    a = _dot(top, w1_ref[:half, :])
    b = _dot(bot, w1_ref[half:, :])
    ncp = a.shape[0]
    pre = a + pltpu.roll(b, ncp - 1, 0) + b1_ref[...]
    hid = _gelu_tanh(pre).astype(BF16)
    return _dot(hid, w2_ref[...])


def _compress_kernel(kc_ref, vc_ref, pek_ref, w1k_ref, b1k_ref, w2k_ref, kn_ref,
                     pev_ref, w1v_ref, b1v_ref, w2v_ref, k_out, v_out):
    k = _compress_one(kc_ref, pek_ref, w1k_ref, b1k_ref, w2k_ref)
    kms = jnp.mean(k * k, axis=-1, keepdims=True)
    k_out[0, 0] = (k * lax.rsqrt(kms + EPS) * kn_ref[...]).astype(BF16)
    v_out[0, 0] = _compress_one(vc_ref, pev_ref, w1v_ref, b1v_ref, w2v_ref).astype(BF16)


def _compress_call(kc, vc, pek, w1k, b1k, w2k, kn, pev, w1v, b1v, w2v):
    B, G, ncp, width = kc.shape
    const2 = lambda b, g: (0, 0)
    tok_spec = pl.BlockSpec((1, 1, ncp, width), lambda b, g: (b, g, 0, 0))
    out_spec = pl.BlockSpec((1, 1, ncp, D_HEAD), lambda b, g: (b, g, 0, 0))
    full = lambda a: pl.BlockSpec(a.shape, const2)
    out_shape = jax.ShapeDtypeStruct((B, G, ncp, D_HEAD), BF16)
    return pl.pallas_call(
        _compress_kernel,
        grid=(B, G),
        in_specs=[tok_spec, tok_spec, full(pek), full(w1k), full(b1k), full(w2k), full(kn),
                  full(pev), full(w1v), full(b1v), full(w2v)],
        out_specs=[out_spec, out_spec],
        out_shape=[out_shape, out_shape],
        compiler_params=pltpu.CompilerParams(
            dimension_semantics=("arbitrary", "arbitrary"), vmem_limit_bytes=VMEM_LIMIT),
        name="compress",
    )(kc, vc, pek, w1k, b1k, w2k, kn, pev, w1v, b1v, w2v)


def _rel_bucket_np(dist):
    max_exact = REL_BUCKETS // 2
    d = np.maximum(dist, 1).astype(np.float32)
    log_b = max_exact + (np.log(d / np.float32(max_exact)) / np.float32(math.log(REL_MAX_DIST / max_exact))
                         * np.float32(REL_BUCKETS - max_exact)).astype(np.int32)
    log_b = np.clip(log_b, max_exact, REL_BUCKETS - 1)
    return np.where(dist < max_exact, np.maximum(dist, 0), log_b)


def _bias_tables(rel_bias):
    rb = rel_bias.astype(F32) * LOG2E
    qq = np.arange(SEL_BLOCK)[:, None]
    kp = np.arange(SEL_BLOCK)[None, :]
    tile_d = [m * SEL_BLOCK + qq - kp for m in range(N_TOEPLITZ)]
    s_ = np.arange(SUB)[:, None, None]
    e_ = np.arange(CBAND)[None, None, :]
    dist_c = qq[None, :, :] + CMP_STRIDE * (e_ - 4 * (SUB - 1) + 4 * s_) - (3 * CMP_STRIDE + CMP_BLOCK - 1)
    all_d = np.concatenate([d.reshape(-1) for d in tile_d] + [dist_c.reshape(-1)])
    onehot = np.eye(REL_BUCKETS, dtype=np.float32)[_rel_bucket_np(all_d)]
    vals = jnp.dot(jnp.asarray(onehot), rb, precision=lax.Precision.HIGHEST)
    vals = jnp.where(jnp.asarray(all_d >= 0)[:, None], vals, NEG)
    n_t = N_TOEPLITZ * SEL_BLOCK * SEL_BLOCK
    toep = jnp.transpose(vals[:n_t].reshape(N_TOEPLITZ, SEL_BLOCK, SEL_BLOCK, NSA_HEADS), (0, 3, 1, 2))
    band = jnp.transpose(vals[n_t:].reshape(SUB, SEL_BLOCK, CBAND, NSA_HEADS), (3, 0, 1, 2))
    far = rb[REL_BUCKETS - 1]
    far_tile = jnp.broadcast_to(far[:, None, None], (NSA_HEADS, SEL_BLOCK, SEL_BLOCK))
    neg_tile = jnp.full((NSA_HEADS, SEL_BLOCK, SEL_BLOCK), NEG, F32)
    edge_tile = jnp.where(jnp.asarray(kp > qq)[None], far_tile, NEG)

    def tile(m, windowed):
        if m < 0 or (windowed and m > WIN_BLOCKS):
            return neg_tile
        if windowed and m == WIN_BLOCKS:
            return edge_tile
        return toep[m] if m < N_TOEPLITZ else far_tile

    def table(first_m, n_blocks, windowed):
        rows = [jnp.concatenate([tile(first_m + s - c, windowed) for c in range(n_blocks)], axis=2)
                for s in range(SUB)]
        t = jnp.concatenate(rows, axis=1)
        return t.reshape(NSA_KV_HEADS, ROWS, n_blocks * SEL_BLOCK)

    far_rows = jnp.repeat(far, TQ).reshape(NSA_KV_HEADS, ROWS, 1)
    near_t = table(BLOCKS_PER_TILE, 2 * BLOCKS_PER_TILE, False) - far_rows
    win_t = table(WIN_BLOCKS, WIN_BLOCKS + BLOCKS_PER_TILE, True) - far_rows
    band_t = band.reshape(NSA_KV_HEADS, ROWS, CBAND)
    band_hi = band_t.astype(BF16)
    band_lo = (band_t - band_hi.astype(F32)).astype(BF16)
    far_hi = far_rows.astype(BF16)
    far_lo = (far_rows - far_hi.astype(F32)).astype(BF16)
    q_half = jnp.concatenate([jnp.zeros((NSA_KV_HEADS, ROWS, D_HEAD), BF16),
                              jnp.full((NSA_KV_HEADS, ROWS, 1), NEG, BF16), far_hi, far_lo,
                              jnp.zeros((NSA_KV_HEADS, ROWS, LANES - D_HEAD - 3), BF16)], axis=2)
    neg_col = jnp.full((NSA_KV_HEADS, ROWS, 1), NEG, BF16)
    zeros = jnp.zeros((NSA_KV_HEADS, ROWS, D_HEAD - 2 * CBAND - 3), BF16)
    q_lanes = jnp.zeros((NSA_KV_HEADS, ROWS, D_HEAD), BF16)
    cmp_lhs = jnp.concatenate([q_lanes, band_hi, band_lo, far_hi, far_lo, neg_col, zeros], axis=2)
    return near_t, win_t, q_half, cmp_lhs


def _row_max(lane_tiles):
    m = jnp.max(functools.reduce(jnp.maximum, lane_tiles), axis=1, keepdims=True)
    return jnp.broadcast_to(m, lane_tiles[0].shape)


def _nsa_kernel(q_ref, kc_ref, vc_ref, ks_ref, vs_ref, kw_ref, vw_ref, g_ref,
                near_ref, win_ref, qhalf_ref, cmpl_ref, ovt_ref, eye_ref,
                o_ref, ksa, vsa, kwa, vwa, kca, vca, sc_scr, lhs_scr, lhsc_scr, m_scr, al_scr, inv_scr, acc_scr, sa_scr, sb_scr, pa_scr, pb_scr,
                sw_scr, pw_scr, ps_scr, out_scr, cnt_scr):
    qt = pl.program_id(2)
    first = qt * SUB
    T = ks_ref.shape[2]
    ncp = kc_ref.shape[2]

    @pl.when(qt == 0)
    def _init():
        n_chunks = ksa.shape[0] // KT
        r_io = lax.broadcasted_iota(jnp.int32, (KT, KAUG), 0)
        l_io = lax.broadcasted_iota(jnp.int32, (KT, KAUG), 1)
        lv = lax.broadcasted_iota(jnp.int32, (KT, VAUG), 1)
        vpat = jnp.where(lv == D_HEAD, 1.0, 0.0).astype(BF16)

        def fill(c, carry):
            rows = pl.ds(pl.multiple_of(c * KT, KT), KT)
            key = r_io + (c * KT - KPAD)
            blk = key // SEL_BLOCK
            real = (key >= 0) & (key < T)
            hot = real & (((l_io < D_HEAD) & (l_io == blk))
                          | ((l_io >= D_HEAD) & (l_io < 2 * D_HEAD) & (l_io - D_HEAD == blk)))
            hot = hot | ((~real) & (l_io == 3 * D_HEAD)) | (real & (l_io > 3 * D_HEAD) & (l_io <= 3 * D_HEAD + 2))
            pat = jnp.where(hot, 1.0, 0.0).astype(BF16)
            ksa[rows, :] = pat
            kwa[rows, :] = pat
            vsa[rows, :] = vpat
            vwa[rows, :] = vpat
            return carry

        lax.fori_loop(0, n_chunks, fill, 0)
        ksa[KPAD:KPAD + T, 2 * D_HEAD:3 * D_HEAD] = ks_ref[0, 0]
        kwa[KPAD:KPAD + T, 2 * D_HEAD:3 * D_HEAD] = kw_ref[0, 0]
        vsa[KPAD:KPAD + T, 0:D_HEAD] = vs_ref[0, 0]
        vwa[KPAD:KPAD + T, 0:D_HEAD] = vw_ref[0, 0]
        lhs_scr[:, 0:LANES] = jnp.zeros((ROWS, LANES), BF16)
        lc = lax.broadcasted_iota(jnp.int32, (ncp, VAUG), 1)
        kca[...] = jnp.zeros((ncp, VAUG), BF16)
        kca[:, 0:D_HEAD] = kc_ref[0, 0]
        vca[...] = jnp.where(lc == D_HEAD, 1.0, 0.0).astype(BF16)
        vca[:, 0:D_HEAD] = vc_ref[0, 0]

    q4 = q_ref[0].reshape(ROWS, D_HEAD)
    gates = g_ref[0, 0]
    row_blocks = [slice(r, r + RB) for r in range(0, ROWS, RB)]

    gate_rows = jnp.concatenate(
        [gates if hh == 0 else pltpu.roll(gates, LANES - hh * N_BRANCH, 1) for hh in range(NSA_GROUP)], axis=0)

    def gate_col(branch):
        return gate_rows[:, branch:branch + 1]

    lane_c = lax.broadcasted_iota(jnp.int32, (ncp, LANES), 1)
    n_c = lax.broadcasted_iota(jnp.int32, (ncp, LANES), 0)
    e = lane_c - D_HEAD
    base = 4 * first + (4 * SUB - 1)
    ind = (((e >= 0) & (e < CBAND) & (n_c == base - e))
           | ((e >= CBAND) & (e < 2 * CBAND) & (n_c == base - (e - CBAND)))
           | (((e == 2 * CBAND) | (e == 2 * CBAND + 1)) & (n_c < base - (CBAND - 1)))
           | ((e == 2 * CBAND + 2) & (n_c > base)))
    rhs_c = jnp.where(lane_c < D_HEAD, kca[...], jnp.where(ind, 1.0, 0.0).astype(BF16))
    lhsc_scr[...] = cmpl_ref[0]
    lhsc_scr[:, 0:D_HEAD] = q4
    sa_scr[:, 0:ncp] = _halves(_dot_nt, lhsc_scr[...], rhs_c)

    lhs_scr[:, LANES:2 * LANES] = qhalf_ref[0]
    lhs_scr[:, LANES:LANES + D_HEAD] = q4
    lhs_win = jnp.concatenate([jnp.zeros((ROWS, LANES), BF16), lhs_scr[:, LANES:2 * LANES]], axis=1)
    win_start = pl.multiple_of(KPAD + (first - WIN_BLOCKS) * SEL_BLOCK, KT)
    sw_scr[...] = _halves(_dot_nt, lhs_win, kwa[pl.ds(win_start, KS + KT), :])

    for i, rows in enumerate(row_blocks):
        s = [sa_scr[rows, c * LANES:(c + 1) * LANES] for c in range(ncp // LANES)]
        m = _row_max(s)
        p = [jnp.where(t > 0.5 * NEG, jnp.exp2(t - m), 0.0) for t in s]
        l = jnp.sum(functools.reduce(jnp.add, p), axis=1, keepdims=True)
        inv = jnp.where(l > 0.0, 1.0 / l, 0.0)
        inv_scr[rows] = inv
        inv_b = jnp.broadcast_to(inv, (RB, LANES))
        tok = slice((i % (TQ // RB)) * RB, (i % (TQ // RB) + 1) * RB)
        for c in range(ncp // LANES):
            lanes = slice(c * LANES, (c + 1) * LANES)
            pb_scr[rows, lanes] = p[c].astype(BF16)
            if i < TQ // RB:
                ps_scr[tok, lanes] = p[c] * inv_b
            else:
                ps_scr[tok, lanes] += p[c] * inv_b
    acc_c = _halves(_dot, pb_scr[:, 0:ncp], vca[...])
    out_scr[...] = acc_c[:, :D_HEAD] * (gate_col(0) * inv_scr[...])
    ps_hi, ps_lo = _split_bf16(ps_scr[...])
    ovt = ovt_ref[...]
    imp_t = _dot_nt(ovt, ps_hi) + _dot_nt(ovt, ps_lo)

    j_t = lax.broadcasted_iota(jnp.int32, (MAX_BLOCKS, TQ), 0)
    cur = first + lax.broadcasted_iota(jnp.int32, (MAX_BLOCKS, TQ), 1) // SEL_BLOCK
    forced = (j_t == 0) | (j_t == cur) | (j_t == cur - 1)
    score_t = jnp.where(forced, FORCED_SCORE, jnp.where(j_t <= cur, imp_t, NEG))
    sc_scr[...] = score_t

    cnt_scr[...] = jnp.zeros((MAX_BLOCKS, TQ), jnp.int32)
    sub_io = lax.broadcasted_iota(jnp.int32, (8, TQ), 0)
    n_groups = MAX_BLOCKS // 8
    for grp in range(n_groups):
        @pl.when(grp * 8 < first + SUB)
        def _count_group():
            targets = [score_t[8 * jg:8 * jg + 8] for jg in range(n_groups)]
            acc = [jnp.zeros((8, TQ), jnp.int32) for _ in range(n_groups)]
            for k in range(8):
                row_i = jnp.broadcast_to(sc_scr[grp * 8 + k:grp * 8 + k + 1, :], (8, TQ))
                for jg in range(n_groups):
                    if jg > grp:
                        beats = row_i >= targets[jg]
                    elif jg < grp:
                        beats = row_i > targets[jg]
                    else:
                        beats = (row_i > targets[jg]) | ((row_i == targets[jg]) & (sub_io > k))
                    acc[jg] = acc[jg] + beats.astype(jnp.int32)
            cnt_scr[...] += jnp.concatenate(acc, axis=0)

    cnt = cnt_scr[...]
    mask_t = jnp.where((cnt < SEL_TOPK) & (j_t <= cur), 0.0, NEG).astype(BF16)
    mask = _dot_nt(eye_ref[...], mask_t).astype(BF16)

    for hh in range(NSA_GROUP):
        lhs_scr[hh * TQ:(hh + 1) * TQ, 0:MAX_BLOCKS] = mask
    m_scr[...] = jnp.full((ROWS, LANES), NEG, F32)
    acc_scr[...] = jnp.zeros((ROWS, VAUG), F32)

    def tile_start(k):
        return pl.multiple_of(KPAD + (first - BLOCKS_PER_TILE) * SEL_BLOCK - k * KS, KT)

    def scores(k, dst):
        dst[...] = _halves(_dot_nt, lhs_scr[...], ksa[pl.ds(tile_start(k), KS), :])

    def consume(k, src, p_scr, with_table):
        n_lt = KS // LANES
        for rows in row_blocks:
            s = [src[rows, c * LANES:(c + 1) * LANES] for c in range(n_lt)]
            if with_table:
                s = [s[c] + near_ref[0, rows, c * LANES:(c + 1) * LANES] for c in range(n_lt)]
            m_prev = m_scr[rows]
            m = jnp.maximum(m_prev, _row_max(s))
            al_scr[rows] = jnp.exp2(m_prev - m)
            m_scr[rows] = m
            for c in range(n_lt):
                p_scr[rows, c * LANES:(c + 1) * LANES] = jnp.exp2((s[c] - m).astype(BF16))
        acc_scr[...] = acc_scr[...] * al_scr[...] + _dot(p_scr[:, 0:KS], vsa[pl.ds(tile_start(k), KS), :])

    scores(0, sa_scr)

    for rows in row_blocks:
        s = [sw_scr[rows, c * LANES:(c + 1) * LANES] + win_ref[0, rows, c * LANES:(c + 1) * LANES]
             for c in range((KS + KT) // LANES)]
        m = _row_max(s)
        for c in range((KS + KT) // LANES):
            pw_scr[rows, c * LANES:(c + 1) * LANES] = jnp.exp2((s[c] - m).astype(BF16))
    acc_w = _halves(_dot, pw_scr[...], vwa[pl.ds(win_start, KS + KT), :])
    out_scr[...] += acc_w[:, :D_HEAD] * (gate_col(2) / acc_w[:, D_HEAD:D_HEAD + 1])

    def pair(with_table, u, carry):
        scores(2 * u + 1, sb_scr)
        consume(2 * u, sa_scr, pa_scr, with_table)
        scores(2 * u + 2, sa_scr)
        consume(2 * u + 1, sb_scr, pb_scr, False)
        return carry

    n_tiles = (qt + 2) // 2
    pair(True, 0, 0)
    lax.fori_loop(1, jnp.maximum(n_tiles // 2, 1), functools.partial(pair, False), 0)

    @pl.when((n_tiles % 2 == 1) & (n_tiles >= 3))
    def _odd_tile():
        consume(n_tiles - 1, sa_scr, pa_scr, False)

    acc_s = acc_scr[...]

    o = out_scr[...] + acc_s[:, :D_HEAD] * (gate_col(1) / acc_s[:, D_HEAD:D_HEAD + 1])
    for hh in range(NSA_GROUP):
        o_ref[0, :, hh * D_HEAD:(hh + 1) * D_HEAD] = o[hh * TQ:(hh + 1) * TQ].astype(BF16)


def _nsa_call(q, kcmp, vcmp, ks, vs, kw, vw, gates, near_t, win_t, q_half, cmp_lhs, ovt, eye):
    B, _, T, _ = q.shape
    G = NSA_KV_HEADS
    ncp = kcmp.shape[2]
    rows_kv = KPAD + T
    per_bg =lambda a: pl.BlockSpec((1, 1) + a.shape[2:], lambda b, g, i: (b, g, 0, 0))
    per_g = lambda a: pl.BlockSpec((1,) + a.shape[1:], lambda b, g, i: (g, 0, 0))
    const2 = lambda a: pl.BlockSpec(a.shape, lambda b, g, i: (0, 0))
    return pl.pallas_call(
        _nsa_kernel,
        grid=(B, G, T // TQ),
        in_specs=[
            pl.BlockSpec((1, NSA_GROUP, TQ, D_HEAD), lambda b, g, i: (b, g, i, 0)),
            per_bg(kcmp), per_bg(vcmp), per_bg(ks), per_bg(vs), per_bg(kw), per_bg(vw),
            pl.BlockSpec((1, 1, TQ, LANES), lambda b, g, i: (b, g, i, 0)),
            per_g(near_t), per_g(win_t), per_g(q_half), per_g(cmp_lhs), const2(ovt), const2(eye),
        ],
        out_specs=pl.BlockSpec((1, TQ, NSA_GROUP * D_HEAD), lambda b, g, i: (b, i, g)),
        out_shape=jax.ShapeDtypeStruct((B, T, D_NSA), BF16),
        scratch_shapes=[
            pltpu.VMEM((rows_kv, KAUG), BF16), pltpu.VMEM((rows_kv, VAUG), BF16),
            pltpu.VMEM((rows_kv, KAUG), BF16), pltpu.VMEM((rows_kv, VAUG), BF16),
            pltpu.VMEM((ncp, VAUG), BF16), pltpu.VMEM((ncp, VAUG), BF16),
            pltpu.VMEM((MAX_BLOCKS, TQ), F32),
            pltpu.VMEM((ROWS, KAUG), BF16), pltpu.VMEM((ROWS, LANES), BF16),
            pltpu.VMEM((ROWS, LANES), F32), pltpu.VMEM((ROWS, LANES), F32), pltpu.VMEM((ROWS, 1), F32),
            pltpu.VMEM((ROWS, VAUG), F32),
            pltpu.VMEM((ROWS, KS), F32), pltpu.VMEM((ROWS, KS), F32),
            pltpu.VMEM((ROWS, KS), BF16), pltpu.VMEM((ROWS, KS), BF16),
            pltpu.VMEM((ROWS, KS + KT), F32), pltpu.VMEM((ROWS, KS + KT), BF16),
            pltpu.VMEM((TQ, ncp), F32), pltpu.VMEM((ROWS, D_HEAD), F32), pltpu.VMEM((MAX_BLOCKS, TQ), jnp.int32),
        ],
        compiler_params=pltpu.CompilerParams(
            dimension_semantics=("arbitrary", "arbitrary", "arbitrary"), vmem_limit_bytes=VMEM_LIMIT),
        name="nsa",
    )(q, kcmp, vcmp, ks, vs, kw, vw, gates, near_t, win_t, q_half, cmp_lhs, ovt, eye)


def _ffn_kernel(x_ref, on_ref, os_ref, wo_n_ref, wo_s_ref, fn_ref, wup_ref, cw_ref, cb_ref, wdn_ref, pm_ref,
                out_ref, prev_ref, perm_ref, h_ref, ua_scr, ub_scr, act_ref):
    tm = x_ref.shape[1]
    seg = tm // 8
    n_chunks = wup_ref.shape[1]

    @pl.when(pl.program_id(1) == 0)
    def _start_of_sequence():
        prev_ref[...] = jnp.zeros(prev_ref.shape, F32)

    x1 = x_ref[0] + _dot(on_ref[0], wo_n_ref[...]) + _dot(os_ref[0], wo_s_ref[...])
    ms = jnp.mean(x1 * x1, axis=-1, keepdims=True)
    hn = (x1 * lax.rsqrt(ms + EPS) * fn_ref[...]).astype(BF16)
    out_ref[0] = x1
    n_lt = x1.shape[1] // LANES
    h_ref[...] = _halves(_dot, pm_ref[...], hn).astype(BF16)

    def up(c, dst):
        h = h_ref[...]
        for half in range(2):
            dst[half] = _dot(h, wup_ref[half, c])

    first_sublane = lax.broadcasted_iota(jnp.int32, (8, FFN_CHUNK), 0) == 0

    def conv_gate(c, src):
        for r0 in range(0, tm, FFN_ROWS):
            ys = []
            for half in range(2):
                w = cw_ref[half, c]
                x0 = src[half, r0:r0 + FFN_ROWS]
                if r0 == 0:
                    wrap = [jnp.where(first_sublane, pltpu.roll(prev_ref[half, c, 8 * k:8 * k + 8], 1, 0),
                                      pltpu.roll(src[half, tm - 16 + 8 * k:tm - 8 + 8 * k], 1, 0)) for k in range(2)]
                    x1_ = jnp.concatenate([wrap[1], src[half, 0:FFN_ROWS - 8]], axis=0)
                    x2_ = jnp.concatenate([wrap[0], wrap[1], src[half, 0:FFN_ROWS - 16]], axis=0)
                else:
                    x1_ = src[half, r0 - 8:r0 + FFN_ROWS - 8]
                    x2_ = src[half, r0 - 16:r0 + FFN_ROWS - 16]
                ys.append(x0 * w[2:3] + x1_ * w[1:2] + x2_ * w[0:1] + cb_ref[half, c])
            a, g = ys
            col = pl.multiple_of(c * FFN_CHUNK, FFN_CHUNK)
            act_ref[r0:r0 + FFN_ROWS, pl.ds(col, FFN_CHUNK)] = ((g + g * jnp.tanh(g)) * a).astype(BF16)
        for half in range(2):
            prev_ref[half, c] = src[half, tm - 16:tm]

    up(0, ua_scr)

    def pair(u, carry):
        up(2 * u + 1, ub_scr)
        conv_gate(2 * u, ua_scr)
        up(2 * u + 2, ua_scr)
        conv_gate(2 * u + 1, ub_scr)
        return carry

    lax.fori_loop(0, (n_chunks - 1) // 2, pair, 0)
    conv_gate(n_chunks - 1, ua_scr)
    y = _halves(_dot, act_ref[...], wdn_ref[...])
    for c in range(n_lt):
        perm_ref[c] = y[:, c * LANES:(c + 1) * LANES]
    for sgm in range(8):
        natural = jnp.concatenate([perm_ref[c, pl.ds(sgm, seg, stride=8), :] for c in range(n_lt)], axis=1)
        out_ref[0, sgm * seg:(sgm + 1) * seg, :] += natural


def _ffn_call(x, o_nsa, o_sgu, wo_n, wo_s, ffn_norm, w_up, conv_w, conv_b, w_down, tm):
    B, T, _ = x.shape
    r = np.arange(tm)
    perm = jnp.asarray(((r % 8) * (tm // 8) + r // 8)[:, None] == r[None, :], BF16)
    n_chunks = w_up.shape[1]
    assert n_chunks % 2 == 1
    resident =lambda a: pl.BlockSpec(a.shape, lambda b, i: (0,) * a.ndim, pipeline_mode=pl.Buffered(1))
    tile = lambda w: pl.BlockSpec((1, tm, w), lambda b, i: (b, i, 0))
    return pl.pallas_call(
        _ffn_kernel,
        grid=(B, T // tm),
        in_specs=[tile(D_MODEL), tile(D_NSA), tile(D_GMLP), resident(wo_n), resident(wo_s),
                  resident(ffn_norm), resident(w_up), resident(conv_w), resident(conv_b), resident(w_down),
                  resident(perm)],
        out_specs=tile(D_MODEL),
        out_shape=jax.ShapeDtypeStruct((B, T, D_MODEL), F32),
        scratch_shapes=[pltpu.VMEM((2, n_chunks, 16, FFN_CHUNK), F32), pltpu.VMEM((D_MODEL // LANES, tm, LANES), F32),
                        pltpu.VMEM((tm, D_MODEL), BF16),
                        pltpu.VMEM((2, tm, FFN_CHUNK), F32), pltpu.VMEM((2, tm, FFN_CHUNK), F32),
                        pltpu.VMEM((tm, D_FF), BF16)],
        compiler_params=pltpu.CompilerParams(
            dimension_semantics=("arbitrary", "arbitrary"), vmem_limit_bytes=VMEM_LIMIT),
        name="ffn",
    )(x, o_nsa, o_sgu, wo_n, wo_s, ffn_norm, w_up, conv_w, conv_b, w_down, perm)


def _mixers(x, rel_bias, attn_norm, w_in, q_norm, k_norm_cmp, k_norm_slc, k_norm_win,
            cmp_pe_k, cmp_w1_k, cmp_b1_k, cmp_w2_k, cmp_pe_v, cmp_w1_v, cmp_b1_v, cmp_w2_v,
            sgu_norm, sgu_w, sgu_b):
    B, T, _ = x.shape
    assert T % 512 == 0 and T // SEL_BLOCK <= MAX_BLOCKS and (T // CMP_STRIDE) % LANES == 0
    ncp = T // CMP_STRIDE
    G, R = NSA_KV_HEADS, NSA_GROUP

    o_kv = D_NSA
    o_g = D_NSA + 6 * D_KV
    o_uv = o_g + N_GATES
    w_q = w_in[:, :o_kv].astype(BF16)
    w_kv = w_in[:, o_kv:o_g].astype(BF16)
    per_group = NSA_GROUP * N_BRANCH
    w_g = jnp.pad(w_in[:, o_g:o_uv].reshape(D_MODEL, G, per_group), ((0, 0), (0, 0), (0, LANES - per_group)))
    w_g = w_g.reshape(D_MODEL, G * LANES).astype(BF16)
    w_uv = w_in[:, o_uv:].astype(BF16)
    qn = (jnp.tile(q_norm, NSA_HEADS) * (D_HEAD ** -0.5 * LOG2E)).reshape(1, D_NSA)
    ksn = jnp.tile(k_norm_slc, G).reshape(1, D_KV)
    kwn = jnp.tile(k_norm_win, G).reshape(1, D_KV)
    sgu_bt = jnp.repeat(sgu_b.T, GMLP_GROUP_DIM, axis=1)
    grp = np.arange(D_NSA) // D_HEAD
    bd = jnp.asarray(grp[:, None] == grp[None, :], BF16)

    q, kc, vc, ks, vs, kw, vw, gates, o_sgu = _proj_call(
        x, attn_norm.reshape(1, D_MODEL), w_q, w_kv, w_g, w_uv, qn, ksn, kwn,
        sgu_norm.reshape(1, D_GMLP), sgu_w, sgu_bt, bd, tm=1024)

    half = CMP_STRIDE * D_HEAD
    kcmp, vcmp = _compress_call(
        kc.reshape(B, G, ncp, half), vc.reshape(B, G, ncp, half),
        cmp_pe_k.reshape(2, half), cmp_w1_k.astype(BF16), cmp_b1_k.reshape(1, CMP_HIDDEN),
        cmp_w2_k.astype(BF16), k_norm_cmp.reshape(1, D_HEAD),
        cmp_pe_v.reshape(2, half), cmp_w1_v.astype(BF16), cmp_b1_v.reshape(1, CMP_HIDDEN),
        cmp_w2_v.astype(BF16))

    near_t, win_t, q_half, cmp_lhs = _bias_tables(rel_bias)
    jj = np.arange(MAX_BLOCKS)[:, None]
    nn = np.arange(ncp)[None, :]
    ovt = jnp.asarray((nn >= 4 * jj - 1) & (nn <= 4 * jj + 3), BF16)
    eye = jnp.asarray(np.eye(TQ), BF16)

    o_nsa = _nsa_call(q, kcmp, vcmp, ks, vs, kw, vw, gates, near_t, win_t, q_half, cmp_lhs, ovt, eye)
    return o_nsa, o_sgu


def _layer(x, rel_bias, attn_norm, w_in, q_norm, k_norm_cmp, k_norm_slc, k_norm_win,
           cmp_pe_k, cmp_w1_k, cmp_b1_k, cmp_w2_k, cmp_pe_v, cmp_w1_v, cmp_b1_v, cmp_w2_v,
           sgu_norm, sgu_w, sgu_b, w_out, ffn_norm, w_up, conv_w, conv_b, w_down):
    o_nsa, o_sgu = _mixers(x, rel_bias, attn_norm, w_in, q_norm, k_norm_cmp, k_norm_slc, k_norm_win,
                           cmp_pe_k, cmp_w1_k, cmp_b1_k, cmp_w2_k, cmp_pe_v, cmp_w1_v, cmp_b1_v, cmp_w2_v,
                           sgu_norm, sgu_w, sgu_b)
    wo = w_out.astype(BF16)
    n_chunks = D_FF // FFN_CHUNK
    wu = jnp.transpose(w_up.astype(BF16).reshape(D_MODEL, 2, n_chunks, FFN_CHUNK), (1, 2, 0, 3))
    half_gate = jnp.asarray([1.0, 0.5], F32).reshape(2, 1, 1, 1)
    cw = jnp.transpose(conv_w.reshape(conv_w.shape[0], 2, n_chunks, FFN_CHUNK), (1, 2, 0, 3)) * half_gate
    cb = conv_b.reshape(2, n_chunks, 1, FFN_CHUNK) * half_gate
    wd = w_down.astype(BF16)
    return _ffn_call(x, o_nsa, o_sgu, wo[:D_NSA], wo[D_NSA:], ffn_norm.reshape(1, D_MODEL), wu, cw, cb, wd, tm=512)


def kernel(x, rel_bias, attn_norm, w_in, q_norm, k_norm_cmp, k_norm_slc, k_norm_win,
           cmp_pe_k, cmp_w1_k, cmp_b1_k, cmp_w2_k, cmp_pe_v, cmp_w1_v, cmp_b1_v, cmp_w2_v,
           sgu_norm, sgu_w, sgu_b, w_out, ffn_norm, w_up, conv_w, conv_b, w_down):
    depth = attn_norm.shape[0]
    for l in range(depth):
        x = _layer(x, rel_bias, attn_norm[l], w_in[l], q_norm[l], k_norm_cmp[l], k_norm_slc[l], k_norm_win[l],
                   cmp_pe_k[l], cmp_w1_k[l], cmp_b1_k[l], cmp_w2_k[l], cmp_pe_v[l], cmp_w1_v[l], cmp_b1_v[l],
                   cmp_w2_v[l], sgu_norm[l], sgu_w[l], sgu_b[l], w_out[l], ffn_norm[l], w_up[l], conv_w[l],
                   conv_b[l], w_down[l])
    return x
```

```python
import functools
import math

import numpy as np
import jax
import jax.numpy as jnp
from jax import lax
from jax.experimental import pallas as pl
from jax.experimental.pallas import tpu as pltpu

F32 = jnp.float32
BF16 = jnp.bfloat16

D_MODEL = 1024
D_HEAD = 64
NSA_HEADS = 8
NSA_KV_HEADS = 2
NSA_GROUP = NSA_HEADS // NSA_KV_HEADS
D_NSA = NSA_HEADS * D_HEAD
D_KV = NSA_KV_HEADS * D_HEAD
N_BRANCH = 3
N_GATES = NSA_HEADS * N_BRANCH
CMP_BLOCK = 32
CMP_STRIDE = 16
CMP_HIDDEN = 256
SEL_BLOCK = 64
SEL_TOPK = 16
WINDOW = 512
GMLP_GROUPS = 8
GMLP_GROUP_DIM = 64
D_GMLP = GMLP_GROUPS * GMLP_GROUP_DIM
CHUNK = 128
D_MIX = D_NSA + D_GMLP
REL_BUCKETS = 32
REL_MAX_DIST = 128
D_FF = 2816
EPS = 1e-6
NEG = -1e30
FORCED_SCORE = 1e4
LOG2E = 1.4426950408889634

LANES = 128
TQ = 256
SUB = TQ // SEL_BLOCK
ROWS = NSA_GROUP * TQ
KT = 256
KS = 2 * KT
RB = 64
BLOCKS_PER_TILE = KT // SEL_BLOCK
WIN_BLOCKS = WINDOW // SEL_BLOCK
KPAD = 1280
N_TOEPLITZ = 3
MAX_BLOCKS = 64
KAUG = 256
VAUG = 128
CBAND = 28
FFN_CHUNK = 256
FFN_ROWS = 64
VMEM_LIMIT = 56 * 1024 * 1024

_NT = (((1,), (1,)), ((), ()))


def _dot(a, b):
    return jnp.dot(a, b, preferred_element_type=F32)


def _dot_nt(a, b):
    return lax.dot_general(a, b, _NT, preferred_element_type=F32)


def _halves(dot, a, b):
    h = a.shape[0] // 2
    return jnp.concatenate([dot(a[:h], b), dot(a[h:], b)], axis=0)


def _split_bf16(x):
    hi = x.astype(BF16)
    lo = (x - hi.astype(F32)).astype(BF16)
    return hi, lo


def _gelu_tanh(x):
    return 0.5 * x * (1.0 + jnp.tanh(0.7978845608028654 * (x + 0.044715 * (x * x * x))))


def _sigmoid(x):
    return 0.5 * (1.0 + jnp.tanh(0.5 * x))


def _group_mean_sq(t, ones_blockdiag, width):
    t2 = t * t
    hi, lo = _split_bf16(t2)
    return (_dot(hi, ones_blockdiag) + _dot(lo, ones_blockdiag)) * (1.0 / width)


def _proj_kernel(x_ref, an_ref, wq_ref, wkv_ref, wg_ref, wuv_ref, qn_ref, ksn_ref, kwn_ref,
                 sgun_ref, sguw_ref, sgub_ref, bd_ref,
                 q_out, kc_out, vc_out, ks_out, vs_out, kw_out, vw_out, g_out, sgu_out):
    tm = x_ref.shape[1]
    x = x_ref[0]
    ms = jnp.mean(x * x, axis=-1, keepdims=True)
    h = (x * lax.rsqrt(ms + EPS) * an_ref[...]).astype(BF16)

    bd = bd_ref[...]
    q = _halves(_dot, h, wq_ref[...])
    qn = q * lax.rsqrt(_group_mean_sq(q, bd, D_HEAD) + EPS) * qn_ref[...]
    for hh in range(NSA_HEADS):
        q_out[0, hh] = qn[:, hh * D_HEAD:(hh + 1) * D_HEAD].astype(BF16)

    kv = _halves(_dot, h, wkv_ref[...])
    bd_kv = bd[:D_KV, :D_KV]
    kc = kv[:, 0 * D_KV:1 * D_KV]
    vc = kv[:, 1 * D_KV:2 * D_KV]
    ks = kv[:, 2 * D_KV:3 * D_KV]
    vs = kv[:, 3 * D_KV:4 * D_KV]
    kw = kv[:, 4 * D_KV:5 * D_KV]
    vw = kv[:, 5 * D_KV:6 * D_KV]
    ks = ks * lax.rsqrt(_group_mean_sq(ks, bd_kv, D_HEAD) + EPS) * ksn_ref[...]
    kw = kw * lax.rsqrt(_group_mean_sq(kw, bd_kv, D_HEAD) + EPS) * kwn_ref[...]
    for t, o_ref in ((kc, kc_out), (vc, vc_out), (ks, ks_out), (vs, vs_out), (kw, kw_out), (vw, vw_out)):
        for g in range(NSA_KV_HEADS):
            o_ref[0, g] = t[:, g * D_HEAD:(g + 1) * D_HEAD].astype(BF16)

    gates = _sigmoid(_dot(h, wg_ref[...]))
    for g in range(NSA_KV_HEADS):
        g_out[0, g] = gates[:, g * LANES:(g + 1) * LANES]

    uv = _gelu_tanh(_halves(_dot, h, wuv_ref[...]))
    u = uv[:, :D_GMLP]
    v = uv[:, D_GMLP:]
    vms = jnp.mean(v * v, axis=-1, keepdims=True)
    vb = (v * lax.rsqrt(vms + EPS) * sgun_ref[...]).astype(BF16)

    row = lax.broadcasted_iota(jnp.int32, (CHUNK, CHUNK), 0)
    col = lax.broadcasted_iota(jnp.int32, (CHUNK, CHUNK), 1)
    tril = col <= row
    w_tril = [jnp.where(tril, sguw_ref[g], 0.0).astype(BF16) for g in range(GMLP_GROUPS)]
    w_pair = [jnp.concatenate([w_tril[2 * p], w_tril[2 * p + 1]], axis=1) for p in range(D_GMLP // LANES)]
    first_half = lax.broadcasted_iota(jnp.int32, (CHUNK, LANES), 1) < GMLP_GROUP_DIM
    zero = jnp.zeros((CHUNK, LANES), BF16)
    for c in range(tm // CHUNK):
        rows = slice(c * CHUNK, (c + 1) * CHUNK)
        zs = []
        for p in range(D_GMLP // LANES):
            blk = vb[rows, p * LANES:(p + 1) * LANES]
            rhs = jnp.concatenate([jnp.where(first_half, blk, zero), jnp.where(first_half, zero, blk)], axis=0)
            zs.append(_dot(w_pair[p], rhs))
        z = jnp.concatenate(zs, axis=1) + sgub_ref[...]
        sgu_out[0, rows, :] = (u[rows, :] * z).astype(BF16)


def _proj_call(x, attn_norm, w_q, w_kv, w_g, w_uv, qn, ksn, kwn, sgu_norm, sgu_w, sgu_bt, bd, tm):
    B, T, _ = x.shape
    const2 = lambda b, i: (0, 0)
    const3 = lambda b, i: (0, 0, 0)
    head_spec = lambda nh: pl.BlockSpec((1, nh, tm, D_HEAD), lambda b, i: (b, 0, i, 0))
    kv_shape = jax.ShapeDtypeStruct((B, NSA_KV_HEADS, T, D_HEAD), BF16)
    return pl.pallas_call(
        _proj_kernel,
        grid=(B, T // tm),
        in_specs=[
            pl.BlockSpec((1, tm, D_MODEL), lambda b, i: (b, i, 0)),
            pl.BlockSpec((1, D_MODEL), const2),
            pl.BlockSpec(w_q.shape, const2),
            pl.BlockSpec(w_kv.shape, const2),
            pl.BlockSpec(w_g.shape, const2),
            pl.BlockSpec(w_uv.shape, const2),
            pl.BlockSpec((1, D_NSA), const2),
            pl.BlockSpec((1, D_KV), const2),
            pl.BlockSpec((1, D_KV), const2),
            pl.BlockSpec((1, D_GMLP), const2),
            pl.BlockSpec(sgu_w.shape, const3),
            pl.BlockSpec(sgu_bt.shape, const2),
            pl.BlockSpec(bd.shape, const2),
        ],
        out_specs=[
            head_spec(NSA_HEADS),
            head_spec(NSA_KV_HEADS), head_spec(NSA_KV_HEADS), head_spec(NSA_KV_HEADS),
            head_spec(NSA_KV_HEADS), head_spec(NSA_KV_HEADS), head_spec(NSA_KV_HEADS),
            pl.BlockSpec((1, NSA_KV_HEADS, tm, LANES), lambda b, i: (b, 0, i, 0)),
            pl.BlockSpec((1, tm, D_GMLP), lambda b, i: (b, i, 0)),
        ],
        out_shape=[
            jax.ShapeDtypeStruct((B, NSA_HEADS, T, D_HEAD), BF16),
            kv_shape, kv_shape, kv_shape, kv_shape, kv_shape, kv_shape,
            jax.ShapeDtypeStruct((B, NSA_KV_HEADS, T, LANES), F32),
            jax.ShapeDtypeStruct((B, T, D_GMLP), BF16),
        ],
        compiler_params=pltpu.CompilerParams(
            dimension_semantics=("arbitrary", "arbitrary"), vmem_limit_bytes=VMEM_LIMIT),
        name="proj",
    )(x, attn_norm, w_q, w_kv, w_g, w_uv, qn, ksn, kwn, sgu_norm, sgu_w, sgu_bt, bd)


def _compress_one(tok_ref, pe_ref, w1_ref, b1_ref, w2_ref):
    half = CMP_STRIDE * D_HEAD
    tok = tok_ref[0, 0].astype(F32)
    top = (tok + pe_ref[0:1, :]).astype(BF16)
    bot = (tok + pe_ref[1:2, :]).astype(BF16)
    a = _dot(top, w1_ref[:half, :])
    b = _dot(bot, w1_ref[half:, :])
    ncp = a.shape[0]
    pre = a + pltpu.roll(b, ncp - 1, 0) + b1_ref[...]
    hid = _gelu_tanh(pre).astype(BF16)
    return _dot(hid, w2_ref[...])


def _compress_kernel(kc_ref, vc_ref, pek_ref, w1k_ref, b1k_ref, w2k_ref, kn_ref,
                     pev_ref, w1v_ref, b1v_ref, w2v_ref, k_out, v_out):
    k = _compress_one(kc_ref, pek_ref, w1k_ref, b1k_ref, w2k_ref)
    kms = jnp.mean(k * k, axis=-1, keepdims=True)
    k_out[0, 0] = (k * lax.rsqrt(kms + EPS) * kn_ref[...]).astype(BF16)
    v_out[0, 0] = _compress_one(vc_ref, pev_ref, w1v_ref, b1v_ref, w2v_ref).astype(BF16)


def _compress_call(kc, vc, pek, w1k, b1k, w2k, kn, pev, w1v, b1v, w2v):
    B, G, ncp, width = kc.shape
    const2 = lambda b, g: (0, 0)
    tok_spec = pl.BlockSpec((1, 1, ncp, width), lambda b, g: (b, g, 0, 0))
    out_spec = pl.BlockSpec((1, 1, ncp, D_HEAD), lambda b, g: (b, g, 0, 0))
    full = lambda a: pl.BlockSpec(a.shape, const2)
    out_shape = jax.ShapeDtypeStruct((B, G, ncp, D_HEAD), BF16)
    return pl.pallas_call(
        _compress_kernel,
        grid=(B, G),
        in_specs=[tok_spec, tok_spec, full(pek), full(w1k), full(b1k), full(w2k), full(kn),
                  full(pev), full(w1v), full(b1v), full(w2v)],
        out_specs=[out_spec, out_spec],
        out_shape=[out_shape, out_shape],
        compiler_params=pltpu.CompilerParams(
            dimension_semantics=("arbitrary", "arbitrary"), vmem_limit_bytes=VMEM_LIMIT),
        name="compress",
    )(kc, vc, pek, w1k, b1k, w2k, kn, pev, w1v, b1v, w2v)


def _rel_bucket_np(dist):
    max_exact = REL_BUCKETS // 2
    d = np.maximum(dist, 1).astype(np.float32)
    log_b = max_exact + (np.log(d / np.float32(max_exact)) / np.float32(math.log(REL_MAX_DIST / max_exact))
                         * np.float32(REL_BUCKETS - max_exact)).astype(np.int32)
    log_b = np.clip(log_b, max_exact, REL_BUCKETS - 1)
    return np.where(dist < max_exact, np.maximum(dist, 0), log_b)


def _bias_tables(rel_bias):
    rb = rel_bias.astype(F32) * LOG2E
    qq = np.arange(SEL_BLOCK)[:, None]
    kp = np.arange(SEL_BLOCK)[None, :]
    tile_d = [m * SEL_BLOCK + qq - kp for m in range(N_TOEPLITZ)]
    s_ = np.arange(SUB)[:, None, None]
    e_ = np.arange(CBAND)[None, None, :]
    dist_c = qq[None, :, :] + CMP_STRIDE * (e_ - 4 * (SUB - 1) + 4 * s_) - (3 * CMP_STRIDE + CMP_BLOCK - 1)
    all_d = np.concatenate([d.reshape(-1) for d in tile_d] + [dist_c.reshape(-1)])
    onehot = np.eye(REL_BUCKETS, dtype=np.float32)[_rel_bucket_np(all_d)]
    vals = jnp.dot(jnp.asarray(onehot), rb, precision=lax.Precision.HIGHEST)
    vals = jnp.where(jnp.asarray(all_d >= 0)[:, None], vals, NEG)
    n_t = N_TOEPLITZ * SEL_BLOCK * SEL_BLOCK
    toep = jnp.transpose(vals[:n_t].reshape(N_TOEPLITZ, SEL_BLOCK, SEL_BLOCK, NSA_HEADS), (0, 3, 1, 2))
    band = jnp.transpose(vals[n_t:].reshape(SUB, SEL_BLOCK, CBAND, NSA_HEADS), (3, 0, 1, 2))
    far = rb[REL_BUCKETS - 1]
    far_tile = jnp.broadcast_to(far[:, None, None], (NSA_HEADS, SEL_BLOCK, SEL_BLOCK))
    neg_tile = jnp.full((NSA_HEADS, SEL_BLOCK, SEL_BLOCK), NEG, F32)
    edge_tile = jnp.where(jnp.asarray(kp > qq)[None], far_tile, NEG)

    def tile(m, windowed):
        if m < 0 or (windowed and m > WIN_BLOCKS):
            return neg_tile
        if windowed and m == WIN_BLOCKS:
            return edge_tile
        return toep[m] if m < N_TOEPLITZ else far_tile

    def table(first_m, n_blocks, windowed):
        rows = [jnp.concatenate([tile(first_m + s - c, windowed) for c in range(n_blocks)], axis=2)
                for s in range(SUB)]
        t = jnp.concatenate(rows, axis=1)
        return t.reshape(NSA_KV_HEADS, ROWS, n_blocks * SEL_BLOCK)

    far_rows = jnp.repeat(far, TQ).reshape(NSA_KV_HEADS, ROWS, 1)
    near_t = table(BLOCKS_PER_TILE, 2 * BLOCKS_PER_TILE, False) - far_rows
    win_t = table(WIN_BLOCKS, WIN_BLOCKS + BLOCKS_PER_TILE, True) - far_rows
    band_t = band.reshape(NSA_KV_HEADS, ROWS, CBAND)
    band_hi = band_t.astype(BF16)
    band_lo = (band_t - band_hi.astype(F32)).astype(BF16)
    far_hi = far_rows.astype(BF16)
    far_lo = (far_rows - far_hi.astype(F32)).astype(BF16)
    q_half = jnp.concatenate([jnp.zeros((NSA_KV_HEADS, ROWS, D_HEAD), BF16),
                              jnp.full((NSA_KV_HEADS, ROWS, 1), NEG, BF16), far_hi, far_lo,
                              jnp.zeros((NSA_KV_HEADS, ROWS, LANES - D_HEAD - 3), BF16)], axis=2)
    neg_col = jnp.full((NSA_KV_HEADS, ROWS, 1), NEG, BF16)
    zeros = jnp.zeros((NSA_KV_HEADS, ROWS, D_HEAD - 2 * CBAND - 3), BF16)
    q_lanes = jnp.zeros((NSA_KV_HEADS, ROWS, D_HEAD), BF16)
    cmp_lhs = jnp.concatenate([q_lanes, band_hi, band_lo, far_hi, far_lo, neg_col, zeros], axis=2)
    return near_t, win_t, q_half, cmp_lhs


def _store_lane_tiles(dst, x):
    for c in range(x.shape[1] // LANES):
        dst[c] = x[:, c * LANES:(c + 1) * LANES]


def _row_max(lane_tiles):
    m = jnp.max(functools.reduce(jnp.maximum, lane_tiles), axis=1, keepdims=True)
    return jnp.broadcast_to(m, lane_tiles[0].shape)


def _nsa_kernel(q_ref, kc_ref, vc_ref, ks_ref, vs_ref, kw_ref, vw_ref, g_ref,
                near_ref, win_ref, qhalf_ref, cmpl_ref, ovt_ref, eye_ref,
                o_ref, ksa, vsa, kwa, vwa, kca, vca, sc_scr, lhs_scr, lhsc_scr, m_scr, al_scr, inv_scr, acc_scr, sa_scr, sb_scr, pa_scr, pb_scr,
                sw_scr, pw_scr, ps_scr, out_scr, cnt_scr):
    qt = pl.program_id(2)
    first = qt * SUB
    T = ks_ref.shape[2]
    ncp = kc_ref.shape[2]

    @pl.when(qt == 0)
    def _init():
        n_chunks = ksa.shape[0] // KT
        r_io = lax.broadcasted_iota(jnp.int32, (KT, KAUG), 0)
        l_io = lax.broadcasted_iota(jnp.int32, (KT, KAUG), 1)
        lv = lax.broadcasted_iota(jnp.int32, (KT, VAUG), 1)
        vpat = jnp.where(lv == D_HEAD, 1.0, 0.0).astype(BF16)

        def fill(c, carry):
            rows = pl.ds(pl.multiple_of(c * KT, KT), KT)
            key = r_io + (c * KT - KPAD)
            blk = key // SEL_BLOCK
            real = (key >= 0) & (key < T)
            hot = real & (((l_io < D_HEAD) & (l_io == blk))
                          | ((l_io >= D_HEAD) & (l_io < 2 * D_HEAD) & (l_io - D_HEAD == blk)))
            hot = hot | ((~real) & (l_io == 3 * D_HEAD)) | (real & (l_io > 3 * D_HEAD) & (l_io <= 3 * D_HEAD + 2))
            pat = jnp.where(hot, 1.0, 0.0).astype(BF16)
            ksa[rows, :] = pat
            kwa[rows, :] = pat
            vsa[rows, :] = vpat
            vwa[rows, :] = vpat
            return carry

        lax.fori_loop(0, n_chunks, fill, 0)
        ksa[KPAD:KPAD + T, 2 * D_HEAD:3 * D_HEAD] = ks_ref[0, 0]
        kwa[KPAD:KPAD + T, 2 * D_HEAD:3 * D_HEAD] = kw_ref[0, 0]
        vsa[KPAD:KPAD + T, 0:D_HEAD] = vs_ref[0, 0]
        vwa[KPAD:KPAD + T, 0:D_HEAD] = vw_ref[0, 0]
        lhs_scr[:, 0:LANES] = jnp.zeros((ROWS, LANES), BF16)
        lc = lax.broadcasted_iota(jnp.int32, (ncp, VAUG), 1)
        kca[...] = jnp.zeros((ncp, VAUG), BF16)
        kca[:, 0:D_HEAD] = kc_ref[0, 0]
        vca[...] = jnp.where(lc == D_HEAD, 1.0, 0.0).astype(BF16)
        vca[:, 0:D_HEAD] = vc_ref[0, 0]

    q4 = q_ref[0].reshape(ROWS, D_HEAD)
    gates = g_ref[0, 0]
    row_blocks = [slice(r, r + RB) for r in range(0, ROWS, RB)]

    gate_rows = jnp.concatenate(
        [gates if hh == 0 else pltpu.roll(gates, LANES - hh * N_BRANCH, 1) for hh in range(NSA_GROUP)], axis=0)

    def gate_col(branch):
        return gate_rows[:, branch:branch + 1]

    lane_c = lax.broadcasted_iota(jnp.int32, (ncp, LANES), 1)
    n_c = lax.broadcasted_iota(jnp.int32, (ncp, LANES), 0)
    e = lane_c - D_HEAD
    base = 4 * first + (4 * SUB - 1)
    ind = (((e >= 0) & (e < CBAND) & (n_c == base - e))
           | ((e >= CBAND) & (e < 2 * CBAND) & (n_c == base - (e - CBAND)))
           | (((e == 2 * CBAND) | (e == 2 * CBAND + 1)) & (n_c < base - (CBAND - 1)))
           | ((e == 2 * CBAND + 2) & (n_c > base)))
    rhs_c = jnp.where(lane_c < D_HEAD, kca[...], jnp.where(ind, 1.0, 0.0).astype(BF16))
    lhsc_scr[...] = cmpl_ref[0]
    lhsc_scr[:, 0:D_HEAD] = q4
    _store_lane_tiles(sa_scr, _halves(_dot_nt, lhsc_scr[...], rhs_c))

    lhs_scr[:, LANES:2 * LANES] = qhalf_ref[0]
    lhs_scr[:, LANES:LANES + D_HEAD] = q4
    lhs_win = jnp.concatenate([jnp.zeros((ROWS, LANES), BF16), lhs_scr[:, LANES:2 * LANES]], axis=1)
    win_start = pl.multiple_of(KPAD + (first - WIN_BLOCKS) * SEL_BLOCK, KT)
    _store_lane_tiles(sw_scr, _halves(_dot_nt, lhs_win, kwa[pl.ds(win_start, KS + KT), :]))

    for i, rows in enumerate(row_blocks):
        s = [sa_scr[c, rows, :] for c in range(ncp // LANES)]
        m = _row_max(s)
        p = [jnp.where(t > 0.5 * NEG, jnp.exp2(t - m), 0.0) for t in s]
        l = jnp.sum(functools.reduce(jnp.add, p), axis=1, keepdims=True)
        inv = jnp.where(l > 0.0, 1.0 / l, 0.0)
        inv_scr[rows] = inv
        inv_b = jnp.broadcast_to(inv, (RB, LANES))
        tok = slice((i % (TQ // RB)) * RB, (i % (TQ // RB) + 1) * RB)
        for c in range(ncp // LANES):
            lanes = slice(c * LANES, (c + 1) * LANES)
            pb_scr[rows, lanes] = p[c].astype(BF16)
            if i < TQ // RB:
                ps_scr[tok, lanes] = p[c] * inv_b
            else:
                ps_scr[tok, lanes] += p[c] * inv_b
    acc_c = _halves(_dot, pb_scr[:, 0:ncp], vca[...])
    out_scr[...] = acc_c[:, :D_HEAD] * (gate_col(0) * inv_scr[...])
    ps_hi, ps_lo = _split_bf16(ps_scr[...])
    ovt = ovt_ref[...]
    imp_t = _dot_nt(ovt, ps_hi) + _dot_nt(ovt, ps_lo)

    j_t = lax.broadcasted_iota(jnp.int32, (MAX_BLOCKS, TQ), 0)
    cur = first + lax.broadcasted_iota(jnp.int32, (MAX_BLOCKS, TQ), 1) // SEL_BLOCK
    forced = (j_t == 0) | (j_t == cur) | (j_t == cur - 1)
    score_t = jnp.where(forced, FORCED_SCORE, jnp.where(j_t <= cur, imp_t, NEG))
    sc_scr[...] = score_t

    cnt_scr[...] = jnp.zeros((MAX_BLOCKS, TQ), jnp.int32)
    sub_io = lax.broadcasted_iota(jnp.int32, (8, TQ), 0)
    n_groups = MAX_BLOCKS // 8
    for grp in range(n_groups):
        @pl.when(grp * 8 < first + SUB)
        def _count_group():
            targets = [score_t[8 * jg:8 * jg + 8] for jg in range(n_groups)]
            acc = [jnp.zeros((8, TQ), jnp.int32) for _ in range(n_groups)]
            for k in range(8):
                row_i = jnp.broadcast_to(sc_scr[grp * 8 + k:grp * 8 + k + 1, :], (8, TQ))
                for jg in range(n_groups):
                    if jg > grp:
                        beats = row_i >= targets[jg]
                    elif jg < grp:
                        beats = row_i > targets[jg]
                    else:
                        beats = (row_i > targets[jg]) | ((row_i == targets[jg]) & (sub_io > k))
                    acc[jg] = acc[jg] + beats.astype(jnp.int32)
            cnt_scr[...] += jnp.concatenate(acc, axis=0)

    cnt = cnt_scr[...]
    mask_t = jnp.where((cnt < SEL_TOPK) & (j_t <= cur), 0.0, NEG).astype(BF16)
    mask = _dot_nt(eye_ref[...], mask_t).astype(BF16)

    for hh in range(NSA_GROUP):
        lhs_scr[hh * TQ:(hh + 1) * TQ, 0:MAX_BLOCKS] = mask
    m_scr[...] = jnp.full((ROWS, LANES), NEG, F32)
    acc_scr[...] = jnp.zeros((ROWS, VAUG), F32)

    def tile_start(k):
        return pl.multiple_of(KPAD + (first - BLOCKS_PER_TILE) * SEL_BLOCK - k * KS, KT)

    def scores(k, dst):
        _store_lane_tiles(dst, _halves(_dot_nt, lhs_scr[...], ksa[pl.ds(tile_start(k), KS), :]))

    def consume(k, src, p_scr, with_table):
        n_lt = KS // LANES
        for rows in row_blocks:
            s = [src[c, rows, :] for c in range(n_lt)]
            if with_table:
                s = [s[c] + near_ref[0, rows, c * LANES:(c + 1) * LANES] for c in range(n_lt)]
            m_prev = m_scr[rows]
            m = jnp.maximum(m_prev, _row_max(s))
            al_scr[rows] = jnp.exp2(m_prev - m)
            m_scr[rows] = m
            for c in range(n_lt):
                p_scr[rows, c * LANES:(c + 1) * LANES] = jnp.exp2((s[c] - m).astype(BF16))
        acc_scr[...] = acc_scr[...] * al_scr[...] + _dot(p_scr[:, 0:KS], vsa[pl.ds(tile_start(k), KS), :])

    scores(0, sa_scr)

    for rows in row_blocks:
        s = [sw_scr[c, rows, :] + win_ref[0, rows, c * LANES:(c + 1) * LANES]
             for c in range((KS + KT) // LANES)]
        m = _row_max(s)
        for c in range((KS + KT) // LANES):
            pw_scr[rows, c * LANES:(c + 1) * LANES] = jnp.exp2((s[c] - m).astype(BF16))
    acc_w = _halves(_dot, pw_scr[...], vwa[pl.ds(win_start, KS + KT), :])
    out_scr[...] += acc_w[:, :D_HEAD] * (gate_col(2) / acc_w[:, D_HEAD:D_HEAD + 1])

    def pair(with_table, u, carry):
        scores(2 * u + 1, sb_scr)
        consume(2 * u, sa_scr, pa_scr, with_table)
        scores(2 * u + 2, sa_scr)
        consume(2 * u + 1, sb_scr, pb_scr, False)
        return carry

    n_tiles = (qt + 2) // 2
    pair(True, 0, 0)
    lax.fori_loop(1, jnp.maximum(n_tiles // 2, 1), functools.partial(pair, False), 0)

    @pl.when((n_tiles % 2 == 1) & (n_tiles >= 3))
    def _odd_tile():
        consume(n_tiles - 1, sa_scr, pa_scr, False)

    acc_s = acc_scr[...]

    o = out_scr[...] + acc_s[:, :D_HEAD] * (gate_col(1) / acc_s[:, D_HEAD:D_HEAD + 1])
    for hh in range(NSA_GROUP):
        o_ref[0, :, hh * D_HEAD:(hh + 1) * D_HEAD] = o[hh * TQ:(hh + 1) * TQ].astype(BF16)


def _nsa_call(q, kcmp, vcmp, ks, vs, kw, vw, gates, near_t, win_t, q_half, cmp_lhs, ovt, eye):
    B, _, T, _ = q.shape
    G = NSA_KV_HEADS
    ncp = kcmp.shape[2]
    rows_kv = KPAD + T
    per_bg =lambda a: pl.BlockSpec((1, 1) + a.shape[2:], lambda b, g, i: (b, g, 0, 0))
    per_g = lambda a: pl.BlockSpec((1,) + a.shape[1:], lambda b, g, i: (g, 0, 0))
    const2 = lambda a: pl.BlockSpec(a.shape, lambda b, g, i: (0, 0))
    return pl.pallas_call(
        _nsa_kernel,
        grid=(B, G, T // TQ),
        in_specs=[
            pl.BlockSpec((1, NSA_GROUP, TQ, D_HEAD), lambda b, g, i: (b, g, i, 0)),
            per_bg(kcmp), per_bg(vcmp), per_bg(ks), per_bg(vs), per_bg(kw), per_bg(vw),
            pl.BlockSpec((1, 1, TQ, LANES), lambda b, g, i: (b, g, i, 0)),
            per_g(near_t), per_g(win_t), per_g(q_half), per_g(cmp_lhs), const2(ovt), const2(eye),
        ],
        out_specs=pl.BlockSpec((1, TQ, NSA_GROUP * D_HEAD), lambda b, g, i: (b, i, g)),
        out_shape=jax.ShapeDtypeStruct((B, T, D_NSA), BF16),
        scratch_shapes=[
            pltpu.VMEM((rows_kv, KAUG), BF16), pltpu.VMEM((rows_kv, VAUG), BF16),
            pltpu.VMEM((rows_kv, KAUG), BF16), pltpu.VMEM((rows_kv, VAUG), BF16),
            pltpu.VMEM((ncp, VAUG), BF16), pltpu.VMEM((ncp, VAUG), BF16),
            pltpu.VMEM((MAX_BLOCKS, TQ), F32),
            pltpu.VMEM((ROWS, KAUG), BF16), pltpu.VMEM((ROWS, LANES), BF16),
            pltpu.VMEM((ROWS, LANES), F32), pltpu.VMEM((ROWS, LANES), F32), pltpu.VMEM((ROWS, 1), F32),
            pltpu.VMEM((ROWS, VAUG), F32),
            pltpu.VMEM((KS // LANES, ROWS, LANES), F32), pltpu.VMEM((KS // LANES, ROWS, LANES), F32),
            pltpu.VMEM((ROWS, KS), BF16), pltpu.VMEM((ROWS, KS), BF16),
            pltpu.VMEM(((KS + KT) // LANES, ROWS, LANES), F32), pltpu.VMEM((ROWS, KS + KT), BF16),
            pltpu.VMEM((TQ, ncp), F32), pltpu.VMEM((ROWS, D_HEAD), F32), pltpu.VMEM((MAX_BLOCKS, TQ), jnp.int32),
        ],
        compiler_params=pltpu.CompilerParams(
            dimension_semantics=("arbitrary", "arbitrary", "arbitrary"), vmem_limit_bytes=VMEM_LIMIT),
        name="nsa",
    )(q, kcmp, vcmp, ks, vs, kw, vw, gates, near_t, win_t, q_half, cmp_lhs, ovt, eye)


def _ffn_kernel(x_ref, on_ref, os_ref, wo_n_ref, wo_s_ref, fn_ref, wup_ref, cw_ref, cb_ref, wdn_ref, pm_ref,
                out_ref, prev_ref, perm_ref, h_ref, ua_scr, ub_scr, act_ref):
    tm = x_ref.shape[1]
    seg = tm // 8
    n_chunks = wup_ref.shape[1]

    @pl.when(pl.program_id(1) == 0)
    def _start_of_sequence():
        prev_ref[...] = jnp.zeros(prev_ref.shape, F32)

    x1 = x_ref[0] + _dot(on_ref[0], wo_n_ref[...]) + _dot(os_ref[0], wo_s_ref[...])
    ms = jnp.mean(x1 * x1, axis=-1, keepdims=True)
    hn = (x1 * lax.rsqrt(ms + EPS) * fn_ref[...]).astype(BF16)
    out_ref[0] = x1
    n_lt = x1.shape[1] // LANES
    h_ref[...] = _halves(_dot, pm_ref[...], hn).astype(BF16)

    def up(c, dst):
        h = h_ref[...]
        for half in range(2):
            dst[half] = _dot(h, wup_ref[half, c])

    first_sublane = lax.broadcasted_iota(jnp.int32, (8, FFN_CHUNK), 0) == 0

    def conv_gate(c, src):
        for r0 in range(0, tm, FFN_ROWS):
            ys = []
            for half in range(2):
                w = cw_ref[half, c]
                x0 = src[half, r0:r0 + FFN_ROWS]
                if r0 == 0:
                    wrap = [jnp.where(first_sublane, pltpu.roll(prev_ref[half, c, 8 * k:8 * k + 8], 1, 0),
                                      pltpu.roll(src[half, tm - 16 + 8 * k:tm - 8 + 8 * k], 1, 0)) for k in range(2)]
                    x1_ = jnp.concatenate([wrap[1], src[half, 0:FFN_ROWS - 8]], axis=0)
                    x2_ = jnp.concatenate([wrap[0], wrap[1], src[half, 0:FFN_ROWS - 16]], axis=0)
                else:
                    x1_ = src[half, r0 - 8:r0 + FFN_ROWS - 8]
                    x2_ = src[half, r0 - 16:r0 + FFN_ROWS - 16]
                ys.append(x0 * w[2:3] + x1_ * w[1:2] + x2_ * w[0:1] + cb_ref[half, c])
            a, g = ys
            col = pl.multiple_of(c * FFN_CHUNK, FFN_CHUNK)
            act_ref[r0:r0 + FFN_ROWS, pl.ds(col, FFN_CHUNK)] = ((g + g * jnp.tanh(g)) * a).astype(BF16)
        for half in range(2):
            prev_ref[half, c] = src[half, tm - 16:tm]

    up(0, ua_scr)

    def pair(u, carry):
        up(2 * u + 1, ub_scr)
        conv_gate(2 * u, ua_scr)
        up(2 * u + 2, ua_scr)
        conv_gate(2 * u + 1, ub_scr)
        return carry

    lax.fori_loop(0, (n_chunks - 1) // 2, pair, 0)
    conv_gate(n_chunks - 1, ua_scr)
    y = _halves(_dot, act_ref[...], wdn_ref[...])
    for c in range(n_lt):
        perm_ref[c] = y[:, c * LANES:(c + 1) * LANES]
    for sgm in range(8):
        natural = jnp.concatenate([perm_ref[c, pl.ds(sgm, seg, stride=8), :] for c in range(n_lt)], axis=1)
        out_ref[0, sgm * seg:(sgm + 1) * seg, :] += natural


def _ffn_call(x, o_nsa, o_sgu, wo_n, wo_s, ffn_norm, w_up, conv_w, conv_b, w_down, tm):
    B, T, _ = x.shape
    r = np.arange(tm)
    perm = jnp.asarray(((r % 8) * (tm // 8) + r // 8)[:, None] == r[None, :], BF16)
    n_chunks = w_up.shape[1]
    assert n_chunks % 2 == 1
    resident =lambda a: pl.BlockSpec(a.shape, lambda b, i: (0,) * a.ndim, pipeline_mode=pl.Buffered(1))
    tile = lambda w: pl.BlockSpec((1, tm, w), lambda b, i: (b, i, 0))
    return pl.pallas_call(
        _ffn_kernel,
        grid=(B, T // tm),
        in_specs=[tile(D_MODEL), tile(D_NSA), tile(D_GMLP), resident(wo_n), resident(wo_s),
                  resident(ffn_norm), resident(w_up), resident(conv_w), resident(conv_b), resident(w_down),
                  resident(perm)],
        out_specs=tile(D_MODEL),
        out_shape=jax.ShapeDtypeStruct((B, T, D_MODEL), F32),
        scratch_shapes=[pltpu.VMEM((2, n_chunks, 16, FFN_CHUNK), F32), pltpu.VMEM((D_MODEL // LANES, tm, LANES), F32),
                        pltpu.VMEM((tm, D_MODEL), BF16),
                        pltpu.VMEM((2, tm, FFN_CHUNK), F32), pltpu.VMEM((2, tm, FFN_CHUNK), F32),
                        pltpu.VMEM((tm, D_FF), BF16)],
        compiler_params=pltpu.CompilerParams(
            dimension_semantics=("arbitrary", "arbitrary"), vmem_limit_bytes=VMEM_LIMIT),
        name="ffn",
    )(x, o_nsa, o_sgu, wo_n, wo_s, ffn_norm, w_up, conv_w, conv_b, w_down, perm)


def _mixers(x, rel_bias, attn_norm, w_in, q_norm, k_norm_cmp, k_norm_slc, k_norm_win,
            cmp_pe_k, cmp_w1_k, cmp_b1_k, cmp_w2_k, cmp_pe_v, cmp_w1_v, cmp_b1_v, cmp_w2_v,
            sgu_norm, sgu_w, sgu_b):
    B, T, _ = x.shape
    assert T % 512 == 0 and T // SEL_BLOCK <= MAX_BLOCKS and (T // CMP_STRIDE) % LANES == 0
    ncp = T // CMP_STRIDE
    G, R = NSA_KV_HEADS, NSA_GROUP

    o_kv = D_NSA
    o_g = D_NSA + 6 * D_KV
    o_uv = o_g + N_GATES
    w_q = w_in[:, :o_kv].astype(BF16)
    w_kv = w_in[:, o_kv:o_g].astype(BF16)
    per_group = NSA_GROUP * N_BRANCH
    w_g = jnp.pad(w_in[:, o_g:o_uv].reshape(D_MODEL, G, per_group), ((0, 0), (0, 0), (0, LANES - per_group)))
    w_g = w_g.reshape(D_MODEL, G * LANES).astype(BF16)
    w_uv = w_in[:, o_uv:].astype(BF16)
    qn = (jnp.tile(q_norm, NSA_HEADS) * (D_HEAD ** -0.5 * LOG2E)).reshape(1, D_NSA)
    ksn = jnp.tile(k_norm_slc, G).reshape(1, D_KV)
    kwn = jnp.tile(k_norm_win, G).reshape(1, D_KV)
    sgu_bt = jnp.repeat(sgu_b.T, GMLP_GROUP_DIM, axis=1)
    grp = np.arange(D_NSA) // D_HEAD
    bd = jnp.asarray(grp[:, None] == grp[None, :], BF16)

    q, kc, vc, ks, vs, kw, vw, gates, o_sgu = _proj_call(
        x, attn_norm.reshape(1, D_MODEL), w_q, w_kv, w_g, w_uv, qn, ksn, kwn,
        sgu_norm.reshape(1, D_GMLP), sgu_w, sgu_bt, bd, tm=1024)

    half = CMP_STRIDE * D_HEAD
    kcmp, vcmp = _compress_call(
        kc.reshape(B, G, ncp, half), vc.reshape(B, G, ncp, half),
        cmp_pe_k.reshape(2, half), cmp_w1_k.astype(BF16), cmp_b1_k.reshape(1, CMP_HIDDEN),
        cmp_w2_k.astype(BF16), k_norm_cmp.reshape(1, D_HEAD),
        cmp_pe_v.reshape(2, half), cmp_w1_v.astype(BF16), cmp_b1_v.reshape(1, CMP_HIDDEN),
        cmp_w2_v.astype(BF16))

    near_t, win_t, q_half, cmp_lhs = _bias_tables(rel_bias)
    jj = np.arange(MAX_BLOCKS)[:, None]
    nn = np.arange(ncp)[None, :]
    ovt = jnp.asarray((nn >= 4 * jj - 1) & (nn <= 4 * jj + 3), BF16)
    eye = jnp.asarray(np.eye(TQ), BF16)

    o_nsa = _nsa_call(q, kcmp, vcmp, ks, vs, kw, vw, gates, near_t, win_t, q_half, cmp_lhs, ovt, eye)
    return o_nsa, o_sgu


def _layer(x, rel_bias, attn_norm, w_in, q_norm, k_norm_cmp, k_norm_slc, k_norm_win,
           cmp_pe_k, cmp_w1_k, cmp_b1_k, cmp_w2_k, cmp_pe_v, cmp_w1_v, cmp_b1_v, cmp_w2_v,
           sgu_norm, sgu_w, sgu_b, w_out, ffn_norm, w_up, conv_w, conv_b, w_down):
    o_nsa, o_sgu = _mixers(x, rel_bias, attn_norm, w_in, q_norm, k_norm_cmp, k_norm_slc, k_norm_win,
                           cmp_pe_k, cmp_w1_k, cmp_b1_k, cmp_w2_k, cmp_pe_v, cmp_w1_v, cmp_b1_v, cmp_w2_v,
                           sgu_norm, sgu_w, sgu_b)
    wo = w_out.astype(BF16)
    n_chunks = D_FF // FFN_CHUNK
    wu = jnp.transpose(w_up.astype(BF16).reshape(D_MODEL, 2, n_chunks, FFN_CHUNK), (1, 2, 0, 3))
    half_gate = jnp.asarray([1.0, 0.5], F32).reshape(2, 1, 1, 1)
    cw = jnp.transpose(conv_w.reshape(conv_w.shape[0], 2, n_chunks, FFN_CHUNK), (1, 2, 0, 3)) * half_gate
    cb = conv_b.reshape(2, n_chunks, 1, FFN_CHUNK) * half_gate
    wd = w_down.astype(BF16)
    return _ffn_call(x, o_nsa, o_sgu, wo[:D_NSA], wo[D_NSA:], ffn_norm.reshape(1, D_MODEL), wu, cw, cb, wd, tm=512)


def kernel(x, rel_bias, attn_norm, w_in, q_norm, k_norm_cmp, k_norm_slc, k_norm_win,
           cmp_pe_k, cmp_w1_k, cmp_b1_k, cmp_w2_k, cmp_pe_v, cmp_w1_v, cmp_b1_v, cmp_w2_v,
           sgu_norm, sgu_w, sgu_b, w_out, ffn_norm, w_up, conv_w, conv_b, w_down):
    depth = attn_norm.shape[0]
    for l in range(depth):
        x = _layer(x, rel_bias, attn_norm[l], w_in[l], q_norm[l], k_norm_cmp[l], k_norm_slc[l], k_norm_win[l],
                   cmp_pe_k[l], cmp_w1_k[l], cmp_b1_k[l], cmp_w2_k[l], cmp_pe_v[l], cmp_w1_v[l], cmp_b1_v[l],
                   cmp_w2_v[l], sgu_norm[l], sgu_w[l], sgu_b[l], w_out[l], ffn_norm[l], w_up[l], conv_w[l],
                   conv_b[l], w_down[l])
    return x
```

```python
import functools
import math

import numpy as np
import jax
import jax.numpy as jnp
from jax import lax
from jax.experimental import pallas as pl
from jax.experimental.pallas import tpu as pltpu

F32 = jnp.float32
BF16 = jnp.bfloat16

D_MODEL = 1024
D_HEAD = 64
NSA_HEADS = 8
NSA_KV_HEADS = 2
NSA_GROUP = NSA_HEADS // NSA_KV_HEADS
D_NSA = NSA_HEADS * D_HEAD
D_KV = NSA_KV_HEADS * D_HEAD
N_BRANCH = 3
N_GATES = NSA_HEADS * N_BRANCH
CMP_BLOCK = 32
CMP_STRIDE = 16
CMP_HIDDEN = 256
SEL_BLOCK = 64
SEL_TOPK = 16
WINDOW = 512
GMLP_GROUPS = 8
GMLP_GROUP_DIM = 64
D_GMLP = GMLP_GROUPS * GMLP_GROUP_DIM
CHUNK = 128
D_MIX = D_NSA + D_GMLP
REL_BUCKETS = 32
REL_MAX_DIST = 128
D_FF = 2816
EPS = 1e-6
NEG = -1e30
FORCED_SCORE = 1e4
LOG2E = 1.4426950408889634

LANES = 128
TQ = 256
SUB = TQ // SEL_BLOCK
ROWS = NSA_GROUP * TQ
KT = 256
KS = 2 * KT
WH = TQ // 2
WKEYS = WINDOW + WH
RB = 64
BLOCKS_PER_TILE = KT // SEL_BLOCK
WIN_BLOCKS = WINDOW // SEL_BLOCK
KPAD = 1280
N_TOEPLITZ = 3
MAX_BLOCKS = 64
KAUG = 256
VAUG = 128
CBAND = 28
FFN_CHUNK = 256
FFN_ROWS = 64
VMEM_LIMIT = 56 * 1024 * 1024

_NT = (((1,), (1,)), ((), ()))


def _dot(a, b):
    return jnp.dot(a, b, preferred_element_type=F32)


def _dot_nt(a, b):
    return lax.dot_general(a, b, _NT, preferred_element_type=F32)


def _halves(dot, a, b):
    h = a.shape[0] // 2
    return jnp.concatenate([dot(a[:h], b), dot(a[h:], b)], axis=0)


def _split_bf16(x):
    hi = x.astype(BF16)
    lo = (x - hi.astype(F32)).astype(BF16)
    return hi, lo


def _gelu_tanh(x):
    return 0.5 * x * (1.0 + jnp.tanh(0.7978845608028654 * (x + 0.044715 * (x * x * x))))


def _sigmoid(x):
    return 0.5 * (1.0 + jnp.tanh(0.5 * x))


def _group_mean_sq(t, ones_blockdiag, width):
    t2 = t * t
    hi, lo = _split_bf16(t2)
    return (_dot(hi, ones_blockdiag) + _dot(lo, ones_blockdiag)) * (1.0 / width)


def _proj_kernel(x_ref, an_ref, wq_ref, wkv_ref, wg_ref, wuv_ref, qn_ref, ksn_ref, kwn_ref,
                 sgun_ref, sguw_ref, sgub_ref, bd_ref,
                 q_out, kc_out, vc_out, ks_out, vs_out, kw_out, vw_out, g_out, sgu_out):
    tm = x_ref.shape[1]
    x = x_ref[0]
    ms = jnp.mean(x * x, axis=-1, keepdims=True)
    h = (x * lax.rsqrt(ms + EPS) * an_ref[...]).astype(BF16)

    bd = bd_ref[...]
    q = _halves(_dot, h, wq_ref[...])
    qn = q * lax.rsqrt(_group_mean_sq(q, bd, D_HEAD) + EPS) * qn_ref[...]
    for hh in range(NSA_HEADS):
        q_out[0, hh] = qn[:, hh * D_HEAD:(hh + 1) * D_HEAD].astype(BF16)

    kv = _halves(_dot, h, wkv_ref[...])
    bd_kv = bd[:D_KV, :D_KV]
    kc = kv[:, 0 * D_KV:1 * D_KV]
    vc = kv[:, 1 * D_KV:2 * D_KV]
    ks = kv[:, 2 * D_KV:3 * D_KV]
    vs = kv[:, 3 * D_KV:4 * D_KV]
    kw = kv[:, 4 * D_KV:5 * D_KV]
    vw = kv[:, 5 * D_KV:6 * D_KV]
    ks = ks * lax.rsqrt(_group_mean_sq(ks, bd_kv, D_HEAD) + EPS) * ksn_ref[...]
    kw = kw * lax.rsqrt(_group_mean_sq(kw, bd_kv, D_HEAD) + EPS) * kwn_ref[...]
    for t, o_ref in ((kc, kc_out), (vc, vc_out), (ks, ks_out), (vs, vs_out), (kw, kw_out), (vw, vw_out)):
        for g in range(NSA_KV_HEADS):
            o_ref[0, g] = t[:, g * D_HEAD:(g + 1) * D_HEAD].astype(BF16)

    gates = _sigmoid(_dot(h, wg_ref[...]))
    for g in range(NSA_KV_HEADS):
        g_out[0, g] = gates[:, g * LANES:(g + 1) * LANES]

    uv = _gelu_tanh(_halves(_dot, h, wuv_ref[...]))
    u = uv[:, :D_GMLP]
    v = uv[:, D_GMLP:]
    vms = jnp.mean(v * v, axis=-1, keepdims=True)
    vb = (v * lax.rsqrt(vms + EPS) * sgun_ref[...]).astype(BF16)

    row = lax.broadcasted_iota(jnp.int32, (CHUNK, CHUNK), 0)
    col = lax.broadcasted_iota(jnp.int32, (CHUNK, CHUNK), 1)
    tril = col <= row
    w_tril = [jnp.where(tril, sguw_ref[g], 0.0).astype(BF16) for g in range(GMLP_GROUPS)]
    w_pair = [jnp.concatenate([w_tril[2 * p], w_tril[2 * p + 1]], axis=1) for p in range(D_GMLP // LANES)]
    first_half = lax.broadcasted_iota(jnp.int32, (CHUNK, LANES), 1) < GMLP_GROUP_DIM
    zero = jnp.zeros((CHUNK, LANES), BF16)
    for c in range(tm // CHUNK):
        rows = slice(c * CHUNK, (c + 1) * CHUNK)
        zs = []
        for p in range(D_GMLP // LANES):
            blk = vb[rows, p * LANES:(p + 1) * LANES]
            rhs = jnp.concatenate([jnp.where(first_half, blk, zero), jnp.where(first_half, zero, blk)], axis=0)
            zs.append(_dot(w_pair[p], rhs))
        z = jnp.concatenate(zs, axis=1) + sgub_ref[...]
        sgu_out[0, rows, :] = (u[rows, :] * z).astype(BF16)


def _proj_call(x, attn_norm, w_q, w_kv, w_g, w_uv, qn, ksn, kwn, sgu_norm, sgu_w, sgu_bt, bd, tm):
    B, T, _ = x.shape
    const2 = lambda b, i: (0, 0)
    const3 = lambda b, i: (0, 0, 0)
    head_spec = lambda nh: pl.BlockSpec((1, nh, tm, D_HEAD), lambda b, i: (b, 0, i, 0))
    kv_shape = jax.ShapeDtypeStruct((B, NSA_KV_HEADS, T, D_HEAD), BF16)
    return pl.pallas_call(
        _proj_kernel,
        grid=(B, T // tm),
        in_specs=[
            pl.BlockSpec((1, tm, D_MODEL), lambda b, i: (b, i, 0)),
            pl.BlockSpec((1, D_MODEL), const2),
            pl.BlockSpec(w_q.shape, const2),
            pl.BlockSpec(w_kv.shape, const2),
            pl.BlockSpec(w_g.shape, const2),
            pl.BlockSpec(w_uv.shape, const2),
            pl.BlockSpec((1, D_NSA), const2),
            pl.BlockSpec((1, D_KV), const2),
            pl.BlockSpec((1, D_KV), const2),
            pl.BlockSpec((1, D_GMLP), const2),
            pl.BlockSpec(sgu_w.shape, const3),
            pl.BlockSpec(sgu_bt.shape, const2),
            pl.BlockSpec(bd.shape, const2),
        ],
        out_specs=[
            head_spec(NSA_HEADS),
            head_spec(NSA_KV_HEADS), head_spec(NSA_KV_HEADS), head_spec(NSA_KV_HEADS),
            head_spec(NSA_KV_HEADS), head_spec(NSA_KV_HEADS), head_spec(NSA_KV_HEADS),
            pl.BlockSpec((1, NSA_KV_HEADS, tm, LANES), lambda b, i: (b, 0, i, 0)),
            pl.BlockSpec((1, tm, D_GMLP), lambda b, i: (b, i, 0)),
        ],
        out_shape=[
            jax.ShapeDtypeStruct((B, NSA_HEADS, T, D_HEAD), BF16),
            kv_shape, kv_shape, kv_shape, kv_shape, kv_shape, kv_shape,
            jax.ShapeDtypeStruct((B, NSA_KV_HEADS, T, LANES), F32),
            jax.ShapeDtypeStruct((B, T, D_GMLP), BF16),
        ],
        compiler_params=pltpu.CompilerParams(
            dimension_semantics=("arbitrary", "arbitrary"), vmem_limit_bytes=VMEM_LIMIT),
        name="proj",
    )(x, attn_norm, w_q, w_kv, w_g, w_uv, qn, ksn, kwn, sgu_norm, sgu_w, sgu_bt, bd)


def _compress_one(tok_ref, pe_ref, w1_ref, b1_ref, w2_ref):
    half = CMP_STRIDE * D_HEAD
    tok = tok_ref[0, 0].astype(F32)
    top = (tok + pe_ref[0:1, :]).astype(BF16)
    bot = (tok + pe_ref[1:2, :]).astype(BF16)
    a = _dot(top, w1_ref[:half, :])
    b = _dot(bot, w1_ref[half:, :])
    ncp = a.shape[0]
    pre = a + pltpu.roll(b, ncp - 1, 0) + b1_ref[...]
    hid = _gelu_tanh(pre).astype(BF16)
    return _dot(hid, w2_ref[...])


def _compress_kernel(kc_ref, vc_ref, pek_ref, w1k_ref, b1k_ref, w2k_ref, kn_ref,
                     pev_ref, w1v_ref, b1v_ref, w2v_ref, k_out, v_out):
    k = _compress_one(kc_ref, pek_ref, w1k_ref, b1k_ref, w2k_ref)
    kms = jnp.mean(k * k, axis=-1, keepdims=True)
    k_out[0, 0] = (k * lax.rsqrt(kms + EPS) * kn_ref[...]).astype(BF16)
    v_out[0, 0] = _compress_one(vc_ref, pev_ref, w1v_ref, b1v_ref, w2v_ref).astype(BF16)


def _compress_call(kc, vc, pek, w1k, b1k, w2k, kn, pev, w1v, b1v, w2v):
    B, G, ncp, width = kc.shape
    const2 = lambda b, g: (0, 0)
    tok_spec = pl.BlockSpec((1, 1, ncp, width), lambda b, g: (b, g, 0, 0))
    out_spec = pl.BlockSpec((1, 1, ncp, D_HEAD), lambda b, g: (b, g, 0, 0))
    full = lambda a: pl.BlockSpec(a.shape, const2)
    out_shape = jax.ShapeDtypeStruct((B, G, ncp, D_HEAD), BF16)
    return pl.pallas_call(
        _compress_kernel,
        grid=(B, G),
        in_specs=[tok_spec, tok_spec, full(pek), full(w1k), full(b1k), full(w2k), full(kn),
                  full(pev), full(w1v), full(b1v), full(w2v)],
        out_specs=[out_spec, out_spec],
        out_shape=[out_shape, out_shape],
        compiler_params=pltpu.CompilerParams(
            dimension_semantics=("arbitrary", "arbitrary"), vmem_limit_bytes=VMEM_LIMIT),
        name="compress",
    )(kc, vc, pek, w1k, b1k, w2k, kn, pev, w1v, b1v, w2v)


def _rel_bucket_np(dist):
    max_exact = REL_BUCKETS // 2
    d = np.maximum(dist, 1).astype(np.float32)
    log_b = max_exact + (np.log(d / np.float32(max_exact)) / np.float32(math.log(REL_MAX_DIST / max_exact))
                         * np.float32(REL_BUCKETS - max_exact)).astype(np.int32)
    log_b = np.clip(log_b, max_exact, REL_BUCKETS - 1)
    return np.where(dist < max_exact, np.maximum(dist, 0), log_b)


def _bias_tables(rel_bias):
    rb = rel_bias.astype(F32) * LOG2E
    qq = np.arange(SEL_BLOCK)[:, None]
    kp = np.arange(SEL_BLOCK)[None, :]
    tile_d = [m * SEL_BLOCK + qq - kp for m in range(N_TOEPLITZ)]
    s_ = np.arange(SUB)[:, None, None]
    e_ = np.arange(CBAND)[None, None, :]
    dist_c = qq[None, :, :] + CMP_STRIDE * (e_ - 4 * (SUB - 1) + 4 * s_) - (3 * CMP_STRIDE + CMP_BLOCK - 1)
    all_d = np.concatenate([d.reshape(-1) for d in tile_d] + [dist_c.reshape(-1)])
    onehot = np.eye(REL_BUCKETS, dtype=np.float32)[_rel_bucket_np(all_d)]
    vals = jnp.dot(jnp.asarray(onehot), rb, precision=lax.Precision.HIGHEST)
    vals = jnp.where(jnp.asarray(all_d >= 0)[:, None], vals, NEG)
    n_t = N_TOEPLITZ * SEL_BLOCK * SEL_BLOCK
    toep = jnp.transpose(vals[:n_t].reshape(N_TOEPLITZ, SEL_BLOCK, SEL_BLOCK, NSA_HEADS), (0, 3, 1, 2))
    band = jnp.transpose(vals[n_t:].reshape(SUB, SEL_BLOCK, CBAND, NSA_HEADS), (3, 0, 1, 2))
    far = rb[REL_BUCKETS - 1]
    far_tile = jnp.broadcast_to(far[:, None, None], (NSA_HEADS, SEL_BLOCK, SEL_BLOCK))
    neg_tile = jnp.full((NSA_HEADS, SEL_BLOCK, SEL_BLOCK), NEG, F32)
    edge_tile = jnp.where(jnp.asarray(kp > qq)[None], far_tile, NEG)

    def tile(m, windowed):
        if m < 0 or (windowed and m > WIN_BLOCKS):
            return neg_tile
        if windowed and m == WIN_BLOCKS:
            return edge_tile
        return toep[m] if m < N_TOEPLITZ else far_tile

    def table(first_m, n_blocks, windowed):
        rows = [jnp.concatenate([tile(first_m + s - c, windowed) for c in range(n_blocks)], axis=2)
                for s in range(SUB)]
        t = jnp.concatenate(rows, axis=1)
        return t.reshape(NSA_KV_HEADS, ROWS, n_blocks * SEL_BLOCK)

    far_rows = jnp.repeat(far, TQ).reshape(NSA_KV_HEADS, ROWS, 1)
    near_t = table(BLOCKS_PER_TILE, 2 * BLOCKS_PER_TILE, False) - far_rows
    win_t = table(WIN_BLOCKS, WIN_BLOCKS + BLOCKS_PER_TILE, True) - far_rows
    win_t = win_t.reshape(NSA_KV_HEADS, NSA_GROUP, 2, WH, (WIN_BLOCKS + BLOCKS_PER_TILE) * SEL_BLOCK)
    win_t = jnp.stack([win_t[:, :, half, :, half * WH:half * WH + WKEYS].reshape(NSA_KV_HEADS, NSA_GROUP * WH, WKEYS)
                       for half in range(2)], axis=1)
    band_t = band.reshape(NSA_KV_HEADS, ROWS, CBAND)
    band_hi = band_t.astype(BF16)
    band_lo = (band_t - band_hi.astype(F32)).astype(BF16)
    far_hi = far_rows.astype(BF16)
    far_lo = (far_rows - far_hi.astype(F32)).astype(BF16)
    q_half = jnp.concatenate([jnp.zeros((NSA_KV_HEADS, ROWS, D_HEAD), BF16),
                              jnp.full((NSA_KV_HEADS, ROWS, 1), NEG, BF16), far_hi, far_lo,
                              jnp.zeros((NSA_KV_HEADS, ROWS, LANES - D_HEAD - 3), BF16)], axis=2)
    neg_col = jnp.full((NSA_KV_HEADS, ROWS, 1), NEG, BF16)
    zeros = jnp.zeros((NSA_KV_HEADS, ROWS, D_HEAD - 2 * CBAND - 3), BF16)
    q_lanes = jnp.zeros((NSA_KV_HEADS, ROWS, D_HEAD), BF16)
    cmp_lhs = jnp.concatenate([q_lanes, band_hi, band_lo, far_hi, far_lo, neg_col, zeros], axis=2)
    return near_t, win_t, q_half, cmp_lhs


def _row_max(lane_tiles):
    m = jnp.max(functools.reduce(jnp.maximum, lane_tiles), axis=1, keepdims=True)
    return jnp.broadcast_to(m, lane_tiles[0].shape)


def _nsa_kernel(q_ref, kc_ref, vc_ref, ks_ref, vs_ref, kw_ref, vw_ref, g_ref,
                near_ref, win_ref, qhalf_ref, cmpl_ref, ovt_ref, eye_ref,
                o_ref, ksa, vsa, kwa, vwa, kca, vca, sc_scr, lhs_scr, lhsc_scr, m_scr, al_scr, inv_scr, acc_scr, sa_scr, sb_scr, pa_scr, pb_scr,
                sw_scr, pw_scr, ps_scr, out_scr, cnt_scr):
    qt = pl.program_id(2)
    first = qt * SUB
    T = ks_ref.shape[2]
    ncp = kc_ref.shape[2]

    @pl.when(qt == 0)
    def _init():
        n_chunks = ksa.shape[0] // KT
        r_io = lax.broadcasted_iota(jnp.int32, (KT, KAUG), 0)
        l_io = lax.broadcasted_iota(jnp.int32, (KT, KAUG), 1)
        lv = lax.broadcasted_iota(jnp.int32, (KT, VAUG), 1)
        vpat = jnp.where(lv == D_HEAD, 1.0, 0.0).astype(BF16)

        def fill(c, carry):
            rows = pl.ds(pl.multiple_of(c * KT, KT), KT)
            key = r_io + (c * KT - KPAD)
            blk = key // SEL_BLOCK
            real = (key >= 0) & (key < T)
            hot = real & (((l_io < D_HEAD) & (l_io == blk))
                          | ((l_io >= D_HEAD) & (l_io < 2 * D_HEAD) & (l_io - D_HEAD == blk)))
            hot = hot | ((~real) & (l_io == 3 * D_HEAD)) | (real & (l_io > 3 * D_HEAD) & (l_io <= 3 * D_HEAD + 2))
            pat = jnp.where(hot, 1.0, 0.0).astype(BF16)
            ksa[rows, :] = pat
            kwa[rows, :] = pat
            vsa[rows, :] = vpat
            vwa[rows, :] = vpat
            return carry

        lax.fori_loop(0, n_chunks, fill, 0)
        ksa[KPAD:KPAD + T, 2 * D_HEAD:3 * D_HEAD] = ks_ref[0, 0]
        kwa[KPAD:KPAD + T, 2 * D_HEAD:3 * D_HEAD] = kw_ref[0, 0]
        vsa[KPAD:KPAD + T, 0:D_HEAD] = vs_ref[0, 0]
        vwa[KPAD:KPAD + T, 0:D_HEAD] = vw_ref[0, 0]
        lhs_scr[:, 0:LANES] = jnp.zeros((ROWS, LANES), BF16)
        lc = lax.broadcasted_iota(jnp.int32, (ncp, VAUG), 1)
        kca[...] = jnp.zeros((ncp, VAUG), BF16)
        kca[:, 0:D_HEAD] = kc_ref[0, 0]
        vca[...] = jnp.where(lc == D_HEAD, 1.0, 0.0).astype(BF16)
        vca[:, 0:D_HEAD] = vc_ref[0, 0]

    q4 = q_ref[0].reshape(ROWS, D_HEAD)
    gates = g_ref[0, 0]
    row_blocks = [slice(r, r + RB) for r in range(0, ROWS, RB)]

    gate_rows = jnp.concatenate(
        [gates if hh == 0 else pltpu.roll(gates, LANES - hh * N_BRANCH, 1) for hh in range(NSA_GROUP)], axis=0)

    def gate_col(branch):
        return gate_rows[:, branch:branch + 1]

    lane_c = lax.broadcasted_iota(jnp.int32, (ncp, LANES), 1)
    n_c = lax.broadcasted_iota(jnp.int32, (ncp, LANES), 0)
    e = lane_c - D_HEAD
    base = 4 * first + (4 * SUB - 1)
    ind = (((e >= 0) & (e < CBAND) & (n_c == base - e))
           | ((e >= CBAND) & (e < 2 * CBAND) & (n_c == base - (e - CBAND)))
           | (((e == 2 * CBAND) | (e == 2 * CBAND + 1)) & (n_c < base - (CBAND - 1)))
           | ((e == 2 * CBAND + 2) & (n_c > base)))
    rhs_c = jnp.where(lane_c < D_HEAD, kca[...], jnp.where(ind, 1.0, 0.0).astype(BF16))
    lhsc_scr[...] = cmpl_ref[0]
    lhsc_scr[:, 0:D_HEAD] = q4
    sa_scr[:, 0:ncp] = _halves(_dot_nt, lhsc_scr[...], rhs_c)

    lhs_scr[:, LANES:2 * LANES] = qhalf_ref[0]
    lhs_scr[:, LANES:LANES + D_HEAD] = q4
    win_start = pl.multiple_of(KPAD + (first - WIN_BLOCKS) * SEL_BLOCK, KT)
    for half in range(2):
        rows_q = jnp.concatenate([lhs_scr[hh * TQ + half * WH:hh * TQ + (half + 1) * WH, LANES:2 * LANES]
                                  for hh in range(NSA_GROUP)], axis=0)
        lhs_win = jnp.concatenate([jnp.zeros((NSA_GROUP * WH, LANES), BF16), rows_q], axis=1)
        sw_scr[half] = _dot_nt(lhs_win, kwa[pl.ds(pl.multiple_of(win_start + half * WH, WH), WKEYS), :])

    for i, rows in enumerate(row_blocks):
        s = [sa_scr[rows, c * LANES:(c + 1) * LANES] for c in range(ncp // LANES)]
        m = _row_max(s)
        p = [jnp.where(t > 0.5 * NEG, jnp.exp2(t - m), 0.0) for t in s]
        l = jnp.sum(functools.reduce(jnp.add, p), axis=1, keepdims=True)
        inv = jnp.where(l > 0.0, 1.0 / l, 0.0)
        inv_scr[rows] = inv
        inv_b = jnp.broadcast_to(inv, (RB, LANES))
        tok = slice((i % (TQ // RB)) * RB, (i % (TQ // RB) + 1) * RB)
        for c in range(ncp // LANES):
            lanes = slice(c * LANES, (c + 1) * LANES)
            pb_scr[rows, lanes] = p[c].astype(BF16)
            if i < TQ // RB:
                ps_scr[tok, lanes] = p[c] * inv_b
            else:
                ps_scr[tok, lanes] += p[c] * inv_b
    acc_c = _halves(_dot, pb_scr[:, 0:ncp], vca[...])
    out_scr[...] = acc_c[:, :D_HEAD] * (gate_col(0) * inv_scr[...])
    ps_hi, ps_lo = _split_bf16(ps_scr[...])
    ovt = ovt_ref[...]
    imp_t = _dot_nt(ovt, ps_hi) + _dot_nt(ovt, ps_lo)

    j_t = lax.broadcasted_iota(jnp.int32, (MAX_BLOCKS, TQ), 0)
    cur = first + lax.broadcasted_iota(jnp.int32, (MAX_BLOCKS, TQ), 1) // SEL_BLOCK
    forced = (j_t == 0) | (j_t == cur) | (j_t == cur - 1)
    score_t = jnp.where(forced, FORCED_SCORE, jnp.where(j_t <= cur, imp_t, NEG))
    sc_scr[...] = score_t

    cnt_scr[...] = jnp.zeros((MAX_BLOCKS, TQ), jnp.int32)
    sub_io = lax.broadcasted_iota(jnp.int32, (8, TQ), 0)
    n_groups = MAX_BLOCKS // 8
    for grp in range(n_groups):
        @pl.when(grp * 8 < first + SUB)
        def _count_group():
            targets = [score_t[8 * jg:8 * jg + 8] for jg in range(n_groups)]
            acc = [jnp.zeros((8, TQ), jnp.int32) for _ in range(n_groups)]
            for k in range(8):
                row_i = jnp.broadcast_to(sc_scr[grp * 8 + k:grp * 8 + k + 1, :], (8, TQ))
                for jg in range(n_groups):
                    if jg > grp:
                        beats = row_i >= targets[jg]
                    elif jg < grp:
                        beats = row_i > targets[jg]
                    else:
                        beats = (row_i > targets[jg]) | ((row_i == targets[jg]) & (sub_io > k))
                    acc[jg] = acc[jg] + beats.astype(jnp.int32)
            cnt_scr[...] += jnp.concatenate(acc, axis=0)

    cnt = cnt_scr[...]
    mask_t = jnp.where((cnt < SEL_TOPK) & (j_t <= cur), 0.0, NEG).astype(BF16)
    mask = _dot_nt(eye_ref[...], mask_t).astype(BF16)

    for hh in range(NSA_GROUP):
        lhs_scr[hh * TQ:(hh + 1) * TQ, 0:MAX_BLOCKS] = mask
    m_scr[...] = jnp.full((ROWS, LANES), NEG, F32)
    acc_scr[...] = jnp.zeros((ROWS, VAUG), F32)

    def tile_start(k):
        return pl.multiple_of(KPAD + (first - BLOCKS_PER_TILE) * SEL_BLOCK - k * KS, KT)

    def scores(k, dst):
        dst[...] = _halves(_dot_nt, lhs_scr[...], ksa[pl.ds(tile_start(k), KS), :])

    def consume(k, src, p_scr, with_table):
        n_lt = KS // LANES
        for rows in row_blocks:
            s = [src[rows, c * LANES:(c + 1) * LANES] for c in range(n_lt)]
            if with_table:
                s = [s[c] + near_ref[0, rows, c * LANES:(c + 1) * LANES] for c in range(n_lt)]
            m_prev = m_scr[rows]
            m = jnp.maximum(m_prev, _row_max(s))
            al_scr[rows] = jnp.exp2(m_prev - m)
            m_scr[rows] = m
            for c in range(n_lt):
                p_scr[rows, c * LANES:(c + 1) * LANES] = jnp.exp2((s[c] - m).astype(BF16))
        acc_scr[...] = acc_scr[...] * al_scr[...] + _dot(p_scr[:, 0:KS], vsa[pl.ds(tile_start(k), KS), :])

    scores(0, sa_scr)

    gate_w = gate_col(2)
    for half in range(2):
        for r0 in range(0, NSA_GROUP * WH, RB):
            rows = slice(r0, r0 + RB)
            s = [sw_scr[half, rows, c * LANES:(c + 1) * LANES] + win_ref[0, half, rows, c * LANES:(c + 1) * LANES]
                 for c in range(WKEYS // LANES)]
            m = _row_max(s)
            for c in range(WKEYS // LANES):
                pw_scr[half, rows, c * LANES:(c + 1) * LANES] = jnp.exp2((s[c] - m).astype(BF16))
        acc_w = _dot(pw_scr[half], vwa[pl.ds(pl.multiple_of(win_start + half * WH, WH), WKEYS), :])
        for hh in range(NSA_GROUP):
            dst = slice(hh * TQ + half * WH, hh * TQ + (half + 1) * WH)
            src = slice(hh * WH, (hh + 1) * WH)
            out_scr[dst] += acc_w[src, :D_HEAD] * (gate_w[dst] / acc_w[src, D_HEAD:D_HEAD + 1])

    def pair(with_table, u, carry):
        scores(2 * u + 1, sb_scr)
        consume(2 * u, sa_scr, pa_scr, with_table)
        scores(2 * u + 2, sa_scr)
        consume(2 * u + 1, sb_scr, pb_scr, False)
        return carry

    n_tiles = (qt + 2) // 2
    pair(True, 0, 0)
    lax.fori_loop(1, jnp.maximum(n_tiles // 2, 1), functools.partial(pair, False), 0)

    @pl.when((n_tiles % 2 == 1) & (n_tiles >= 3))
    def _odd_tile():
        consume(n_tiles - 1, sa_scr, pa_scr, False)

    acc_s = acc_scr[...]

    o = out_scr[...] + acc_s[:, :D_HEAD] * (gate_col(1) / acc_s[:, D_HEAD:D_HEAD + 1])
    for hh in range(NSA_GROUP):
        o_ref[0, :, hh * D_HEAD:(hh + 1) * D_HEAD] = o[hh * TQ:(hh + 1) * TQ].astype(BF16)


def _nsa_call(q, kcmp, vcmp, ks, vs, kw, vw, gates, near_t, win_t, q_half, cmp_lhs, ovt, eye):
    B, _, T, _ = q.shape
    G = NSA_KV_HEADS
    ncp = kcmp.shape[2]
    rows_kv = KPAD + T
    per_bg =lambda a: pl.BlockSpec((1, 1) + a.shape[2:], lambda b, g, i: (b, g, 0, 0))
    per_g = lambda a: pl.BlockSpec((1,) + a.shape[1:], lambda b, g, i: (g,) + (0,) * (a.ndim - 1))
    const2 = lambda a: pl.BlockSpec(a.shape, lambda b, g, i: (0, 0))
    return pl.pallas_call(
        _nsa_kernel,
        grid=(B, G, T // TQ),
        in_specs=[
            pl.BlockSpec((1, NSA_GROUP, TQ, D_HEAD), lambda b, g, i: (b, g, i, 0)),
            per_bg(kcmp), per_bg(vcmp), per_bg(ks), per_bg(vs), per_bg(kw), per_bg(vw),
            pl.BlockSpec((1, 1, TQ, LANES), lambda b, g, i: (b, g, i, 0)),
            per_g(near_t), per_g(win_t), per_g(q_half), per_g(cmp_lhs), const2(ovt), const2(eye),
        ],
        out_specs=pl.BlockSpec((1, TQ, NSA_GROUP * D_HEAD), lambda b, g, i: (b, i, g)),
        out_shape=jax.ShapeDtypeStruct((B, T, D_NSA), BF16),
        scratch_shapes=[
            pltpu.VMEM((rows_kv, KAUG), BF16), pltpu.VMEM((rows_kv, VAUG), BF16),
            pltpu.VMEM((rows_kv, KAUG), BF16), pltpu.VMEM((rows_kv, VAUG), BF16),
            pltpu.VMEM((ncp, VAUG), BF16), pltpu.VMEM((ncp, VAUG), BF16),
            pltpu.VMEM((MAX_BLOCKS, TQ), F32),
            pltpu.VMEM((ROWS, KAUG), BF16), pltpu.VMEM((ROWS, LANES), BF16),
            pltpu.VMEM((ROWS, LANES), F32), pltpu.VMEM((ROWS, LANES), F32), pltpu.VMEM((ROWS, 1), F32),
            pltpu.VMEM((ROWS, VAUG), F32),
            pltpu.VMEM((ROWS, KS), F32), pltpu.VMEM((ROWS, KS), F32),
            pltpu.VMEM((ROWS, KS), BF16), pltpu.VMEM((ROWS, KS), BF16),
            pltpu.VMEM((2, NSA_GROUP * WH, WKEYS), F32), pltpu.VMEM((2, NSA_GROUP * WH, WKEYS), BF16),
            pltpu.VMEM((TQ, ncp), F32), pltpu.VMEM((ROWS, D_HEAD), F32), pltpu.VMEM((MAX_BLOCKS, TQ), jnp.int32),
        ],
        compiler_params=pltpu.CompilerParams(
            dimension_semantics=("arbitrary", "arbitrary", "arbitrary"), vmem_limit_bytes=VMEM_LIMIT),
        name="nsa",
    )(q, kcmp, vcmp, ks, vs, kw, vw, gates, near_t, win_t, q_half, cmp_lhs, ovt, eye)


def _ffn_kernel(x_ref, on_ref, os_ref, wo_n_ref, wo_s_ref, fn_ref, wup_ref, cw_ref, cb_ref, wdn_ref, pm_ref,
                out_ref, prev_ref, perm_ref, h_ref, ua_scr, ub_scr, act_ref):
    tm = x_ref.shape[1]
    seg = tm // 8
    n_chunks = wup_ref.shape[1]

    @pl.when(pl.program_id(1) == 0)
    def _start_of_sequence():
        prev_ref[...] = jnp.zeros(prev_ref.shape, F32)

    x1 = x_ref[0] + _dot(on_ref[0], wo_n_ref[...]) + _dot(os_ref[0], wo_s_ref[...])
    ms = jnp.mean(x1 * x1, axis=-1, keepdims=True)
    hn = (x1 * lax.rsqrt(ms + EPS) * fn_ref[...]).astype(BF16)
    out_ref[0] = x1
    n_lt = x1.shape[1] // LANES
    h_ref[...] = _halves(_dot, pm_ref[...], hn).astype(BF16)

    def up(c, dst):
        h = h_ref[...]
        for half in range(2):
            dst[half] = _dot(h, wup_ref[half, c])

    first_sublane = lax.broadcasted_iota(jnp.int32, (8, FFN_CHUNK), 0) == 0

    def conv_gate(c, src):
        for r0 in range(0, tm, FFN_ROWS):
            ys = []
            for half in range(2):
                w = cw_ref[half, c]
                x0 = src[half, r0:r0 + FFN_ROWS]
                if r0 == 0:
                    wrap = [jnp.where(first_sublane, pltpu.roll(prev_ref[half, c, 8 * k:8 * k + 8], 1, 0),
                                      pltpu.roll(src[half, tm - 16 + 8 * k:tm - 8 + 8 * k], 1, 0)) for k in range(2)]
                    x1_ = jnp.concatenate([wrap[1], src[half, 0:FFN_ROWS - 8]], axis=0)
                    x2_ = jnp.concatenate([wrap[0], wrap[1], src[half, 0:FFN_ROWS - 16]], axis=0)
                else:
                    x1_ = src[half, r0 - 8:r0 + FFN_ROWS - 8]
                    x2_ = src[half, r0 - 16:r0 + FFN_ROWS - 16]
                ys.append(x0 * w[2:3] + x1_ * w[1:2] + x2_ * w[0:1] + cb_ref[half, c])
            a, g = ys
            col = pl.multiple_of(c * FFN_CHUNK, FFN_CHUNK)
            act_ref[r0:r0 + FFN_ROWS, pl.ds(col, FFN_CHUNK)] = ((g + g * jnp.tanh(g)) * a).astype(BF16)
        for half in range(2):
            prev_ref[half, c] = src[half, tm - 16:tm]

    up(0, ua_scr)

    def pair(u, carry):
        up(2 * u + 1, ub_scr)
        conv_gate(2 * u, ua_scr)
        up(2 * u + 2, ua_scr)
        conv_gate(2 * u + 1, ub_scr)
        return carry

    lax.fori_loop(0, (n_chunks - 1) // 2, pair, 0)
    conv_gate(n_chunks - 1, ua_scr)
    y = _halves(_dot, act_ref[...], wdn_ref[...])
    for c in range(n_lt):
        perm_ref[c] = y[:, c * LANES:(c + 1) * LANES]
    for sgm in range(8):
        natural = jnp.concatenate([perm_ref[c, pl.ds(sgm, seg, stride=8), :] for c in range(n_lt)], axis=1)
        out_ref[0, sgm * seg:(sgm + 1) * seg, :] += natural


def _ffn_call(x, o_nsa, o_sgu, wo_n, wo_s, ffn_norm, w_up, conv_w, conv_b, w_down, tm):
    B, T, _ = x.shape
    r = np.arange(tm)
    perm = jnp.asarray(((r % 8) * (tm // 8) + r // 8)[:, None] == r[None, :], BF16)
    n_chunks = w_up.shape[1]
    assert n_chunks % 2 == 1
    resident =lambda a: pl.BlockSpec(a.shape, lambda b, i: (0,) * a.ndim, pipeline_mode=pl.Buffered(1))
    tile = lambda w: pl.BlockSpec((1, tm, w), lambda b, i: (b, i, 0))
    return pl.pallas_call(
        _ffn_kernel,
        grid=(B, T // tm),
        in_specs=[tile(D_MODEL), tile(D_NSA), tile(D_GMLP), resident(wo_n), resident(wo_s),
                  resident(ffn_norm), resident(w_up), resident(conv_w), resident(conv_b), resident(w_down),
                  resident(perm)],
        out_specs=tile(D_MODEL),
        out_shape=jax.ShapeDtypeStruct((B, T, D_MODEL), F32),
        scratch_shapes=[pltpu.VMEM((2, n_chunks, 16, FFN_CHUNK), F32), pltpu.VMEM((D_MODEL // LANES, tm, LANES), F32),
                        pltpu.VMEM((tm, D_MODEL), BF16),
                        pltpu.VMEM((2, tm, FFN_CHUNK), F32), pltpu.VMEM((2, tm, FFN_CHUNK), F32),
                        pltpu.VMEM((tm, D_FF), BF16)],
        compiler_params=pltpu.CompilerParams(
            dimension_semantics=("arbitrary", "arbitrary"), vmem_limit_bytes=VMEM_LIMIT),
        name="ffn",
    )(x, o_nsa, o_sgu, wo_n, wo_s, ffn_norm, w_up, conv_w, conv_b, w_down, perm)


def _mixers(x, rel_bias, attn_norm, w_in, q_norm, k_norm_cmp, k_norm_slc, k_norm_win,
            cmp_pe_k, cmp_w1_k, cmp_b1_k, cmp_w2_k, cmp_pe_v, cmp_w1_v, cmp_b1_v, cmp_w2_v,
            sgu_norm, sgu_w, sgu_b):
    B, T, _ = x.shape
    assert T % 512 == 0 and T // SEL_BLOCK <= MAX_BLOCKS and (T // CMP_STRIDE) % LANES == 0
    ncp = T // CMP_STRIDE
    G, R = NSA_KV_HEADS, NSA_GROUP

    o_kv = D_NSA
    o_g = D_NSA + 6 * D_KV
    o_uv = o_g + N_GATES
    w_q = w_in[:, :o_kv].astype(BF16)
    w_kv = w_in[:, o_kv:o_g].astype(BF16)
    per_group = NSA_GROUP * N_BRANCH
    w_g = jnp.pad(w_in[:, o_g:o_uv].reshape(D_MODEL, G, per_group), ((0, 0), (0, 0), (0, LANES - per_group)))
    w_g = w_g.reshape(D_MODEL, G * LANES).astype(BF16)
    w_uv = w_in[:, o_uv:].astype(BF16)
    qn = (jnp.tile(q_norm, NSA_HEADS) * (D_HEAD ** -0.5 * LOG2E)).reshape(1, D_NSA)
    ksn = jnp.tile(k_norm_slc, G).reshape(1, D_KV)
    kwn = jnp.tile(k_norm_win, G).reshape(1, D_KV)
    sgu_bt = jnp.repeat(sgu_b.T, GMLP_GROUP_DIM, axis=1)
    grp = np.arange(D_NSA) // D_HEAD
    bd = jnp.asarray(grp[:, None] == grp[None, :], BF16)

    q, kc, vc, ks, vs, kw, vw, gates, o_sgu = _proj_call(
        x, attn_norm.reshape(1, D_MODEL), w_q, w_kv, w_g, w_uv, qn, ksn, kwn,
        sgu_norm.reshape(1, D_GMLP), sgu_w, sgu_bt, bd, tm=1024)

    half = CMP_STRIDE * D_HEAD
    kcmp, vcmp = _compress_call(
        kc.reshape(B, G, ncp, half), vc.reshape(B, G, ncp, half),
        cmp_pe_k.reshape(2, half), cmp_w1_k.astype(BF16), cmp_b1_k.reshape(1, CMP_HIDDEN),
        cmp_w2_k.astype(BF16), k_norm_cmp.reshape(1, D_HEAD),
        cmp_pe_v.reshape(2, half), cmp_w1_v.astype(BF16), cmp_b1_v.reshape(1, CMP_HIDDEN),
        cmp_w2_v.astype(BF16))

    near_t, win_t, q_half, cmp_lhs = _bias_tables(rel_bias)
    jj = np.arange(MAX_BLOCKS)[:, None]
    nn = np.arange(ncp)[None, :]
    ovt = jnp.asarray((nn >= 4 * jj - 1) & (nn <= 4 * jj + 3), BF16)
    eye = jnp.asarray(np.eye(TQ), BF16)

    o_nsa = _nsa_call(q, kcmp, vcmp, ks, vs, kw, vw, gates, near_t, win_t, q_half, cmp_lhs, ovt, eye)
    return o_nsa, o_sgu


def _layer(x, rel_bias, attn_norm, w_in, q_norm, k_norm_cmp, k_norm_slc, k_norm_win,
           cmp_pe_k, cmp_w1_k, cmp_b1_k, cmp_w2_k, cmp_pe_v, cmp_w1_v, cmp_b1_v, cmp_w2_v,
           sgu_norm, sgu_w, sgu_b, w_out, ffn_norm, w_up, conv_w, conv_b, w_down):
    o_nsa, o_sgu = _mixers(x, rel_bias, attn_norm, w_in, q_norm, k_norm_cmp, k_norm_slc, k_norm_win,
                           cmp_pe_k, cmp_w1_k, cmp_b1_k, cmp_w2_k, cmp_pe_v, cmp_w1_v, cmp_b1_v, cmp_w2_v,
                           sgu_norm, sgu_w, sgu_b)
    wo = w_out.astype(BF16)
    n_chunks = D_FF // FFN_CHUNK
    wu = jnp.transpose(w_up.astype(BF16).reshape(D_MODEL, 2, n_chunks, FFN_CHUNK), (1, 2, 0, 3))
    half_gate = jnp.asarray([1.0, 0.5], F32).reshape(2, 1, 1, 1)
    cw = jnp.transpose(conv_w.reshape(conv_w.shape[0], 2, n_chunks, FFN_CHUNK), (1, 2, 0, 3)) * half_gate
    cb = conv_b.reshape(2, n_chunks, 1, FFN_CHUNK) * half_gate
    wd = w_down.astype(BF16)
    return _ffn_call(x, o_nsa, o_sgu, wo[:D_NSA], wo[D_NSA:], ffn_norm.reshape(1, D_MODEL), wu, cw, cb, wd, tm=512)


def kernel(x, rel_bias, attn_norm, w_in, q_norm, k_norm_cmp, k_norm_slc, k_norm_win,
           cmp_pe_k, cmp_w1_k, cmp_b1_k, cmp_w2_k, cmp_pe_v, cmp_w1_v, cmp_b1_v, cmp_w2_v,
           sgu_norm, sgu_w, sgu_b, w_out, ffn_norm, w_up, conv_w, conv_b, w_down):
    depth = attn_norm.shape[0]
    for l in range(depth):
        x = _layer(x, rel_bias, attn_norm[l], w_in[l], q_norm[l], k_norm_cmp[l], k_norm_slc[l], k_norm_win[l],
                   cmp_pe_k[l], cmp_w1_k[l], cmp_b1_k[l], cmp_w2_k[l], cmp_pe_v[l], cmp_w1_v[l], cmp_b1_v[l],
                   cmp_w2_v[l], sgu_norm[l], sgu_w[l], sgu_b[l], w_out[l], ffn_norm[l], w_up[l], conv_w[l],
                   conv_b[l], w_down[l])
    return x
```

```python
import functools
import math

import numpy as np
import jax
import jax.numpy as jnp
from jax import lax
from jax.experimental import pallas as pl
from jax.experimental.pallas import tpu as pltpu

F32 = jnp.float32
BF16 = jnp.bfloat16

D_MODEL = 1024
D_HEAD = 64
NSA_HEADS = 8
NSA_KV_HEADS = 2
NSA_GROUP = NSA_HEADS // NSA_KV_HEADS
D_NSA = NSA_HEADS * D_HEAD
D_KV = NSA_KV_HEADS * D_HEAD
N_BRANCH = 3
N_GATES = NSA_HEADS * N_BRANCH
CMP_BLOCK = 32
CMP_STRIDE = 16
CMP_HIDDEN = 256
SEL_BLOCK = 64
SEL_TOPK = 16
WINDOW = 512
GMLP_GROUPS = 8
GMLP_GROUP_DIM = 64
D_GMLP = GMLP_GROUPS * GMLP_GROUP_DIM
CHUNK = 128
D_MIX = D_NSA + D_GMLP
REL_BUCKETS = 32
REL_MAX_DIST = 128
D_FF = 2816
EPS = 1e-6
NEG = -1e30
FORCED_SCORE = 1e4
LOG2E = 1.4426950408889634

LANES = 128
TQ = 256
SUB = TQ // SEL_BLOCK
ROWS = NSA_GROUP * TQ
KT = 256
KS = 2 * KT
WH = TQ // 2
WKEYS = WINDOW + WH
RB = 64
BLOCKS_PER_TILE = KT // SEL_BLOCK
WIN_BLOCKS = WINDOW // SEL_BLOCK
KPAD = 1280
N_TOEPLITZ = 3
MAX_BLOCKS = 64
KAUG = 256
VAUG = 128
CBAND = 28
FFN_CHUNK = 256
FFN_ROWS = 64
VMEM_LIMIT = 56 * 1024 * 1024

_NT = (((1,), (1,)), ((), ()))


def _dot(a, b):
    return jnp.dot(a, b, preferred_element_type=F32)


def _dot_nt(a, b):
    return lax.dot_general(a, b, _NT, preferred_element_type=F32)


def _halves(dot, a, b):
    h = a.shape[0] // 2
    return jnp.concatenate([dot(a[:h], b), dot(a[h:], b)], axis=0)


def _split_bf16(x):
    hi = x.astype(BF16)
    lo = (x - hi.astype(F32)).astype(BF16)
    return hi, lo


def _gelu_tanh(x):
    return 0.5 * x * (1.0 + jnp.tanh(0.7978845608028654 * (x + 0.044715 * (x * x * x))))


def _sigmoid(x):
    return 0.5 * (1.0 + jnp.tanh(0.5 * x))


def _group_mean_sq(t, ones_blockdiag, width):
    t2 = t * t
    hi, lo = _split_bf16(t2)
    return (_dot(hi, ones_blockdiag) + _dot(lo, ones_blockdiag)) * (1.0 / width)


def _proj_kernel(x_ref, an_ref, wq_ref, wkv_ref, wg_ref, wuv_ref, qn_ref, ksn_ref, kwn_ref,
                 sgun_ref, sguw_ref, sgub_ref, bd_ref,
                 q_out, kc_out, vc_out, ks_out, vs_out, kw_out, vw_out, g_out, sgu_out):
    tm = x_ref.shape[1]
    x = x_ref[0]
    ms = jnp.mean(x * x, axis=-1, keepdims=True)
    h = (x * lax.rsqrt(ms + EPS) * an_ref[...]).astype(BF16)

    bd = bd_ref[...]
    q = _halves(_dot, h, wq_ref[...])
    qn = q * lax.rsqrt(_group_mean_sq(q, bd, D_HEAD) + EPS) * qn_ref[...]
    for hh in range(NSA_HEADS):
        q_out[0, hh] = qn[:, hh * D_HEAD:(hh + 1) * D_HEAD].astype(BF16)

    kv = _halves(_dot, h, wkv_ref[...])
    bd_kv = bd[:D_KV, :D_KV]
    kc = kv[:, 0 * D_KV:1 * D_KV]
    vc = kv[:, 1 * D_KV:2 * D_KV]
    ks = kv[:, 2 * D_KV:3 * D_KV]
    vs = kv[:, 3 * D_KV:4 * D_KV]
    kw = kv[:, 4 * D_KV:5 * D_KV]
    vw = kv[:, 5 * D_KV:6 * D_KV]
    ks = ks * lax.rsqrt(_group_mean_sq(ks, bd_kv, D_HEAD) + EPS) * ksn_ref[...]
    kw = kw * lax.rsqrt(_group_mean_sq(kw, bd_kv, D_HEAD) + EPS) * kwn_ref[...]
    for t, o_ref in ((kc, kc_out), (vc, vc_out), (ks, ks_out), (vs, vs_out), (kw, kw_out), (vw, vw_out)):
        for g in range(NSA_KV_HEADS):
            o_ref[0, g] = t[:, g * D_HEAD:(g + 1) * D_HEAD].astype(BF16)

    gates = _sigmoid(_dot(h, wg_ref[...]))
    for g in range(NSA_KV_HEADS):
        g_out[0, g] = gates[:, g * LANES:(g + 1) * LANES]

    uv = _gelu_tanh(_halves(_dot, h, wuv_ref[...]))
    u = uv[:, :D_GMLP]
    v = uv[:, D_GMLP:]
    vms = jnp.mean(v * v, axis=-1, keepdims=True)
    vb = (v * lax.rsqrt(vms + EPS) * sgun_ref[...]).astype(BF16)

    row = lax.broadcasted_iota(jnp.int32, (CHUNK, CHUNK), 0)
    col = lax.broadcasted_iota(jnp.int32, (CHUNK, CHUNK), 1)
    tril = col <= row
    w_tril = [jnp.where(tril, sguw_ref[g], 0.0).astype(BF16) for g in range(GMLP_GROUPS)]
    w_pair = [jnp.concatenate([w_tril[2 * p], w_tril[2 * p + 1]], axis=1) for p in range(D_GMLP // LANES)]
    first_half = lax.broadcasted_iota(jnp.int32, (CHUNK, LANES), 1) < GMLP_GROUP_DIM
    zero = jnp.zeros((CHUNK, LANES), BF16)
    for c in range(tm // CHUNK):
        rows = slice(c * CHUNK, (c + 1) * CHUNK)
        zs = []
        for p in range(D_GMLP // LANES):
            blk = vb[rows, p * LANES:(p + 1) * LANES]
            rhs = jnp.concatenate([jnp.where(first_half, blk, zero), jnp.where(first_half, zero, blk)], axis=0)
            zs.append(_dot(w_pair[p], rhs))
        z = jnp.concatenate(zs, axis=1) + sgub_ref[...]
        sgu_out[0, rows, :] = (u[rows, :] * z).astype(BF16)


def _proj_call(x, attn_norm, w_q, w_kv, w_g, w_uv, qn, ksn, kwn, sgu_norm, sgu_w, sgu_bt, bd, tm):
    B, T, _ = x.shape
    const2 = lambda b, i: (0, 0)
    const3 = lambda b, i: (0, 0, 0)
    head_spec = lambda nh: pl.BlockSpec((1, nh, tm, D_HEAD), lambda b, i: (b, 0, i, 0))
    kv_shape = jax.ShapeDtypeStruct((B, NSA_KV_HEADS, T, D_HEAD), BF16)
    return pl.pallas_call(
        _proj_kernel,
        grid=(B, T // tm),
        in_specs=[
            pl.BlockSpec((1, tm, D_MODEL), lambda b, i: (b, i, 0)),
            pl.BlockSpec((1, D_MODEL), const2),
            pl.BlockSpec(w_q.shape, const2),
            pl.BlockSpec(w_kv.shape, const2),
            pl.BlockSpec(w_g.shape, const2),
            pl.BlockSpec(w_uv.shape, const2),
            pl.BlockSpec((1, D_NSA), const2),
            pl.BlockSpec((1, D_KV), const2),
            pl.BlockSpec((1, D_KV), const2),
            pl.BlockSpec((1, D_GMLP), const2),
            pl.BlockSpec(sgu_w.shape, const3),
            pl.BlockSpec(sgu_bt.shape, const2),
            pl.BlockSpec(bd.shape, const2),
        ],
        out_specs=[
            head_spec(NSA_HEADS),
            head_spec(NSA_KV_HEADS), head_spec(NSA_KV_HEADS), head_spec(NSA_KV_HEADS),
            head_spec(NSA_KV_HEADS), head_spec(NSA_KV_HEADS), head_spec(NSA_KV_HEADS),
            pl.BlockSpec((1, NSA_KV_HEADS, tm, LANES), lambda b, i: (b, 0, i, 0)),
            pl.BlockSpec((1, tm, D_GMLP), lambda b, i: (b, i, 0)),
        ],
        out_shape=[
            jax.ShapeDtypeStruct((B, NSA_HEADS, T, D_HEAD), BF16),
            kv_shape, kv_shape, kv_shape, kv_shape, kv_shape, kv_shape,
            jax.ShapeDtypeStruct((B, NSA_KV_HEADS, T, LANES), F32),
            jax.ShapeDtypeStruct((B, T, D_GMLP), BF16),
        ],
        compiler_params=pltpu.CompilerParams(
            dimension_semantics=("arbitrary", "arbitrary"), vmem_limit_bytes=VMEM_LIMIT),
        name="proj",
    )(x, attn_norm, w_q, w_kv, w_g, w_uv, qn, ksn, kwn, sgu_norm, sgu_w, sgu_bt, bd)


def _compress_one(tok_ref, pe_ref, w1_ref, b1_ref, w2_ref):
    half = CMP_STRIDE * D_HEAD
    tok = tok_ref[0, 0].astype(F32)
    top = (tok + pe_ref[0:1, :]).astype(BF16)
    bot = (tok + pe_ref[1:2, :]).astype(BF16)
    a = _dot(top, w1_ref[:half, :])
    b = _dot(bot, w1_ref[half:, :])
    ncp = a.shape[0]
    pre = a + pltpu.roll(b, ncp - 1, 0) + b1_ref[...]
    hid = _gelu_tanh(pre).astype(BF16)
    return _dot(hid, w2_ref[...])


def _compress_kernel(kc_ref, vc_ref, pek_ref, w1k_ref, b1k_ref, w2k_ref, kn_ref,
                     pev_ref, w1v_ref, b1v_ref, w2v_ref, k_out, v_out):
    k = _compress_one(kc_ref, pek_ref, w1k_ref, b1k_ref, w2k_ref)
    kms = jnp.mean(k * k, axis=-1, keepdims=True)
    k_out[0, 0] = (k * lax.rsqrt(kms + EPS) * kn_ref[...]).astype(BF16)
    v_out[0, 0] = _compress_one(vc_ref, pev_ref, w1v_ref, b1v_ref, w2v_ref).astype(BF16)


def _compress_call(kc, vc, pek, w1k, b1k, w2k, kn, pev, w1v, b1v, w2v):
    B, G, ncp, width = kc.shape
    const2 = lambda b, g: (0, 0)
    tok_spec = pl.BlockSpec((1, 1, ncp, width), lambda b, g: (b, g, 0, 0))
    out_spec = pl.BlockSpec((1, 1, ncp, D_HEAD), lambda b, g: (b, g, 0, 0))
    full = lambda a: pl.BlockSpec(a.shape, const2)
    out_shape = jax.ShapeDtypeStruct((B, G, ncp, D_HEAD), BF16)
    return pl.pallas_call(
        _compress_kernel,
        grid=(B, G),
        in_specs=[tok_spec, tok_spec, full(pek), full(w1k), full(b1k), full(w2k), full(kn),
                  full(pev), full(w1v), full(b1v), full(w2v)],
        out_specs=[out_spec, out_spec],
        out_shape=[out_shape, out_shape],
        compiler_params=pltpu.CompilerParams(
            dimension_semantics=("arbitrary", "arbitrary"), vmem_limit_bytes=VMEM_LIMIT),
        name="compress",
    )(kc, vc, pek, w1k, b1k, w2k, kn, pev, w1v, b1v, w2v)


def _rel_bucket_np(dist):
    max_exact = REL_BUCKETS // 2
    d = np.maximum(dist, 1).astype(np.float32)
    log_b = max_exact + (np.log(d / np.float32(max_exact)) / np.float32(math.log(REL_MAX_DIST / max_exact))
                         * np.float32(REL_BUCKETS - max_exact)).astype(np.int32)
    log_b = np.clip(log_b, max_exact, REL_BUCKETS - 1)
    return np.where(dist < max_exact, np.maximum(dist, 0), log_b)


def _bias_tables(rel_bias):
    rb = rel_bias.astype(F32) * LOG2E
    qq = np.arange(SEL_BLOCK)[:, None]
    kp = np.arange(SEL_BLOCK)[None, :]
    tile_d = [m * SEL_BLOCK + qq - kp for m in range(N_TOEPLITZ)]
    s_ = np.arange(SUB)[:, None, None]
    e_ = np.arange(CBAND)[None, None, :]
    dist_c = qq[None, :, :] + CMP_STRIDE * (e_ - 4 * (SUB - 1) + 4 * s_) - (3 * CMP_STRIDE + CMP_BLOCK - 1)
    all_d = np.concatenate([d.reshape(-1) for d in tile_d] + [dist_c.reshape(-1)])
    onehot = np.eye(REL_BUCKETS, dtype=np.float32)[_rel_bucket_np(all_d)]
    vals = jnp.dot(jnp.asarray(onehot), rb, precision=lax.Precision.HIGHEST)
    vals = jnp.where(jnp.asarray(all_d >= 0)[:, None], vals, NEG)
    n_t = N_TOEPLITZ * SEL_BLOCK * SEL_BLOCK
    toep = jnp.transpose(vals[:n_t].reshape(N_TOEPLITZ, SEL_BLOCK, SEL_BLOCK, NSA_HEADS), (0, 3, 1, 2))
    band = jnp.transpose(vals[n_t:].reshape(SUB, SEL_BLOCK, CBAND, NSA_HEADS), (3, 0, 1, 2))
    far = rb[REL_BUCKETS - 1]
    far_tile = jnp.broadcast_to(far[:, None, None], (NSA_HEADS, SEL_BLOCK, SEL_BLOCK))
    neg_tile = jnp.full((NSA_HEADS, SEL_BLOCK, SEL_BLOCK), NEG, F32)
    edge_tile = jnp.where(jnp.asarray(kp > qq)[None], far_tile, NEG)

    def tile(m, windowed):
        if m < 0 or (windowed and m > WIN_BLOCKS):
            return neg_tile
        if windowed and m == WIN_BLOCKS:
            return edge_tile
        return toep[m] if m < N_TOEPLITZ else far_tile

    def table(first_m, n_blocks, windowed):
        rows = [jnp.concatenate([tile(first_m + s - c, windowed) for c in range(n_blocks)], axis=2)
                for s in range(SUB)]
        t = jnp.concatenate(rows, axis=1)
        return t.reshape(NSA_KV_HEADS, ROWS, n_blocks * SEL_BLOCK)

    far_rows = jnp.repeat(far, TQ).reshape(NSA_KV_HEADS, ROWS, 1)
    near_t = table(BLOCKS_PER_TILE, 2 * BLOCKS_PER_TILE, False) - far_rows
    win_t = table(WIN_BLOCKS, WIN_BLOCKS + BLOCKS_PER_TILE, True) - far_rows
    win_t = win_t.reshape(NSA_KV_HEADS, NSA_GROUP, 2, WH, (WIN_BLOCKS + BLOCKS_PER_TILE) * SEL_BLOCK)
    win_t = jnp.stack([win_t[:, :, half, :, half * WH:half * WH + WKEYS].reshape(NSA_KV_HEADS, NSA_GROUP * WH, WKEYS)
                       for half in range(2)], axis=1)
    band_t = band.reshape(NSA_KV_HEADS, ROWS, CBAND)
    band_hi = band_t.astype(BF16)
    band_lo = (band_t - band_hi.astype(F32)).astype(BF16)
    far_hi = far_rows.astype(BF16)
    far_lo = (far_rows - far_hi.astype(F32)).astype(BF16)
    q_half = jnp.concatenate([jnp.zeros((NSA_KV_HEADS, ROWS, D_HEAD), BF16),
                              jnp.full((NSA_KV_HEADS, ROWS, 1), NEG, BF16), far_hi, far_lo,
                              jnp.zeros((NSA_KV_HEADS, ROWS, LANES - D_HEAD - 3), BF16)], axis=2)
    neg_col = jnp.full((NSA_KV_HEADS, ROWS, 1), NEG, BF16)
    zeros = jnp.zeros((NSA_KV_HEADS, ROWS, D_HEAD - 2 * CBAND - 3), BF16)
    q_lanes = jnp.zeros((NSA_KV_HEADS, ROWS, D_HEAD), BF16)
    cmp_lhs = jnp.concatenate([q_lanes, band_hi, band_lo, far_hi, far_lo, neg_col, zeros], axis=2)
    return near_t, win_t, q_half, cmp_lhs


def _row_max(lane_tiles):
    m = jnp.max(functools.reduce(jnp.maximum, lane_tiles), axis=1, keepdims=True)
    return jnp.broadcast_to(m, lane_tiles[0].shape)


def _nsa_kernel(q_ref, kc_ref, vc_ref, ks_ref, vs_ref, kw_ref, vw_ref, g_ref,
                near_ref, win_ref, qhalf_ref, cmpl_ref, ovt_ref, eye_ref,
                o_ref, ksa, vsa, kwa, vwa, kca, vca, sc_scr, lhs_scr, lhsc_scr, m_scr, al_scr, inv_scr, acc_scr, sa_scr, sb_scr, pa_scr, pb_scr,
                sw_scr, pw_scr, ps_scr, out_scr, cnt_scr):
    qt = pl.program_id(2)
    first = qt * SUB
    T = ks_ref.shape[2]
    ncp = kc_ref.shape[2]

    @pl.when(qt == 0)
    def _init():
        n_chunks = ksa.shape[0] // KT
        r_io = lax.broadcasted_iota(jnp.int32, (KT, KAUG), 0)
        l_io = lax.broadcasted_iota(jnp.int32, (KT, KAUG), 1)
        lv = lax.broadcasted_iota(jnp.int32, (KT, VAUG), 1)
        vpat = jnp.where(lv == D_HEAD, 1.0, 0.0).astype(BF16)

        def fill(c, carry):
            rows = pl.ds(pl.multiple_of(c * KT, KT), KT)
            key = r_io + (c * KT - KPAD)
            blk = key // SEL_BLOCK
            real = (key >= 0) & (key < T)
            hot = real & (((l_io < D_HEAD) & (l_io == blk))
                          | ((l_io >= D_HEAD) & (l_io < 2 * D_HEAD) & (l_io - D_HEAD == blk)))
            hot = hot | ((~real) & (l_io == 3 * D_HEAD)) | (real & (l_io > 3 * D_HEAD) & (l_io <= 3 * D_HEAD + 2))
            pat = jnp.where(hot, 1.0, 0.0).astype(BF16)
            ksa[rows, :] = pat
            kwa[rows, :] = pat
            vsa[rows, :] = vpat
            vwa[rows, :] = vpat
            return carry

        lax.fori_loop(0, n_chunks, fill, 0)
        ksa[KPAD:KPAD + T, 2 * D_HEAD:3 * D_HEAD] = ks_ref[0, 0]
        kwa[KPAD:KPAD + T, 2 * D_HEAD:3 * D_HEAD] = kw_ref[0, 0]
        vsa[KPAD:KPAD + T, 0:D_HEAD] = vs_ref[0, 0]
        vwa[KPAD:KPAD + T, 0:D_HEAD] = vw_ref[0, 0]
        lhs_scr[:, 0:LANES] = jnp.zeros((ROWS, LANES), BF16)
        lc = lax.broadcasted_iota(jnp.int32, (ncp, VAUG), 1)
        kca[...] = jnp.zeros((ncp, VAUG), BF16)
        kca[:, 0:D_HEAD] = kc_ref[0, 0]
        vca[...] = jnp.where(lc == D_HEAD, 1.0, 0.0).astype(BF16)
        vca[:, 0:D_HEAD] = vc_ref[0, 0]

    q4 = q_ref[0].reshape(ROWS, D_HEAD)
    gates = g_ref[0, 0]
    row_blocks = [slice(r, r + RB) for r in range(0, ROWS, RB)]

    gate_rows = jnp.concatenate(
        [gates if hh == 0 else pltpu.roll(gates, LANES - hh * N_BRANCH, 1) for hh in range(NSA_GROUP)], axis=0)

    def gate_col(branch):
        return gate_rows[:, branch:branch + 1]

    lane_c = lax.broadcasted_iota(jnp.int32, (ncp, LANES), 1)
    n_c = lax.broadcasted_iota(jnp.int32, (ncp, LANES), 0)
    e = lane_c - D_HEAD
    base = 4 * first + (4 * SUB - 1)
    ind = (((e >= 0) & (e < CBAND) & (n_c == base - e))
           | ((e >= CBAND) & (e < 2 * CBAND) & (n_c == base - (e - CBAND)))
           | (((e == 2 * CBAND) | (e == 2 * CBAND + 1)) & (n_c < base - (CBAND - 1)))
           | ((e == 2 * CBAND + 2) & (n_c > base)))
    rhs_c = jnp.where(lane_c < D_HEAD, kca[...], jnp.where(ind, 1.0, 0.0).astype(BF16))
    lhsc_scr[...] = cmpl_ref[0]
    lhsc_scr[:, 0:D_HEAD] = q4
    sa_scr[:, 0:ncp] = _halves(_dot_nt, lhsc_scr[...], rhs_c)

    lhs_scr[:, LANES:2 * LANES] = qhalf_ref[0]
    lhs_scr[:, LANES:LANES + D_HEAD] = q4
    win_start = pl.multiple_of(KPAD + (first - WIN_BLOCKS) * SEL_BLOCK, KT)
    for half in range(2):
        rows_q = jnp.concatenate([lhs_scr[hh * TQ + half * WH:hh * TQ + (half + 1) * WH, LANES:2 * LANES]
                                  for hh in range(NSA_GROUP)], axis=0)
        lhs_win = jnp.concatenate([jnp.zeros((NSA_GROUP * WH, LANES), BF16), rows_q], axis=1)
        sw_scr[half] = _dot_nt(lhs_win, kwa[pl.ds(pl.multiple_of(win_start + half * WH, WH), WKEYS), :])

    for i, rows in enumerate(row_blocks):
        s = [sa_scr[rows, c * LANES:(c + 1) * LANES] for c in range(ncp // LANES)]
        m = _row_max(s)
        p = [jnp.where(t > 0.5 * NEG, jnp.exp2(t - m), 0.0) for t in s]
        l = jnp.sum(functools.reduce(jnp.add, p), axis=1, keepdims=True)
        inv = jnp.where(l > 0.0, 1.0 / l, 0.0)
        inv_scr[rows] = inv
        inv_b = jnp.broadcast_to(inv, (RB, LANES))
        tok = slice((i % (TQ // RB)) * RB, (i % (TQ // RB) + 1) * RB)
        for c in range(ncp // LANES):
            lanes = slice(c * LANES, (c + 1) * LANES)
            pb_scr[rows, lanes] = p[c].astype(BF16)
            if i < TQ // RB:
                ps_scr[tok, lanes] = p[c] * inv_b
            else:
                ps_scr[tok, lanes] += p[c] * inv_b
    acc_c = _halves(_dot, pb_scr[:, 0:ncp], vca[...])
    out_scr[...] = acc_c[:, :D_HEAD] * (gate_col(0) * inv_scr[...])
    ps_hi, ps_lo = _split_bf16(ps_scr[...])
    ovt = ovt_ref[...]
    imp_t = _dot_nt(ovt, ps_hi) + _dot_nt(ovt, ps_lo)

    j_t = lax.broadcasted_iota(jnp.int32, (MAX_BLOCKS, TQ), 0)
    cur = first + lax.broadcasted_iota(jnp.int32, (MAX_BLOCKS, TQ), 1) // SEL_BLOCK
    forced = (j_t == 0) | (j_t == cur) | (j_t == cur - 1)
    score_t = jnp.where(forced, FORCED_SCORE, jnp.where(j_t <= cur, imp_t, NEG))
    sc_scr[...] = score_t

    cnt_scr[...] = jnp.zeros((MAX_BLOCKS, TQ), jnp.int32)
    sub_io = lax.broadcasted_iota(jnp.int32, (8, TQ), 0)
    n_groups = MAX_BLOCKS // 8
    for grp in range(n_groups):
        @pl.when(grp * 8 < first + SUB)
        def _count_group():
            targets = [score_t[8 * jg:8 * jg + 8] for jg in range(n_groups)]
            acc = [jnp.zeros((8, TQ), jnp.int32) for _ in range(n_groups)]
            for k in range(8):
                row_i = jnp.broadcast_to(sc_scr[grp * 8 + k:grp * 8 + k + 1, :], (8, TQ))
                for jg in range(n_groups):
                    if jg > grp:
                        beats = row_i >= targets[jg]
                    elif jg < grp:
                        beats = row_i > targets[jg]
                    else:
                        beats = (row_i > targets[jg]) | ((row_i == targets[jg]) & (sub_io > k))
                    acc[jg] = acc[jg] + beats.astype(jnp.int32)
            cnt_scr[...] += jnp.concatenate(acc, axis=0)

    cnt = cnt_scr[...]
    mask_t = jnp.where((cnt < SEL_TOPK) & (j_t <= cur), 0.0, NEG).astype(BF16)
    mask = _dot_nt(eye_ref[...], mask_t).astype(BF16)

    for hh in range(NSA_GROUP):
        lhs_scr[hh * TQ:(hh + 1) * TQ, 0:MAX_BLOCKS] = mask
    m_scr[...] = jnp.full((ROWS, LANES), NEG, F32)
    acc_scr[...] = jnp.zeros((ROWS, VAUG), F32)

    def tile_start(k):
        return pl.multiple_of(KPAD + (first - BLOCKS_PER_TILE) * SEL_BLOCK - k * KS, KT)

    def scores(k, dst):
        dst[...] = _halves(_dot_nt, lhs_scr[...], ksa[pl.ds(tile_start(k), KS), :])

    def consume(k, src, p_scr, with_table):
        n_lt = KS // LANES
        for rows in row_blocks:
            n_use = n_lt
            if with_table:
                sub = (rows.start % TQ) // SEL_BLOCK
                n_use = -(-(BLOCKS_PER_TILE + sub + 1) * SEL_BLOCK // LANES)
                n_zero = (sub + BLOCKS_PER_TILE + 1 - N_TOEPLITZ) * SEL_BLOCK // LANES
            s = [src[rows, c * LANES:(c + 1) * LANES] for c in range(n_use)]
            if with_table:
                s = [s[c] if c < n_zero else s[c] + near_ref[0, rows, c * LANES:(c + 1) * LANES]
                     for c in range(n_use)]
            m_prev = m_scr[rows]
            m = jnp.maximum(m_prev, _row_max(s))
            al_scr[rows] = jnp.exp2(m_prev - m)
            m_scr[rows] = m
            for c in range(n_use):
                p_scr[rows, c * LANES:(c + 1) * LANES] = jnp.exp2((s[c] - m).astype(BF16))
            for c in range(n_use, n_lt):
                p_scr[rows, c * LANES:(c + 1) * LANES] = jnp.zeros((RB, LANES), BF16)
        acc_scr[...] = acc_scr[...] * al_scr[...] + _dot(p_scr[:, 0:KS], vsa[pl.ds(tile_start(k), KS), :])

    scores(0, sa_scr)

    gate_w = gate_col(2)
    for half in range(2):
        for r0 in range(0, NSA_GROUP * WH, RB):
            rows = slice(r0, r0 + RB)
            s = [sw_scr[half, rows, c * LANES:(c + 1) * LANES] if c in (1, 2) else
                 sw_scr[half, rows, c * LANES:(c + 1) * LANES] + win_ref[0, half, rows, c * LANES:(c + 1) * LANES]
                 for c in range(WKEYS // LANES)]
            m = _row_max(s)
            for c in range(WKEYS // LANES):
                pw_scr[half, rows, c * LANES:(c + 1) * LANES] = jnp.exp2((s[c] - m).astype(BF16))
        acc_w = _dot(pw_scr[half], vwa[pl.ds(pl.multiple_of(win_start + half * WH, WH), WKEYS), :])
        for hh in range(NSA_GROUP):
            dst = slice(hh * TQ + half * WH, hh * TQ + (half + 1) * WH)
            src = slice(hh * WH, (hh + 1) * WH)
            out_scr[dst] += acc_w[src, :D_HEAD] * (gate_w[dst] / acc_w[src, D_HEAD:D_HEAD + 1])

    def pair(with_table, u, carry):
        scores(2 * u + 1, sb_scr)
        consume(2 * u, sa_scr, pa_scr, with_table)
        scores(2 * u + 2, sa_scr)
        consume(2 * u + 1, sb_scr, pb_scr, False)
        return carry

    n_tiles = (qt + 2) // 2
    pair(True, 0, 0)
    lax.fori_loop(1, jnp.maximum(n_tiles // 2, 1), functools.partial(pair, False), 0)

    @pl.when((n_tiles % 2 == 1) & (n_tiles >= 3))
    def _odd_tile():
        consume(n_tiles - 1, sa_scr, pa_scr, False)

    acc_s = acc_scr[...]

    o = out_scr[...] + acc_s[:, :D_HEAD] * (gate_col(1) / acc_s[:, D_HEAD:D_HEAD + 1])
    for hh in range(NSA_GROUP):
        o_ref[0, :, hh * D_HEAD:(hh + 1) * D_HEAD] = o[hh * TQ:(hh + 1) * TQ].astype(BF16)


def _nsa_call(q, kcmp, vcmp, ks, vs, kw, vw, gates, near_t, win_t, q_half, cmp_lhs, ovt, eye):
    B, _, T, _ = q.shape
    G = NSA_KV_HEADS
    ncp = kcmp.shape[2]
    rows_kv = KPAD + T
    per_bg =lambda a: pl.BlockSpec((1, 1) + a.shape[2:], lambda b, g, i: (b, g, 0, 0))
    per_g = lambda a: pl.BlockSpec((1,) + a.shape[1:], lambda b, g, i: (g,) + (0,) * (a.ndim - 1))
    const2 = lambda a: pl.BlockSpec(a.shape, lambda b, g, i: (0, 0))
    return pl.pallas_call(
        _nsa_kernel,
        grid=(B, G, T // TQ),
        in_specs=[
            pl.BlockSpec((1, NSA_GROUP, TQ, D_HEAD), lambda b, g, i: (b, g, i, 0)),
            per_bg(kcmp), per_bg(vcmp), per_bg(ks), per_bg(vs), per_bg(kw), per_bg(vw),
            pl.BlockSpec((1, 1, TQ, LANES), lambda b, g, i: (b, g, i, 0)),
            per_g(near_t), per_g(win_t), per_g(q_half), per_g(cmp_lhs), const2(ovt), const2(eye),
        ],
        out_specs=pl.BlockSpec((1, TQ, NSA_GROUP * D_HEAD), lambda b, g, i: (b, i, g)),
        out_shape=jax.ShapeDtypeStruct((B, T, D_NSA), BF16),
        scratch_shapes=[
            pltpu.VMEM((rows_kv, KAUG), BF16), pltpu.VMEM((rows_kv, VAUG), BF16),
            pltpu.VMEM((rows_kv, KAUG), BF16), pltpu.VMEM((rows_kv, VAUG), BF16),
            pltpu.VMEM((ncp, VAUG), BF16), pltpu.VMEM((ncp, VAUG), BF16),
            pltpu.VMEM((MAX_BLOCKS, TQ), F32),
            pltpu.VMEM((ROWS, KAUG), BF16), pltpu.VMEM((ROWS, LANES), BF16),
            pltpu.VMEM((ROWS, LANES), F32), pltpu.VMEM((ROWS, LANES), F32), pltpu.VMEM((ROWS, 1), F32),
            pltpu.VMEM((ROWS, VAUG), F32),
            pltpu.VMEM((ROWS, KS), F32), pltpu.VMEM((ROWS, KS), F32),
            pltpu.VMEM((ROWS, KS), BF16), pltpu.VMEM((ROWS, KS), BF16),
            pltpu.VMEM((2, NSA_GROUP * WH, WKEYS), F32), pltpu.VMEM((2, NSA_GROUP * WH, WKEYS), BF16),
            pltpu.VMEM((TQ, ncp), F32), pltpu.VMEM((ROWS, D_HEAD), F32), pltpu.VMEM((MAX_BLOCKS, TQ), jnp.int32),
        ],
        compiler_params=pltpu.CompilerParams(
            dimension_semantics=("arbitrary", "arbitrary", "arbitrary"), vmem_limit_bytes=VMEM_LIMIT),
        name="nsa",
    )(q, kcmp, vcmp, ks, vs, kw, vw, gates, near_t, win_t, q_half, cmp_lhs, ovt, eye)


def _ffn_kernel(x_ref, on_ref, os_ref, wo_n_ref, wo_s_ref, fn_ref, wup_ref, cw_ref, cb_ref, wdn_ref, pm_ref,
                out_ref, prev_ref, perm_ref, h_ref, ua_scr, ub_scr, act_ref):
    tm = x_ref.shape[1]
    seg = tm // 8
    n_chunks = wup_ref.shape[1]

    @pl.when(pl.program_id(1) == 0)
    def _start_of_sequence():
        prev_ref[...] = jnp.zeros(prev_ref.shape, F32)

    x1 = x_ref[0] + _dot(on_ref[0], wo_n_ref[...]) + _dot(os_ref[0], wo_s_ref[...])
    ms = jnp.mean(x1 * x1, axis=-1, keepdims=True)
    hn = (x1 * lax.rsqrt(ms + EPS) * fn_ref[...]).astype(BF16)
    out_ref[0] = x1
    n_lt = x1.shape[1] // LANES
    h_ref[...] = _halves(_dot, pm_ref[...], hn).astype(BF16)

    def up(c, dst):
        h = h_ref[...]
        for half in range(2):
            dst[half] = _dot(h, wup_ref[half, c])

    first_sublane = lax.broadcasted_iota(jnp.int32, (8, FFN_CHUNK), 0) == 0

    def conv_gate(c, src):
        for r0 in range(0, tm, FFN_ROWS):
            ys = []
            for half in range(2):
                w = cw_ref[half, c]
                x0 = src[half, r0:r0 + FFN_ROWS]
                if r0 == 0:
                    wrap = [jnp.where(first_sublane, pltpu.roll(prev_ref[half, c, 8 * k:8 * k + 8], 1, 0),
                                      pltpu.roll(src[half, tm - 16 + 8 * k:tm - 8 + 8 * k], 1, 0)) for k in range(2)]
                    x1_ = jnp.concatenate([wrap[1], src[half, 0:FFN_ROWS - 8]], axis=0)
                    x2_ = jnp.concatenate([wrap[0], wrap[1], src[half, 0:FFN_ROWS - 16]], axis=0)
                else:
                    x1_ = src[half, r0 - 8:r0 + FFN_ROWS - 8]
                    x2_ = src[half, r0 - 16:r0 + FFN_ROWS - 16]
                ys.append(x0 * w[2:3] + x1_ * w[1:2] + x2_ * w[0:1] + cb_ref[half, c])
            a, g = ys
            col = pl.multiple_of(c * FFN_CHUNK, FFN_CHUNK)
            act_ref[r0:r0 + FFN_ROWS, pl.ds(col, FFN_CHUNK)] = ((g + g * jnp.tanh(g)) * a).astype(BF16)
        for half in range(2):
            prev_ref[half, c] = src[half, tm - 16:tm]

    up(0, ua_scr)

    def pair(u, carry):
        up(2 * u + 1, ub_scr)
        conv_gate(2 * u, ua_scr)
        up(2 * u + 2, ua_scr)
        conv_gate(2 * u + 1, ub_scr)
        return carry

    lax.fori_loop(0, (n_chunks - 1) // 2, pair, 0)
    conv_gate(n_chunks - 1, ua_scr)
    y = _halves(_dot, act_ref[...], wdn_ref[...])
    for c in range(n_lt):
        perm_ref[c] = y[:, c * LANES:(c + 1) * LANES]
    for sgm in range(8):
        natural = jnp.concatenate([perm_ref[c, pl.ds(sgm, seg, stride=8), :] for c in range(n_lt)], axis=1)
        out_ref[0, sgm * seg:(sgm + 1) * seg, :] += natural


def _ffn_call(x, o_nsa, o_sgu, wo_n, wo_s, ffn_norm, w_up, conv_w, conv_b, w_down, tm):
    B, T, _ = x.shape
    r = np.arange(tm)
    perm = jnp.asarray(((r % 8) * (tm // 8) + r // 8)[:, None] == r[None, :], BF16)
    n_chunks = w_up.shape[1]
    assert n_chunks % 2 == 1
    resident =lambda a: pl.BlockSpec(a.shape, lambda b, i: (0,) * a.ndim, pipeline_mode=pl.Buffered(1))
    tile = lambda w: pl.BlockSpec((1, tm, w), lambda b, i: (b, i, 0))
    return pl.pallas_call(
        _ffn_kernel,
        grid=(B, T // tm),
        in_specs=[tile(D_MODEL), tile(D_NSA), tile(D_GMLP), resident(wo_n), resident(wo_s),
                  resident(ffn_norm), resident(w_up), resident(conv_w), resident(conv_b), resident(w_down),
                  resident(perm)],
        out_specs=tile(D_MODEL),
        out_shape=jax.ShapeDtypeStruct((B, T, D_MODEL), F32),
        scratch_shapes=[pltpu.VMEM((2, n_chunks, 16, FFN_CHUNK), F32), pltpu.VMEM((D_MODEL // LANES, tm, LANES), F32),
                        pltpu.VMEM((tm, D_MODEL), BF16),
                        pltpu.VMEM((2, tm, FFN_CHUNK), F32), pltpu.VMEM((2, tm, FFN_CHUNK), F32),
                        pltpu.VMEM((tm, D_FF), BF16)],
        compiler_params=pltpu.CompilerParams(
            dimension_semantics=("arbitrary", "arbitrary"), vmem_limit_bytes=VMEM_LIMIT),
        name="ffn",
    )(x, o_nsa, o_sgu, wo_n, wo_s, ffn_norm, w_up, conv_w, conv_b, w_down, perm)


def _mixers(x, rel_bias, attn_norm, w_in, q_norm, k_norm_cmp, k_norm_slc, k_norm_win,
            cmp_pe_k, cmp_w1_k, cmp_b1_k, cmp_w2_k, cmp_pe_v, cmp_w1_v, cmp_b1_v, cmp_w2_v,
            sgu_norm, sgu_w, sgu_b):
    B, T, _ = x.shape
    assert T % 512 == 0 and T // SEL_BLOCK <= MAX_BLOCKS and (T // CMP_STRIDE) % LANES == 0
    ncp = T // CMP_STRIDE
    G, R = NSA_KV_HEADS, NSA_GROUP

    o_kv = D_NSA
    o_g = D_NSA + 6 * D_KV
    o_uv = o_g + N_GATES
    w_q = w_in[:, :o_kv].astype(BF16)
    w_kv = w_in[:, o_kv:o_g].astype(BF16)
    per_group = NSA_GROUP * N_BRANCH
    w_g = jnp.pad(w_in[:, o_g:o_uv].reshape(D_MODEL, G, per_group), ((0, 0), (0, 0), (0, LANES - per_group)))
    w_g = w_g.reshape(D_MODEL, G * LANES).astype(BF16)
    w_uv = w_in[:, o_uv:].astype(BF16)
    qn = (jnp.tile(q_norm, NSA_HEADS) * (D_HEAD ** -0.5 * LOG2E)).reshape(1, D_NSA)
    ksn = jnp.tile(k_norm_slc, G).reshape(1, D_KV)
    kwn = jnp.tile(k_norm_win, G).reshape(1, D_KV)
    sgu_bt = jnp.repeat(sgu_b.T, GMLP_GROUP_DIM, axis=1)
    grp = np.arange(D_NSA) // D_HEAD
    bd = jnp.asarray(grp[:, None] == grp[None, :], BF16)

    q, kc, vc, ks, vs, kw, vw, gates, o_sgu = _proj_call(
        x, attn_norm.reshape(1, D_MODEL), w_q, w_kv, w_g, w_uv, qn, ksn, kwn,
        sgu_norm.reshape(1, D_GMLP), sgu_w, sgu_bt, bd, tm=1024)

    half = CMP_STRIDE * D_HEAD
    kcmp, vcmp = _compress_call(
        kc.reshape(B, G, ncp, half), vc.reshape(B, G, ncp, half),
        cmp_pe_k.reshape(2, half), cmp_w1_k.astype(BF16), cmp_b1_k.reshape(1, CMP_HIDDEN),
        cmp_w2_k.astype(BF16), k_norm_cmp.reshape(1, D_HEAD),
        cmp_pe_v.reshape(2, half), cmp_w1_v.astype(BF16), cmp_b1_v.reshape(1, CMP_HIDDEN),
        cmp_w2_v.astype(BF16))

    near_t, win_t, q_half, cmp_lhs = _bias_tables(rel_bias)
    jj = np.arange(MAX_BLOCKS)[:, None]
    nn = np.arange(ncp)[None, :]
    ovt = jnp.asarray((nn >= 4 * jj - 1) & (nn <= 4 * jj + 3), BF16)
    eye = jnp.asarray(np.eye(TQ), BF16)

    o_nsa = _nsa_call(q, kcmp, vcmp, ks, vs, kw, vw, gates, near_t, win_t, q_half, cmp_lhs, ovt, eye)
    return o_nsa, o_sgu


def _layer(x, rel_bias, attn_norm, w_in, q_norm, k_norm_cmp, k_norm_slc, k_norm_win,
           cmp_pe_k, cmp_w1_k, cmp_b1_k, cmp_w2_k, cmp_pe_v, cmp_w1_v, cmp_b1_v, cmp_w2_v,
           sgu_norm, sgu_w, sgu_b, w_out, ffn_norm, w_up, conv_w, conv_b, w_down):
    o_nsa, o_sgu = _mixers(x, rel_bias, attn_norm, w_in, q_norm, k_norm_cmp, k_norm_slc, k_norm_win,
                           cmp_pe_k, cmp_w1_k, cmp_b1_k, cmp_w2_k, cmp_pe_v, cmp_w1_v, cmp_b1_v, cmp_w2_v,
                           sgu_norm, sgu_w, sgu_b)
    wo = w_out.astype(BF16)
    n_chunks = D_FF // FFN_CHUNK
    wu = jnp.transpose(w_up.astype(BF16).reshape(D_MODEL, 2, n_chunks, FFN_CHUNK), (1, 2, 0, 3))
    half_gate = jnp.asarray([1.0, 0.5], F32).reshape(2, 1, 1, 1)
    cw = jnp.transpose(conv_w.reshape(conv_w.shape[0], 2, n_chunks, FFN_CHUNK), (1, 2, 0, 3)) * half_gate
    cb = conv_b.reshape(2, n_chunks, 1, FFN_CHUNK) * half_gate
    wd = w_down.astype(BF16)
    return _ffn_call(x, o_nsa, o_sgu, wo[:D_NSA], wo[D_NSA:], ffn_norm.reshape(1, D_MODEL), wu, cw, cb, wd, tm=512)


def kernel(x, rel_bias, attn_norm, w_in, q_norm, k_norm_cmp, k_norm_slc, k_norm_win,
           cmp_pe_k, cmp_w1_k, cmp_b1_k, cmp_w2_k, cmp_pe_v, cmp_w1_v, cmp_b1_v, cmp_w2_v,
           sgu_norm, sgu_w, sgu_b, w_out, ffn_norm, w_up, conv_w, conv_b, w_down):
    depth = attn_norm.shape[0]
    for l in range(depth):
        x = _layer(x, rel_bias, attn_norm[l], w_in[l], q_norm[l], k_norm_cmp[l], k_norm_slc[l], k_norm_win[l],
                   cmp_pe_k[l], cmp_w1_k[l], cmp_b1_k[l], cmp_w2_k[l], cmp_pe_v[l], cmp_w1_v[l], cmp_b1_v[l],
                   cmp_w2_v[l], sgu_norm[l], sgu_w[l], sgu_b[l], w_out[l], ffn_norm[l], w_up[l], conv_w[l],
                   conv_b[l], w_down[l])
    return x
```

```python
import functools
import math

import numpy as np
import jax
import jax.numpy as jnp
from jax import lax
from jax.experimental import pallas as pl
from jax.experimental.pallas import tpu as pltpu

F32 = jnp.float32
BF16 = jnp.bfloat16

D_MODEL = 1024
D_HEAD = 64
NSA_HEADS = 8
NSA_KV_HEADS = 2
NSA_GROUP = NSA_HEADS // NSA_KV_HEADS
D_NSA = NSA_HEADS * D_HEAD
D_KV = NSA_KV_HEADS * D_HEAD
N_BRANCH = 3
N_GATES = NSA_HEADS * N_BRANCH
CMP_BLOCK = 32
CMP_STRIDE = 16
CMP_HIDDEN = 256
SEL_BLOCK = 64
SEL_TOPK = 16
WINDOW = 512
GMLP_GROUPS = 8
GMLP_GROUP_DIM = 64
D_GMLP = GMLP_GROUPS * GMLP_GROUP_DIM
CHUNK = 128
D_MIX = D_NSA + D_GMLP
REL_BUCKETS = 32
REL_MAX_DIST = 128
D_FF = 2816
EPS = 1e-6
NEG = -1e30
FORCED_SCORE = 1e4
LOG2E = 1.4426950408889634

LANES = 128
TQ = 256
SUB = TQ // SEL_BLOCK
ROWS = NSA_GROUP * TQ
KT = 256
KS = 2 * KT
WH = TQ // 2
WKEYS = WINDOW + WH
RB = 64
BLOCKS_PER_TILE = KT // SEL_BLOCK
WIN_BLOCKS = WINDOW // SEL_BLOCK
KPAD = 1280
N_TOEPLITZ = 3
MAX_BLOCKS = 64
KAUG = 256
VAUG = 128
CBAND = 28
FFN_CHUNK = 256
FFN_ROWS = 64
VMEM_LIMIT = 56 * 1024 * 1024

_NT = (((1,), (1,)), ((), ()))


def _dot(a, b):
    return jnp.dot(a, b, preferred_element_type=F32)


def _dot_nt(a, b):
    return lax.dot_general(a, b, _NT, preferred_element_type=F32)


def _halves(dot, a, b):
    h = a.shape[0] // 2
    return jnp.concatenate([dot(a[:h], b), dot(a[h:], b)], axis=0)


def _split_bf16(x):
    hi = x.astype(BF16)
    lo = (x - hi.astype(F32)).astype(BF16)
    return hi, lo


def _gelu_tanh(x):
    return 0.5 * x * (1.0 + jnp.tanh(0.7978845608028654 * (x + 0.044715 * (x * x * x))))


def _sigmoid(x):
    return 0.5 * (1.0 + jnp.tanh(0.5 * x))


def _group_mean_sq(t, ones_blockdiag, width):
    t2 = t * t
    hi, lo = _split_bf16(t2)
    return (_dot(hi, ones_blockdiag) + _dot(lo, ones_blockdiag)) * (1.0 / width)


def _proj_kernel(x_ref, an_ref, wq_ref, wkv_ref, wg_ref, wuv_ref, qn_ref, ksn_ref, kwn_ref,
                 sgun_ref, sguw_ref, sgub_ref, bd_ref,
                 q_out, kc_out, vc_out, ks_out, vs_out, kw_out, vw_out, g_out, sgu_out):
    tm = x_ref.shape[1]
    x = x_ref[0]
    ms = jnp.mean(x * x, axis=-1, keepdims=True)
    h = (x * lax.rsqrt(ms + EPS) * an_ref[...]).astype(BF16)

    bd = bd_ref[...]
    q = _halves(_dot, h, wq_ref[...])
    qn = q * lax.rsqrt(_group_mean_sq(q, bd, D_HEAD) + EPS) * qn_ref[...]
    for hh in range(NSA_HEADS):
        q_out[0, hh] = qn[:, hh * D_HEAD:(hh + 1) * D_HEAD].astype(BF16)

    kv = _halves(_dot, h, wkv_ref[...])
    bd_kv = bd[:D_KV, :D_KV]
    kc = kv[:, 0 * D_KV:1 * D_KV]
    vc = kv[:, 1 * D_KV:2 * D_KV]
    ks = kv[:, 2 * D_KV:3 * D_KV]
    vs = kv[:, 3 * D_KV:4 * D_KV]
    kw = kv[:, 4 * D_KV:5 * D_KV]
    vw = kv[:, 5 * D_KV:6 * D_KV]
    ks = ks * lax.rsqrt(_group_mean_sq(ks, bd_kv, D_HEAD) + EPS) * ksn_ref[...]
    kw = kw * lax.rsqrt(_group_mean_sq(kw, bd_kv, D_HEAD) + EPS) * kwn_ref[...]
    for t, o_ref in ((kc, kc_out), (vc, vc_out), (ks, ks_out), (vs, vs_out), (kw, kw_out), (vw, vw_out)):
        for g in range(NSA_KV_HEADS):
            o_ref[0, g] = t[:, g * D_HEAD:(g + 1) * D_HEAD].astype(BF16)

    gates = _sigmoid(_dot(h, wg_ref[...]))
    for g in range(NSA_KV_HEADS):
        g_out[0, g] = gates[:, g * LANES:(g + 1) * LANES]

    uv = _gelu_tanh(_halves(_dot, h, wuv_ref[...]))
    u = uv[:, :D_GMLP]
    v = uv[:, D_GMLP:]
    vms = jnp.mean(v * v, axis=-1, keepdims=True)
    vb = (v * lax.rsqrt(vms + EPS) * sgun_ref[...]).astype(BF16)

    row = lax.broadcasted_iota(jnp.int32, (CHUNK, CHUNK), 0)
    col = lax.broadcasted_iota(jnp.int32, (CHUNK, CHUNK), 1)
    tril = col <= row
    w_tril = [jnp.where(tril, sguw_ref[g], 0.0).astype(BF16) for g in range(GMLP_GROUPS)]
    w_pair = [jnp.concatenate([w_tril[2 * p], w_tril[2 * p + 1]], axis=1) for p in range(D_GMLP // LANES)]
    first_half = lax.broadcasted_iota(jnp.int32, (CHUNK, LANES), 1) < GMLP_GROUP_DIM
    zero = jnp.zeros((CHUNK, LANES), BF16)
    for c in range(tm // CHUNK):
        rows = slice(c * CHUNK, (c + 1) * CHUNK)
        zs = []
        for p in range(D_GMLP // LANES):
            blk = vb[rows, p * LANES:(p + 1) * LANES]
            rhs = jnp.concatenate([jnp.where(first_half, blk, zero), jnp.where(first_half, zero, blk)], axis=0)
            zs.append(_dot(w_pair[p], rhs))
        z = jnp.concatenate(zs, axis=1) + sgub_ref[...]
        sgu_out[0, rows, :] = (u[rows, :] * z).astype(BF16)


def _proj_call(x, attn_norm, w_q, w_kv, w_g, w_uv, qn, ksn, kwn, sgu_norm, sgu_w, sgu_bt, bd, tm):
    B, T, _ = x.shape
    const2 = lambda b, i: (0, 0)
    const3 = lambda b, i: (0, 0, 0)
    head_spec = lambda nh: pl.BlockSpec((1, nh, tm, D_HEAD), lambda b, i: (b, 0, i, 0))
    kv_shape = jax.ShapeDtypeStruct((B, NSA_KV_HEADS, T, D_HEAD), BF16)
    return pl.pallas_call(
        _proj_kernel,
        grid=(B, T // tm),
        in_specs=[
            pl.BlockSpec((1, tm, D_MODEL), lambda b, i: (b, i, 0)),
            pl.BlockSpec((1, D_MODEL), const2),
            pl.BlockSpec(w_q.shape, const2),
            pl.BlockSpec(w_kv.shape, const2),
            pl.BlockSpec(w_g.shape, const2),
            pl.BlockSpec(w_uv.shape, const2),
            pl.BlockSpec((1, D_NSA), const2),
            pl.BlockSpec((1, D_KV), const2),
            pl.BlockSpec((1, D_KV), const2),
            pl.BlockSpec((1, D_GMLP), const2),
            pl.BlockSpec(sgu_w.shape, const3),
            pl.BlockSpec(sgu_bt.shape, const2),
            pl.BlockSpec(bd.shape, const2),
        ],
        out_specs=[
            head_spec(NSA_HEADS),
            head_spec(NSA_KV_HEADS), head_spec(NSA_KV_HEADS), head_spec(NSA_KV_HEADS),
            head_spec(NSA_KV_HEADS), head_spec(NSA_KV_HEADS), head_spec(NSA_KV_HEADS),
            pl.BlockSpec((1, NSA_KV_HEADS, tm, LANES), lambda b, i: (b, 0, i, 0)),
            pl.BlockSpec((1, tm, D_GMLP), lambda b, i: (b, i, 0)),
        ],
        out_shape=[
            jax.ShapeDtypeStruct((B, NSA_HEADS, T, D_HEAD), BF16),
            kv_shape, kv_shape, kv_shape, kv_shape, kv_shape, kv_shape,
            jax.ShapeDtypeStruct((B, NSA_KV_HEADS, T, LANES), F32),
            jax.ShapeDtypeStruct((B, T, D_GMLP), BF16),
        ],
        compiler_params=pltpu.CompilerParams(
            dimension_semantics=("arbitrary", "arbitrary"), vmem_limit_bytes=VMEM_LIMIT),
        name="proj",
    )(x, attn_norm, w_q, w_kv, w_g, w_uv, qn, ksn, kwn, sgu_norm, sgu_w, sgu_bt, bd)


def _compress_one(tok_ref, pe_ref, w1_ref, b1_ref, w2_ref):
    half = CMP_STRIDE * D_HEAD
    tok = tok_ref[0, 0].astype(F32)
    top = (tok + pe_ref[0:1, :]).astype(BF16)
    bot = (tok + pe_ref[1:2, :]).astype(BF16)
    a = _dot(top, w1_ref[:half, :])
    b = _dot(bot, w1_ref[half:, :])
    ncp = a.shape[0]
    pre = a + pltpu.roll(b, ncp - 1, 0) + b1_ref[...]
    hid = _gelu_tanh(pre).astype(BF16)
    return _dot(hid, w2_ref[...])


def _compress_kernel(kc_ref, vc_ref, pek_ref, w1k_ref, b1k_ref, w2k_ref, kn_ref,
                     pev_ref, w1v_ref, b1v_ref, w2v_ref, k_out, v_out):
    k = _compress_one(kc_ref, pek_ref, w1k_ref, b1k_ref, w2k_ref)
    kms = jnp.mean(k * k, axis=-1, keepdims=True)
    k_out[0, 0] = (k * lax.rsqrt(kms + EPS) * kn_ref[...]).astype(BF16)
    v_out[0, 0] = _compress_one(vc_ref, pev_ref, w1v_ref, b1v_ref, w2v_ref).astype(BF16)


def _compress_call(kc, vc, pek, w1k, b1k, w2k, kn, pev, w1v, b1v, w2v):
    B, G, ncp, width = kc.shape
    const2 = lambda b, g: (0, 0)
    tok_spec = pl.BlockSpec((1, 1, ncp, width), lambda b, g: (b, g, 0, 0))
    out_spec = pl.BlockSpec((1, 1, ncp, D_HEAD), lambda b, g: (b, g, 0, 0))
    full = lambda a: pl.BlockSpec(a.shape, const2)
    out_shape = jax.ShapeDtypeStruct((B, G, ncp, D_HEAD), BF16)
    return pl.pallas_call(
        _compress_kernel,
        grid=(B, G),
        in_specs=[tok_spec, tok_spec, full(pek), full(w1k), full(b1k), full(w2k), full(kn),
                  full(pev), full(w1v), full(b1v), full(w2v)],
        out_specs=[out_spec, out_spec],
        out_shape=[out_shape, out_shape],
        compiler_params=pltpu.CompilerParams(
            dimension_semantics=("arbitrary", "arbitrary"), vmem_limit_bytes=VMEM_LIMIT),
        name="compress",
    )(kc, vc, pek, w1k, b1k, w2k, kn, pev, w1v, b1v, w2v)


def _rel_bucket_np(dist):
    max_exact = REL_BUCKETS // 2
    d = np.maximum(dist, 1).astype(np.float32)
    log_b = max_exact + (np.log(d / np.float32(max_exact)) / np.float32(math.log(REL_MAX_DIST / max_exact))
                         * np.float32(REL_BUCKETS - max_exact)).astype(np.int32)
    log_b = np.clip(log_b, max_exact, REL_BUCKETS - 1)
    return np.where(dist < max_exact, np.maximum(dist, 0), log_b)


def _bias_tables(rel_bias):
    rb = rel_bias.astype(F32) * LOG2E
    qq = np.arange(SEL_BLOCK)[:, None]
    kp = np.arange(SEL_BLOCK)[None, :]
    tile_d = [m * SEL_BLOCK + qq - kp for m in range(N_TOEPLITZ)]
    s_ = np.arange(SUB)[:, None, None]
    e_ = np.arange(CBAND)[None, None, :]
    dist_c = qq[None, :, :] + CMP_STRIDE * (e_ - 4 * (SUB - 1) + 4 * s_) - (3 * CMP_STRIDE + CMP_BLOCK - 1)
    all_d = np.concatenate([d.reshape(-1) for d in tile_d] + [dist_c.reshape(-1)])
    onehot = np.eye(REL_BUCKETS, dtype=np.float32)[_rel_bucket_np(all_d)]
    vals = jnp.dot(jnp.asarray(onehot), rb, precision=lax.Precision.HIGHEST)
    vals = jnp.where(jnp.asarray(all_d >= 0)[:, None], vals, NEG)
    n_t = N_TOEPLITZ * SEL_BLOCK * SEL_BLOCK
    toep = jnp.transpose(vals[:n_t].reshape(N_TOEPLITZ, SEL_BLOCK, SEL_BLOCK, NSA_HEADS), (0, 3, 1, 2))
    band = jnp.transpose(vals[n_t:].reshape(SUB, SEL_BLOCK, CBAND, NSA_HEADS), (3, 0, 1, 2))
    far = rb[REL_BUCKETS - 1]
    far_tile = jnp.broadcast_to(far[:, None, None], (NSA_HEADS, SEL_BLOCK, SEL_BLOCK))
    neg_tile = jnp.full((NSA_HEADS, SEL_BLOCK, SEL_BLOCK), NEG, F32)
    edge_tile = jnp.where(jnp.asarray(kp > qq)[None], far_tile, NEG)

    def tile(m, windowed):
        if m < 0 or (windowed and m > WIN_BLOCKS):
            return neg_tile
        if windowed and m == WIN_BLOCKS:
            return edge_tile
        return toep[m] if m < N_TOEPLITZ else far_tile

    def table(first_m, n_blocks, windowed):
        rows = [jnp.concatenate([tile(first_m + s - c, windowed) for c in range(n_blocks)], axis=2)
                for s in range(SUB)]
        t = jnp.concatenate(rows, axis=1)
        return t.reshape(NSA_KV_HEADS, ROWS, n_blocks * SEL_BLOCK)

    far_rows = jnp.repeat(far, TQ).reshape(NSA_KV_HEADS, ROWS, 1)
    near_t = table(BLOCKS_PER_TILE, 2 * BLOCKS_PER_TILE, False) - far_rows
    win_t = table(WIN_BLOCKS, WIN_BLOCKS + BLOCKS_PER_TILE, True) - far_rows
    win_t = win_t.reshape(NSA_KV_HEADS, NSA_GROUP, 2, WH, (WIN_BLOCKS + BLOCKS_PER_TILE) * SEL_BLOCK)
    win_t = jnp.stack([win_t[:, :, half, :, half * WH:half * WH + WKEYS].reshape(NSA_KV_HEADS, NSA_GROUP * WH, WKEYS)
                       for half in range(2)], axis=1)
    band_t = band.reshape(NSA_KV_HEADS, ROWS, CBAND)
    band_hi = band_t.astype(BF16)
    band_lo = (band_t - band_hi.astype(F32)).astype(BF16)
    far_hi = far_rows.astype(BF16)
    far_lo = (far_rows - far_hi.astype(F32)).astype(BF16)
    q_half = jnp.concatenate([jnp.zeros((NSA_KV_HEADS, ROWS, D_HEAD), BF16),
                              jnp.full((NSA_KV_HEADS, ROWS, 1), NEG, BF16), far_hi, far_lo,
                              jnp.zeros((NSA_KV_HEADS, ROWS, LANES - D_HEAD - 3), BF16)], axis=2)
    neg_col = jnp.full((NSA_KV_HEADS, ROWS, 1), NEG, BF16)
    zeros = jnp.zeros((NSA_KV_HEADS, ROWS, D_HEAD - 2 * CBAND - 3), BF16)
    q_lanes = jnp.zeros((NSA_KV_HEADS, ROWS, D_HEAD), BF16)
    cmp_lhs = jnp.concatenate([q_lanes, band_hi, band_lo, far_hi, far_lo, neg_col, zeros], axis=2)
    return near_t, win_t, q_half, cmp_lhs


def _row_max(lane_tiles):
    m = jnp.max(functools.reduce(jnp.maximum, lane_tiles), axis=1, keepdims=True)
    return jnp.broadcast_to(m, lane_tiles[0].shape)


def _nsa_kernel(q_ref, kc_ref, vc_ref, ks_ref, vs_ref, kw_ref, vw_ref, g_ref,
                near_ref, win_ref, qhalf_ref, cmpl_ref, ovt_ref, eye_ref,
                o_ref, ksa, vsa, kwa, vwa, kca, vca, sc_scr, lhs_scr, lhsc_scr, m_scr, al_scr, inv_scr, acc_scr, sa_scr, sb_scr, pa_scr, pb_scr,
                sw_scr, pw_scr, ps_scr, out_scr, cnt_scr):
    qt = pl.program_id(2)
    first = qt * SUB
    T = ks_ref.shape[2]
    ncp = kc_ref.shape[2]

    @pl.when(qt == 0)
    def _init():
        n_chunks = ksa.shape[0] // KT
        r_io = lax.broadcasted_iota(jnp.int32, (KT, KAUG), 0)
        l_io = lax.broadcasted_iota(jnp.int32, (KT, KAUG), 1)
        lv = lax.broadcasted_iota(jnp.int32, (KT, VAUG), 1)
        vpat = jnp.where(lv == D_HEAD, 1.0, 0.0).astype(BF16)

        def fill(c, carry):
            rows = pl.ds(pl.multiple_of(c * KT, KT), KT)
            key = r_io + (c * KT - KPAD)
            blk = key // SEL_BLOCK
            real = (key >= 0) & (key < T)
            hot = real & (((l_io < D_HEAD) & (l_io == blk))
                          | ((l_io >= D_HEAD) & (l_io < 2 * D_HEAD) & (l_io - D_HEAD == blk)))
            hot = hot | ((~real) & (l_io == 3 * D_HEAD)) | (real & (l_io > 3 * D_HEAD) & (l_io <= 3 * D_HEAD + 2))
            pat = jnp.where(hot, 1.0, 0.0).astype(BF16)
            ksa[rows, :] = pat
            kwa[rows, :] = pat
            vsa[rows, :] = vpat
            vwa[rows, :] = vpat
            return carry

        lax.fori_loop(0, n_chunks, fill, 0)
        ksa[KPAD:KPAD + T, 2 * D_HEAD:3 * D_HEAD] = ks_ref[0, 0]
        kwa[KPAD:KPAD + T, 2 * D_HEAD:3 * D_HEAD] = kw_ref[0, 0]
        vsa[KPAD:KPAD + T, 0:D_HEAD] = vs_ref[0, 0]
        vwa[KPAD:KPAD + T, 0:D_HEAD] = vw_ref[0, 0]
        lhs_scr[:, 0:LANES] = jnp.zeros((ROWS, LANES), BF16)
        lc = lax.broadcasted_iota(jnp.int32, (ncp, VAUG), 1)
        kca[...] = jnp.zeros((ncp, VAUG), BF16)
        kca[:, 0:D_HEAD] = kc_ref[0, 0]
        vca[...] = jnp.where(lc == D_HEAD, 1.0, 0.0).astype(BF16)
        vca[:, 0:D_HEAD] = vc_ref[0, 0]

    q4 = q_ref[0].reshape(ROWS, D_HEAD)
    gates = g_ref[0, 0]
    row_blocks = [slice(r, r + RB) for r in range(0, ROWS, RB)]

    gate_rows = jnp.concatenate(
        [gates if hh == 0 else pltpu.roll(gates, LANES - hh * N_BRANCH, 1) for hh in range(NSA_GROUP)], axis=0)

    def gate_col(branch):
        return gate_rows[:, branch:branch + 1]

    lane_c = lax.broadcasted_iota(jnp.int32, (ncp, LANES), 1)
    n_c = lax.broadcasted_iota(jnp.int32, (ncp, LANES), 0)
    e = lane_c - D_HEAD
    base = 4 * first + (4 * SUB - 1)
    ind = (((e >= 0) & (e < CBAND) & (n_c == base - e))
           | ((e >= CBAND) & (e < 2 * CBAND) & (n_c == base - (e - CBAND)))
           | (((e == 2 * CBAND) | (e == 2 * CBAND + 1)) & (n_c < base - (CBAND - 1)))
           | ((e == 2 * CBAND + 2) & (n_c > base)))
    rhs_c = jnp.where(lane_c < D_HEAD, kca[...], jnp.where(ind, 1.0, 0.0).astype(BF16))
    lhsc_scr[...] = cmpl_ref[0]
    lhsc_scr[:, 0:D_HEAD] = q4
    sa_scr[:, 0:ncp] = _halves(_dot_nt, lhsc_scr[...], rhs_c)

    lhs_scr[:, LANES:2 * LANES] = qhalf_ref[0]
    lhs_scr[:, LANES:LANES + D_HEAD] = q4
    win_start = pl.multiple_of(KPAD + (first - WIN_BLOCKS) * SEL_BLOCK, KT)
    for half in range(2):
        rows_q = jnp.concatenate([lhs_scr[hh * TQ + half * WH:hh * TQ + (half + 1) * WH, LANES:2 * LANES]
                                  for hh in range(NSA_GROUP)], axis=0)
        lhs_win = jnp.concatenate([jnp.zeros((NSA_GROUP * WH, LANES), BF16), rows_q], axis=1)
        sw_scr[half] = _dot_nt(lhs_win, kwa[pl.ds(pl.multiple_of(win_start + half * WH, WH), WKEYS), :])

    for i, rows in enumerate(row_blocks):
        s = [sa_scr[rows, c * LANES:(c + 1) * LANES] for c in range(ncp // LANES)]
        m = _row_max(s)
        p = [jnp.where(t > 0.5 * NEG, jnp.exp2(t - m), 0.0) for t in s]
        l = jnp.sum(functools.reduce(jnp.add, p), axis=1, keepdims=True)
        inv = jnp.where(l > 0.0, 1.0 / l, 0.0)
        inv_scr[rows] = inv
        inv_b = jnp.broadcast_to(inv, (RB, LANES))
        tok = slice((i % (TQ // RB)) * RB, (i % (TQ // RB) + 1) * RB)
        for c in range(ncp // LANES):
            lanes = slice(c * LANES, (c + 1) * LANES)
            pb_scr[rows, lanes] = p[c].astype(BF16)
            if i < TQ // RB:
                ps_scr[tok, lanes] = p[c] * inv_b
            else:
                ps_scr[tok, lanes] += p[c] * inv_b
    acc_c = _halves(_dot, pb_scr[:, 0:ncp], vca[...])
    out_scr[...] = acc_c[:, :D_HEAD] * (gate_col(0) * inv_scr[...])
    ps_hi, ps_lo = _split_bf16(ps_scr[...])
    ovt = ovt_ref[...]
    imp_t = _dot_nt(ovt, ps_hi) + _dot_nt(ovt, ps_lo)

    j_t = lax.broadcasted_iota(jnp.int32, (MAX_BLOCKS, TQ), 0)
    cur = first + lax.broadcasted_iota(jnp.int32, (MAX_BLOCKS, TQ), 1) // SEL_BLOCK
    forced = (j_t == 0) | (j_t == cur) | (j_t == cur - 1)
    score_t = jnp.where(forced, FORCED_SCORE, jnp.where(j_t <= cur, imp_t, NEG))
    sc_scr[...] = score_t

    cnt_scr[...] = jnp.zeros((MAX_BLOCKS, TQ), jnp.int32)
    sub_io = lax.broadcasted_iota(jnp.int32, (8, TQ), 0)
    n_groups = MAX_BLOCKS // 8
    for grp in range(n_groups):
        @pl.when(grp * 8 < first + SUB)
        def _count_group():
            targets = [score_t[8 * jg:8 * jg + 8] for jg in range(n_groups)]
            acc = [jnp.zeros((8, TQ), jnp.int32) for _ in range(n_groups)]
            for k in range(8):
                row_i = jnp.broadcast_to(sc_scr[grp * 8 + k:grp * 8 + k + 1, :], (8, TQ))
                for jg in range(n_groups):
                    if jg > grp:
                        beats = row_i >= targets[jg]
                    elif jg < grp:
                        beats = row_i > targets[jg]
                    else:
                        beats = (row_i > targets[jg]) | ((row_i == targets[jg]) & (sub_io > k))
                    acc[jg] = acc[jg] + beats.astype(jnp.int32)
            cnt_scr[...] += jnp.concatenate(acc, axis=0)

    cnt = cnt_scr[...]
    mask_t = jnp.where((cnt < SEL_TOPK) & (j_t <= cur), 0.0, NEG).astype(BF16)
    mask = _dot_nt(eye_ref[...], mask_t).astype(BF16)

    for hh in range(NSA_GROUP):
        lhs_scr[hh * TQ:(hh + 1) * TQ, 0:MAX_BLOCKS] = mask

    def tile_start(k):
        return pl.multiple_of(KPAD + (first - BLOCKS_PER_TILE) * SEL_BLOCK - k * KS, KT)

    def scores(k, dst):
        dst[...] = _halves(_dot_nt, lhs_scr[...], ksa[pl.ds(tile_start(k), KS), :])

    def consume(k, src, p_scr, with_table):
        n_lt = KS // LANES
        for rows in row_blocks:
            n_use = n_lt
            if with_table:
                sub = (rows.start % TQ) // SEL_BLOCK
                n_use = -(-(BLOCKS_PER_TILE + sub + 1) * SEL_BLOCK // LANES)
                n_zero = (sub + BLOCKS_PER_TILE + 1 - N_TOEPLITZ) * SEL_BLOCK // LANES
            s = [src[rows, c * LANES:(c + 1) * LANES] for c in range(n_use)]
            if with_table:
                s = [s[c] if c < n_zero else s[c] + near_ref[0, rows, c * LANES:(c + 1) * LANES]
                     for c in range(n_use)]
            if with_table:
                m = _row_max(s)
            else:
                m_prev = m_scr[rows]
                m = jnp.maximum(m_prev, _row_max(s))
                al_scr[rows] = jnp.exp2(m_prev - m)
            m_scr[rows] = m
            for c in range(n_use):
                p_scr[rows, c * LANES:(c + 1) * LANES] = jnp.exp2((s[c] - m).astype(BF16))
            for c in range(n_use, n_lt):
                p_scr[rows, c * LANES:(c + 1) * LANES] = jnp.zeros((RB, LANES), BF16)
        pv = _dot(p_scr[:, 0:KS], vsa[pl.ds(tile_start(k), KS), :])
        acc_scr[...] = pv if with_table else acc_scr[...] * al_scr[...] + pv

    scores(0, sa_scr)

    gate_w = gate_col(2)
    for half in range(2):
        for r0 in range(0, NSA_GROUP * WH, RB):
            rows = slice(r0, r0 + RB)
            s = [sw_scr[half, rows, c * LANES:(c + 1) * LANES] if c in (1, 2) else
                 sw_scr[half, rows, c * LANES:(c + 1) * LANES] + win_ref[0, half, rows, c * LANES:(c + 1) * LANES]
                 for c in range(WKEYS // LANES)]
            m = _row_max(s)
            for c in range(WKEYS // LANES):
                pw_scr[half, rows, c * LANES:(c + 1) * LANES] = jnp.exp2((s[c] - m).astype(BF16))
        acc_w = _dot(pw_scr[half], vwa[pl.ds(pl.multiple_of(win_start + half * WH, WH), WKEYS), :])
        for hh in range(NSA_GROUP):
            dst = slice(hh * TQ + half * WH, hh * TQ + (half + 1) * WH)
            src = slice(hh * WH, (hh + 1) * WH)
            out_scr[dst] += acc_w[src, :D_HEAD] * (gate_w[dst] / acc_w[src, D_HEAD:D_HEAD + 1])

    def pair(with_table, u, carry):
        scores(2 * u + 1, sb_scr)
        consume(2 * u, sa_scr, pa_scr, with_table)
        scores(2 * u + 2, sa_scr)
        consume(2 * u + 1, sb_scr, pb_scr, False)
        return carry

    n_tiles = (qt + 2) // 2
    pair(True, 0, 0)
    lax.fori_loop(1, jnp.maximum(n_tiles // 2, 1), functools.partial(pair, False), 0)

    @pl.when((n_tiles % 2 == 1) & (n_tiles >= 3))
    def _odd_tile():
        consume(n_tiles - 1, sa_scr, pa_scr, False)

    acc_s = acc_scr[...]

    o = out_scr[...] + acc_s[:, :D_HEAD] * (gate_col(1) / acc_s[:, D_HEAD:D_HEAD + 1])
    for hh in range(NSA_GROUP):
        o_ref[0, :, hh * D_HEAD:(hh + 1) * D_HEAD] = o[hh * TQ:(hh + 1) * TQ].astype(BF16)


def _nsa_call(q, kcmp, vcmp, ks, vs, kw, vw, gates, near_t, win_t, q_half, cmp_lhs, ovt, eye):
    B, _, T, _ = q.shape
    G = NSA_KV_HEADS
    ncp = kcmp.shape[2]
    rows_kv = KPAD + T
    per_bg =lambda a: pl.BlockSpec((1, 1) + a.shape[2:], lambda b, g, i: (b, g, 0, 0))
    per_g = lambda a: pl.BlockSpec((1,) + a.shape[1:], lambda b, g, i: (g,) + (0,) * (a.ndim - 1))
    const2 = lambda a: pl.BlockSpec(a.shape, lambda b, g, i: (0, 0))
    return pl.pallas_call(
        _nsa_kernel,
        grid=(B, G, T // TQ),
        in_specs=[
            pl.BlockSpec((1, NSA_GROUP, TQ, D_HEAD), lambda b, g, i: (b, g, i, 0)),
            per_bg(kcmp), per_bg(vcmp), per_bg(ks), per_bg(vs), per_bg(kw), per_bg(vw),
            pl.BlockSpec((1, 1, TQ, LANES), lambda b, g, i: (b, g, i, 0)),
            per_g(near_t), per_g(win_t), per_g(q_half), per_g(cmp_lhs), const2(ovt), const2(eye),
        ],
        out_specs=pl.BlockSpec((1, TQ, NSA_GROUP * D_HEAD), lambda b, g, i: (b, i, g)),
        out_shape=jax.ShapeDtypeStruct((B, T, D_NSA), BF16),
        scratch_shapes=[
            pltpu.VMEM((rows_kv, KAUG), BF16), pltpu.VMEM((rows_kv, VAUG), BF16),
            pltpu.VMEM((rows_kv, KAUG), BF16), pltpu.VMEM((rows_kv, VAUG), BF16),
            pltpu.VMEM((ncp, VAUG), BF16), pltpu.VMEM((ncp, VAUG), BF16),
            pltpu.VMEM((MAX_BLOCKS, TQ), F32),
            pltpu.VMEM((ROWS, KAUG), BF16), pltpu.VMEM((ROWS, LANES), BF16),
            pltpu.VMEM((ROWS, LANES), F32), pltpu.VMEM((ROWS, LANES), F32), pltpu.VMEM((ROWS, 1), F32),
            pltpu.VMEM((ROWS, VAUG), F32),
            pltpu.VMEM((ROWS, KS), F32), pltpu.VMEM((ROWS, KS), F32),
            pltpu.VMEM((ROWS, KS), BF16), pltpu.VMEM((ROWS, KS), BF16),
            pltpu.VMEM((2, NSA_GROUP * WH, WKEYS), F32), pltpu.VMEM((2, NSA_GROUP * WH, WKEYS), BF16),
            pltpu.VMEM((TQ, ncp), F32), pltpu.VMEM((ROWS, D_HEAD), F32), pltpu.VMEM((MAX_BLOCKS, TQ), jnp.int32),
        ],
        compiler_params=pltpu.CompilerParams(
            dimension_semantics=("arbitrary", "arbitrary", "arbitrary"), vmem_limit_bytes=VMEM_LIMIT),
        name="nsa",
    )(q, kcmp, vcmp, ks, vs, kw, vw, gates, near_t, win_t, q_half, cmp_lhs, ovt, eye)


def _ffn_kernel(x_ref, on_ref, os_ref, wo_n_ref, wo_s_ref, fn_ref, wup_ref, cw_ref, cb_ref, wdn_ref, pm_ref,
                out_ref, prev_ref, perm_ref, h_ref, ua_scr, ub_scr, act_ref):
    tm = x_ref.shape[1]
    seg = tm // 8
    n_chunks = wup_ref.shape[1]

    @pl.when(pl.program_id(1) == 0)
    def _start_of_sequence():
        prev_ref[...] = jnp.zeros(prev_ref.shape, F32)

    x1 = x_ref[0] + _dot(on_ref[0], wo_n_ref[...]) + _dot(os_ref[0], wo_s_ref[...])
    ms = jnp.mean(x1 * x1, axis=-1, keepdims=True)
    hn = (x1 * lax.rsqrt(ms + EPS) * fn_ref[...]).astype(BF16)
    out_ref[0] = x1
    n_lt = x1.shape[1] // LANES
    h_ref[...] = _halves(_dot, pm_ref[...], hn).astype(BF16)

    def up(c, dst):
        h = h_ref[...]
        for half in range(2):
            dst[half] = _dot(h, wup_ref[half, c])

    first_sublane = lax.broadcasted_iota(jnp.int32, (8, FFN_CHUNK), 0) == 0

    def conv_gate(c, src):
        for r0 in range(0, tm, FFN_ROWS):
            ys = []
            for half in range(2):
                w = cw_ref[half, c]
                x0 = src[half, r0:r0 + FFN_ROWS]
                if r0 == 0:
                    wrap = [jnp.where(first_sublane, pltpu.roll(prev_ref[half, c, 8 * k:8 * k + 8], 1, 0),
                                      pltpu.roll(src[half, tm - 16 + 8 * k:tm - 8 + 8 * k], 1, 0)) for k in range(2)]
                    x1_ = jnp.concatenate([wrap[1], src[half, 0:FFN_ROWS - 8]], axis=0)
                    x2_ = jnp.concatenate([wrap[0], wrap[1], src[half, 0:FFN_ROWS - 16]], axis=0)
                else:
                    x1_ = src[half, r0 - 8:r0 + FFN_ROWS - 8]
                    x2_ = src[half, r0 - 16:r0 + FFN_ROWS - 16]
                ys.append(x0 * w[2:3] + x1_ * w[1:2] + x2_ * w[0:1] + cb_ref[half, c])
            a, g = ys
            col = pl.multiple_of(c * FFN_CHUNK, FFN_CHUNK)
            act_ref[r0:r0 + FFN_ROWS, pl.ds(col, FFN_CHUNK)] = ((g + g * jnp.tanh(g)) * a).astype(BF16)
        for half in range(2):
            prev_ref[half, c] = src[half, tm - 16:tm]

    up(0, ua_scr)

    def pair(u, carry):
        up(2 * u + 1, ub_scr)
        conv_gate(2 * u, ua_scr)
        up(2 * u + 2, ua_scr)
        conv_gate(2 * u + 1, ub_scr)
        return carry

    lax.fori_loop(0, (n_chunks - 1) // 2, pair, 0)
    conv_gate(n_chunks - 1, ua_scr)
    y = _halves(_dot, act_ref[...], wdn_ref[...])
    for c in range(n_lt):
        perm_ref[c] = y[:, c * LANES:(c + 1) * LANES]
    for sgm in range(8):
        natural = jnp.concatenate([perm_ref[c, pl.ds(sgm, seg, stride=8), :] for c in range(n_lt)], axis=1)
        out_ref[0, sgm * seg:(sgm + 1) * seg, :] += natural


def _ffn_call(x, o_nsa, o_sgu, wo_n, wo_s, ffn_norm, w_up, conv_w, conv_b, w_down, tm):
    B, T, _ = x.shape
    r = np.arange(tm)
    perm = jnp.asarray(((r % 8) * (tm // 8) + r // 8)[:, None] == r[None, :], BF16)
    n_chunks = w_up.shape[1]
    assert n_chunks % 2 == 1
    resident =lambda a: pl.BlockSpec(a.shape, lambda b, i: (0,) * a.ndim, pipeline_mode=pl.Buffered(1))
    tile = lambda w: pl.BlockSpec((1, tm, w), lambda b, i: (b, i, 0))
    return pl.pallas_call(
        _ffn_kernel,
        grid=(B, T // tm),
        in_specs=[tile(D_MODEL), tile(D_NSA), tile(D_GMLP), resident(wo_n), resident(wo_s),
                  resident(ffn_norm), resident(w_up), resident(conv_w), resident(conv_b), resident(w_down),
                  resident(perm)],
        out_specs=tile(D_MODEL),
        out_shape=jax.ShapeDtypeStruct((B, T, D_MODEL), F32),
        scratch_shapes=[pltpu.VMEM((2, n_chunks, 16, FFN_CHUNK), F32), pltpu.VMEM((D_MODEL // LANES, tm, LANES), F32),
                        pltpu.VMEM((tm, D_MODEL), BF16),
                        pltpu.VMEM((2, tm, FFN_CHUNK), F32), pltpu.VMEM((2, tm, FFN_CHUNK), F32),
                        pltpu.VMEM((tm, D_FF), BF16)],
        compiler_params=pltpu.CompilerParams(
            dimension_semantics=("arbitrary", "arbitrary"), vmem_limit_bytes=VMEM_LIMIT),
        name="ffn",
    )(x, o_nsa, o_sgu, wo_n, wo_s, ffn_norm, w_up, conv_w, conv_b, w_down, perm)


def _mixers(x, rel_bias, attn_norm, w_in, q_norm, k_norm_cmp, k_norm_slc, k_norm_win,
            cmp_pe_k, cmp_w1_k, cmp_b1_k, cmp_w2_k, cmp_pe_v, cmp_w1_v, cmp_b1_v, cmp_w2_v,
            sgu_norm, sgu_w, sgu_b):
    B, T, _ = x.shape
    assert T % 512 == 0 and T // SEL_BLOCK <= MAX_BLOCKS and (T // CMP_STRIDE) % LANES == 0
    ncp = T // CMP_STRIDE
    G, R = NSA_KV_HEADS, NSA_GROUP

    o_kv = D_NSA
    o_g = D_NSA + 6 * D_KV
    o_uv = o_g + N_GATES
    w_q = w_in[:, :o_kv].astype(BF16)
    w_kv = w_in[:, o_kv:o_g].astype(BF16)
    per_group = NSA_GROUP * N_BRANCH
    w_g = jnp.pad(w_in[:, o_g:o_uv].reshape(D_MODEL, G, per_group), ((0, 0), (0, 0), (0, LANES - per_group)))
    w_g = w_g.reshape(D_MODEL, G * LANES).astype(BF16)
    w_uv = w_in[:, o_uv:].astype(BF16)
    qn = (jnp.tile(q_norm, NSA_HEADS) * (D_HEAD ** -0.5 * LOG2E)).reshape(1, D_NSA)
    ksn = jnp.tile(k_norm_slc, G).reshape(1, D_KV)
    kwn = jnp.tile(k_norm_win, G).reshape(1, D_KV)
    sgu_bt = jnp.repeat(sgu_b.T, GMLP_GROUP_DIM, axis=1)
    grp = np.arange(D_NSA) // D_HEAD
    bd = jnp.asarray(grp[:, None] == grp[None, :], BF16)

    q, kc, vc, ks, vs, kw, vw, gates, o_sgu = _proj_call(
        x, attn_norm.reshape(1, D_MODEL), w_q, w_kv, w_g, w_uv, qn, ksn, kwn,
        sgu_norm.reshape(1, D_GMLP), sgu_w, sgu_bt, bd, tm=1024)

    half = CMP_STRIDE * D_HEAD
    kcmp, vcmp = _compress_call(
        kc.reshape(B, G, ncp, half), vc.reshape(B, G, ncp, half),
        cmp_pe_k.reshape(2, half), cmp_w1_k.astype(BF16), cmp_b1_k.reshape(1, CMP_HIDDEN),
        cmp_w2_k.astype(BF16), k_norm_cmp.reshape(1, D_HEAD),
        cmp_pe_v.reshape(2, half), cmp_w1_v.astype(BF16), cmp_b1_v.reshape(1, CMP_HIDDEN),
        cmp_w2_v.astype(BF16))

    near_t, win_t, q_half, cmp_lhs = _bias_tables(rel_bias)
    jj = np.arange(MAX_BLOCKS)[:, None]
    nn = np.arange(ncp)[None, :]
    ovt = jnp.asarray((nn >= 4 * jj - 1) & (nn <= 4 * jj + 3), BF16)
    eye = jnp.asarray(np.eye(TQ), BF16)

    o_nsa = _nsa_call(q, kcmp, vcmp, ks, vs, kw, vw, gates, near_t, win_t, q_half, cmp_lhs, ovt, eye)
    return o_nsa, o_sgu


def _layer(x, rel_bias, attn_norm, w_in, q_norm, k_norm_cmp, k_norm_slc, k_norm_win,
           cmp_pe_k, cmp_w1_k, cmp_b1_k, cmp_w2_k, cmp_pe_v, cmp_w1_v, cmp_b1_v, cmp_w2_v,
           sgu_norm, sgu_w, sgu_b, w_out, ffn_norm, w_up, conv_w, conv_b, w_down):
    o_nsa, o_sgu = _mixers(x, rel_bias, attn_norm, w_in, q_norm, k_norm_cmp, k_norm_slc, k_norm_win,
                           cmp_pe_k, cmp_w1_k, cmp_b1_k, cmp_w2_k, cmp_pe_v, cmp_w1_v, cmp_b1_v, cmp_w2_v,
                           sgu_norm, sgu_w, sgu_b)
    wo = w_out.astype(BF16)
    n_chunks = D_FF // FFN_CHUNK
    wu = jnp.transpose(w_up.astype(BF16).reshape(D_MODEL, 2, n_chunks, FFN_CHUNK), (1, 2, 0, 3))
    half_gate = jnp.asarray([1.0, 0.5], F32).reshape(2, 1, 1, 1)
    cw = jnp.transpose(conv_w.reshape(conv_w.shape[0], 2, n_chunks, FFN_CHUNK), (1, 2, 0, 3)) * half_gate
    cb = conv_b.reshape(2, n_chunks, 1, FFN_CHUNK) * half_gate
    wd = w_down.astype(BF16)
    return _ffn_call(x, o_nsa, o_sgu, wo[:D_NSA], wo[D_NSA:], ffn_norm.reshape(1, D_MODEL), wu, cw, cb, wd, tm=512)


def kernel(x, rel_bias, attn_norm, w_in, q_norm, k_norm_cmp, k_norm_slc, k_norm_win,
           cmp_pe_k, cmp_w1_k, cmp_b1_k, cmp_w2_k, cmp_pe_v, cmp_w1_v, cmp_b1_v, cmp_w2_v,
           sgu_norm, sgu_w, sgu_b, w_out, ffn_norm, w_up, conv_w, conv_b, w_down):
    depth = attn_norm.shape[0]
    for l in range(depth):
        x = _layer(x, rel_bias, attn_norm[l], w_in[l], q_norm[l], k_norm_cmp[l], k_norm_slc[l], k_norm_win[l],
                   cmp_pe_k[l], cmp_w1_k[l], cmp_b1_k[l], cmp_w2_k[l], cmp_pe_v[l], cmp_w1_v[l], cmp_b1_v[l],
                   cmp_w2_v[l], sgu_norm[l], sgu_w[l], sgu_b[l], w_out[l], ffn_norm[l], w_up[l], conv_w[l],
                   conv_b[l], w_down[l])
    return x
```

```python
import functools
import math

import numpy as np
import jax
import jax.numpy as jnp
from jax import lax
from jax.experimental import pallas as pl
from jax.experimental.pallas import tpu as pltpu

F32 = jnp.float32
BF16 = jnp.bfloat16

D_MODEL = 1024
D_HEAD = 64
NSA_HEADS = 8
NSA_KV_HEADS = 2
NSA_GROUP = NSA_HEADS // NSA_KV_HEADS
D_NSA = NSA_HEADS * D_HEAD
D_KV = NSA_KV_HEADS * D_HEAD
N_BRANCH = 3
N_GATES = NSA_HEADS * N_BRANCH
CMP_BLOCK = 32
CMP_STRIDE = 16
CMP_HIDDEN = 256
SEL_BLOCK = 64
SEL_TOPK = 16
WINDOW = 512
GMLP_GROUPS = 8
GMLP_GROUP_DIM = 64
D_GMLP = GMLP_GROUPS * GMLP_GROUP_DIM
CHUNK = 128
D_MIX = D_NSA + D_GMLP
REL_BUCKETS = 32
REL_MAX_DIST = 128
D_FF = 2816
EPS = 1e-6
NEG = -1e30
FORCED_SCORE = 1e4
LOG2E = 1.4426950408889634

LANES = 128
TQ = 256
SUB = TQ // SEL_BLOCK
ROWS = NSA_GROUP * TQ
KT = 256
KS = 2 * KT
WH = TQ // 2
WKEYS = WINDOW + WH
RB = 64
BLOCKS_PER_TILE = KT // SEL_BLOCK
WIN_BLOCKS = WINDOW // SEL_BLOCK
KPAD = 1280
N_TOEPLITZ = 3
MAX_BLOCKS = 64
KAUG = 256
VAUG = 128
CBAND = 28
FFN_CHUNK = 256
FFN_ROWS = 64
VMEM_LIMIT = 56 * 1024 * 1024

_NT = (((1,), (1,)), ((), ()))


def _dot(a, b):
    return jnp.dot(a, b, preferred_element_type=F32)


def _dot_nt(a, b):
    return lax.dot_general(a, b, _NT, preferred_element_type=F32)


def _halves(dot, a, b):
    h = a.shape[0] // 2
    return jnp.concatenate([dot(a[:h], b), dot(a[h:], b)], axis=0)


def _split_bf16(x):
    hi = x.astype(BF16)
    lo = (x - hi.astype(F32)).astype(BF16)
    return hi, lo


def _gelu_tanh(x):
    return 0.5 * x * (1.0 + jnp.tanh(0.7978845608028654 * (x + 0.044715 * (x * x * x))))


def _sigmoid(x):
    return 0.5 * (1.0 + jnp.tanh(0.5 * x))


def _group_mean_sq(t, ones_blockdiag, width):
    t2 = t * t
    hi, lo = _split_bf16(t2)
    return (_dot(hi, ones_blockdiag) + _dot(lo, ones_blockdiag)) * (1.0 / width)


def _proj_kernel(x_ref, an_ref, wq_ref, wkv_ref, wg_ref, wuv_ref, qn_ref, ksn_ref, kwn_ref,
                 sgun_ref, sguw_ref, sgub_ref, bd_ref,
                 q_out, kc_out, vc_out, ks_out, vs_out, kw_out, vw_out, g_out, sgu_out):
    tm = x_ref.shape[1]
    x = x_ref[0]
    ms = jnp.mean(x * x, axis=-1, keepdims=True)
    h = (x * lax.rsqrt(ms + EPS) * an_ref[...]).astype(BF16)

    bd = bd_ref[...]
    q = _halves(_dot, h, wq_ref[...])
    qn = q * lax.rsqrt(_group_mean_sq(q, bd, D_HEAD) + EPS) * qn_ref[...]
    for hh in range(NSA_HEADS):
        q_out[0, hh] = qn[:, hh * D_HEAD:(hh + 1) * D_HEAD].astype(BF16)

    kv = _halves(_dot, h, wkv_ref[...])
    bd_kv = bd[:D_KV, :D_KV]
    kc = kv[:, 0 * D_KV:1 * D_KV]
    vc = kv[:, 1 * D_KV:2 * D_KV]
    ks = kv[:, 2 * D_KV:3 * D_KV]
    vs = kv[:, 3 * D_KV:4 * D_KV]
    kw = kv[:, 4 * D_KV:5 * D_KV]
    vw = kv[:, 5 * D_KV:6 * D_KV]
    ks = ks * lax.rsqrt(_group_mean_sq(ks, bd_kv, D_HEAD) + EPS) * ksn_ref[...]
    kw = kw * lax.rsqrt(_group_mean_sq(kw, bd_kv, D_HEAD) + EPS) * kwn_ref[...]
    for t, o_ref in ((kc, kc_out), (vc, vc_out), (ks, ks_out), (vs, vs_out), (kw, kw_out), (vw, vw_out)):
        for g in range(NSA_KV_HEADS):
            o_ref[0, g] = t[:, g * D_HEAD:(g + 1) * D_HEAD].astype(BF16)

    gates = _sigmoid(_dot(h, wg_ref[...]))
    for g in range(NSA_KV_HEADS):
        g_out[0, g] = gates[:, g * LANES:(g + 1) * LANES]

    uv = _gelu_tanh(_halves(_dot, h, wuv_ref[...]))
    u = uv[:, :D_GMLP]
    v = uv[:, D_GMLP:]
    vms = jnp.mean(v * v, axis=-1, keepdims=True)
    vb = (v * lax.rsqrt(vms + EPS) * sgun_ref[...]).astype(BF16)

    row = lax.broadcasted_iota(jnp.int32, (CHUNK, CHUNK), 0)
    col = lax.broadcasted_iota(jnp.int32, (CHUNK, CHUNK), 1)
    tril = col <= row
    w_tril = [jnp.where(tril, sguw_ref[g], 0.0).astype(BF16) for g in range(GMLP_GROUPS)]
    w_pair = [jnp.concatenate([w_tril[2 * p], w_tril[2 * p + 1]], axis=1) for p in range(D_GMLP // LANES)]
    first_half = lax.broadcasted_iota(jnp.int32, (CHUNK, LANES), 1) < GMLP_GROUP_DIM
    zero = jnp.zeros((CHUNK, LANES), BF16)
    for c in range(tm // CHUNK):
        rows = slice(c * CHUNK, (c + 1) * CHUNK)
        zs = []
        for p in range(D_GMLP // LANES):
            blk = vb[rows, p * LANES:(p + 1) * LANES]
            rhs = jnp.concatenate([jnp.where(first_half, blk, zero), jnp.where(first_half, zero, blk)], axis=0)
            zs.append(_dot(w_pair[p], rhs))
        z = jnp.concatenate(zs, axis=1) + sgub_ref[...]
        sgu_out[0, rows, :] = (u[rows, :] * z).astype(BF16)


def _proj_call(x, attn_norm, w_q, w_kv, w_g, w_uv, qn, ksn, kwn, sgu_norm, sgu_w, sgu_bt, bd, tm):
    B, T, _ = x.shape
    const2 = lambda b, i: (0, 0)
    const3 = lambda b, i: (0, 0, 0)
    head_spec = lambda nh: pl.BlockSpec((1, nh, tm, D_HEAD), lambda b, i: (b, 0, i, 0))
    kv_shape = jax.ShapeDtypeStruct((B, NSA_KV_HEADS, T, D_HEAD), BF16)
    return pl.pallas_call(
        _proj_kernel,
        grid=(B, T // tm),
        in_specs=[
            pl.BlockSpec((1, tm, D_MODEL), lambda b, i: (b, i, 0)),
            pl.BlockSpec((1, D_MODEL), const2),
            pl.BlockSpec(w_q.shape, const2),
            pl.BlockSpec(w_kv.shape, const2),
            pl.BlockSpec(w_g.shape, const2),
            pl.BlockSpec(w_uv.shape, const2),
            pl.BlockSpec((1, D_NSA), const2),
            pl.BlockSpec((1, D_KV), const2),
            pl.BlockSpec((1, D_KV), const2),
            pl.BlockSpec((1, D_GMLP), const2),
            pl.BlockSpec(sgu_w.shape, const3),
            pl.BlockSpec(sgu_bt.shape, const2),
            pl.BlockSpec(bd.shape, const2),
        ],
        out_specs=[
            head_spec(NSA_HEADS),
            head_spec(NSA_KV_HEADS), head_spec(NSA_KV_HEADS), head_spec(NSA_KV_HEADS),
            head_spec(NSA_KV_HEADS), head_spec(NSA_KV_HEADS), head_spec(NSA_KV_HEADS),
            pl.BlockSpec((1, NSA_KV_HEADS, tm, LANES), lambda b, i: (b, 0, i, 0)),
            pl.BlockSpec((1, tm, D_GMLP), lambda b, i: (b, i, 0)),
        ],
        out_shape=[
            jax.ShapeDtypeStruct((B, NSA_HEADS, T, D_HEAD), BF16),
            kv_shape, kv_shape, kv_shape, kv_shape, kv_shape, kv_shape,
            jax.ShapeDtypeStruct((B, NSA_KV_HEADS, T, LANES), F32),
            jax.ShapeDtypeStruct((B, T, D_GMLP), BF16),
        ],
        compiler_params=pltpu.CompilerParams(
            dimension_semantics=("arbitrary", "arbitrary"), vmem_limit_bytes=VMEM_LIMIT),
        name="proj",
    )(x, attn_norm, w_q, w_kv, w_g, w_uv, qn, ksn, kwn, sgu_norm, sgu_w, sgu_bt, bd)


def _compress_one(tok_ref, pe_ref, w1_ref, b1_ref, w2_ref):
    half = CMP_STRIDE * D_HEAD
    tok = tok_ref[0, 0].astype(F32)
    top = (tok + pe_ref[0:1, :]).astype(BF16)
    bot = (tok + pe_ref[1:2, :]).astype(BF16)
    a = _dot(top, w1_ref[:half, :])
    b = _dot(bot, w1_ref[half:, :])
    ncp = a.shape[0]
    pre = a + pltpu.roll(b, ncp - 1, 0) + b1_ref[...]
    hid = _gelu_tanh(pre).astype(BF16)
    return _dot(hid, w2_ref[...])


def _compress_kernel(kc_ref, vc_ref, pek_ref, w1k_ref, b1k_ref, w2k_ref, kn_ref,
                     pev_ref, w1v_ref, b1v_ref, w2v_ref, k_out, v_out):
    k = _compress_one(kc_ref, pek_ref, w1k_ref, b1k_ref, w2k_ref)
    kms = jnp.mean(k * k, axis=-1, keepdims=True)
    k_out[0, 0] = (k * lax.rsqrt(kms + EPS) * kn_ref[...]).astype(BF16)
    v_out[0, 0] = _compress_one(vc_ref, pev_ref, w1v_ref, b1v_ref, w2v_ref).astype(BF16)


def _compress_call(kc, vc, pek, w1k, b1k, w2k, kn, pev, w1v, b1v, w2v):
    B, G, ncp, width = kc.shape
    const2 = lambda b, g: (0, 0)
    tok_spec = pl.BlockSpec((1, 1, ncp, width), lambda b, g: (b, g, 0, 0))
    out_spec = pl.BlockSpec((1, 1, ncp, D_HEAD), lambda b, g: (b, g, 0, 0))
    full = lambda a: pl.BlockSpec(a.shape, const2)
    out_shape = jax.ShapeDtypeStruct((B, G, ncp, D_HEAD), BF16)
    return pl.pallas_call(
        _compress_kernel,
        grid=(B, G),
        in_specs=[tok_spec, tok_spec, full(pek), full(w1k), full(b1k), full(w2k), full(kn),
                  full(pev), full(w1v), full(b1v), full(w2v)],
        out_specs=[out_spec, out_spec],
        out_shape=[out_shape, out_shape],
        compiler_params=pltpu.CompilerParams(
            dimension_semantics=("arbitrary", "arbitrary"), vmem_limit_bytes=VMEM_LIMIT),
        name="compress",
    )(kc, vc, pek, w1k, b1k, w2k, kn, pev, w1v, b1v, w2v)


def _rel_bucket_np(dist):
    max_exact = REL_BUCKETS // 2
    d = np.maximum(dist, 1).astype(np.float32)
    log_b = max_exact + (np.log(d / np.float32(max_exact)) / np.float32(math.log(REL_MAX_DIST / max_exact))
                         * np.float32(REL_BUCKETS - max_exact)).astype(np.int32)
    log_b = np.clip(log_b, max_exact, REL_BUCKETS - 1)
    return np.where(dist < max_exact, np.maximum(dist, 0), log_b)


def _bias_tables(rel_bias):
    rb = rel_bias.astype(F32) * LOG2E
    qq = np.arange(SEL_BLOCK)[:, None]
    kp = np.arange(SEL_BLOCK)[None, :]
    tile_d = [m * SEL_BLOCK + qq - kp for m in range(N_TOEPLITZ)]
    s_ = np.arange(SUB)[:, None, None]
    e_ = np.arange(CBAND)[None, None, :]
    dist_c = qq[None, :, :] + CMP_STRIDE * (e_ - 4 * (SUB - 1) + 4 * s_) - (3 * CMP_STRIDE + CMP_BLOCK - 1)
    all_d = np.concatenate([d.reshape(-1) for d in tile_d] + [dist_c.reshape(-1)])
    onehot = np.eye(REL_BUCKETS, dtype=np.float32)[_rel_bucket_np(all_d)]
    vals = jnp.dot(jnp.asarray(onehot), rb, precision=lax.Precision.HIGHEST)
    vals = jnp.where(jnp.asarray(all_d >= 0)[:, None], vals, NEG)
    n_t = N_TOEPLITZ * SEL_BLOCK * SEL_BLOCK
    toep = jnp.transpose(vals[:n_t].reshape(N_TOEPLITZ, SEL_BLOCK, SEL_BLOCK, NSA_HEADS), (0, 3, 1, 2))
    band = jnp.transpose(vals[n_t:].reshape(SUB, SEL_BLOCK, CBAND, NSA_HEADS), (3, 0, 1, 2))
    far = rb[REL_BUCKETS - 1]
    far_tile = jnp.broadcast_to(far[:, None, None], (NSA_HEADS, SEL_BLOCK, SEL_BLOCK))
    neg_tile = jnp.full((NSA_HEADS, SEL_BLOCK, SEL_BLOCK), NEG, F32)
    edge_tile = jnp.where(jnp.asarray(kp > qq)[None], far_tile, NEG)

    def tile(m, windowed):
        if m < 0 or (windowed and m > WIN_BLOCKS):
            return neg_tile
        if windowed and m == WIN_BLOCKS:
            return edge_tile
        return toep[m] if m < N_TOEPLITZ else far_tile

    def table(first_m, n_blocks, windowed):
        rows = [jnp.concatenate([tile(first_m + s - c, windowed) for c in range(n_blocks)], axis=2)
                for s in range(SUB)]
        t = jnp.concatenate(rows, axis=1)
        return t.reshape(NSA_KV_HEADS, ROWS, n_blocks * SEL_BLOCK)

    far_rows = jnp.repeat(far, TQ).reshape(NSA_KV_HEADS, ROWS, 1)
    near_t = table(BLOCKS_PER_TILE, 2 * BLOCKS_PER_TILE, False) - far_rows
    win_t = table(WIN_BLOCKS, WIN_BLOCKS + BLOCKS_PER_TILE, True) - far_rows
    win_t = win_t.reshape(NSA_KV_HEADS, NSA_GROUP, 2, WH, (WIN_BLOCKS + BLOCKS_PER_TILE) * SEL_BLOCK)
    win_t = jnp.stack([win_t[:, :, half, :, half * WH:half * WH + WKEYS].reshape(NSA_KV_HEADS, NSA_GROUP * WH, WKEYS)
                       for half in range(2)], axis=1)
    band_t = band.reshape(NSA_KV_HEADS, ROWS, CBAND)
    band_hi = band_t.astype(BF16)
    band_lo = (band_t - band_hi.astype(F32)).astype(BF16)
    far_hi = far_rows.astype(BF16)
    far_lo = (far_rows - far_hi.astype(F32)).astype(BF16)
    q_half = jnp.concatenate([jnp.zeros((NSA_KV_HEADS, ROWS, D_HEAD), BF16),
                              jnp.full((NSA_KV_HEADS, ROWS, 1), NEG, BF16), far_hi, far_lo,
                              jnp.zeros((NSA_KV_HEADS, ROWS, LANES - D_HEAD - 3), BF16)], axis=2)
    neg_col = jnp.full((NSA_KV_HEADS, ROWS, 1), NEG, BF16)
    zeros = jnp.zeros((NSA_KV_HEADS, ROWS, D_HEAD - 2 * CBAND - 3), BF16)
    q_lanes = jnp.zeros((NSA_KV_HEADS, ROWS, D_HEAD), BF16)
    cmp_lhs = jnp.concatenate([q_lanes, band_hi, band_lo, far_hi, far_lo, neg_col, zeros], axis=2)
    return near_t, win_t, q_half, cmp_lhs


def _row_max(lane_tiles):
    m = jnp.max(functools.reduce(jnp.maximum, lane_tiles), axis=1, keepdims=True)
    return jnp.broadcast_to(m, lane_tiles[0].shape)


def _nsa_kernel(q_ref, kc_ref, vc_ref, ks_ref, vs_ref, kw_ref, vw_ref, g_ref,
                near_ref, win_ref, qhalf_ref, cmpl_ref, ovt_ref, eye_ref,
                o_ref, ksa, vsa, kwa, vwa, kca, vca, sc_scr, lhs_scr, lhsc_scr, m_scr, al_scr, inv_scr, acc_scr, sa_scr, sb_scr, pa_scr, pb_scr,
                sw_scr, pw_scr, ps_scr, out_scr, cnt_scr):
    qt = pl.program_id(2)
    first = qt * SUB
    T = ks_ref.shape[2]
    ncp = kc_ref.shape[2]

    @pl.when(qt == 0)
    def _init():
        n_chunks = ksa.shape[0] // KT
        r_io = lax.broadcasted_iota(jnp.int32, (KT, KAUG), 0)
        l_io = lax.broadcasted_iota(jnp.int32, (KT, KAUG), 1)
        lv = lax.broadcasted_iota(jnp.int32, (KT, VAUG), 1)
        vpat = jnp.where(lv == D_HEAD, 1.0, 0.0).astype(BF16)

        def fill(c, carry):
            rows = pl.ds(pl.multiple_of(c * KT, KT), KT)
            key = r_io + (c * KT - KPAD)
            blk = key // SEL_BLOCK
            real = (key >= 0) & (key < T)
            hot = real & (((l_io < D_HEAD) & (l_io == blk))
                          | ((l_io >= D_HEAD) & (l_io < 2 * D_HEAD) & (l_io - D_HEAD == blk)))
            hot = hot | ((~real) & (l_io == 3 * D_HEAD)) | (real & (l_io > 3 * D_HEAD) & (l_io <= 3 * D_HEAD + 2))
            pat = jnp.where(hot, 1.0, 0.0).astype(BF16)
            ksa[rows, :] = pat
            kwa[rows, :] = pat
            vsa[rows, :] = vpat
            vwa[rows, :] = vpat
            return carry

        lax.fori_loop(0, n_chunks, fill, 0)
        ksa[KPAD:KPAD + T, 2 * D_HEAD:3 * D_HEAD] = ks_ref[0, 0]
        kwa[KPAD:KPAD + T, 2 * D_HEAD:3 * D_HEAD] = kw_ref[0, 0]
        vsa[KPAD:KPAD + T, 0:D_HEAD] = vs_ref[0, 0]
        vwa[KPAD:KPAD + T, 0:D_HEAD] = vw_ref[0, 0]
        lhs_scr[:, 0:LANES] = jnp.zeros((ROWS, LANES), BF16)
        lc = lax.broadcasted_iota(jnp.int32, (ncp, VAUG), 1)
        kca[...] = jnp.zeros((ncp, VAUG), BF16)
        kca[:, 0:D_HEAD] = kc_ref[0, 0]
        vca[...] = jnp.where(lc == D_HEAD, 1.0, 0.0).astype(BF16)
        vca[:, 0:D_HEAD] = vc_ref[0, 0]

    q4 = q_ref[0].reshape(ROWS, D_HEAD)
    gates = g_ref[0, 0]
    row_blocks = [slice(r, r + RB) for r in range(0, ROWS, RB)]

    gate_rows = jnp.concatenate(
        [gates if hh == 0 else pltpu.roll(gates, LANES - hh * N_BRANCH, 1) for hh in range(NSA_GROUP)], axis=0)

    def gate_col(branch):
        return gate_rows[:, branch:branch + 1]

    lane_c = lax.broadcasted_iota(jnp.int32, (ncp, LANES), 1)
    n_c = lax.broadcasted_iota(jnp.int32, (ncp, LANES), 0)
    e = lane_c - D_HEAD
    base = 4 * first + (4 * SUB - 1)
    ind = (((e >= 0) & (e < CBAND) & (n_c == base - e))
           | ((e >= CBAND) & (e < 2 * CBAND) & (n_c == base - (e - CBAND)))
           | (((e == 2 * CBAND) | (e == 2 * CBAND + 1)) & (n_c < base - (CBAND - 1)))
           | ((e == 2 * CBAND + 2) & (n_c > base)))
    rhs_c = jnp.where(lane_c < D_HEAD, kca[...], jnp.where(ind, 1.0, 0.0).astype(BF16))
    lhsc_scr[...] = cmpl_ref[0]
    lhsc_scr[:, 0:D_HEAD] = q4
    sa_scr[:, 0:ncp] = _halves(_dot_nt, lhsc_scr[...], rhs_c)

    lhs_scr[:, LANES:2 * LANES] = qhalf_ref[0]
    lhs_scr[:, LANES:LANES + D_HEAD] = q4
    win_start = pl.multiple_of(KPAD + (first - WIN_BLOCKS) * SEL_BLOCK, KT)
    for half in range(2):
        rows_q = jnp.concatenate([lhs_scr[hh * TQ + half * WH:hh * TQ + (half + 1) * WH, LANES:2 * LANES]
                                  for hh in range(NSA_GROUP)], axis=0)
        lhs_win = jnp.concatenate([jnp.zeros((NSA_GROUP * WH, LANES), BF16), rows_q], axis=1)
        sw_scr[half] = _dot_nt(lhs_win, kwa[pl.ds(pl.multiple_of(win_start + half * WH, WH), WKEYS), :])

    for i, rows in enumerate(row_blocks):
        s = [sa_scr[rows, c * LANES:(c + 1) * LANES] for c in range(ncp // LANES)]
        m = _row_max(s)
        p = [jnp.exp2(t - m) for t in s]
        l = jnp.sum(functools.reduce(jnp.add, p), axis=1, keepdims=True)
        inv = jnp.where(m[:, 0:1] > 0.5 * NEG, 1.0 / l, 0.0)
        inv_scr[rows] = inv
        inv_b = jnp.broadcast_to(inv, (RB, LANES))
        tok = slice((i % (TQ // RB)) * RB, (i % (TQ // RB) + 1) * RB)
        for c in range(ncp // LANES):
            lanes = slice(c * LANES, (c + 1) * LANES)
            pb_scr[rows, lanes] = p[c].astype(BF16)
            if i < TQ // RB:
                ps_scr[tok, lanes] = p[c] * inv_b
            else:
                ps_scr[tok, lanes] += p[c] * inv_b
    acc_c = _halves(_dot, pb_scr[:, 0:ncp], vca[...])
    out_scr[...] = acc_c[:, :D_HEAD] * (gate_col(0) * inv_scr[...])
    ps_hi, ps_lo = _split_bf16(ps_scr[...])
    ovt = ovt_ref[...]
    imp_t = _dot_nt(ovt, ps_hi) + _dot_nt(ovt, ps_lo)

    j_t = lax.broadcasted_iota(jnp.int32, (MAX_BLOCKS, TQ), 0)
    cur = first + lax.broadcasted_iota(jnp.int32, (MAX_BLOCKS, TQ), 1) // SEL_BLOCK
    forced = (j_t == 0) | (j_t == cur) | (j_t == cur - 1)
    score_t = jnp.where(forced, FORCED_SCORE, jnp.where(j_t <= cur, imp_t, NEG))
    sc_scr[...] = score_t

    cnt_scr[...] = jnp.zeros((MAX_BLOCKS, TQ), jnp.int32)
    sub_io = lax.broadcasted_iota(jnp.int32, (8, TQ), 0)
    n_groups = MAX_BLOCKS // 8
    for grp in range(n_groups):
        @pl.when(grp * 8 < first + SUB)
        def _count_group():
            targets = [score_t[8 * jg:8 * jg + 8] for jg in range(n_groups)]
            acc = [jnp.zeros((8, TQ), jnp.int32) for _ in range(n_groups)]
            for k in range(8):
                row_i = jnp.broadcast_to(sc_scr[grp * 8 + k:grp * 8 + k + 1, :], (8, TQ))
                for jg in range(n_groups):
                    if jg > grp:
                        beats = row_i >= targets[jg]
                    elif jg < grp:
                        beats = row_i > targets[jg]
                    else:
                        beats = (row_i > targets[jg]) | ((row_i == targets[jg]) & (sub_io > k))
                    acc[jg] = acc[jg] + beats.astype(jnp.int32)
            cnt_scr[...] += jnp.concatenate(acc, axis=0)

    cnt = cnt_scr[...]
    mask_t = jnp.where((cnt < SEL_TOPK) & (j_t <= cur), 0.0, NEG).astype(BF16)
    mask = _dot_nt(eye_ref[...], mask_t).astype(BF16)

    for hh in range(NSA_GROUP):
        lhs_scr[hh * TQ:(hh + 1) * TQ, 0:MAX_BLOCKS] = mask

    def tile_start(k):
        return pl.multiple_of(KPAD + (first - BLOCKS_PER_TILE) * SEL_BLOCK - k * KS, KT)

    def scores(k, dst):
        dst[...] = _halves(_dot_nt, lhs_scr[...], ksa[pl.ds(tile_start(k), KS), :])

    def consume(k, src, p_scr, with_table):
        n_lt = KS // LANES
        for rows in row_blocks:
            n_use = n_lt
            if with_table:
                sub = (rows.start % TQ) // SEL_BLOCK
                n_use = -(-(BLOCKS_PER_TILE + sub + 1) * SEL_BLOCK // LANES)
                n_zero = (sub + BLOCKS_PER_TILE + 1 - N_TOEPLITZ) * SEL_BLOCK // LANES
            s = [src[rows, c * LANES:(c + 1) * LANES] for c in range(n_use)]
            if with_table:
                s = [s[c] if c < n_zero else s[c] + near_ref[0, rows, c * LANES:(c + 1) * LANES]
                     for c in range(n_use)]
            if with_table:
                m = _row_max(s)
            else:
                m_prev = m_scr[rows]
                m = jnp.maximum(m_prev, _row_max(s))
                al_scr[rows] = jnp.exp2(m_prev - m)
            m_scr[rows] = m
            for c in range(n_use):
                p_scr[rows, c * LANES:(c + 1) * LANES] = jnp.exp2((s[c] - m).astype(BF16))
            for c in range(n_use, n_lt):
                p_scr[rows, c * LANES:(c + 1) * LANES] = jnp.zeros((RB, LANES), BF16)
        pv = _dot(p_scr[:, 0:KS], vsa[pl.ds(tile_start(k), KS), :])
        acc_scr[...] = pv if with_table else acc_scr[...] * al_scr[...] + pv

    scores(0, sa_scr)

    gate_w = gate_col(2)
    for half in range(2):
        for r0 in range(0, NSA_GROUP * WH, RB):
            rows = slice(r0, r0 + RB)
            s = [sw_scr[half, rows, c * LANES:(c + 1) * LANES] if c in (1, 2) else
                 sw_scr[half, rows, c * LANES:(c + 1) * LANES] + win_ref[0, half, rows, c * LANES:(c + 1) * LANES]
                 for c in range(WKEYS // LANES)]
            m = _row_max(s)
            for c in range(WKEYS // LANES):
                pw_scr[half, rows, c * LANES:(c + 1) * LANES] = jnp.exp2((s[c] - m).astype(BF16))
        acc_w = _dot(pw_scr[half], vwa[pl.ds(pl.multiple_of(win_start + half * WH, WH), WKEYS), :])
        for hh in range(NSA_GROUP):
            dst = slice(hh * TQ + half * WH, hh * TQ + (half + 1) * WH)
            src = slice(hh * WH, (hh + 1) * WH)
            out_scr[dst] += acc_w[src, :D_HEAD] * (gate_w[dst] / acc_w[src, D_HEAD:D_HEAD + 1])

    def pair(with_table, u, carry):
        scores(2 * u + 1, sb_scr)
        consume(2 * u, sa_scr, pa_scr, with_table)
        scores(2 * u + 2, sa_scr)
        consume(2 * u + 1, sb_scr, pb_scr, False)
        return carry

    n_tiles = (qt + 2) // 2
    pair(True, 0, 0)
    lax.fori_loop(1, jnp.maximum(n_tiles // 2, 1), functools.partial(pair, False), 0)

    @pl.when((n_tiles % 2 == 1) & (n_tiles >= 3))
    def _odd_tile():
        consume(n_tiles - 1, sa_scr, pa_scr, False)

    acc_s = acc_scr[...]

    o = out_scr[...] + acc_s[:, :D_HEAD] * (gate_col(1) / acc_s[:, D_HEAD:D_HEAD + 1])
    for hh in range(NSA_GROUP):
        o_ref[0, :, hh * D_HEAD:(hh + 1) * D_HEAD] = o[hh * TQ:(hh + 1) * TQ].astype(BF16)


def _nsa_call(q, kcmp, vcmp, ks, vs, kw, vw, gates, near_t, win_t, q_half, cmp_lhs, ovt, eye):
    B, _, T, _ = q.shape
    G = NSA_KV_HEADS
    ncp = kcmp.shape[2]
    rows_kv = KPAD + T
    per_bg =lambda a: pl.BlockSpec((1, 1) + a.shape[2:], lambda b, g, i: (b, g, 0, 0))
    per_g = lambda a: pl.BlockSpec((1,) + a.shape[1:], lambda b, g, i: (g,) + (0,) * (a.ndim - 1))
    const2 = lambda a: pl.BlockSpec(a.shape, lambda b, g, i: (0, 0))
    return pl.pallas_call(
        _nsa_kernel,
        grid=(B, G, T // TQ),
        in_specs=[
            pl.BlockSpec((1, NSA_GROUP, TQ, D_HEAD), lambda b, g, i: (b, g, i, 0)),
            per_bg(kcmp), per_bg(vcmp), per_bg(ks), per_bg(vs), per_bg(kw), per_bg(vw),
            pl.BlockSpec((1, 1, TQ, LANES), lambda b, g, i: (b, g, i, 0)),
            per_g(near_t), per_g(win_t), per_g(q_half), per_g(cmp_lhs), const2(ovt), const2(eye),
        ],
        out_specs=pl.BlockSpec((1, TQ, NSA_GROUP * D_HEAD), lambda b, g, i: (b, i, g)),
        out_shape=jax.ShapeDtypeStruct((B, T, D_NSA), BF16),
        scratch_shapes=[
            pltpu.VMEM((rows_kv, KAUG), BF16), pltpu.VMEM((rows_kv, VAUG), BF16),
            pltpu.VMEM((rows_kv, KAUG), BF16), pltpu.VMEM((rows_kv, VAUG), BF16),
            pltpu.VMEM((ncp, VAUG), BF16), pltpu.VMEM((ncp, VAUG), BF16),
            pltpu.VMEM((MAX_BLOCKS, TQ), F32),
            pltpu.VMEM((ROWS, KAUG), BF16), pltpu.VMEM((ROWS, LANES), BF16),
            pltpu.VMEM((ROWS, LANES), F32), pltpu.VMEM((ROWS, LANES), F32), pltpu.VMEM((ROWS, 1), F32),
            pltpu.VMEM((ROWS, VAUG), F32),
            pltpu.VMEM((ROWS, KS), F32), pltpu.VMEM((ROWS, KS), F32),
            pltpu.VMEM((ROWS, KS), BF16), pltpu.VMEM((ROWS, KS), BF16),
            pltpu.VMEM((2, NSA_GROUP * WH, WKEYS), F32), pltpu.VMEM((2, NSA_GROUP * WH, WKEYS), BF16),
            pltpu.VMEM((TQ, ncp), F32), pltpu.VMEM((ROWS, D_HEAD), F32), pltpu.VMEM((MAX_BLOCKS, TQ), jnp.int32),
        ],
        compiler_params=pltpu.CompilerParams(
            dimension_semantics=("arbitrary", "arbitrary", "arbitrary"), vmem_limit_bytes=VMEM_LIMIT),
        name="nsa",
    )(q, kcmp, vcmp, ks, vs, kw, vw, gates, near_t, win_t, q_half, cmp_lhs, ovt, eye)


def _ffn_kernel(x_ref, on_ref, os_ref, wo_n_ref, wo_s_ref, fn_ref, wup_ref, cw_ref, cb_ref, wdn_ref, pm_ref,
                out_ref, prev_ref, perm_ref, h_ref, ua_scr, ub_scr, act_ref):
    tm = x_ref.shape[1]
    seg = tm // 8
    n_chunks = wup_ref.shape[1]

    @pl.when(pl.program_id(1) == 0)
    def _start_of_sequence():
        prev_ref[...] = jnp.zeros(prev_ref.shape, F32)

    x1 = x_ref[0] + _dot(on_ref[0], wo_n_ref[...]) + _dot(os_ref[0], wo_s_ref[...])
    ms = jnp.mean(x1 * x1, axis=-1, keepdims=True)
    hn = (x1 * lax.rsqrt(ms + EPS) * fn_ref[...]).astype(BF16)
    out_ref[0] = x1
    n_lt = x1.shape[1] // LANES
    h_ref[...] = _halves(_dot, pm_ref[...], hn).astype(BF16)

    def up(c, dst):
        h = h_ref[...]
        for half in range(2):
            dst[half] = _dot(h, wup_ref[half, c])

    first_sublane = lax.broadcasted_iota(jnp.int32, (8, FFN_CHUNK), 0) == 0

    def conv_gate(c, src):
        for r0 in range(0, tm, FFN_ROWS):
            ys = []
            for half in range(2):
                w = cw_ref[half, c]
                x0 = src[half, r0:r0 + FFN_ROWS]
                if r0 == 0:
                    wrap = [jnp.where(first_sublane, pltpu.roll(prev_ref[half, c, 8 * k:8 * k + 8], 1, 0),
                                      pltpu.roll(src[half, tm - 16 + 8 * k:tm - 8 + 8 * k], 1, 0)) for k in range(2)]
                    x1_ = jnp.concatenate([wrap[1], src[half, 0:FFN_ROWS - 8]], axis=0)
                    x2_ = jnp.concatenate([wrap[0], wrap[1], src[half, 0:FFN_ROWS - 16]], axis=0)
                else:
                    x1_ = src[half, r0 - 8:r0 + FFN_ROWS - 8]
                    x2_ = src[half, r0 - 16:r0 + FFN_ROWS - 16]
                ys.append(x0 * w[2:3] + x1_ * w[1:2] + x2_ * w[0:1] + cb_ref[half, c])
            a, g = ys
            col = pl.multiple_of(c * FFN_CHUNK, FFN_CHUNK)
            act_ref[r0:r0 + FFN_ROWS, pl.ds(col, FFN_CHUNK)] = ((g + g * jnp.tanh(g)) * a).astype(BF16)
        for half in range(2):
            prev_ref[half, c] = src[half, tm - 16:tm]

    up(0, ua_scr)

    def pair(u, carry):
        up(2 * u + 1, ub_scr)
        conv_gate(2 * u, ua_scr)
        up(2 * u + 2, ua_scr)
        conv_gate(2 * u + 1, ub_scr)
        return carry

    lax.fori_loop(0, (n_chunks - 1) // 2, pair, 0)
    conv_gate(n_chunks - 1, ua_scr)
    y = _halves(_dot, act_ref[...], wdn_ref[...])
    for c in range(n_lt):
        perm_ref[c] = y[:, c * LANES:(c + 1) * LANES]
    for sgm in range(8):
        natural = jnp.concatenate([perm_ref[c, pl.ds(sgm, seg, stride=8), :] for c in range(n_lt)], axis=1)
        out_ref[0, sgm * seg:(sgm + 1) * seg, :] += natural


def _ffn_call(x, o_nsa, o_sgu, wo_n, wo_s, ffn_norm, w_up, conv_w, conv_b, w_down, tm):
    B, T, _ = x.shape
    r = np.arange(tm)
    perm = jnp.asarray(((r % 8) * (tm // 8) + r // 8)[:, None] == r[None, :], BF16)
    n_chunks = w_up.shape[1]
    assert n_chunks % 2 == 1
    resident =lambda a: pl.BlockSpec(a.shape, lambda b, i: (0,) * a.ndim, pipeline_mode=pl.Buffered(1))
    tile = lambda w: pl.BlockSpec((1, tm, w), lambda b, i: (b, i, 0))
    return pl.pallas_call(
        _ffn_kernel,
        grid=(B, T // tm),
        in_specs=[tile(D_MODEL), tile(D_NSA), tile(D_GMLP), resident(wo_n), resident(wo_s),
                  resident(ffn_norm), resident(w_up), resident(conv_w), resident(conv_b), resident(w_down),
                  resident(perm)],
        out_specs=tile(D_MODEL),
        out_shape=jax.ShapeDtypeStruct((B, T, D_MODEL), F32),
        scratch_shapes=[pltpu.VMEM((2, n_chunks, 16, FFN_CHUNK), F32), pltpu.VMEM((D_MODEL // LANES, tm, LANES), F32),
                        pltpu.VMEM((tm, D_MODEL), BF16),
                        pltpu.VMEM((2, tm, FFN_CHUNK), F32), pltpu.VMEM((2, tm, FFN_CHUNK), F32),
                        pltpu.VMEM((tm, D_FF), BF16)],
        compiler_params=pltpu.CompilerParams(
            dimension_semantics=("arbitrary", "arbitrary"), vmem_limit_bytes=VMEM_LIMIT),
        name="ffn",
    )(x, o_nsa, o_sgu, wo_n, wo_s, ffn_norm, w_up, conv_w, conv_b, w_down, perm)


def _mixers(x, rel_bias, attn_norm, w_in, q_norm, k_norm_cmp, k_norm_slc, k_norm_win,
            cmp_pe_k, cmp_w1_k, cmp_b1_k, cmp_w2_k, cmp_pe_v, cmp_w1_v, cmp_b1_v, cmp_w2_v,
            sgu_norm, sgu_w, sgu_b):
    B, T, _ = x.shape
    assert T % 512 == 0 and T // SEL_BLOCK <= MAX_BLOCKS and (T // CMP_STRIDE) % LANES == 0
    ncp = T // CMP_STRIDE
    G, R = NSA_KV_HEADS, NSA_GROUP

    o_kv = D_NSA
    o_g = D_NSA + 6 * D_KV
    o_uv = o_g + N_GATES
    w_q = w_in[:, :o_kv].astype(BF16)
    w_kv = w_in[:, o_kv:o_g].astype(BF16)
    per_group = NSA_GROUP * N_BRANCH
    w_g = jnp.pad(w_in[:, o_g:o_uv].reshape(D_MODEL, G, per_group), ((0, 0), (0, 0), (0, LANES - per_group)))
    w_g = w_g.reshape(D_MODEL, G * LANES).astype(BF16)
    w_uv = w_in[:, o_uv:].astype(BF16)
    qn = (jnp.tile(q_norm, NSA_HEADS) * (D_HEAD ** -0.5 * LOG2E)).reshape(1, D_NSA)
    ksn = jnp.tile(k_norm_slc, G).reshape(1, D_KV)
    kwn = jnp.tile(k_norm_win, G).reshape(1, D_KV)
    sgu_bt = jnp.repeat(sgu_b.T, GMLP_GROUP_DIM, axis=1)
    grp = np.arange(D_NSA) // D_HEAD
    bd = jnp.asarray(grp[:, None] == grp[None, :], BF16)

    q, kc, vc, ks, vs, kw, vw, gates, o_sgu = _proj_call(
        x, attn_norm.reshape(1, D_MODEL), w_q, w_kv, w_g, w_uv, qn, ksn, kwn,
        sgu_norm.reshape(1, D_GMLP), sgu_w, sgu_bt, bd, tm=1024)

    half = CMP_STRIDE * D_HEAD
    kcmp, vcmp = _compress_call(
        kc.reshape(B, G, ncp, half), vc.reshape(B, G, ncp, half),
        cmp_pe_k.reshape(2, half), cmp_w1_k.astype(BF16), cmp_b1_k.reshape(1, CMP_HIDDEN),
        cmp_w2_k.astype(BF16), k_norm_cmp.reshape(1, D_HEAD),
        cmp_pe_v.reshape(2, half), cmp_w1_v.astype(BF16), cmp_b1_v.reshape(1, CMP_HIDDEN),
        cmp_w2_v.astype(BF16))

    near_t, win_t, q_half, cmp_lhs = _bias_tables(rel_bias)
    jj = np.arange(MAX_BLOCKS)[:, None]
    nn = np.arange(ncp)[None, :]
    ovt = jnp.asarray((nn >= 4 * jj - 1) & (nn <= 4 * jj + 3), BF16)
    eye = jnp.asarray(np.eye(TQ), BF16)

    o_nsa = _nsa_call(q, kcmp, vcmp, ks, vs, kw, vw, gates, near_t, win_t, q_half, cmp_lhs, ovt, eye)
    return o_nsa, o_sgu


def _layer(x, rel_bias, attn_norm, w_in, q_norm, k_norm_cmp, k_norm_slc, k_norm_win,
           cmp_pe_k, cmp_w1_k, cmp_b1_k, cmp_w2_k, cmp_pe_v, cmp_w1_v, cmp_b1_v, cmp_w2_v,
           sgu_norm, sgu_w, sgu_b, w_out, ffn_norm, w_up, conv_w, conv_b, w_down):
    o_nsa, o_sgu = _mixers(x, rel_bias, attn_norm, w_in, q_norm, k_norm_cmp, k_norm_slc, k_norm_win,
                           cmp_pe_k, cmp_w1_k, cmp_b1_k, cmp_w2_k, cmp_pe_v, cmp_w1_v, cmp_b1_v, cmp_w2_v,
                           sgu_norm, sgu_w, sgu_b)
    wo = w_out.astype(BF16)
    n_chunks = D_FF // FFN_CHUNK
    wu = jnp.transpose(w_up.astype(BF16).reshape(D_MODEL, 2, n_chunks, FFN_CHUNK), (1, 2, 0, 3))
    half_gate = jnp.asarray([1.0, 0.5], F32).reshape(2, 1, 1, 1)
    cw = jnp.transpose(conv_w.reshape(conv_w.shape[0], 2, n_chunks, FFN_CHUNK), (1, 2, 0, 3)) * half_gate
    cb = conv_b.reshape(2, n_chunks, 1, FFN_CHUNK) * half_gate
    wd = w_down.astype(BF16)
    return _ffn_call(x, o_nsa, o_sgu, wo[:D_NSA], wo[D_NSA:], ffn_norm.reshape(1, D_MODEL), wu, cw, cb, wd, tm=512)


def kernel(x, rel_bias, attn_norm, w_in, q_norm, k_norm_cmp, k_norm_slc, k_norm_win,
           cmp_pe_k, cmp_w1_k, cmp_b1_k, cmp_w2_k, cmp_pe_v, cmp_w1_v, cmp_b1_v, cmp_w2_v,
           sgu_norm, sgu_w, sgu_b, w_out, ffn_norm, w_up, conv_w, conv_b, w_down):
    depth = attn_norm.shape[0]
    for l in range(depth):
        x = _layer(x, rel_bias, attn_norm[l], w_in[l], q_norm[l], k_norm_cmp[l], k_norm_slc[l], k_norm_win[l],
                   cmp_pe_k[l], cmp_w1_k[l], cmp_b1_k[l], cmp_w2_k[l], cmp_pe_v[l], cmp_w1_v[l], cmp_b1_v[l],
                   cmp_w2_v[l], sgu_norm[l], sgu_w[l], sgu_b[l], w_out[l], ffn_norm[l], w_up[l], conv_w[l],
                   conv_b[l], w_down[l])
    return x
```
